```python
import math
import jax, jax.numpy as jnp
from jax import lax
import numpy as np

D_MODEL = 1024
BATCH = 8
SEQ = 2048
DEPTH = 2
DEC_BATCH = 128
DEC_SEQ = 1
PAST_LEN = 8192
PAGE_SIZE = 128

SSM_CH_PER_GROUP = 16
SSM_GROUPS = 24
SSM_WIDTH = SSM_GROUPS * SSM_CH_PER_GROUP
SSM_STATE = 64
SSM_DT_MIN = 0.001
SSM_DT_MAX = 0.1
RET_HEADS = 4
RET_DK = 64
RET_DV = 64
RET_WIDTH = RET_HEADS * RET_DV
RET_CHUNK = 128
ATT_Q_HEADS = 6
ATT_KV_HEADS = 2
ATT_HEAD_DIM = 64
ATT_WIDTH = ATT_Q_HEADS * ATT_HEAD_DIM
WINDOW = 128
MIX_WIDTH = SSM_WIDTH + RET_WIDTH + ATT_WIDTH
IN_WIDTH = SSM_WIDTH + 2 * RET_HEADS * RET_DK + 2 * RET_WIDTH + (ATT_Q_HEADS + 2 * ATT_KV_HEADS) * ATT_HEAD_DIM
FFN_HIDDEN = ((8 * D_MODEL + 3 * 256 - 1) // (3 * 256)) * 256
DEEPNORM_ALPHA = (2 * DEPTH) ** 0.25
DEEPNORM_BETA = (8 * DEPTH) ** -0.25
LN_EPS = 1e-5
NEG_INF = -1e30

kernel_name = 'hymba_s5_retnet_swa_deepnorm_step'


def _split_points():
    sizes = [SSM_WIDTH, RET_HEADS * RET_DK, RET_HEADS * RET_DK, RET_WIDTH, RET_WIDTH,
             ATT_Q_HEADS * ATT_HEAD_DIM, ATT_KV_HEADS * ATT_HEAD_DIM, ATT_KV_HEADS * ATT_HEAD_DIM]
    return [int(v) for v in np.cumsum(sizes)[:-1]]


def _alibi_slopes(n):
    def pow2(k):
        start = 2.0 ** (-8.0 / k)
        return [start ** (i + 1) for i in range(k)]
    if n & (n - 1) == 0:
        sl = pow2(n)
    else:
        c = 2 ** int(math.floor(math.log2(n)))
        sl = pow2(c) + pow2(2 * c)[0::2][: n - c]
    return jnp.asarray(sl, jnp.float32)


def _layernorm(x, g, b):
    xf = x.astype(jnp.float32)
    mu = jnp.mean(xf, -1, keepdims=True)
    var = jnp.mean(jnp.square(xf - mu), -1, keepdims=True)
    return ((xf - mu) * lax.rsqrt(var + LN_EPS) * g.astype(jnp.float32) + b.astype(jnp.float32)).astype(x.dtype)


def _ssm_combine(e1, e2):
    a1, b1 = e1
    a2, b2 = e2
    return a1 * a2, a2 * b1 + b2


def _s5(u, p, l, h0):
    f32 = jnp.float32
    bsz, L, _ = u.shape
    uf = u.astype(f32).reshape(bsz, L, SSM_GROUPS, SSM_CH_PER_GROUP)
    lam = lax.complex(p['ssm_lambda_re'][l].astype(f32), p['ssm_lambda_im'][l].astype(f32))
    step = jnp.exp(p['ssm_log_step'][l].astype(f32))[:, None]
    lam_dt = lam * step
    lam_bar = jnp.exp(lam_dt)
    b_mat = lax.complex(p['ssm_b_re'][l].astype(f32), p['ssm_b_im'][l].astype(f32))
    b_bar = ((lam_bar - 1.0) / lam)[..., None] * b_mat
    bu = jnp.einsum('gph,blgh->blgp', b_bar, uf.astype(jnp.complex64))
    _, h = lax.associative_scan(_ssm_combine, (jnp.broadcast_to(lam_bar, bu.shape), bu), axis=1)
    if h0 is not None:
        t = jnp.arange(1, L + 1, dtype=f32)[:, None, None]
        h = h + jnp.exp(lam_dt[None] * t)[None] * h0[:, None]
    c_mat = lax.complex(p['ssm_c_re'][l].astype(f32), p['ssm_c_im'][l].astype(f32))
    d = p['ssm_d'][l].astype(f32).reshape(SSM_GROUPS, SSM_CH_PER_GROUP)
    y = jnp.einsum('ghp,blgp->blgh', c_mat, h).real + d * uf
    return y.reshape(bsz, L, SSM_WIDTH), h[:, -1]


def _retention(q, k, v, s0):
    f32 = jnp.float32
    bsz, L = q.shape[:2]
    c = math.gcd(L, RET_CHUNK)
    n = L // c
    log_g = jnp.log1p(-(2.0 ** (-5.0 - jnp.arange(RET_HEADS, dtype=f32))))
    idx = jnp.arange(c)
    diff = idx[:, None] - idx[None, :]
    decay_mask = jnp.where(diff >= 0, jnp.exp(log_g[:, None, None] * jnp.maximum(diff, 0).astype(f32)), 0.0)
    inner_decay = jnp.exp(log_g[:, None] * (idx + 1).astype(f32))[None, :, :, None]
    kv_decay = jnp.exp(log_g[:, None] * (c - 1 - idx).astype(f32))[None, :, :, None]
    chunk_decay = jnp.exp(log_g * c)[None, :, None, None]

    def to_chunks(t):
        return t.astype(f32).reshape(bsz, n, c, RET_HEADS, t.shape[-1]).transpose(1, 0, 3, 2, 4)

    def step(s, inp):
        qi, ki, vi = inp
        sc = jnp.einsum('bhqd,bhcd->bhqc', qi, ki) * decay_mask
        o = jnp.einsum('bhqc,bhce->bhqe', sc, vi) + jnp.einsum('bhqd,bhde->bhqe', qi, s) * inner_decay
        s_new = chunk_decay * s + jnp.einsum('bhcd,bhce->bhde', ki * kv_decay, vi)
        return s_new, o

    s_last, o = lax.scan(step, s0, (to_chunks(q), to_chunks(k), to_chunks(v)))
    o = o.transpose(1, 0, 3, 2, 4).reshape(bsz, L, RET_HEADS, RET_DV)
    return o, s_last


def _sink_softmax(s, sink):
    m = jnp.maximum(jnp.max(s, -1, keepdims=True), sink)
    e = jnp.exp(s - m)
    return e / (jnp.sum(e, -1, keepdims=True) + jnp.exp(sink - m))


def _swa_prompt(q, k, v, sinks):
    f32 = jnp.float32
    bsz, L = q.shape[:2]
    nb = L // WINDOW
    grp = ATT_Q_HEADS // ATT_KV_HEADS
    qb = q.astype(f32).reshape(bsz, nb, WINDOW, ATT_KV_HEADS, grp, ATT_HEAD_DIM)

    def with_prev(t):
        tb = t.astype(f32).reshape(bsz, nb, WINDOW, ATT_KV_HEADS, ATT_HEAD_DIM)
        prev = jnp.concatenate([jnp.zeros_like(tb[:, :1]), tb[:, :-1]], axis=1)
        return jnp.concatenate([prev, tb], axis=2)

    kc, vc = with_prev(k), with_prev(v)
    s = jnp.einsum('bnqkgd,bnskd->bnkgqs', qb, kc) * (ATT_HEAD_DIM ** -0.5)
    kpos = jnp.arange(2 * WINDOW)
    dist = (jnp.arange(WINDOW) + WINDOW)[:, None] - kpos[None, :]
    valid = (dist >= 0) & (dist <= WINDOW)
    valid = valid[None] & ((jnp.arange(nb)[:, None, None] > 0) | (kpos[None, None, :] >= WINDOW))
    slopes = _alibi_slopes(ATT_Q_HEADS).reshape(ATT_KV_HEADS, grp)[:, :, None, None]
    s = s - slopes * dist.astype(f32)
    s = jnp.where(valid[None, :, None, None], s, NEG_INF)
    w = _sink_softmax(s, sinks.astype(f32).reshape(ATT_KV_HEADS, grp)[:, :, None, None])
    o = jnp.einsum('bnkgqs,bnskd->bnqkgd', w, vc)
    wb = min(WINDOW, L)
    return o.reshape(bsz, L, ATT_WIDTH), k[:, L - wb:], v[:, L - wb:]


def _swa_sample(q, k, v, k_buf, v_buf, sinks):
    f32 = jnp.float32
    bsz, t = q.shape[:2]
    wb = k_buf.shape[1]
    grp = ATT_Q_HEADS // ATT_KV_HEADS
    k_all = jnp.concatenate([k_buf.astype(k.dtype), k], axis=1)
    v_all = jnp.concatenate([v_buf.astype(v.dtype), v], axis=1)
    qg = q.astype(f32).reshape(bsz, t, ATT_KV_HEADS, grp, ATT_HEAD_DIM)
    s = jnp.einsum('btkgd,bskd->bkgts', qg, k_all.astype(f32)) * (ATT_HEAD_DIM ** -0.5)
    dist = (wb + jnp.arange(t))[:, None] - jnp.arange(wb + t)[None, :]
    valid = (dist >= 0) & (dist <= WINDOW)
    slopes = _alibi_slopes(ATT_Q_HEADS).reshape(ATT_KV_HEADS, grp)[:, :, None, None]
    s = jnp.where(valid, s - slopes * dist.astype(f32), NEG_INF)
    w = _sink_softmax(s, sinks.astype(f32).reshape(ATT_KV_HEADS, grp)[:, :, None, None])
    o = jnp.einsum('bkgts,bskd->btkgd', w, v_all.astype(f32))
    return o.reshape(bsz, t, ATT_WIDTH), k_all[:, t:], v_all[:, t:]


def _layer(x, l, p, ssm_h0, ret_s0, win_kv):
    f32 = jnp.float32
    bsz, L, _ = x.shape
    proj = x @ p['w_in'][l]
    u, rq, rk, rv, rg, aq, ak, av = jnp.split(proj, _split_points(), axis=-1)
    y_ssm, h_last = _s5(u, p, l, ssm_h0)
    z = jax.nn.gelu(y_ssm)
    z = z * jax.nn.sigmoid(z @ p['ssm_glu_w'][l].astype(f32) + p['ssm_glu_b'][l].astype(f32))
    if ret_s0 is None:
        ret_s0 = jnp.zeros((bsz, RET_HEADS, RET_DK, RET_DV), f32)
    o_r, s_last = _retention(rq.reshape(bsz, L, RET_HEADS, RET_DK),
                             rk.reshape(bsz, L, RET_HEADS, RET_DK) * (RET_DK ** -0.5),
                             rv.reshape(bsz, L, RET_HEADS, RET_DV), ret_s0)
    mu = jnp.mean(o_r, -1, keepdims=True)
    var = jnp.mean(jnp.square(o_r - mu), -1, keepdims=True)
    o_r = ((o_r - mu) * lax.rsqrt(var + LN_EPS)).reshape(bsz, L, RET_WIDTH)
    o_r = o_r * p['ret_gn_g'][l].astype(f32) + p['ret_gn_b'][l].astype(f32)
    o_r = jax.nn.silu(rg.astype(f32)) * o_r
    aq = aq.reshape(bsz, L, ATT_Q_HEADS, ATT_HEAD_DIM)
    ak = ak.reshape(bsz, L, ATT_KV_HEADS, ATT_HEAD_DIM)
    av = av.reshape(bsz, L, ATT_KV_HEADS, ATT_HEAD_DIM)
    if win_kv is None:
        o_a, k_buf, v_buf = _swa_prompt(aq, ak, av, p['attn_sinks'][l])
    else:
        o_a, k_buf, v_buf = _swa_sample(aq, ak, av, win_kv[0], win_kv[1], p['attn_sinks'][l])
    mix = jnp.concatenate([z.astype(x.dtype), o_r.astype(x.dtype), o_a.astype(x.dtype)], axis=-1) @ p['w_out'][l]
    x = _layernorm(DEEPNORM_ALPHA * x + mix, p['ln1_g'][l], p['ln1_b'][l])
    hid = jax.nn.silu(x @ p['ffn_w1'][l]) * (x @ p['ffn_w3'][l])
    x = _layernorm(DEEPNORM_ALPHA * x + hid @ p['ffn_w2'][l], p['ln2_g'][l], p['ln2_b'][l])
    return x, h_last, s_last, k_buf, v_buf


def _trunk(x, p, ssm_re, ssm_im, ret_s, win_k, win_v):
    hs, ss, ks, vs = [], [], [], []
    for l in range(DEPTH):
        if ssm_re is None:
            h0, s0, kv = None, None, None
        else:
            h0 = lax.complex(ssm_re[l].astype(jnp.float32), ssm_im[l].astype(jnp.float32))
            s0 = ret_s[l].astype(jnp.float32)
            kv = (win_k[l], win_v[l])
        x, h, s, kb, vb = _layer(x, l, p, h0, s0, kv)
        hs.append(h)
        ss.append(s)
        ks.append(kb)
        vs.append(vb)
    h = jnp.stack(hs)
    return x, h.real, h.imag, jnp.stack(ss), jnp.stack(ks), jnp.stack(vs)


def setup_inputs(seed: int = 0) -> dict:
    key = jax.random.key(seed)
    keys = iter(jax.random.split(key, 32))
    f32 = jnp.float32

    def nrm(shape, scale):
        return scale * jax.random.normal(next(keys), shape, f32)

    beta = DEEPNORM_BETA
    wb = min(WINDOW, PAST_LEN)
    col_scale = jnp.concatenate([
        jnp.full((SSM_WIDTH,), beta, f32),
        jnp.ones((2 * RET_HEADS * RET_DK,), f32),
        jnp.full((RET_WIDTH,), beta, f32),
        jnp.ones((RET_WIDTH,), f32),
        jnp.ones(((ATT_Q_HEADS + ATT_KV_HEADS) * ATT_HEAD_DIM,), f32),
        jnp.full((ATT_KV_HEADS * ATT_HEAD_DIM,), beta, f32)])
    n_idx = jnp.arange(SSM_STATE, dtype=f32)
    gp = (DEPTH, SSM_GROUPS, SSM_STATE)
    x_prompt = nrm((BATCH, SEQ, D_MODEL), 1.0)
    x_sample = nrm((DEC_BATCH, DEC_SEQ, D_MODEL), 1.0)
    state_ssm_re = nrm((DEPTH, DEC_BATCH, SSM_GROUPS, SSM_STATE), 0.3)
    state_ssm_im = nrm((DEPTH, DEC_BATCH, SSM_GROUPS, SSM_STATE), 0.3)
    state_ret = nrm((DEPTH, DEC_BATCH, RET_HEADS, RET_DK, RET_DV), 1.0)
    cache_win_k = nrm((DEPTH, DEC_BATCH, wb, ATT_KV_HEADS, ATT_HEAD_DIM), 1.0)
    cache_win_v = nrm((DEPTH, DEC_BATCH, wb, ATT_KV_HEADS, ATT_HEAD_DIM), 1.0)
    w_in = nrm((DEPTH, D_MODEL, IN_WIDTH), D_MODEL ** -0.5) * col_scale
    ssm_lambda_re = -0.5 + nrm(gp, 0.01)
    ssm_lambda_im = math.pi * n_idx + nrm(gp, 0.01)
    ssm_log_step = math.log(SSM_DT_MIN) + jax.random.uniform(next(keys), (DEPTH, SSM_GROUPS), f32) * (math.log(SSM_DT_MAX) - math.log(SSM_DT_MIN))
    ssm_b_re = nrm((DEPTH, SSM_GROUPS, SSM_STATE, SSM_CH_PER_GROUP), SSM_CH_PER_GROUP ** -0.5)
    ssm_b_im = nrm((DEPTH, SSM_GROUPS, SSM_STATE, SSM_CH_PER_GROUP), SSM_CH_PER_GROUP ** -0.5)
    ssm_c_re = nrm((DEPTH, SSM_GROUPS, SSM_CH_PER_GROUP, SSM_STATE), (2 * SSM_STATE) ** -0.5)
    ssm_c_im = nrm((DEPTH, SSM_GROUPS, SSM_CH_PER_GROUP, SSM_STATE), (2 * SSM_STATE) ** -0.5)
    ssm_d = nrm((DEPTH, SSM_WIDTH), 1.0)
    ssm_glu_w = nrm((DEPTH, SSM_WIDTH, SSM_WIDTH), SSM_WIDTH ** -0.5)
    ssm_glu_b = nrm((DEPTH, SSM_WIDTH), 0.02)
    ret_gn_g = 1.0 + nrm((DEPTH, RET_WIDTH), 0.02)
    ret_gn_b = nrm((DEPTH, RET_WIDTH), 0.02)
    attn_sinks = nrm((DEPTH, ATT_Q_HEADS), 0.5)
    w_out = nrm((DEPTH, MIX_WIDTH, D_MODEL), (MIX_WIDTH ** -0.5) * beta)
    ln1_g = 1.0 + nrm((DEPTH, D_MODEL), 0.02)
    ln1_b = nrm((DEPTH, D_MODEL), 0.02)
    ffn_w1 = nrm((DEPTH, D_MODEL, FFN_HIDDEN), (D_MODEL ** -0.5) * beta)
    ffn_w3 = nrm((DEPTH, D_MODEL, FFN_HIDDEN), (D_MODEL ** -0.5) * beta)
    ffn_w2 = nrm((DEPTH, FFN_HIDDEN, D_MODEL), (FFN_HIDDEN ** -0.5) * beta)
    ln2_g = 1.0 + nrm((DEPTH, D_MODEL), 0.02)
    ln2_b = nrm((DEPTH, D_MODEL), 0.02)
    return {'x_prompt': x_prompt, 'x_sample': x_sample,
            'state_ssm_re': state_ssm_re, 'state_ssm_im': state_ssm_im, 'state_ret': state_ret,
            'cache_win_k': cache_win_k, 'cache_win_v': cache_win_v,
            'w_in': w_in, 'ssm_lambda_re': ssm_lambda_re, 'ssm_lambda_im': ssm_lambda_im,
            'ssm_log_step': ssm_log_step, 'ssm_b_re': ssm_b_re, 'ssm_b_im': ssm_b_im,
            'ssm_c_re': ssm_c_re, 'ssm_c_im': ssm_c_im, 'ssm_d': ssm_d,
            'ssm_glu_w': ssm_glu_w, 'ssm_glu_b': ssm_glu_b,
            'ret_gn_g': ret_gn_g, 'ret_gn_b': ret_gn_b, 'attn_sinks': attn_sinks,
            'w_out': w_out, 'ln1_g': ln1_g, 'ln1_b': ln1_b,
            'ffn_w1': ffn_w1, 'ffn_w3': ffn_w3, 'ffn_w2': ffn_w2,
            'ln2_g': ln2_g, 'ln2_b': ln2_b}


def reference(x_prompt, x_sample, state_ssm_re, state_ssm_im, state_ret, cache_win_k, cache_win_v,
              w_in, ssm_lambda_re, ssm_lambda_im, ssm_log_step, ssm_b_re, ssm_b_im, ssm_c_re, ssm_c_im,
              ssm_d, ssm_glu_w, ssm_glu_b, ret_gn_g, ret_gn_b, attn_sinks, w_out, ln1_g, ln1_b,
              ffn_w1, ffn_w3, ffn_w2, ln2_g, ln2_b):
    p = {'w_in': w_in, 'ssm_lambda_re': ssm_lambda_re, 'ssm_lambda_im': ssm_lambda_im,
         'ssm_log_step': ssm_log_step, 'ssm_b_re': ssm_b_re, 'ssm_b_im': ssm_b_im,
         'ssm_c_re': ssm_c_re, 'ssm_c_im': ssm_c_im, 'ssm_d': ssm_d,
         'ssm_glu_w': ssm_glu_w, 'ssm_glu_b': ssm_glu_b, 'ret_gn_g': ret_gn_g, 'ret_gn_b': ret_gn_b,
         'attn_sinks': attn_sinks, 'w_out': w_out, 'ln1_g': ln1_g, 'ln1_b': ln1_b,
         'ffn_w1': ffn_w1, 'ffn_w3': ffn_w3, 'ffn_w2': ffn_w2, 'ln2_g': ln2_g, 'ln2_b': ln2_b}
    y_prompt, p_ssm_re, p_ssm_im, p_ret, p_win_k, p_win_v = _trunk(x_prompt, p, None, None, None, None, None)
    y_sample, s_ssm_re, s_ssm_im, s_ret, s_win_k, s_win_v = _trunk(
        x_sample, p, state_ssm_re, state_ssm_im, state_ret, cache_win_k, cache_win_v)
    return (y_prompt, y_sample, p_ssm_re, p_ssm_im, p_ret, p_win_k, p_win_v,
            s_ssm_re, s_ssm_im, s_ret, s_win_k, s_win_v)
```

```python
import functools
import math

import numpy as np
import jax
import jax.numpy as jnp
from jax import lax
from jax.experimental import pallas as pl
from jax.experimental.pallas import tpu as pltpu

F32 = jnp.float32
BF16 = jnp.bfloat16

D_MODEL = 1024
DEPTH = 2
SSM_GROUPS = 24
SSM_CH = 16
SSM_STATE = 64
SSM_WIDTH = SSM_GROUPS * SSM_CH
SSM_LANES = SSM_GROUPS * SSM_STATE
RET_HEADS = 4
RET_D = 64
RET_WIDTH = RET_HEADS * RET_D
ATT_Q_HEADS = 6
ATT_KV_HEADS = 2
ATT_D = 64
ATT_WIDTH = ATT_Q_HEADS * ATT_D
WINDOW = 128
FFN_HIDDEN = 2816
ALPHA = (2 * DEPTH) ** 0.25
LN_EPS = 1e-5
NEG_INF = -1e30

TC = 128
LANES = 128
N_TILES = SSM_LANES // LANES
HALF = 64
Q_ORDER = (0, 3, 2, 5, 1, 4)
ROW_TILE = 512
VMEM_LIMIT = 56 * 2 ** 20


def _alibi_slopes(n):
    def pow2(k):
        start = 2.0 ** (-8.0 / k)
        return [start ** (i + 1) for i in range(k)]
    if n & (n - 1) == 0:
        return pow2(n)
    c = 2 ** int(math.floor(math.log2(n)))
    return pow2(c) + pow2(2 * c)[0::2][: n - c]


def _cparams(n_axes):
    return pltpu.CompilerParams(dimension_semantics=("arbitrary",) * n_axes, vmem_limit_bytes=VMEM_LIMIT)


def _const_spec(shape):
    nd = len(shape)
    return pl.BlockSpec(shape, lambda *_: (0,) * nd, pipeline_mode=pl.Buffered(1))


def _dot(a, b):
    return jnp.dot(a, b, preferred_element_type=F32)


def _dot_nt(a, b):
    return lax.dot_general(a, b, (((1,), (1,)), ((), ())), preferred_element_type=F32)


def _dot_tn(a, b):
    return lax.dot_general(a, b, (((0,), (0,)), ((), ())), preferred_element_type=F32)


def _layernorm(v, g, b):
    mu = jnp.mean(v, -1, keepdims=True)
    d = v - mu
    var = jnp.mean(d * d, -1, keepdims=True)
    return d * lax.rsqrt(var + LN_EPS) * g + b


def _gelu_tanh(x):
    return 0.5 * x * (1.0 + jnp.tanh(math.sqrt(2.0 / math.pi) * (x + 0.044715 * (x * x * x))))


def _sigmoid(x):
    return 1.0 / (1.0 + jnp.exp(-x))


def _inproj_kernel(x_ref, w_ref, pr_ref, pa_ref, pu_ref):
    xb = x_ref[...].astype(BF16)
    pr_ref[...] = _dot(xb, w_ref[:, 0:1024])
    pa_ref[...] = _dot(xb, w_ref[:, 1024:1664])
    pu_ref[...] = _dot(xb, w_ref[:, 1664:2048])


def _inproj(x2, w):
    t = x2.shape[0]
    tm = min(ROW_TILE, t)
    return pl.pallas_call(
        _inproj_kernel,
        grid=(t // tm,),
        in_specs=[pl.BlockSpec((tm, D_MODEL), lambda i: (i, 0)), _const_spec((D_MODEL, 2048))],
        out_specs=[pl.BlockSpec((tm, 1024), lambda i: (i, 0)),
                   pl.BlockSpec((tm, 640), lambda i: (i, 0)),
                   pl.BlockSpec((tm, SSM_WIDTH), lambda i: (i, 0))],
        out_shape=[jax.ShapeDtypeStruct((t, 1024), F32),
                   jax.ShapeDtypeStruct((t, 640), F32),
                   jax.ShapeDtypeStruct((t, SSM_WIDTH), F32)],
        compiler_params=_cparams(1),
        name="inproj",
    )(x2, w)


def _s5_bu(ub, bre_ref, bim_ref):
    res, ims = [], []
    for i in range(3):
        ui = ub[:, LANES * i:LANES * (i + 1)]
        res.append(_dot(ui, bre_ref[i]))
        ims.append(_dot(ui, bim_ref[i]))
    return res, ims


def _s5_out(u, h_re, h_im, cre_ref, cim_ref, d_ref, gw_ref, gb_ref):
    ys = []
    for i in range(3):
        ys.append(_dot(h_re[i].astype(BF16), cre_ref[i]) + _dot(h_im[i].astype(BF16), cim_ref[i]))
    y = jnp.concatenate(ys, axis=1) + d_ref[...] * u
    z = _gelu_tanh(y)
    gate = _sigmoid(_dot(z.astype(BF16), gw_ref[...]) + gb_ref[...])
    return z * gate


def _s5_prompt_kernel(u_ref, bre_ref, bim_ref, cre_ref, cim_ref, ar_ref, ai_ref, d_ref, gw_ref, gb_ref,
                      z_ref, hlast_ref, h_scr, hc_scr):
    n = pl.program_id(0)
    nb = u_ref.shape[0]
    rows = nb * TC

    @pl.when(n == 0)
    def _():
        hc_scr[...] = jnp.zeros_like(hc_scr)

    u = u_ref[...].reshape(rows, SSM_WIDTH)
    res, ims = _s5_bu(u.astype(BF16), bre_ref, bim_ref)
    for i in range(3):
        for k in range(4):
            h_scr[4 * i + k] = res[i][:, LANES * k:LANES * (k + 1)]
            h_scr[N_TILES + 4 * i + k] = ims[i][:, LANES * k:LANES * (k + 1)]

    def step(t, carry):
        out = []
        for c in range(N_TILES):
            sl = slice(LANES * c, LANES * (c + 1))
            ar, ai = ar_ref[:, sl], ai_ref[:, sl]
            hr, hi = carry[c], carry[N_TILES + c]
            nr = ar * hr - ai * hi + h_scr[c, pl.ds(t, nb, stride=TC), :]
            ni = ar * hi + ai * hr + h_scr[N_TILES + c, pl.ds(t, nb, stride=TC), :]
            h_scr[c, pl.ds(t, nb, stride=TC), :] = nr
            h_scr[N_TILES + c, pl.ds(t, nb, stride=TC), :] = ni
            out.append((nr, ni))
        return tuple(o[0] for o in out) + tuple(o[1] for o in out)

    init = tuple(hc_scr[:, LANES * c:LANES * (c + 1)] for c in range(2 * N_TILES))
    fin = lax.fori_loop(0, TC, step, init, unroll=2)
    for c in range(2 * N_TILES):
        hc_scr[:, LANES * c:LANES * (c + 1)] = fin[c]
        hlast_ref[:, LANES * c:LANES * (c + 1)] = fin[c]

    h_re = [jnp.concatenate([h_scr[4 * i + k] for k in range(4)], axis=1) for i in range(3)]
    h_im = [jnp.concatenate([h_scr[N_TILES + 4 * i + k] for k in range(4)], axis=1) for i in range(3)]
    zz = _s5_out(u, h_re, h_im, cre_ref, cim_ref, d_ref, gw_ref, gb_ref)
    z_ref[...] = zz.astype(BF16).reshape(nb, TC, SSM_WIDTH)


def _s5_weight_specs():
    return [_const_spec((3, LANES, 512)), _const_spec((3, LANES, 512)),
            _const_spec((3, 512, LANES)), _const_spec((3, 512, LANES)),
            _const_spec((1, SSM_LANES)), _const_spec((1, SSM_LANES)),
            _const_spec((1, SSM_WIDTH)), _const_spec((SSM_WIDTH, SSM_WIDTH)), _const_spec((1, SSM_WIDTH))]


def _s5_weights(p):
    return (p["bre"], p["bim"], p["cre"], p["cim"], p["ar"], p["ai"], p["ssm_d"], p["glu_w"], p["glu_b"])


def _s5_prompt(u3, p):
    nb, length, _ = u3.shape
    return pl.pallas_call(
        _s5_prompt_kernel,
        grid=(length // TC,),
        in_specs=[pl.BlockSpec((nb, TC, SSM_WIDTH), lambda n: (0, n, 0))] + _s5_weight_specs(),
        out_specs=[pl.BlockSpec((nb, TC, SSM_WIDTH), lambda n: (0, n, 0)),
                   pl.BlockSpec((nb, 2 * SSM_LANES), lambda n: (0, 0))],
        out_shape=[jax.ShapeDtypeStruct((nb, length, SSM_WIDTH), BF16),
                   jax.ShapeDtypeStruct((nb, 2 * SSM_LANES), F32)],
        scratch_shapes=[pltpu.VMEM((2 * N_TILES, nb * TC, LANES), F32), pltpu.VMEM((nb, 2 * SSM_LANES), F32)],
        compiler_params=_cparams(1),
        name="s5_prompt",
    )(u3, *_s5_weights(p))


def _s5_sample_kernel(u_ref, h0r_ref, h0i_ref, bre_ref, bim_ref, cre_ref, cim_ref, ar_ref, ai_ref, d_ref,
                      gw_ref, gb_ref, z_ref, hr_ref, hi_ref):
    u = u_ref[...]
    res, ims = _s5_bu(u.astype(BF16), bre_ref, bim_ref)
    h_re, h_im = [], []
    for i in range(3):
        sl = slice(512 * i, 512 * (i + 1))
        ar, ai = ar_ref[:, sl], ai_ref[:, sl]
        h0r, h0i = h0r_ref[:, sl], h0i_ref[:, sl]
        nr = ar * h0r - ai * h0i + res[i]
        ni = ar * h0i + ai * h0r + ims[i]
        hr_ref[:, sl] = nr
        hi_ref[:, sl] = ni
        h_re.append(nr)
        h_im.append(ni)
    z_ref[...] = _s5_out(u, h_re, h_im, cre_ref, cim_ref, d_ref, gw_ref, gb_ref).astype(BF16)


def _s5_sample(u, h0r, h0i, p):
    nb = u.shape[0]
    full = lambda c: pl.BlockSpec((nb, c), lambda i: (0, 0))
    return pl.pallas_call(
        _s5_sample_kernel,
        grid=(1,),
        in_specs=[full(SSM_WIDTH), full(SSM_LANES), full(SSM_LANES)] + _s5_weight_specs(),
        out_specs=[full(SSM_WIDTH), full(SSM_LANES), full(SSM_LANES)],
        out_shape=[jax.ShapeDtypeStruct((nb, SSM_WIDTH), BF16),
                   jax.ShapeDtypeStruct((nb, SSM_LANES), F32),
                   jax.ShapeDtypeStruct((nb, SSM_LANES), F32)],
        compiler_params=_cparams(1),
        name="s5_sample",
    )(u, h0r, h0i, *_s5_weights(p))


def _ret_gammas():
    return [1.0 - 2.0 ** (-5.0 - h) for h in range(RET_HEADS)]


def _ret_consts():
    log_g = np.log1p(-(2.0 ** (-5.0 - np.arange(RET_HEADS, dtype=np.float64))))
    idx = np.arange(TC)
    diff = idx[:, None] - idx[None, :]
    dmask = np.where(diff >= 0, np.exp(log_g[:, None, None] * np.maximum(diff, 0)), 0.0)
    inner = np.exp(log_g[:, None] * (idx + 1))
    kvd = np.exp(log_g[:, None] * (TC - 1 - idx))
    cd = np.exp(log_g * TC)
    indec = np.zeros((2, TC, LANES))
    kvdec = np.zeros((2, TC, LANES))
    cdec = np.zeros((2, LANES, LANES))
    blk = np.zeros((LANES, LANES))
    for j in range(2):
        for hh in range(2):
            sl = slice(HALF * hh, HALF * (hh + 1))
            indec[j, :, sl] = inner[2 * j + hh][:, None]
            kvdec[j, :, sl] = kvd[2 * j + hh][:, None]
            cdec[j, sl, sl] = cd[2 * j + hh]
            blk[sl, sl] = 1.0
    as32 = lambda a: jnp.asarray(a, F32)
    return as32(dmask), as32(kvdec), as32(indec), as32(cdec), as32(blk)


def _halfnorm(o, lo):
    zero = jnp.zeros_like(o)
    s_lo = jnp.sum(jnp.where(lo, o, zero), -1, keepdims=True)
    s_hi = jnp.sum(jnp.where(lo, zero, o), -1, keepdims=True)
    d = o - jnp.where(lo, s_lo, s_hi) * (1.0 / HALF)
    d2 = d * d
    v_lo = jnp.sum(jnp.where(lo, d2, zero), -1, keepdims=True)
    v_hi = jnp.sum(jnp.where(lo, zero, d2), -1, keepdims=True)
    var = jnp.where(lo, v_lo, v_hi) * (1.0 / HALF)
    return d * lax.rsqrt(var + LN_EPS)


def _ret_prompt_kernel(pr_ref, dmask_ref, kvdec_ref, indec_ref, cdec_ref, blk_ref, gng_ref, gnb_ref,
                       o_ref, slast_ref, s_scr):
    n = pl.program_id(1)

    @pl.when(n == 0)
    def _():
        s_scr[...] = jnp.zeros_like(s_scr)

    lo = lax.broadcasted_iota(jnp.int32, (TC, LANES), 1) < HALF
    zero = jnp.zeros((TC, LANES), F32)
    for j in range(2):
        q2 = pr_ref[0, :, LANES * j:LANES * (j + 1)]
        k2 = pr_ref[0, :, 256 + LANES * j:256 + LANES * (j + 1)]
        v2 = pr_ref[0, :, 512 + LANES * j:512 + LANES * (j + 1)]
        g2 = pr_ref[0, :, 768 + LANES * j:768 + LANES * (j + 1)]
        k2b = k2.astype(BF16)
        v2b = v2.astype(BF16)
        state = s_scr[j]
        halves = []
        for hh in range(2):
            qm = (jnp.where(lo, q2, zero) if hh == 0 else jnp.where(lo, zero, q2)).astype(BF16)
            sc = _dot_nt(qm, k2b) * dmask_ref[2 * j + hh]
            halves.append(_dot(sc.astype(BF16), v2b))
        o = jnp.where(lo, halves[0], halves[1])
        o = o + _dot(q2.astype(BF16), state.astype(BF16)) * indec_ref[j]
        kd = (k2 * kvdec_ref[j]).astype(BF16)
        s_scr[j] = cdec_ref[j] * state + blk_ref[...] * _dot_tn(kd, v2b)
        sl = slice(LANES * j, LANES * (j + 1))
        o = _halfnorm(o, lo) * gng_ref[:, sl] + gnb_ref[:, sl]
        o_ref[0, :, sl] = (g2 * _sigmoid(g2) * o).astype(BF16)
    slast_ref[0] = s_scr[...]


def _ret_prompt(pr3, p):
    nb, length, _ = pr3.shape
    dmask, kvdec, indec, cdec, blk = _ret_consts()
    return pl.pallas_call(
        _ret_prompt_kernel,
        grid=(nb, length // TC),
        in_specs=[pl.BlockSpec((1, TC, 1024), lambda b, n: (b, n, 0)),
                  _const_spec((RET_HEADS, TC, TC)), _const_spec((2, TC, LANES)), _const_spec((2, TC, LANES)),
                  _const_spec((2, LANES, LANES)), _const_spec((LANES, LANES)),
                  _const_spec((1, RET_WIDTH)), _const_spec((1, RET_WIDTH))],
        out_specs=[pl.BlockSpec((1, TC, RET_WIDTH), lambda b, n: (b, n, 0)),
                   pl.BlockSpec((1, 2, LANES, LANES), lambda b, n: (b, 0, 0, 0))],
        out_shape=[jax.ShapeDtypeStruct((nb, length, RET_WIDTH), BF16),
                   jax.ShapeDtypeStruct((nb, 2, LANES, LANES), F32)],
        scratch_shapes=[pltpu.VMEM((2, LANES, LANES), F32)],
        compiler_params=_cparams(2),
        name="ret_prompt",
    )(pr3, dmask, kvdec, indec, cdec, blk, p["gn_g"], p["gn_b"])


def _att_bias():
    slopes = _alibi_slopes(ATT_Q_HEADS)
    qpos = np.arange(TC)[:, None] + TC
    kpos = np.arange(2 * TC)[None, :]
    dist = qpos - kpos
    valid = (dist >= 0) & (dist <= WINDOW)
    bias = np.stack([np.where(valid, -slopes[h] * dist, NEG_INF) for h in Q_ORDER])
    first = np.where(np.arange(2 * TC) < TC, NEG_INF, 0.0)[None, :]
    return jnp.asarray(bias, F32), jnp.asarray(first, F32)


def _att_prompt_kernel(sink_ref, pa_ref, bias_ref, first_ref, o_ref, kout_ref, vout_ref, kcat, vaug, bias_scr):
    n = pl.program_id(1)

    @pl.when(n == 0)
    def _():
        kcat[0:TC, :] = jnp.zeros((TC, LANES), BF16)
        vaug[0:TC, 0:LANES] = jnp.zeros((TC, LANES), BF16)
        vaug[:, LANES:2 * LANES] = jnp.ones((2 * TC, LANES), BF16)
        bias_scr[...] = bias_ref[...] + first_ref[...]

    @pl.when(n == 1)
    def _():
        bias_scr[...] = bias_ref[...]

    kcur = pa_ref[0, :, 384:512]
    vcur = pa_ref[0, :, 512:640]
    kout_ref[0] = kcur
    vout_ref[0] = vcur
    kcat[TC:2 * TC, :] = kcur.astype(BF16)
    vaug[TC:2 * TC, 0:LANES] = vcur.astype(BF16)
    kc = kcat[...]
    va = vaug[...]

    lo = lax.broadcasted_iota(jnp.int32, (TC, LANES), 1) < HALF
    zero = jnp.zeros((TC, LANES), F32)
    for c in range(3):
        qcol = pa_ref[0, :, LANES * c:LANES * (c + 1)]
        halves = []
        for hh in range(2):
            pos = 2 * c + hh
            sink = sink_ref[pos]
            qm = (jnp.where(lo, qcol, zero) if hh == 0 else jnp.where(lo, zero, qcol)).astype(BF16)
            s = _dot_nt(qm, kc) + bias_scr[pos]
            m = jnp.maximum(jnp.max(s, -1, keepdims=True), sink)
            e = jnp.exp(s - m)
            r = _dot(e.astype(BF16), va)
            den = r[:, LANES:2 * LANES] + jnp.exp(sink - m)
            halves.append(r[:, 0:LANES] / den)
        o_ref[0, :, LANES * c:LANES * (c + 1)] = jnp.where(lo, halves[0], halves[1]).astype(BF16)

    kcat[0:TC, :] = kcat[TC:2 * TC, :]
    vaug[0:TC, 0:LANES] = vaug[TC:2 * TC, 0:LANES]


def _att_prompt(pa3, p):
    nb, length, _ = pa3.shape
    bias, first = _att_bias()
    return pl.pallas_call(
        _att_prompt_kernel,
        grid=(nb, length // TC),
        in_specs=[pl.BlockSpec(memory_space=pltpu.SMEM),
                  pl.BlockSpec((1, TC, 640), lambda b, n: (b, n, 0)),
                  _const_spec((ATT_Q_HEADS, TC, 2 * TC)), _const_spec((1, 2 * TC))],
        out_specs=[pl.BlockSpec((1, TC, ATT_WIDTH), lambda b, n: (b, n, 0)),
                   pl.BlockSpec((1, TC, LANES), lambda b, n: (b, 0, 0)),
                   pl.BlockSpec((1, TC, LANES), lambda b, n: (b, 0, 0))],
        out_shape=[jax.ShapeDtypeStruct((nb, length, ATT_WIDTH), BF16),
                   jax.ShapeDtypeStruct((nb, TC, LANES), F32),
                   jax.ShapeDtypeStruct((nb, TC, LANES), F32)],
        scratch_shapes=[pltpu.VMEM((2 * TC, LANES), BF16), pltpu.VMEM((2 * TC, 2 * LANES), BF16),
                        pltpu.VMEM((ATT_Q_HEADS, TC, 2 * TC), F32)],
        compiler_params=_cparams(2),
        name="att_prompt",
    )(p["sinks"], pa3, bias, first)


SAMPLE_ROWS = 16


def _mix_sample_kernel(qt_ref, kt_ref, v4_ref, g4_ref, s_ref, qb_ref, kn_ref, vn_ref, kc_ref, vc_ref,
                       gng_ref, gnb_ref, bias_ref, sink_ref,
                       or_ref, oa_ref, sn_ref, kout_ref, vout_ref):
    gammas = _ret_gammas()
    lo = lax.broadcasted_iota(jnp.int32, (4, LANES), 1) < HALF
    sink = sink_ref[:, 0:1]
    bias = bias_ref[...]

    def body(i, carry):
        qt = qt_ref[i]
        kt = kt_ref[i]
        v4 = v4_ref[i]
        g4 = g4_ref[i]
        rows = []
        for h in range(RET_HEADS):
            sl = slice(RET_D * h, RET_D * (h + 1))
            sn = gammas[h] * s_ref[i, sl, :] + kt[:, h:h + 1] * v4[h:h + 1, :]
            sn_ref[i, sl, :] = sn
            rows.append(jnp.sum(qt[:, h:h + 1] * sn, axis=0, keepdims=True))
        o4 = jnp.concatenate(rows, axis=0)
        mu = jnp.mean(o4, -1, keepdims=True)
        d = o4 - mu
        var = jnp.mean(d * d, -1, keepdims=True)
        o4 = d * lax.rsqrt(var + LN_EPS) * gng_ref[...] + gnb_ref[...]
        or_ref[i] = g4 * _sigmoid(g4) * o4

        kb = kc_ref[i]
        vb = vc_ref[i]
        q8 = qb_ref[i]
        knew = kn_ref[i]
        vnew = vn_ref[i]
        s = _dot_nt(q8.astype(BF16), kb.astype(BF16)) + bias
        s_self = jnp.sum(q8 * knew, -1, keepdims=True)
        m = jnp.maximum(jnp.maximum(jnp.max(s, -1, keepdims=True), s_self), sink)
        e = jnp.exp(s - m)
        e_self = jnp.exp(s_self - m)
        den = jnp.sum(e, -1, keepdims=True) + e_self + jnp.exp(sink - m)
        o8 = (_dot(e.astype(BF16), vb.astype(BF16)) + e_self * vnew) / den
        oa_ref[i] = jnp.where(lo, o8[0:4], o8[4:8])

        kout_ref[i, 0:WINDOW - 1, :] = kc_ref[i, 1:WINDOW, :]
        kout_ref[i, WINDOW - 1:WINDOW, :] = knew
        vout_ref[i, 0:WINDOW - 1, :] = vc_ref[i, 1:WINDOW, :]
        vout_ref[i, WINDOW - 1:WINDOW, :] = vnew
        return carry

    lax.fori_loop(0, SAMPLE_ROWS, body, 0)


def _mix_sample(qt, kt, v4, g4, s, qb, kn, vn, kc, vc, gng4, gnb4, bias8, sink8):
    nb = qt.shape[0]
    bb = SAMPLE_ROWS
    blk = lambda *dims: pl.BlockSpec((bb,) + dims, lambda i: (i,) + (0,) * len(dims))
    return pl.pallas_call(
        _mix_sample_kernel,
        grid=(nb // bb,),
        in_specs=[blk(RET_D, RET_HEADS), blk(RET_D, RET_HEADS), blk(RET_HEADS, RET_D), blk(RET_HEADS, RET_D),
                  blk(RET_WIDTH, RET_D), blk(8, LANES), blk(1, LANES), blk(1, LANES),
                  blk(WINDOW, LANES), blk(WINDOW, LANES),
                  _const_spec((RET_HEADS, RET_D)), _const_spec((RET_HEADS, RET_D)),
                  _const_spec((8, LANES)), _const_spec((8, LANES))],
        out_specs=[blk(RET_HEADS, RET_D), blk(4, LANES), blk(RET_WIDTH, RET_D), blk(WINDOW, LANES), blk(WINDOW, LANES)],
        out_shape=[jax.ShapeDtypeStruct((nb, RET_HEADS, RET_D), F32),
                   jax.ShapeDtypeStruct((nb, 4, LANES), F32),
                   jax.ShapeDtypeStruct((nb, RET_WIDTH, RET_D), F32),
                   jax.ShapeDtypeStruct((nb, WINDOW, LANES), F32),
                   jax.ShapeDtypeStruct((nb, WINDOW, LANES), F32)],
        compiler_params=_cparams(1),
        name="mix_sample",
    )(qt, kt, v4, g4, s, qb, kn, vn, kc, vc, gng4, gnb4, bias8, sink8)


def _outproj_kernel(z_ref, r_ref, a_ref, x_ref, wz_ref, wr_ref, wa_ref, g_ref, b_ref, o_ref):
    mix = _dot(z_ref[...], wz_ref[...]) + _dot(r_ref[...], wr_ref[...]) + _dot(a_ref[...], wa_ref[...])
    o_ref[...] = _layernorm(ALPHA * x_ref[...] + mix, g_ref[...], b_ref[...])


def _outproj(z, r, a, x2, p):
    t = x2.shape[0]
    tm = min(ROW_TILE, t)
    row = lambda c: pl.BlockSpec((tm, c), lambda i: (i, 0))
    return pl.pallas_call(
        _outproj_kernel,
        grid=(t // tm,),
        in_specs=[row(SSM_WIDTH), row(RET_WIDTH), row(ATT_WIDTH), row(D_MODEL),
                  _const_spec((SSM_WIDTH, D_MODEL)), _const_spec((RET_WIDTH, D_MODEL)),
                  _const_spec((ATT_WIDTH, D_MODEL)), _const_spec((1, D_MODEL)), _const_spec((1, D_MODEL))],
        out_specs=row(D_MODEL),
        out_shape=jax.ShapeDtypeStruct((t, D_MODEL), F32),
        compiler_params=_cparams(1),
        name="outproj",
    )(z, r, a, x2, p["wo_z"], p["wo_r"], p["wo_a"], p["ln1_g"], p["ln1_b"])


def _ffn_kernel(x_ref, w1_ref, w3_ref, w2_ref, g_ref, b_ref, o_ref):
    x = x_ref[...]
    xb = x.astype(BF16)
    h1 = _dot(xb, w1_ref[...])
    h3 = _dot(xb, w3_ref[...])
    hid = (h1 * _sigmoid(h1) * h3).astype(BF16)
    o_ref[...] = _layernorm(ALPHA * x + _dot(hid, w2_ref[...]), g_ref[...], b_ref[...])


def _ffn(x2, p):
    t = x2.shape[0]
    tm = min(ROW_TILE, t)
    row = pl.BlockSpec((tm, D_MODEL), lambda i: (i, 0))
    return pl.pallas_call(
        _ffn_kernel,
        grid=(t // tm,),
        in_specs=[row, _const_spec((D_MODEL, FFN_HIDDEN)), _const_spec((D_MODEL, FFN_HIDDEN)),
                  _const_spec((FFN_HIDDEN, D_MODEL)), _const_spec((1, D_MODEL)), _const_spec((1, D_MODEL))],
        out_specs=row,
        out_shape=jax.ShapeDtypeStruct((t, D_MODEL), F32),
        compiler_params=_cparams(1),
        name="ffn",
    )(x2, p["w1"], p["w3"], p["w2"], p["ln2_g"], p["ln2_b"])


def _prep_layer(l, w_in, lam_re, lam_im, log_step, b_re, b_im, c_re, c_im, ssm_d, glu_w, glu_b,
                gn_g, gn_b, sinks, w_out, ln1_g, ln1_b, w1, w3, w2, ln2_g, ln2_b):
    w = w_in[l]
    o = 0
    parts = []
    for size in (SSM_WIDTH, RET_WIDTH, RET_WIDTH, RET_WIDTH, RET_WIDTH, ATT_WIDTH, LANES, LANES):
        parts.append(w[:, o:o + size])
        o += size
    w_u, w_rq, w_rk, w_rv, w_rg, w_aq, w_ak, w_av = parts
    order = jnp.asarray(Q_ORDER)
    w_aq = w_aq.reshape(D_MODEL, ATT_Q_HEADS, ATT_D)[:, order, :].reshape(D_MODEL, ATT_WIDTH)
    w_all = jnp.concatenate([w_rq, w_rk * (RET_D ** -0.5), w_rv, w_rg, w_aq * (ATT_D ** -0.5), w_ak, w_av, w_u], axis=1)

    lam = lax.complex(lam_re[l], lam_im[l])
    step = jnp.exp(log_step[l])[:, None]
    lam_bar = jnp.exp(lam * step)
    b_bar = ((lam_bar - 1.0) / lam)[..., None] * lax.complex(b_re[l], b_im[l])
    eye8 = jnp.eye(8, dtype=F32)

    def b_blocks(m):
        return jnp.einsum("igph,gk->ighkp", m.reshape(3, 8, SSM_STATE, SSM_CH), eye8).reshape(3, LANES, 512)

    def c_blocks(m):
        return jnp.einsum("ighp,gk->ikpgh", m.reshape(3, 8, SSM_CH, SSM_STATE), eye8).reshape(3, 512, LANES)

    wo = w_out[l]
    wo_a = wo[SSM_WIDTH + RET_WIDTH:].reshape(ATT_Q_HEADS, ATT_D, D_MODEL)[order].reshape(ATT_WIDTH, D_MODEL)
    sink_pos = sinks[l][order]
    sink8 = jnp.zeros((8,), F32).at[0:3].set(sink_pos[0::2]).at[4:7].set(sink_pos[1::2])
    row = lambda a: a[l].reshape(1, -1)
    return dict(
        w_in=w_all.astype(BF16),
        bre=b_blocks(b_bar.real).astype(BF16), bim=b_blocks(b_bar.imag).astype(BF16),
        cre=c_blocks(c_re[l]).astype(BF16), cim=c_blocks(-c_im[l]).astype(BF16),
        ar=lam_bar.real.reshape(1, SSM_LANES), ai=lam_bar.imag.reshape(1, SSM_LANES),
        ssm_d=row(ssm_d), glu_w=glu_w[l].astype(BF16), glu_b=row(glu_b),
        gn_g=row(gn_g), gn_b=row(gn_b),
        gn_g4=gn_g[l].reshape(RET_HEADS, RET_D), gn_b4=gn_b[l].reshape(RET_HEADS, RET_D),
        sinks=sink_pos, sink8=jnp.broadcast_to(sink8[:, None], (8, LANES)),
        wo_z=wo[:SSM_WIDTH].astype(BF16), wo_r=wo[SSM_WIDTH:SSM_WIDTH + RET_WIDTH].astype(BF16),
        wo_a=wo_a.astype(BF16),
        ln1_g=row(ln1_g), ln1_b=row(ln1_b),
        w1=w1[l].astype(BF16), w3=w3[l].astype(BF16), w2=w2[l].astype(BF16),
        ln2_g=row(ln2_g), ln2_b=row(ln2_b),
    )


def _prompt_layer(x3, p):
    nb, length, _ = x3.shape
    t = nb * length
    x2 = x3.reshape(t, D_MODEL)
    pr, pa, pu = _inproj(x2, p["w_in"])
    z, hlast = _s5_prompt(pu.reshape(nb, length, SSM_WIDTH), p)
    o_r, sblk = _ret_prompt(pr.reshape(nb, length, 1024), p)
    o_a, klast, vlast = _att_prompt(pa.reshape(nb, length, 640), p)
    x1 = _outproj(z.reshape(t, SSM_WIDTH), o_r.reshape(t, RET_WIDTH), o_a.reshape(t, ATT_WIDTH), x2, p)
    y = _ffn(x1, p).reshape(nb, length, D_MODEL)
    h_re = hlast[:, :SSM_LANES].reshape(nb, SSM_GROUPS, SSM_STATE)
    h_im = hlast[:, SSM_LANES:].reshape(nb, SSM_GROUPS, SSM_STATE)
    s4 = jnp.stack([sblk[:, h // 2, HALF * (h % 2):HALF * (h % 2 + 1), HALF * (h % 2):HALF * (h % 2 + 1)]
                    for h in range(RET_HEADS)], axis=1)
    kv_shape = (nb, TC, ATT_KV_HEADS, ATT_D)
    return y, h_re, h_im, s4, klast.reshape(kv_shape), vlast.reshape(kv_shape)


def _sample_bias8():
    slopes = _alibi_slopes(ATT_Q_HEADS)
    dist = WINDOW - np.arange(WINDOW)
    bias = np.zeros((8, WINDOW))
    for c in range(3):
        bias[c] = -slopes[Q_ORDER[2 * c]] * dist
        bias[4 + c] = -slopes[Q_ORDER[2 * c + 1]] * dist
    return jnp.asarray(bias, F32)


def _sample_layer(x2, p, h0r, h0i, s0, kc, vc):
    nb = x2.shape[0]
    pr, pa, pu = _inproj(x2, p["w_in"])
    z, h_re, h_im = _s5_sample(pu, h0r.reshape(nb, SSM_LANES), h0i.reshape(nb, SSM_LANES), p)
    q4 = pr[:, 0:256].reshape(nb, RET_HEADS, RET_D)
    k4 = pr[:, 256:512].reshape(nb, RET_HEADS, RET_D)
    v4 = pr[:, 512:768].reshape(nb, RET_HEADS, RET_D)
    g4 = pr[:, 768:1024].reshape(nb, RET_HEADS, RET_D)
    qcols = pa[:, 0:ATT_WIDTH].reshape(nb, 3, LANES)
    low = (jnp.arange(LANES) < HALF)[None, None, :]
    zero1 = jnp.zeros((nb, 1, LANES), F32)
    qb = jnp.concatenate([jnp.where(low, qcols, 0.0), zero1, jnp.where(low, 0.0, qcols), zero1], axis=1)
    kn = pa[:, 384:512].reshape(nb, 1, LANES)
    vn = pa[:, 512:640].reshape(nb, 1, LANES)
    o_r, o_a, s_new, k_out, v_out = _mix_sample(
        q4.transpose(0, 2, 1), k4.transpose(0, 2, 1), v4, g4, s0.reshape(nb, RET_WIDTH, RET_D),
        qb, kn, vn, kc.reshape(nb, WINDOW, LANES), vc.reshape(nb, WINDOW, LANES),
        p["gn_g4"], p["gn_b4"], _sample_bias8(), p["sink8"])
    o_r = o_r.reshape(nb, RET_WIDTH).astype(BF16)
    o_a = o_a[:, 0:3].reshape(nb, ATT_WIDTH).astype(BF16)
    x1 = _outproj(z, o_r, o_a, x2, p)
    y = _ffn(x1, p)
    kv_shape = (nb, WINDOW, ATT_KV_HEADS, ATT_D)
    return (y, h_re.reshape(nb, SSM_GROUPS, SSM_STATE), h_im.reshape(nb, SSM_GROUPS, SSM_STATE),
            s_new.reshape(nb, RET_HEADS, RET_D, RET_D), k_out.reshape(kv_shape), v_out.reshape(kv_shape))


def kernel(x_prompt, x_sample, state_ssm_re, state_ssm_im, state_ret, cache_win_k, cache_win_v, w_in, ssm_lambda_re, ssm_lambda_im, ssm_log_step, ssm_b_re, ssm_b_im, ssm_c_re, ssm_c_im, ssm_d, ssm_glu_w, ssm_glu_b, ret_gn_g, ret_gn_b, attn_sinks, w_out, ln1_g, ln1_b, ffn_w1, ffn_w3, ffn_w2, ln2_g, ln2_b):
    weights = (w_in, ssm_lambda_re, ssm_lambda_im, ssm_log_step, ssm_b_re, ssm_b_im, ssm_c_re, ssm_c_im, ssm_d,
               ssm_glu_w, ssm_glu_b, ret_gn_g, ret_gn_b, attn_sinks, w_out, ln1_g, ln1_b, ffn_w1, ffn_w3, ffn_w2,
               ln2_g, ln2_b)
    layers = [_prep_layer(l, *weights) for l in range(DEPTH)]

    xp = x_prompt
    p_out = []
    for p in layers:
        xp, *states = _prompt_layer(xp, p)
        p_out.append(states)

    nb = x_sample.shape[0]
    xs = x_sample.reshape(nb, D_MODEL)
    s_out = []
    for l, p in enumerate(layers):
        xs, *states = _sample_layer(xs, p, state_ssm_re[l], state_ssm_im[l], state_ret[l],
                                    cache_win_k[l], cache_win_v[l])
        s_out.append(states)

    stack = lambda outs, i: jnp.stack([o[i] for o in outs])
    return (xp, xs.reshape(nb, 1, D_MODEL),
            stack(p_out, 0), stack(p_out, 1), stack(p_out, 2), stack(p_out, 3), stack(p_out, 4),
            stack(s_out, 0), stack(s_out, 1), stack(s_out, 2), stack(s_out, 3), stack(s_out, 4))
```

```python
import functools
import math

import numpy as np
import jax
import jax.numpy as jnp
from jax import lax
from jax.experimental import pallas as pl
from jax.experimental.pallas import tpu as pltpu

F32 = jnp.float32
BF16 = jnp.bfloat16

D_MODEL = 1024
DEPTH = 2
SSM_GROUPS = 24
SSM_CH = 16
SSM_STATE = 64
SSM_WIDTH = SSM_GROUPS * SSM_CH
SSM_LANES = SSM_GROUPS * SSM_STATE
RET_HEADS = 4
RET_D = 64
RET_WIDTH = RET_HEADS * RET_D
ATT_Q_HEADS = 6
ATT_KV_HEADS = 2
ATT_D = 64
ATT_WIDTH = ATT_Q_HEADS * ATT_D
WINDOW = 128
FFN_HIDDEN = 2816
ALPHA = (2 * DEPTH) ** 0.25
LN_EPS = 1e-5
NEG_INF = -1e30

TC = 128
LANES = 128
N_TILES = SSM_LANES // LANES
HALF = 64
Q_ORDER = (0, 3, 2, 5, 1, 4)
ROW_TILE = 512
VMEM_LIMIT = 56 * 2 ** 20


def _alibi_slopes(n):
    def pow2(k):
        start = 2.0 ** (-8.0 / k)
        return [start ** (i + 1) for i in range(k)]
    if n & (n - 1) == 0:
        return pow2(n)
    c = 2 ** int(math.floor(math.log2(n)))
    return pow2(c) + pow2(2 * c)[0::2][: n - c]


def _cparams(n_axes):
    return pltpu.CompilerParams(dimension_semantics=("arbitrary",) * n_axes, vmem_limit_bytes=VMEM_LIMIT)


def _const_spec(shape):
    nd = len(shape)
    return pl.BlockSpec(shape, lambda *_: (0,) * nd, pipeline_mode=pl.Buffered(1))


def _dot(a, b):
    return jnp.dot(a, b, preferred_element_type=F32)


def _dot_nt(a, b):
    return lax.dot_general(a, b, (((1,), (1,)), ((), ())), preferred_element_type=F32)


def _dot_tn(a, b):
    return lax.dot_general(a, b, (((0,), (0,)), ((), ())), preferred_element_type=F32)


def _layernorm(v, g, b):
    mu = jnp.mean(v, -1, keepdims=True)
    d = v - mu
    var = jnp.mean(d * d, -1, keepdims=True)
    return d * lax.rsqrt(var + LN_EPS) * g + b


def _gelu_tanh(x):
    return 0.5 * x * (1.0 + jnp.tanh(math.sqrt(2.0 / math.pi) * (x + 0.044715 * (x * x * x))))


def _sigmoid(x):
    return 1.0 / (1.0 + jnp.exp(-x))


def _inproj_kernel(x_ref, w_ref, pr_ref, pa_ref, pu_ref):
    xb = x_ref[...].astype(BF16)
    pr_ref[...] = _dot(xb, w_ref[:, 0:1024])
    pa_ref[...] = _dot(xb, w_ref[:, 1024:1664])
    pu_ref[...] = _dot(xb, w_ref[:, 1664:2048])


def _inproj(x2, w):
    t = x2.shape[0]
    tm = min(ROW_TILE, t)
    return pl.pallas_call(
        _inproj_kernel,
        grid=(t // tm,),
        in_specs=[pl.BlockSpec((tm, D_MODEL), lambda i: (i, 0)), _const_spec((D_MODEL, 2048))],
        out_specs=[pl.BlockSpec((tm, 1024), lambda i: (i, 0)),
                   pl.BlockSpec((tm, 640), lambda i: (i, 0)),
                   pl.BlockSpec((tm, SSM_WIDTH), lambda i: (i, 0))],
        out_shape=[jax.ShapeDtypeStruct((t, 1024), F32),
                   jax.ShapeDtypeStruct((t, 640), F32),
                   jax.ShapeDtypeStruct((t, SSM_WIDTH), F32)],
        compiler_params=_cparams(1),
        name="inproj",
    )(x2, w)


def _s5_bu(ub, bre_ref, bim_ref):
    res, ims = [], []
    for i in range(3):
        ui = ub[:, LANES * i:LANES * (i + 1)]
        res.append(_dot(ui, bre_ref[i]))
        ims.append(_dot(ui, bim_ref[i]))
    return res, ims


def _s5_out(u, h_re, h_im, cre_ref, cim_ref, d_ref, gw_ref, gb_ref):
    ys = []
    for i in range(3):
        ys.append(_dot(h_re[i].astype(BF16), cre_ref[i]) + _dot(h_im[i].astype(BF16), cim_ref[i]))
    y = jnp.concatenate(ys, axis=1) + d_ref[...] * u
    z = _gelu_tanh(y)
    gate = _sigmoid(_dot(z.astype(BF16), gw_ref[...]) + gb_ref[...])
    return z * gate


def _s5_prompt_kernel(u_ref, bre_ref, bim_ref, cre_ref, cim_ref, ar_ref, ai_ref, d_ref, gw_ref, gb_ref,
                      z_ref, hlast_ref, h_scr, hc_scr, bt_scr, tm_scr):
    n = pl.program_id(0)
    nb = u_ref.shape[0]
    rows = nb * TC

    @pl.when(n == 0)
    def _():
        hc_scr[...] = jnp.zeros_like(hc_scr)

    u_bt = u_ref[...].reshape(rows, SSM_WIDTH)
    for c in range(3):
        bt_scr[c] = u_bt[:, LANES * c:LANES * (c + 1)]

    def to_time_major(t, carry):
        r0 = pl.multiple_of(t * nb, nb)
        for c in range(3):
            tm_scr[c, pl.ds(r0, nb), :] = bt_scr[c, pl.ds(t, nb, stride=TC), :]
        return carry

    lax.fori_loop(0, TC, to_time_major, 0, unroll=4)
    u = jnp.concatenate([tm_scr[c] for c in range(3)], axis=1)

    res, ims = _s5_bu(u.astype(BF16), bre_ref, bim_ref)
    for i in range(3):
        for k in range(4):
            h_scr[4 * i + k] = res[i][:, LANES * k:LANES * (k + 1)]
            h_scr[N_TILES + 4 * i + k] = ims[i][:, LANES * k:LANES * (k + 1)]

    def step(t, carry):
        r0 = pl.multiple_of(t * nb, nb)
        out = []
        for c in range(N_TILES):
            sl = slice(LANES * c, LANES * (c + 1))
            ar, ai = ar_ref[:, sl], ai_ref[:, sl]
            hr, hi = carry[c], carry[N_TILES + c]
            nr = ar * hr - ai * hi + h_scr[c, pl.ds(r0, nb), :]
            ni = ar * hi + ai * hr + h_scr[N_TILES + c, pl.ds(r0, nb), :]
            h_scr[c, pl.ds(r0, nb), :] = nr
            h_scr[N_TILES + c, pl.ds(r0, nb), :] = ni
            out.append((nr, ni))
        return tuple(o[0] for o in out) + tuple(o[1] for o in out)

    init = tuple(hc_scr[:, LANES * c:LANES * (c + 1)] for c in range(2 * N_TILES))
    fin = lax.fori_loop(0, TC, step, init, unroll=2)
    for c in range(2 * N_TILES):
        hc_scr[:, LANES * c:LANES * (c + 1)] = fin[c]
        hlast_ref[:, LANES * c:LANES * (c + 1)] = fin[c]

    h_re = [jnp.concatenate([h_scr[4 * i + k] for k in range(4)], axis=1) for i in range(3)]
    h_im = [jnp.concatenate([h_scr[N_TILES + 4 * i + k] for k in range(4)], axis=1) for i in range(3)]
    zz = _s5_out(u, h_re, h_im, cre_ref, cim_ref, d_ref, gw_ref, gb_ref)
    for c in range(3):
        tm_scr[c] = zz[:, LANES * c:LANES * (c + 1)]

    def to_batch_major(t, carry):
        r0 = pl.multiple_of(t * nb, nb)
        for c in range(3):
            bt_scr[c, pl.ds(t, nb, stride=TC), :] = tm_scr[c, pl.ds(r0, nb), :]
        return carry

    lax.fori_loop(0, TC, to_batch_major, 0, unroll=4)
    z_bt = jnp.concatenate([bt_scr[c] for c in range(3)], axis=1)
    z_ref[...] = z_bt.astype(BF16).reshape(nb, TC, SSM_WIDTH)


def _s5_weight_specs():
    return [_const_spec((3, LANES, 512)), _const_spec((3, LANES, 512)),
            _const_spec((3, 512, LANES)), _const_spec((3, 512, LANES)),
            _const_spec((1, SSM_LANES)), _const_spec((1, SSM_LANES)),
            _const_spec((1, SSM_WIDTH)), _const_spec((SSM_WIDTH, SSM_WIDTH)), _const_spec((1, SSM_WIDTH))]


def _s5_weights(p):
    return (p["bre"], p["bim"], p["cre"], p["cim"], p["ar"], p["ai"], p["ssm_d"], p["glu_w"], p["glu_b"])


def _s5_prompt(u3, p):
    nb, length, _ = u3.shape
    return pl.pallas_call(
        _s5_prompt_kernel,
        grid=(length // TC,),
        in_specs=[pl.BlockSpec((nb, TC, SSM_WIDTH), lambda n: (0, n, 0))] + _s5_weight_specs(),
        out_specs=[pl.BlockSpec((nb, TC, SSM_WIDTH), lambda n: (0, n, 0)),
                   pl.BlockSpec((nb, 2 * SSM_LANES), lambda n: (0, 0))],
        out_shape=[jax.ShapeDtypeStruct((nb, length, SSM_WIDTH), BF16),
                   jax.ShapeDtypeStruct((nb, 2 * SSM_LANES), F32)],
        scratch_shapes=[pltpu.VMEM((2 * N_TILES, nb * TC, LANES), F32), pltpu.VMEM((nb, 2 * SSM_LANES), F32),
                        pltpu.VMEM((3, nb * TC, LANES), F32), pltpu.VMEM((3, nb * TC, LANES), F32)],
        compiler_params=_cparams(1),
        name="s5_prompt",
    )(u3, *_s5_weights(p))


def _s5_sample_kernel(u_ref, h0r_ref, h0i_ref, bre_ref, bim_ref, cre_ref, cim_ref, ar_ref, ai_ref, d_ref,
                      gw_ref, gb_ref, z_ref, hr_ref, hi_ref):
    u = u_ref[...]
    res, ims = _s5_bu(u.astype(BF16), bre_ref, bim_ref)
    h_re, h_im = [], []
    for i in range(3):
        sl = slice(512 * i, 512 * (i + 1))
        ar, ai = ar_ref[:, sl], ai_ref[:, sl]
        h0r, h0i = h0r_ref[:, sl], h0i_ref[:, sl]
        nr = ar * h0r - ai * h0i + res[i]
        ni = ar * h0i + ai * h0r + ims[i]
        hr_ref[:, sl] = nr
        hi_ref[:, sl] = ni
        h_re.append(nr)
        h_im.append(ni)
    z_ref[...] = _s5_out(u, h_re, h_im, cre_ref, cim_ref, d_ref, gw_ref, gb_ref).astype(BF16)


def _s5_sample(u, h0r, h0i, p):
    nb = u.shape[0]
    full = lambda c: pl.BlockSpec((nb, c), lambda i: (0, 0))
    return pl.pallas_call(
        _s5_sample_kernel,
        grid=(1,),
        in_specs=[full(SSM_WIDTH), full(SSM_LANES), full(SSM_LANES)] + _s5_weight_specs(),
        out_specs=[full(SSM_WIDTH), full(SSM_LANES), full(SSM_LANES)],
        out_shape=[jax.ShapeDtypeStruct((nb, SSM_WIDTH), BF16),
                   jax.ShapeDtypeStruct((nb, SSM_LANES), F32),
                   jax.ShapeDtypeStruct((nb, SSM_LANES), F32)],
        compiler_params=_cparams(1),
        name="s5_sample",
    )(u, h0r, h0i, *_s5_weights(p))


def _ret_gammas():
    return [1.0 - 2.0 ** (-5.0 - h) for h in range(RET_HEADS)]


def _ret_consts():
    log_g = np.log1p(-(2.0 ** (-5.0 - np.arange(RET_HEADS, dtype=np.float64))))
    idx = np.arange(TC)
    diff = idx[:, None] - idx[None, :]
    dmask = np.where(diff >= 0, np.exp(log_g[:, None, None] * np.maximum(diff, 0)), 0.0)
    inner = np.exp(log_g[:, None] * (idx + 1))
    kvd = np.exp(log_g[:, None] * (TC - 1 - idx))
    cd = np.exp(log_g * TC)
    indec = np.zeros((2, TC, LANES))
    kvdec = np.zeros((2, TC, LANES))
    cdec = np.zeros((2, LANES, LANES))
    blk = np.zeros((LANES, LANES))
    for j in range(2):
        for hh in range(2):
            sl = slice(HALF * hh, HALF * (hh + 1))
            indec[j, :, sl] = inner[2 * j + hh][:, None]
            kvdec[j, :, sl] = kvd[2 * j + hh][:, None]
            cdec[j, sl, sl] = cd[2 * j + hh]
            blk[sl, sl] = 1.0
    as32 = lambda a: jnp.asarray(a, F32)
    return as32(dmask), as32(kvdec), as32(indec), as32(cdec), as32(blk)


def _halfnorm(o, lo):
    zero = jnp.zeros_like(o)
    s_lo = jnp.sum(jnp.where(lo, o, zero), -1, keepdims=True)
    s_hi = jnp.sum(jnp.where(lo, zero, o), -1, keepdims=True)
    d = o - jnp.where(lo, s_lo, s_hi) * (1.0 / HALF)
    d2 = d * d
    v_lo = jnp.sum(jnp.where(lo, d2, zero), -1, keepdims=True)
    v_hi = jnp.sum(jnp.where(lo, zero, d2), -1, keepdims=True)
    var = jnp.where(lo, v_lo, v_hi) * (1.0 / HALF)
    return d * lax.rsqrt(var + LN_EPS)


def _ret_prompt_kernel(pr_ref, dmask_ref, kvdec_ref, indec_ref, cdec_ref, blk_ref, gng_ref, gnb_ref,
                       o_ref, slast_ref, s_scr):
    n = pl.program_id(1)

    @pl.when(n == 0)
    def _():
        s_scr[...] = jnp.zeros_like(s_scr)

    lo = lax.broadcasted_iota(jnp.int32, (TC, LANES), 1) < HALF
    zero = jnp.zeros((TC, LANES), F32)
    for j in range(2):
        q2 = pr_ref[0, :, LANES * j:LANES * (j + 1)]
        k2 = pr_ref[0, :, 256 + LANES * j:256 + LANES * (j + 1)]
        v2 = pr_ref[0, :, 512 + LANES * j:512 + LANES * (j + 1)]
        g2 = pr_ref[0, :, 768 + LANES * j:768 + LANES * (j + 1)]
        k2b = k2.astype(BF16)
        v2b = v2.astype(BF16)
        state = s_scr[j]
        halves = []
        for hh in range(2):
            qm = (jnp.where(lo, q2, zero) if hh == 0 else jnp.where(lo, zero, q2)).astype(BF16)
            sc = _dot_nt(qm, k2b) * dmask_ref[2 * j + hh]
            halves.append(_dot(sc.astype(BF16), v2b))
        o = jnp.where(lo, halves[0], halves[1])
        o = o + _dot(q2.astype(BF16), state.astype(BF16)) * indec_ref[j]
        kd = (k2 * kvdec_ref[j]).astype(BF16)
        s_scr[j] = cdec_ref[j] * state + blk_ref[...] * _dot_tn(kd, v2b)
        sl = slice(LANES * j, LANES * (j + 1))
        o = _halfnorm(o, lo) * gng_ref[:, sl] + gnb_ref[:, sl]
        o_ref[0, :, sl] = (g2 * _sigmoid(g2) * o).astype(BF16)
    slast_ref[0] = s_scr[...]


def _ret_prompt(pr3, p):
    nb, length, _ = pr3.shape
    dmask, kvdec, indec, cdec, blk = _ret_consts()
    return pl.pallas_call(
        _ret_prompt_kernel,
        grid=(nb, length // TC),
        in_specs=[pl.BlockSpec((1, TC, 1024), lambda b, n: (b, n, 0)),
                  _const_spec((RET_HEADS, TC, TC)), _const_spec((2, TC, LANES)), _const_spec((2, TC, LANES)),
                  _const_spec((2, LANES, LANES)), _const_spec((LANES, LANES)),
                  _const_spec((1, RET_WIDTH)), _const_spec((1, RET_WIDTH))],
        out_specs=[pl.BlockSpec((1, TC, RET_WIDTH), lambda b, n: (b, n, 0)),
                   pl.BlockSpec((1, 2, LANES, LANES), lambda b, n: (b, 0, 0, 0))],
        out_shape=[jax.ShapeDtypeStruct((nb, length, RET_WIDTH), BF16),
                   jax.ShapeDtypeStruct((nb, 2, LANES, LANES), F32)],
        scratch_shapes=[pltpu.VMEM((2, LANES, LANES), F32)],
        compiler_params=_cparams(2),
        name="ret_prompt",
    )(pr3, dmask, kvdec, indec, cdec, blk, p["gn_g"], p["gn_b"])


def _att_bias():
    slopes = _alibi_slopes(ATT_Q_HEADS)
    qpos = np.arange(TC)[:, None] + TC
    kpos = np.arange(2 * TC)[None, :]
    dist = qpos - kpos
    valid = (dist >= 0) & (dist <= WINDOW)
    bias = np.stack([np.where(valid, -slopes[h] * dist, NEG_INF) for h in Q_ORDER])
    first = np.where(np.arange(2 * TC) < TC, NEG_INF, 0.0)[None, :]
    return jnp.asarray(bias, F32), jnp.asarray(first, F32)


def _att_prompt_kernel(sink_ref, pa_ref, bias_ref, first_ref, o_ref, kout_ref, vout_ref, kcat, vaug, bias_scr):
    n = pl.program_id(1)

    @pl.when(n == 0)
    def _():
        kcat[0:TC, :] = jnp.zeros((TC, LANES), BF16)
        vaug[0:TC, 0:LANES] = jnp.zeros((TC, LANES), BF16)
        vaug[:, LANES:2 * LANES] = jnp.ones((2 * TC, LANES), BF16)
        bias_scr[...] = bias_ref[...] + first_ref[...]

    @pl.when(n == 1)
    def _():
        bias_scr[...] = bias_ref[...]

    kcur = pa_ref[0, :, 384:512]
    vcur = pa_ref[0, :, 512:640]
    kout_ref[0] = kcur
    vout_ref[0] = vcur
    kcat[TC:2 * TC, :] = kcur.astype(BF16)
    vaug[TC:2 * TC, 0:LANES] = vcur.astype(BF16)
    kc = kcat[...]
    va = vaug[...]

    lo = lax.broadcasted_iota(jnp.int32, (TC, LANES), 1) < HALF
    zero = jnp.zeros((TC, LANES), F32)
    for c in range(3):
        qcol = pa_ref[0, :, LANES * c:LANES * (c + 1)]
        halves = []
        for hh in range(2):
            pos = 2 * c + hh
            sink = sink_ref[pos]
            qm = (jnp.where(lo, qcol, zero) if hh == 0 else jnp.where(lo, zero, qcol)).astype(BF16)
            s = _dot_nt(qm, kc) + bias_scr[pos]
            m = jnp.maximum(jnp.max(s, -1, keepdims=True), sink)
            e = jnp.exp(s - m)
            r = _dot(e.astype(BF16), va)
            den = r[:, LANES:2 * LANES] + jnp.exp(sink - m)
            halves.append(r[:, 0:LANES] / den)
        o_ref[0, :, LANES * c:LANES * (c + 1)] = jnp.where(lo, halves[0], halves[1]).astype(BF16)

    kcat[0:TC, :] = kcat[TC:2 * TC, :]
    vaug[0:TC, 0:LANES] = vaug[TC:2 * TC, 0:LANES]


def _att_prompt(pa3, p):
    nb, length, _ = pa3.shape
    bias, first = _att_bias()
    return pl.pallas_call(
        _att_prompt_kernel,
        grid=(nb, length // TC),
        in_specs=[pl.BlockSpec(memory_space=pltpu.SMEM),
                  pl.BlockSpec((1, TC, 640), lambda b, n: (b, n, 0)),
                  _const_spec((ATT_Q_HEADS, TC, 2 * TC)), _const_spec((1, 2 * TC))],
        out_specs=[pl.BlockSpec((1, TC, ATT_WIDTH), lambda b, n: (b, n, 0)),
                   pl.BlockSpec((1, TC, LANES), lambda b, n: (b, 0, 0)),
                   pl.BlockSpec((1, TC, LANES), lambda b, n: (b, 0, 0))],
        out_shape=[jax.ShapeDtypeStruct((nb, length, ATT_WIDTH), BF16),
                   jax.ShapeDtypeStruct((nb, TC, LANES), F32),
                   jax.ShapeDtypeStruct((nb, TC, LANES), F32)],
        scratch_shapes=[pltpu.VMEM((2 * TC, LANES), BF16), pltpu.VMEM((2 * TC, 2 * LANES), BF16),
                        pltpu.VMEM((ATT_Q_HEADS, TC, 2 * TC), F32)],
        compiler_params=_cparams(2),
        name="att_prompt",
    )(p["sinks"], pa3, bias, first)


SAMPLE_ROWS = 16


def _mix_sample_kernel(qt_ref, kt_ref, v4_ref, g4_ref, s_ref, qb_ref, kn_ref, vn_ref, kc_ref, vc_ref,
                       gng_ref, gnb_ref, bias_ref, sink_ref,
                       or_ref, oa_ref, sn_ref, kout_ref, vout_ref):
    gammas = _ret_gammas()
    lo = lax.broadcasted_iota(jnp.int32, (4, LANES), 1) < HALF
    sink = sink_ref[:, 0:1]
    bias = bias_ref[...]

    def body(i, carry):
        qt = qt_ref[i]
        kt = kt_ref[i]
        v4 = v4_ref[i]
        g4 = g4_ref[i]
        rows = []
        for h in range(RET_HEADS):
            sl = slice(RET_D * h, RET_D * (h + 1))
            sn = gammas[h] * s_ref[i, sl, :] + kt[:, h:h + 1] * v4[h:h + 1, :]
            sn_ref[i, sl, :] = sn
            rows.append(jnp.sum(qt[:, h:h + 1] * sn, axis=0, keepdims=True))
        o4 = jnp.concatenate(rows, axis=0)
        mu = jnp.mean(o4, -1, keepdims=True)
        d = o4 - mu
        var = jnp.mean(d * d, -1, keepdims=True)
        o4 = d * lax.rsqrt(var + LN_EPS) * gng_ref[...] + gnb_ref[...]
        or_ref[i] = g4 * _sigmoid(g4) * o4

        kb = kc_ref[i]
        vb = vc_ref[i]
        q8 = qb_ref[i]
        knew = kn_ref[i]
        vnew = vn_ref[i]
        s = _dot_nt(q8.astype(BF16), kb.astype(BF16)) + bias
        s_self = jnp.sum(q8 * knew, -1, keepdims=True)
        m = jnp.maximum(jnp.maximum(jnp.max(s, -1, keepdims=True), s_self), sink)
        e = jnp.exp(s - m)
        e_self = jnp.exp(s_self - m)
        den = jnp.sum(e, -1, keepdims=True) + e_self + jnp.exp(sink - m)
        o8 = (_dot(e.astype(BF16), vb.astype(BF16)) + e_self * vnew) / den
        oa_ref[i] = jnp.where(lo, o8[0:4], o8[4:8])

        kout_ref[i, 0:WINDOW - 1, :] = kc_ref[i, 1:WINDOW, :]
        kout_ref[i, WINDOW - 1:WINDOW, :] = knew
        vout_ref[i, 0:WINDOW - 1, :] = vc_ref[i, 1:WINDOW, :]
        vout_ref[i, WINDOW - 1:WINDOW, :] = vnew
        return carry

    lax.fori_loop(0, SAMPLE_ROWS, body, 0)


def _mix_sample(qt, kt, v4, g4, s, qb, kn, vn, kc, vc, gng4, gnb4, bias8, sink8):
    nb = qt.shape[0]
    bb = SAMPLE_ROWS
    blk = lambda *dims: pl.BlockSpec((bb,) + dims, lambda i: (i,) + (0,) * len(dims))
    return pl.pallas_call(
        _mix_sample_kernel,
        grid=(nb // bb,),
        in_specs=[blk(RET_D, RET_HEADS), blk(RET_D, RET_HEADS), blk(RET_HEADS, RET_D), blk(RET_HEADS, RET_D),
                  blk(RET_WIDTH, RET_D), blk(8, LANES), blk(1, LANES), blk(1, LANES),
                  blk(WINDOW, LANES), blk(WINDOW, LANES),
                  _const_spec((RET_HEADS, RET_D)), _const_spec((RET_HEADS, RET_D)),
                  _const_spec((8, LANES)), _const_spec((8, LANES))],
        out_specs=[blk(RET_HEADS, RET_D), blk(4, LANES), blk(RET_WIDTH, RET_D), blk(WINDOW, LANES), blk(WINDOW, LANES)],
        out_shape=[jax.ShapeDtypeStruct((nb, RET_HEADS, RET_D), F32),
                   jax.ShapeDtypeStruct((nb, 4, LANES), F32),
                   jax.ShapeDtypeStruct((nb, RET_WIDTH, RET_D), F32),
                   jax.ShapeDtypeStruct((nb, WINDOW, LANES), F32),
                   jax.ShapeDtypeStruct((nb, WINDOW, LANES), F32)],
        compiler_params=_cparams(1),
        name="mix_sample",
    )(qt, kt, v4, g4, s, qb, kn, vn, kc, vc, gng4, gnb4, bias8, sink8)


def _outproj_kernel(z_ref, r_ref, a_ref, x_ref, wz_ref, wr_ref, wa_ref, g_ref, b_ref, o_ref):
    mix = _dot(z_ref[...], wz_ref[...]) + _dot(r_ref[...], wr_ref[...]) + _dot(a_ref[...], wa_ref[...])
    o_ref[...] = _layernorm(ALPHA * x_ref[...] + mix, g_ref[...], b_ref[...])


def _outproj(z, r, a, x2, p):
    t = x2.shape[0]
    tm = min(ROW_TILE, t)
    row = lambda c: pl.BlockSpec((tm, c), lambda i: (i, 0))
    return pl.pallas_call(
        _outproj_kernel,
        grid=(t // tm,),
        in_specs=[row(SSM_WIDTH), row(RET_WIDTH), row(ATT_WIDTH), row(D_MODEL),
                  _const_spec((SSM_WIDTH, D_MODEL)), _const_spec((RET_WIDTH, D_MODEL)),
                  _const_spec((ATT_WIDTH, D_MODEL)), _const_spec((1, D_MODEL)), _const_spec((1, D_MODEL))],
        out_specs=row(D_MODEL),
        out_shape=jax.ShapeDtypeStruct((t, D_MODEL), F32),
        compiler_params=_cparams(1),
        name="outproj",
    )(z, r, a, x2, p["wo_z"], p["wo_r"], p["wo_a"], p["ln1_g"], p["ln1_b"])


def _ffn_kernel(x_ref, w1_ref, w3_ref, w2_ref, g_ref, b_ref, o_ref):
    x = x_ref[...]
    xb = x.astype(BF16)
    h1 = _dot(xb, w1_ref[...])
    h3 = _dot(xb, w3_ref[...])
    hid = (h1 * _sigmoid(h1) * h3).astype(BF16)
    o_ref[...] = _layernorm(ALPHA * x + _dot(hid, w2_ref[...]), g_ref[...], b_ref[...])


def _ffn(x2, p):
    t = x2.shape[0]
    tm = min(ROW_TILE, t)
    row = pl.BlockSpec((tm, D_MODEL), lambda i: (i, 0))
    return pl.pallas_call(
        _ffn_kernel,
        grid=(t // tm,),
        in_specs=[row, _const_spec((D_MODEL, FFN_HIDDEN)), _const_spec((D_MODEL, FFN_HIDDEN)),
                  _const_spec((FFN_HIDDEN, D_MODEL)), _const_spec((1, D_MODEL)), _const_spec((1, D_MODEL))],
        out_specs=row,
        out_shape=jax.ShapeDtypeStruct((t, D_MODEL), F32),
        compiler_params=_cparams(1),
        name="ffn",
    )(x2, p["w1"], p["w3"], p["w2"], p["ln2_g"], p["ln2_b"])


def _prep_layer(l, w_in, lam_re, lam_im, log_step, b_re, b_im, c_re, c_im, ssm_d, glu_w, glu_b,
                gn_g, gn_b, sinks, w_out, ln1_g, ln1_b, w1, w3, w2, ln2_g, ln2_b):
    w = w_in[l]
    o = 0
    parts = []
    for size in (SSM_WIDTH, RET_WIDTH, RET_WIDTH, RET_WIDTH, RET_WIDTH, ATT_WIDTH, LANES, LANES):
        parts.append(w[:, o:o + size])
        o += size
    w_u, w_rq, w_rk, w_rv, w_rg, w_aq, w_ak, w_av = parts
    order = jnp.asarray(Q_ORDER)
    w_aq = w_aq.reshape(D_MODEL, ATT_Q_HEADS, ATT_D)[:, order, :].reshape(D_MODEL, ATT_WIDTH)
    w_all = jnp.concatenate([w_rq, w_rk * (RET_D ** -0.5), w_rv, w_rg, w_aq * (ATT_D ** -0.5), w_ak, w_av, w_u], axis=1)

    lr, li = lam_re[l], lam_im[l]
    step = jnp.exp(log_step[l])[:, None]
    mag = jnp.exp(lr * step)
    a_re, a_im = mag * jnp.cos(li * step), mag * jnp.sin(li * step)
    den = lr * lr + li * li
    k_re = ((a_re - 1.0) * lr + a_im * li) / den
    k_im = (a_im * lr - (a_re - 1.0) * li) / den
    bb_re = k_re[..., None] * b_re[l] - k_im[..., None] * b_im[l]
    bb_im = k_re[..., None] * b_im[l] + k_im[..., None] * b_re[l]
    eye8 = jnp.eye(8, dtype=F32)

    def b_blocks(m):
        return jnp.einsum("igph,gk->ighkp", m.reshape(3, 8, SSM_STATE, SSM_CH), eye8).reshape(3, LANES, 512)

    def c_blocks(m):
        return jnp.einsum("ighp,gk->ikpgh", m.reshape(3, 8, SSM_CH, SSM_STATE), eye8).reshape(3, 512, LANES)

    wo = w_out[l]
    wo_a = wo[SSM_WIDTH + RET_WIDTH:].reshape(ATT_Q_HEADS, ATT_D, D_MODEL)[order].reshape(ATT_WIDTH, D_MODEL)
    sink_pos = sinks[l][order]
    sink8 = jnp.zeros((8,), F32).at[0:3].set(sink_pos[0::2]).at[4:7].set(sink_pos[1::2])
    row = lambda a: a[l].reshape(1, -1)
    return dict(
        w_in=w_all.astype(BF16),
        bre=b_blocks(bb_re).astype(BF16), bim=b_blocks(bb_im).astype(BF16),
        cre=c_blocks(c_re[l]).astype(BF16), cim=c_blocks(-c_im[l]).astype(BF16),
        ar=a_re.reshape(1, SSM_LANES), ai=a_im.reshape(1, SSM_LANES),
        ssm_d=row(ssm_d), glu_w=glu_w[l].astype(BF16), glu_b=row(glu_b),
        gn_g=row(gn_g), gn_b=row(gn_b),
        gn_g4=gn_g[l].reshape(RET_HEADS, RET_D), gn_b4=gn_b[l].reshape(RET_HEADS, RET_D),
        sinks=sink_pos, sink8=jnp.broadcast_to(sink8[:, None], (8, LANES)),
        wo_z=wo[:SSM_WIDTH].astype(BF16), wo_r=wo[SSM_WIDTH:SSM_WIDTH + RET_WIDTH].astype(BF16),
        wo_a=wo_a.astype(BF16),
        ln1_g=row(ln1_g), ln1_b=row(ln1_b),
        w1=w1[l].astype(BF16), w3=w3[l].astype(BF16), w2=w2[l].astype(BF16),
        ln2_g=row(ln2_g), ln2_b=row(ln2_b),
    )


def _prompt_layer(x3, p):
    nb, length, _ = x3.shape
    t = nb * length
    x2 = x3.reshape(t, D_MODEL)
    pr, pa, pu = _inproj(x2, p["w_in"])
    z, hlast = _s5_prompt(pu.reshape(nb, length, SSM_WIDTH), p)
    o_r, sblk = _ret_prompt(pr.reshape(nb, length, 1024), p)
    o_a, klast, vlast = _att_prompt(pa.reshape(nb, length, 640), p)
    x1 = _outproj(z.reshape(t, SSM_WIDTH), o_r.reshape(t, RET_WIDTH), o_a.reshape(t, ATT_WIDTH), x2, p)
    y = _ffn(x1, p).reshape(nb, length, D_MODEL)
    h_re = hlast[:, :SSM_LANES].reshape(nb, SSM_GROUPS, SSM_STATE)
    h_im = hlast[:, SSM_LANES:].reshape(nb, SSM_GROUPS, SSM_STATE)
    s4 = jnp.stack([sblk[:, h // 2, HALF * (h % 2):HALF * (h % 2 + 1), HALF * (h % 2):HALF * (h % 2 + 1)]
                    for h in range(RET_HEADS)], axis=1)
    kv_shape = (nb, TC, ATT_KV_HEADS, ATT_D)
    return y, h_re, h_im, s4, klast.reshape(kv_shape), vlast.reshape(kv_shape)


def _sample_bias8():
    slopes = _alibi_slopes(ATT_Q_HEADS)
    dist = WINDOW - np.arange(WINDOW)
    bias = np.zeros((8, WINDOW))
    for c in range(3):
        bias[c] = -slopes[Q_ORDER[2 * c]] * dist
        bias[4 + c] = -slopes[Q_ORDER[2 * c + 1]] * dist
    return jnp.asarray(bias, F32)


def _sample_layer(x2, p, h0r, h0i, s0, kc, vc):
    nb = x2.shape[0]
    pr, pa, pu = _inproj(x2, p["w_in"])
    z, h_re, h_im = _s5_sample(pu, h0r.reshape(nb, SSM_LANES), h0i.reshape(nb, SSM_LANES), p)
    q4 = pr[:, 0:256].reshape(nb, RET_HEADS, RET_D)
    k4 = pr[:, 256:512].reshape(nb, RET_HEADS, RET_D)
    v4 = pr[:, 512:768].reshape(nb, RET_HEADS, RET_D)
    g4 = pr[:, 768:1024].reshape(nb, RET_HEADS, RET_D)
    qcols = pa[:, 0:ATT_WIDTH].reshape(nb, 3, LANES)
    low = (jnp.arange(LANES) < HALF)[None, None, :]
    zero1 = jnp.zeros((nb, 1, LANES), F32)
    qb = jnp.concatenate([jnp.where(low, qcols, 0.0), zero1, jnp.where(low, 0.0, qcols), zero1], axis=1)
    kn = pa[:, 384:512].reshape(nb, 1, LANES)
    vn = pa[:, 512:640].reshape(nb, 1, LANES)
    o_r, o_a, s_new, k_out, v_out = _mix_sample(
        q4.transpose(0, 2, 1), k4.transpose(0, 2, 1), v4, g4, s0.reshape(nb, RET_WIDTH, RET_D),
        qb, kn, vn, kc.reshape(nb, WINDOW, LANES), vc.reshape(nb, WINDOW, LANES),
        p["gn_g4"], p["gn_b4"], _sample_bias8(), p["sink8"])
    o_r = o_r.reshape(nb, RET_WIDTH).astype(BF16)
    o_a = o_a[:, 0:3].reshape(nb, ATT_WIDTH).astype(BF16)
    x1 = _outproj(z, o_r, o_a, x2, p)
    y = _ffn(x1, p)
    kv_shape = (nb, WINDOW, ATT_KV_HEADS, ATT_D)
    return (y, h_re.reshape(nb, SSM_GROUPS, SSM_STATE), h_im.reshape(nb, SSM_GROUPS, SSM_STATE),
            s_new.reshape(nb, RET_HEADS, RET_D, RET_D), k_out.reshape(kv_shape), v_out.reshape(kv_shape))


def kernel(x_prompt, x_sample, state_ssm_re, state_ssm_im, state_ret, cache_win_k, cache_win_v, w_in, ssm_lambda_re, ssm_lambda_im, ssm_log_step, ssm_b_re, ssm_b_im, ssm_c_re, ssm_c_im, ssm_d, ssm_glu_w, ssm_glu_b, ret_gn_g, ret_gn_b, attn_sinks, w_out, ln1_g, ln1_b, ffn_w1, ffn_w3, ffn_w2, ln2_g, ln2_b):
    weights = (w_in, ssm_lambda_re, ssm_lambda_im, ssm_log_step, ssm_b_re, ssm_b_im, ssm_c_re, ssm_c_im, ssm_d,
               ssm_glu_w, ssm_glu_b, ret_gn_g, ret_gn_b, attn_sinks, w_out, ln1_g, ln1_b, ffn_w1, ffn_w3, ffn_w2,
               ln2_g, ln2_b)
    layers = [_prep_layer(l, *weights) for l in range(DEPTH)]

    xp = x_prompt
    p_out = []
    for p in layers:
        xp, *states = _prompt_layer(xp, p)
        p_out.append(states)

    nb = x_sample.shape[0]
    xs = x_sample.reshape(nb, D_MODEL)
    s_out = []
    for l, p in enumerate(layers):
        xs, *states = _sample_layer(xs, p, state_ssm_re[l], state_ssm_im[l], state_ret[l],
                                    cache_win_k[l], cache_win_v[l])
        s_out.append(states)

    stack = lambda outs, i: jnp.stack([o[i] for o in outs])
    return (xp, xs.reshape(nb, 1, D_MODEL),
            stack(p_out, 0), stack(p_out, 1), stack(p_out, 2), stack(p_out, 3), stack(p_out, 4),
            stack(s_out, 0), stack(s_out, 1), stack(s_out, 2), stack(s_out, 3), stack(s_out, 4))
```

```python
import functools
import math

import numpy as np
import jax
import jax.numpy as jnp
from jax import lax
from jax.experimental import pallas as pl
from jax.experimental.pallas import tpu as pltpu

F32 = jnp.float32
BF16 = jnp.bfloat16

D_MODEL = 1024
DEPTH = 2
SSM_GROUPS = 24
SSM_CH = 16
SSM_STATE = 64
SSM_WIDTH = SSM_GROUPS * SSM_CH
SSM_LANES = SSM_GROUPS * SSM_STATE
RET_HEADS = 4
RET_D = 64
RET_WIDTH = RET_HEADS * RET_D
ATT_Q_HEADS = 6
ATT_KV_HEADS = 2
ATT_GROUP = ATT_Q_HEADS // ATT_KV_HEADS
ATT_D = 64
ATT_WIDTH = ATT_Q_HEADS * ATT_D
WINDOW = 128
FFN_HIDDEN = 2816
IN_WIDTH = 2048
ALPHA = (2 * DEPTH) ** 0.25
LN_EPS = 1e-5
NEG_INF = -1e30

COL_RET = SSM_WIDTH
COL_ATT = SSM_WIDTH + 4 * RET_WIDTH
RET_COLS = 4 * RET_WIDTH
ATT_COLS = ATT_WIDTH + 2 * ATT_KV_HEADS * ATT_D

TC = 128
LANES = 128
N_TILES = SSM_LANES // LANES
HALF = 64
ROW_TILE = 512
SAMPLE_ROWS = 16
VMEM_LIMIT = 56 * 2 ** 20


def _alibi_slopes(n):
    def pow2(k):
        start = 2.0 ** (-8.0 / k)
        return [start ** (i + 1) for i in range(k)]
    if n & (n - 1) == 0:
        return pow2(n)
    c = 2 ** int(math.floor(math.log2(n)))
    return pow2(c) + pow2(2 * c)[0::2][: n - c]


def _cparams(n_axes):
    return pltpu.CompilerParams(dimension_semantics=("arbitrary",) * n_axes, vmem_limit_bytes=VMEM_LIMIT)


def _const_spec(shape):
    nd = len(shape)
    return pl.BlockSpec(shape, lambda *_: (0,) * nd, pipeline_mode=pl.Buffered(1))


def _layer_spec(shape, layer):
    nd = len(shape)
    return pl.BlockSpec((None,) + tuple(shape), lambda *_: (layer,) + (0,) * nd, pipeline_mode=pl.Buffered(1))


def _dot(a, b):
    return jnp.dot(a, b, preferred_element_type=F32)


def _dot_nt(a, b):
    return lax.dot_general(a, b, (((1,), (1,)), ((), ())), preferred_element_type=F32)


def _dot_tn(a, b):
    return lax.dot_general(a, b, (((0,), (0,)), ((), ())), preferred_element_type=F32)


def _layernorm(v, g, b):
    mu = jnp.mean(v, -1, keepdims=True)
    d = v - mu
    var = jnp.mean(d * d, -1, keepdims=True)
    return d * lax.rsqrt(var + LN_EPS) * g + b


def _gelu_tanh(x):
    return 0.5 * x * (1.0 + jnp.tanh(math.sqrt(2.0 / math.pi) * (x + 0.044715 * (x * x * x))))


def _sigmoid(x):
    return 1.0 / (1.0 + jnp.exp(-x))


def _inproj_kernel(x_ref, w_ref, pu_ref, pr_ref, pa_ref):
    xb = x_ref[...].astype(BF16)
    pu_ref[...] = _dot(xb, w_ref[:, 0:COL_RET])
    pr_ref[...] = _dot(xb, w_ref[:, COL_RET:COL_ATT])
    pa_ref[...] = _dot(xb, w_ref[:, COL_ATT:IN_WIDTH])


def _inproj(x2, p, layer):
    t = x2.shape[0]
    tm = min(ROW_TILE, t)
    row = lambda c: pl.BlockSpec((tm, c), lambda i: (i, 0))
    return pl.pallas_call(
        _inproj_kernel,
        grid=(t // tm,),
        in_specs=[row(D_MODEL), _layer_spec((D_MODEL, IN_WIDTH), layer)],
        out_specs=[row(SSM_WIDTH), row(RET_COLS), row(ATT_COLS)],
        out_shape=[jax.ShapeDtypeStruct((t, SSM_WIDTH), F32),
                   jax.ShapeDtypeStruct((t, RET_COLS), F32),
                   jax.ShapeDtypeStruct((t, ATT_COLS), F32)],
        compiler_params=_cparams(1),
        name="inproj",
    )(x2, p["w_in"])


def _s5_bu(ub, bre_ref, bim_ref):
    res, ims = [], []
    for i in range(3):
        ui = ub[:, LANES * i:LANES * (i + 1)]
        res.append(_dot(ui, bre_ref[i]))
        ims.append(_dot(ui, bim_ref[i]))
    return res, ims


def _s5_out(u, h_re, h_im, cre_ref, cim_ref, d_ref, gw_ref, gb_ref):
    ys = []
    for i in range(3):
        ys.append(_dot(h_re[i].astype(BF16), cre_ref[i]) + _dot(h_im[i].astype(BF16), cim_ref[i]))
    y = jnp.concatenate(ys, axis=1) + d_ref[...] * u
    z = _gelu_tanh(y)
    gate = _sigmoid(_dot(z.astype(BF16), gw_ref[...]) + gb_ref[...])
    return z * gate


def _s5_prompt_kernel(u_ref, bre_ref, bim_ref, cre_ref, cim_ref, ar_ref, ai_ref, d_ref, gw_ref, gb_ref,
                      z_ref, hlast_ref, h_scr, hc_scr, bt_scr, tm_scr):
    n = pl.program_id(0)
    nb = u_ref.shape[0]
    rows = nb * TC

    @pl.when(n == 0)
    def _():
        hc_scr[...] = jnp.zeros_like(hc_scr)

    u_bt = u_ref[...].reshape(rows, SSM_WIDTH)
    for c in range(3):
        bt_scr[c] = u_bt[:, LANES * c:LANES * (c + 1)]

    def to_time_major(t, carry):
        r0 = pl.multiple_of(t * nb, nb)
        for c in range(3):
            tm_scr[c, pl.ds(r0, nb), :] = bt_scr[c, pl.ds(t, nb, stride=TC), :]
        return carry

    lax.fori_loop(0, TC, to_time_major, 0, unroll=4)
    u = jnp.concatenate([tm_scr[c] for c in range(3)], axis=1)

    res, ims = _s5_bu(u.astype(BF16), bre_ref, bim_ref)
    for i in range(3):
        for k in range(4):
            h_scr[4 * i + k] = res[i][:, LANES * k:LANES * (k + 1)]
            h_scr[N_TILES + 4 * i + k] = ims[i][:, LANES * k:LANES * (k + 1)]

    def step(t, carry):
        r0 = pl.multiple_of(t * nb, nb)
        out = []
        for c in range(N_TILES):
            sl = slice(LANES * c, LANES * (c + 1))
            ar, ai = ar_ref[:, sl], ai_ref[:, sl]
            hr, hi = carry[c], carry[N_TILES + c]
            nr = ar * hr - ai * hi + h_scr[c, pl.ds(r0, nb), :]
            ni = ar * hi + ai * hr + h_scr[N_TILES + c, pl.ds(r0, nb), :]
            h_scr[c, pl.ds(r0, nb), :] = nr
            h_scr[N_TILES + c, pl.ds(r0, nb), :] = ni
            out.append((nr, ni))
        return tuple(o[0] for o in out) + tuple(o[1] for o in out)

    init = tuple(hc_scr[:, LANES * c:LANES * (c + 1)] for c in range(2 * N_TILES))
    fin = lax.fori_loop(0, TC, step, init, unroll=2)
    for c in range(2 * N_TILES):
        hc_scr[:, LANES * c:LANES * (c + 1)] = fin[c]
        hlast_ref[:, LANES * c:LANES * (c + 1)] = fin[c]

    h_re = [jnp.concatenate([h_scr[4 * i + k] for k in range(4)], axis=1) for i in range(3)]
    h_im = [jnp.concatenate([h_scr[N_TILES + 4 * i + k] for k in range(4)], axis=1) for i in range(3)]
    zz = _s5_out(u, h_re, h_im, cre_ref, cim_ref, d_ref, gw_ref, gb_ref)
    for c in range(3):
        tm_scr[c] = zz[:, LANES * c:LANES * (c + 1)]

    def to_batch_major(t, carry):
        r0 = pl.multiple_of(t * nb, nb)
        for c in range(3):
            bt_scr[c, pl.ds(t, nb, stride=TC), :] = tm_scr[c, pl.ds(r0, nb), :]
        return carry

    lax.fori_loop(0, TC, to_batch_major, 0, unroll=4)
    z_bt = jnp.concatenate([bt_scr[c] for c in range(3)], axis=1)
    z_ref[...] = z_bt.astype(BF16).reshape(nb, TC, SSM_WIDTH)


def _s5_weight_specs(layer):
    ls = lambda *shape: _layer_spec(shape, layer)
    return [ls(3, LANES, 512), ls(3, LANES, 512), ls(3, 512, LANES), ls(3, 512, LANES),
            ls(1, SSM_LANES), ls(1, SSM_LANES), ls(1, SSM_WIDTH), ls(SSM_WIDTH, SSM_WIDTH), ls(1, SSM_WIDTH)]


def _s5_weights(p):
    return (p["bre"], p["bim"], p["cre"], p["cim"], p["ar"], p["ai"], p["ssm_d"], p["glu_w"], p["glu_b"])


def _s5_prompt(u3, p, layer):
    nb, length, _ = u3.shape
    return pl.pallas_call(
        _s5_prompt_kernel,
        grid=(length // TC,),
        in_specs=[pl.BlockSpec((nb, TC, SSM_WIDTH), lambda n: (0, n, 0))] + _s5_weight_specs(layer),
        out_specs=[pl.BlockSpec((nb, TC, SSM_WIDTH), lambda n: (0, n, 0)),
                   pl.BlockSpec((nb, 2 * SSM_LANES), lambda n: (0, 0))],
        out_shape=[jax.ShapeDtypeStruct((nb, length, SSM_WIDTH), BF16),
                   jax.ShapeDtypeStruct((nb, 2 * SSM_LANES), F32)],
        scratch_shapes=[pltpu.VMEM((2 * N_TILES, nb * TC, LANES), F32), pltpu.VMEM((nb, 2 * SSM_LANES), F32),
                        pltpu.VMEM((3, nb * TC, LANES), F32), pltpu.VMEM((3, nb * TC, LANES), F32)],
        compiler_params=_cparams(1),
        name="s5_prompt",
    )(u3, *_s5_weights(p))


def _s5_sample_kernel(u_ref, h0r_ref, h0i_ref, bre_ref, bim_ref, cre_ref, cim_ref, ar_ref, ai_ref, d_ref,
                      gw_ref, gb_ref, z_ref, hr_ref, hi_ref):
    u = u_ref[...]
    res, ims = _s5_bu(u.astype(BF16), bre_ref, bim_ref)
    h_re, h_im = [], []
    for i in range(3):
        sl = slice(512 * i, 512 * (i + 1))
        ar, ai = ar_ref[:, sl], ai_ref[:, sl]
        h0r, h0i = h0r_ref[:, sl], h0i_ref[:, sl]
        nr = ar * h0r - ai * h0i + res[i]
        ni = ar * h0i + ai * h0r + ims[i]
        hr_ref[:, sl] = nr
        hi_ref[:, sl] = ni
        h_re.append(nr)
        h_im.append(ni)
    z_ref[...] = _s5_out(u, h_re, h_im, cre_ref, cim_ref, d_ref, gw_ref, gb_ref).astype(BF16)


def _s5_sample(u, h0r, h0i, p, layer):
    nb = u.shape[0]
    full = lambda c: pl.BlockSpec((nb, c), lambda i: (0, 0))
    state = pl.BlockSpec((None, nb, SSM_LANES), lambda i: (layer, 0, 0))
    return pl.pallas_call(
        _s5_sample_kernel,
        grid=(1,),
        in_specs=[full(SSM_WIDTH), state, state] + _s5_weight_specs(layer),
        out_specs=[full(SSM_WIDTH), full(SSM_LANES), full(SSM_LANES)],
        out_shape=[jax.ShapeDtypeStruct((nb, SSM_WIDTH), BF16),
                   jax.ShapeDtypeStruct((nb, SSM_LANES), F32),
                   jax.ShapeDtypeStruct((nb, SSM_LANES), F32)],
        compiler_params=_cparams(1),
        name="s5_sample",
    )(u, h0r, h0i, *_s5_weights(p))


def _ret_gammas():
    return [1.0 - 2.0 ** (-5.0 - h) for h in range(RET_HEADS)]


def _ret_consts():
    log_g = np.log1p(-(2.0 ** (-5.0 - np.arange(RET_HEADS, dtype=np.float64))))
    idx = np.arange(TC)
    diff = idx[:, None] - idx[None, :]
    dmask = np.where(diff >= 0, np.exp(log_g[:, None, None] * np.maximum(diff, 0)), 0.0)
    inner = np.exp(log_g[:, None] * (idx + 1))
    kvd = np.exp(log_g[:, None] * (TC - 1 - idx))
    cd = np.exp(log_g * TC)
    indec = np.zeros((2, TC, LANES))
    kvdec = np.zeros((2, TC, LANES))
    cdec = np.zeros((2, LANES, LANES))
    blk = np.zeros((LANES, LANES))
    for j in range(2):
        for hh in range(2):
            sl = slice(HALF * hh, HALF * (hh + 1))
            indec[j, :, sl] = inner[2 * j + hh][:, None]
            kvdec[j, :, sl] = kvd[2 * j + hh][:, None]
            cdec[j, sl, sl] = cd[2 * j + hh]
            blk[sl, sl] = 1.0
    as32 = lambda a: jnp.asarray(a, F32)
    return as32(dmask), as32(kvdec), as32(indec), as32(cdec), as32(blk)


def _halfnorm(o, lo):
    zero = jnp.zeros_like(o)
    s_lo = jnp.sum(jnp.where(lo, o, zero), -1, keepdims=True)
    s_hi = jnp.sum(jnp.where(lo, zero, o), -1, keepdims=True)
    d = o - jnp.where(lo, s_lo, s_hi) * (1.0 / HALF)
    d2 = d * d
    v_lo = jnp.sum(jnp.where(lo, d2, zero), -1, keepdims=True)
    v_hi = jnp.sum(jnp.where(lo, zero, d2), -1, keepdims=True)
    var = jnp.where(lo, v_lo, v_hi) * (1.0 / HALF)
    return d * lax.rsqrt(var + LN_EPS)


def _ret_prompt_kernel(pr_ref, dmask_ref, kvdec_ref, indec_ref, cdec_ref, blk_ref, gng_ref, gnb_ref,
                       o_ref, slast_ref, s_scr):
    n = pl.program_id(0)
    nb = pr_ref.shape[0]

    @pl.when(n == 0)
    def _():
        s_scr[...] = jnp.zeros_like(s_scr)

    lo = lax.broadcasted_iota(jnp.int32, (TC, LANES), 1) < HALF
    zero = jnp.zeros((TC, LANES), F32)
    for b in range(nb):
        for j in range(2):
            q2 = pr_ref[b, :, LANES * j:LANES * (j + 1)]
            k2 = pr_ref[b, :, 256 + LANES * j:256 + LANES * (j + 1)]
            v2 = pr_ref[b, :, 512 + LANES * j:512 + LANES * (j + 1)]
            g2 = pr_ref[b, :, 768 + LANES * j:768 + LANES * (j + 1)]
            k2b = k2.astype(BF16)
            v2b = v2.astype(BF16)
            state = s_scr[b, j]
            halves = []
            for hh in range(2):
                qm = (jnp.where(lo, q2, zero) if hh == 0 else jnp.where(lo, zero, q2)).astype(BF16)
                sc = _dot_nt(qm, k2b) * dmask_ref[2 * j + hh]
                halves.append(_dot(sc.astype(BF16), v2b))
            o = jnp.where(lo, halves[0], halves[1])
            o = o + _dot(q2.astype(BF16), state.astype(BF16)) * indec_ref[j]
            kd = (k2 * kvdec_ref[j]).astype(BF16)
            s_new = cdec_ref[j] * state + blk_ref[...] * _dot_tn(kd, v2b)
            s_scr[b, j] = s_new
            slast_ref[b, j] = s_new
            sl = slice(LANES * j, LANES * (j + 1))
            o = _halfnorm(o, lo) * gng_ref[:, sl] + gnb_ref[:, sl]
            o_ref[b, :, sl] = (g2 * _sigmoid(g2) * o).astype(BF16)


def _ret_prompt(pr3, p, layer):
    nb, length, _ = pr3.shape
    dmask, kvdec, indec, cdec, blk = _ret_consts()
    return pl.pallas_call(
        _ret_prompt_kernel,
        grid=(length // TC,),
        in_specs=[pl.BlockSpec((nb, TC, RET_COLS), lambda n: (0, n, 0)),
                  _const_spec((RET_HEADS, TC, TC)), _const_spec((2, TC, LANES)), _const_spec((2, TC, LANES)),
                  _const_spec((2, LANES, LANES)), _const_spec((LANES, LANES)),
                  _layer_spec((1, RET_WIDTH), layer), _layer_spec((1, RET_WIDTH), layer)],
        out_specs=[pl.BlockSpec((nb, TC, RET_WIDTH), lambda n: (0, n, 0)),
                   pl.BlockSpec((nb, 2, LANES, LANES), lambda n: (0, 0, 0, 0))],
        out_shape=[jax.ShapeDtypeStruct((nb, length, RET_WIDTH), BF16),
                   jax.ShapeDtypeStruct((nb, 2, LANES, LANES), F32)],
        scratch_shapes=[pltpu.VMEM((nb, 2, LANES, LANES), F32)],
        compiler_params=_cparams(1),
        name="ret_prompt",
    )(pr3, dmask, kvdec, indec, cdec, blk, p["gn_g"], p["gn_b"])


def _att_bias():
    slopes = _alibi_slopes(ATT_Q_HEADS)
    qpos = np.arange(TC)[:, None] + TC
    kpos = np.arange(2 * TC)[None, :]
    dist = qpos - kpos
    valid = (dist >= 0) & (dist <= WINDOW)
    prev_cur = np.stack([np.where(valid, -slopes[h] * dist, NEG_INF) for h in range(ATT_Q_HEADS)])
    cur_prev = np.concatenate([prev_cur[:, :, TC:], prev_cur[:, :, :TC]], axis=2)
    first = np.where(np.arange(2 * TC) >= TC, NEG_INF, 0.0)[None, :]
    return jnp.asarray(cur_prev, F32), jnp.asarray(prev_cur, F32), jnp.asarray(first, F32)


def _att_prompt_kernel(sink_ref, pa_ref, bias_cp_ref, bias_pc_ref, first_ref, o_ref, kout_ref, vout_ref,
                       kn_scr, ks_scr, vn_scr, vs_scr, bias_scr, *, layer):
    n = pl.program_id(0)
    nb = pa_ref.shape[0]

    @pl.when(n == 0)
    def _():
        for k_scr, v_scr in ((kn_scr, vn_scr), (ks_scr, vs_scr)):
            k_scr[...] = jnp.zeros_like(k_scr)
            v_scr[:, :, 0:LANES] = jnp.zeros((nb, 2 * TC, LANES), BF16)
            v_scr[:, :, LANES:2 * LANES] = jnp.ones((nb, 2 * TC, LANES), BF16)

    even = n % 2 == 0
    row0 = pl.multiple_of((n % 2) * TC, TC)
    first = jnp.where(n == 0, first_ref[...], jnp.zeros_like(first_ref))
    for head in range(ATT_Q_HEADS):
        bias_scr[head] = jnp.where(even, bias_cp_ref[head], bias_pc_ref[head]) + first

    lo = lax.broadcasted_iota(jnp.int32, (TC, LANES), 1) < HALF
    zero = jnp.zeros((TC, LANES), F32)
    for b in range(nb):
        kcur = pa_ref[b, :, ATT_WIDTH:ATT_WIDTH + LANES]
        vcur = pa_ref[b, :, ATT_WIDTH + LANES:ATT_WIDTH + 2 * LANES]
        kout_ref[b] = kcur
        vout_ref[b] = vcur
        kn_scr[b, pl.ds(row0, TC), :] = kcur.astype(BF16)
        ks_scr[b, pl.ds(row0, TC), :] = pltpu.roll(kcur, HALF, 1).astype(BF16)
        vn_scr[b, pl.ds(row0, TC), 0:LANES] = vcur.astype(BF16)
        vs_scr[b, pl.ds(row0, TC), 0:LANES] = pltpu.roll(vcur, HALF, 1).astype(BF16)
        for c in range(3):
            qcol = pa_ref[b, :, LANES * c:LANES * (c + 1)]
            halves = []
            for hh in range(2):
                head = 2 * c + hh
                swapped = hh != head // ATT_GROUP
                kc = (ks_scr if swapped else kn_scr)[b]
                va = (vs_scr if swapped else vn_scr)[b]
                sink = sink_ref[layer, head]
                qm = (jnp.where(lo, qcol, zero) if hh == 0 else jnp.where(lo, zero, qcol)).astype(BF16)
                s = _dot_nt(qm, kc) + bias_scr[head]
                m = jnp.maximum(jnp.max(s, -1, keepdims=True), sink)
                e = jnp.exp(s - m)
                r = _dot(e.astype(BF16), va)
                den = r[:, LANES:2 * LANES] + jnp.exp(sink - m)
                halves.append(r[:, 0:LANES] / den)
            o_ref[b, :, LANES * c:LANES * (c + 1)] = jnp.where(lo, halves[0], halves[1]).astype(BF16)


def _att_prompt(pa3, p, layer):
    nb, length, _ = pa3.shape
    bias_cp, bias_pc, first = _att_bias()
    kv_scr = lambda width: pltpu.VMEM((nb, 2 * TC, width), BF16)
    return pl.pallas_call(
        functools.partial(_att_prompt_kernel, layer=layer),
        grid=(length // TC,),
        in_specs=[pl.BlockSpec(memory_space=pltpu.SMEM),
                  pl.BlockSpec((nb, TC, ATT_COLS), lambda n: (0, n, 0)),
                  _const_spec((ATT_Q_HEADS, TC, 2 * TC)), _const_spec((ATT_Q_HEADS, TC, 2 * TC)),
                  _const_spec((1, 2 * TC))],
        out_specs=[pl.BlockSpec((nb, TC, ATT_WIDTH), lambda n: (0, n, 0)),
                   pl.BlockSpec((nb, TC, LANES), lambda n: (0, 0, 0)),
                   pl.BlockSpec((nb, TC, LANES), lambda n: (0, 0, 0))],
        out_shape=[jax.ShapeDtypeStruct((nb, length, ATT_WIDTH), BF16),
                   jax.ShapeDtypeStruct((nb, TC, LANES), F32),
                   jax.ShapeDtypeStruct((nb, TC, LANES), F32)],
        scratch_shapes=[kv_scr(LANES), kv_scr(LANES), kv_scr(2 * LANES), kv_scr(2 * LANES),
                        pltpu.VMEM((ATT_Q_HEADS, TC, 2 * TC), F32)],
        compiler_params=_cparams(1),
        name="att_prompt",
    )(p["sinks"], pa3, bias_cp, bias_pc, first)


def _mix_sample_kernel(qt_ref, kt_ref, v4_ref, g4_ref, s_ref, qb_ref, kn_ref, vn_ref, kc_ref, vc_ref,
                       gng_ref, gnb_ref, bias_ref, sink_ref, *rest, layer):
    or_ref, oa_ref, sn_all, kout_all, vout_all = rest[-5:]
    if layer:
        for prev_ref, out_all in zip(rest[:3], (sn_all, kout_all, vout_all)):
            out_all[0:layer] = prev_ref[...]
    sn_ref, kout_ref, vout_ref = sn_all.at[layer], kout_all.at[layer], vout_all.at[layer]
    gammas = _ret_gammas()
    sink = sink_ref[:, 0:1]
    bias = bias_ref[...]

    def body(i, carry):
        qt = qt_ref[i]
        kt = kt_ref[i]
        v4 = v4_ref[i]
        g4 = g4_ref[i]
        rows = []
        for h in range(RET_HEADS):
            sl = slice(RET_D * h, RET_D * (h + 1))
            sn = gammas[h] * s_ref[i, sl, :] + kt[:, h:h + 1] * v4[h:h + 1, :]
            sn_ref[i, sl, :] = sn
            rows.append(jnp.sum(qt[:, h:h + 1] * sn, axis=0, keepdims=True))
        o4 = jnp.concatenate(rows, axis=0)
        mu = jnp.mean(o4, -1, keepdims=True)
        d = o4 - mu
        var = jnp.mean(d * d, -1, keepdims=True)
        o4 = d * lax.rsqrt(var + LN_EPS) * gng_ref[...] + gnb_ref[...]
        or_ref[i] = g4 * _sigmoid(g4) * o4

        kb = kc_ref[i]
        vb = vc_ref[i]
        q8 = qb_ref[i]
        knew = kn_ref[i]
        vnew = vn_ref[i]
        s = _dot_nt(q8.astype(BF16), kb.astype(BF16)) + bias
        s_self = jnp.sum(q8 * knew, -1, keepdims=True)
        m = jnp.maximum(jnp.maximum(jnp.max(s, -1, keepdims=True), s_self), sink)
        e = jnp.exp(s - m)
        e_self = jnp.exp(s_self - m)
        den = jnp.sum(e, -1, keepdims=True) + e_self + jnp.exp(sink - m)
        oa_ref[i] = (_dot(e.astype(BF16), vb.astype(BF16)) + e_self * vnew) / den

        kout_ref[i, 0:WINDOW - 1, :] = kc_ref[i, 1:WINDOW, :]
        kout_ref[i, WINDOW - 1:WINDOW, :] = knew
        vout_ref[i, 0:WINDOW - 1, :] = vc_ref[i, 1:WINDOW, :]
        vout_ref[i, WINDOW - 1:WINDOW, :] = vnew
        return carry

    lax.fori_loop(0, SAMPLE_ROWS, body, 0)


def _mix_sample(qt, kt, v4, g4, s_all, qb, kn, vn, kc_all, vc_all, p, bias8, layer, prev):
    nb = qt.shape[0]
    bb = SAMPLE_ROWS
    blk = lambda *dims: pl.BlockSpec((bb,) + dims, lambda i: (i,) + (0,) * len(dims))
    lblk = lambda *dims: pl.BlockSpec((None, bb) + dims, lambda i: (layer, i) + (0,) * len(dims))
    in_specs = [blk(RET_D, RET_HEADS), blk(RET_D, RET_HEADS), blk(RET_HEADS, RET_D), blk(RET_HEADS, RET_D),
                lblk(RET_WIDTH, RET_D), blk(8, LANES), blk(1, LANES), blk(1, LANES),
                lblk(WINDOW, LANES), lblk(WINDOW, LANES),
                _layer_spec((RET_HEADS, RET_D), layer), _layer_spec((RET_HEADS, RET_D), layer),
                _const_spec((8, LANES)), _layer_spec((8, LANES), layer)]
    args = [qt, kt, v4, g4, s_all, qb, kn, vn, kc_all, vc_all, p["gn_g4"], p["gn_b4"], bias8, p["sink8"]]
    stacked = lambda n_layers, *dims: pl.BlockSpec((n_layers, bb) + dims, lambda i: (0, i) + (0,) * len(dims))
    state_dims = ((RET_WIDTH, RET_D), (WINDOW, LANES), (WINDOW, LANES))
    if layer:
        in_specs += [stacked(layer, *dims) for dims in state_dims]
        args += list(prev)
    return pl.pallas_call(
        functools.partial(_mix_sample_kernel, layer=layer),
        grid=(nb // bb,),
        in_specs=in_specs,
        out_specs=[blk(RET_HEADS, RET_D), blk(8, LANES)] + [stacked(layer + 1, *dims) for dims in state_dims],
        out_shape=[jax.ShapeDtypeStruct((nb, RET_HEADS, RET_D), F32),
                   jax.ShapeDtypeStruct((nb, 8, LANES), F32)]
                  + [jax.ShapeDtypeStruct((layer + 1, nb) + dims, F32) for dims in state_dims],
        compiler_params=_cparams(1),
        name="mix_sample",
    )(*args)


def _outproj_kernel(z_ref, r_ref, a_ref, x_ref, w_ref, g_ref, b_ref, o_ref):
    c1, c2 = SSM_WIDTH, SSM_WIDTH + RET_WIDTH
    mix = _dot(z_ref[...], w_ref[0:c1]) + _dot(r_ref[...], w_ref[c1:c2]) + _dot(a_ref[...], w_ref[c2:D_MODEL])
    o_ref[...] = _layernorm(ALPHA * x_ref[...] + mix, g_ref[...], b_ref[...])


def _outproj(z, r, a, x2, p, layer):
    t = x2.shape[0]
    tm = min(ROW_TILE, t)
    row = lambda c: pl.BlockSpec((tm, c), lambda i: (i, 0))
    return pl.pallas_call(
        _outproj_kernel,
        grid=(t // tm,),
        in_specs=[row(SSM_WIDTH), row(RET_WIDTH), row(ATT_WIDTH), row(D_MODEL),
                  _layer_spec((D_MODEL, D_MODEL), layer), _layer_spec((1, D_MODEL), layer),
                  _layer_spec((1, D_MODEL), layer)],
        out_specs=row(D_MODEL),
        out_shape=jax.ShapeDtypeStruct((t, D_MODEL), F32),
        compiler_params=_cparams(1),
        name="outproj",
    )(z, r, a, x2, p["w_out"], p["ln1_g"], p["ln1_b"])


def _ffn_kernel(x_ref, w1_ref, w3_ref, w2_ref, g_ref, b_ref, o_ref):
    x = x_ref[...]
    xb = x.astype(BF16)
    h1 = _dot(xb, w1_ref[...])
    h3 = _dot(xb, w3_ref[...])
    hid = (h1 * _sigmoid(h1) * h3).astype(BF16)
    o_ref[...] = _layernorm(ALPHA * x + _dot(hid, w2_ref[...]), g_ref[...], b_ref[...])


def _ffn(x2, p, layer):
    t = x2.shape[0]
    tm = min(ROW_TILE, t)
    row = pl.BlockSpec((tm, D_MODEL), lambda i: (i, 0))
    return pl.pallas_call(
        _ffn_kernel,
        grid=(t // tm,),
        in_specs=[row, _layer_spec((D_MODEL, FFN_HIDDEN), layer), _layer_spec((D_MODEL, FFN_HIDDEN), layer),
                  _layer_spec((FFN_HIDDEN, D_MODEL), layer), _layer_spec((1, D_MODEL), layer),
                  _layer_spec((1, D_MODEL), layer)],
        out_specs=row,
        out_shape=jax.ShapeDtypeStruct((t, D_MODEL), F32),
        compiler_params=_cparams(1),
        name="ffn",
    )(x2, p["w1"], p["w3"], p["w2"], p["ln2_g"], p["ln2_b"])


def _prep_params(w_in, lam_re, lam_im, log_step, b_re, b_im, c_re, c_im, ssm_d, glu_w, glu_b,
                 gn_g, gn_b, sinks, w_out, ln1_g, ln1_b, w1, w3, w2, ln2_g, ln2_b):
    col_scale = np.ones((IN_WIDTH,), np.float32)
    col_scale[COL_RET + RET_WIDTH:COL_RET + 2 * RET_WIDTH] = RET_D ** -0.5
    col_scale[COL_ATT:COL_ATT + ATT_WIDTH] = ATT_D ** -0.5

    step = jnp.exp(log_step)[..., None]
    mag = jnp.exp(lam_re * step)
    a_re, a_im = mag * jnp.cos(lam_im * step), mag * jnp.sin(lam_im * step)
    den = lam_re * lam_re + lam_im * lam_im
    k_re = ((a_re - 1.0) * lam_re + a_im * lam_im) / den
    k_im = (a_im * lam_re - (a_re - 1.0) * lam_im) / den
    bb_re = k_re[..., None] * b_re - k_im[..., None] * b_im
    bb_im = k_re[..., None] * b_im + k_im[..., None] * b_re
    eye8 = jnp.eye(8, dtype=F32)

    def b_blocks(m):
        m = m.reshape(DEPTH, 3, 8, SSM_STATE, SSM_CH)
        return jnp.einsum("ligph,gk->lighkp", m, eye8).reshape(DEPTH, 3, LANES, 512).astype(BF16)

    def c_blocks(m):
        m = m.reshape(DEPTH, 3, 8, SSM_CH, SSM_STATE)
        return jnp.einsum("lighp,gk->likpgh", m, eye8).reshape(DEPTH, 3, 512, LANES).astype(BF16)

    row = lambda a: a.reshape(DEPTH, 1, -1)
    sink8 = jnp.concatenate([sinks, jnp.zeros((DEPTH, 2), F32)], axis=1)
    return dict(
        w_in=(w_in * col_scale).astype(BF16),
        bre=b_blocks(bb_re), bim=b_blocks(bb_im), cre=c_blocks(c_re), cim=c_blocks(-c_im),
        ar=row(a_re), ai=row(a_im),
        ssm_d=row(ssm_d), glu_w=glu_w.astype(BF16), glu_b=row(glu_b),
        gn_g=row(gn_g), gn_b=row(gn_b),
        gn_g4=gn_g.reshape(DEPTH, RET_HEADS, RET_D), gn_b4=gn_b.reshape(DEPTH, RET_HEADS, RET_D),
        sinks=sinks, sink8=jnp.broadcast_to(sink8[:, :, None], (DEPTH, 8, LANES)),
        w_out=w_out.astype(BF16), ln1_g=row(ln1_g), ln1_b=row(ln1_b),
        w1=w1.astype(BF16), w3=w3.astype(BF16), w2=w2.astype(BF16), ln2_g=row(ln2_g), ln2_b=row(ln2_b),
    )


def _prompt_layer(x3, p, layer):
    nb, length, _ = x3.shape
    t = nb * length
    x2 = x3.reshape(t, D_MODEL)
    pu, pr, pa = _inproj(x2, p, layer)
    z, hlast = _s5_prompt(pu.reshape(nb, length, SSM_WIDTH), p, layer)
    o_r, sblk = _ret_prompt(pr.reshape(nb, length, RET_COLS), p, layer)
    o_a, klast, vlast = _att_prompt(pa.reshape(nb, length, ATT_COLS), p, layer)
    x1 = _outproj(z.reshape(t, SSM_WIDTH), o_r.reshape(t, RET_WIDTH), o_a.reshape(t, ATT_WIDTH), x2, p, layer)
    y = _ffn(x1, p, layer).reshape(nb, length, D_MODEL)
    s4 = jnp.stack([sblk[:, h // 2, HALF * (h % 2):HALF * (h % 2 + 1), HALF * (h % 2):HALF * (h % 2 + 1)]
                    for h in range(RET_HEADS)], axis=1)
    return y, hlast, s4, klast, vlast


def _sample_bias8():
    slopes = _alibi_slopes(ATT_Q_HEADS)
    dist = WINDOW - np.arange(WINDOW)
    bias = np.zeros((8, WINDOW))
    for h in range(ATT_Q_HEADS):
        bias[h] = -slopes[h] * dist
    return jnp.asarray(bias, F32)


def _sample_layer(x2, p, layer, h0r, h0i, s_all, kc_all, vc_all, prev):
    nb = x2.shape[0]
    pu, pr, pa = _inproj(x2, p, layer)
    z, h_re, h_im = _s5_sample(pu, h0r, h0i, p, layer)
    heads = lambda a: a.reshape(nb, RET_HEADS, RET_D)
    q4, k4, v4, g4 = (heads(pr[:, RET_WIDTH * i:RET_WIDTH * (i + 1)]) for i in range(4))
    q6 = pa[:, 0:ATT_WIDTH].reshape(nb, ATT_Q_HEADS, ATT_D)
    kv_of = np.arange(ATT_Q_HEADS) // ATT_GROUP
    qb = jnp.zeros((nb, 8, ATT_KV_HEADS, ATT_D), F32).at[:, np.arange(ATT_Q_HEADS), kv_of].set(q6)
    kn = pa[:, ATT_WIDTH:ATT_WIDTH + LANES].reshape(nb, 1, LANES)
    vn = pa[:, ATT_WIDTH + LANES:ATT_WIDTH + 2 * LANES].reshape(nb, 1, LANES)
    o_r, o8, s_new, k_out, v_out = _mix_sample(
        q4.transpose(0, 2, 1), k4.transpose(0, 2, 1), v4, g4, s_all, qb.reshape(nb, 8, LANES), kn, vn,
        kc_all, vc_all, p, _sample_bias8(), layer, prev)
    o_r = o_r.reshape(nb, RET_WIDTH).astype(BF16)
    o_a = o8.reshape(nb, 8, ATT_KV_HEADS, ATT_D)[:, np.arange(ATT_Q_HEADS), kv_of].reshape(nb, ATT_WIDTH).astype(BF16)
    x1 = _outproj(z, o_r, o_a, x2, p, layer)
    y = _ffn(x1, p, layer)
    return y, h_re, h_im, (s_new, k_out, v_out)


def kernel(x_prompt, x_sample, state_ssm_re, state_ssm_im, state_ret, cache_win_k, cache_win_v, w_in, ssm_lambda_re, ssm_lambda_im, ssm_log_step, ssm_b_re, ssm_b_im, ssm_c_re, ssm_c_im, ssm_d, ssm_glu_w, ssm_glu_b, ret_gn_g, ret_gn_b, attn_sinks, w_out, ln1_g, ln1_b, ffn_w1, ffn_w3, ffn_w2, ln2_g, ln2_b):
    p = _prep_params(w_in, ssm_lambda_re, ssm_lambda_im, ssm_log_step, ssm_b_re, ssm_b_im, ssm_c_re, ssm_c_im,
                     ssm_d, ssm_glu_w, ssm_glu_b, ret_gn_g, ret_gn_b, attn_sinks, w_out, ln1_g, ln1_b,
                     ffn_w1, ffn_w3, ffn_w2, ln2_g, ln2_b)

    xp = x_prompt
    pb = x_prompt.shape[0]
    p_out = []
    for layer in range(DEPTH):
        xp, *states = _prompt_layer(xp, p, layer)
        p_out.append(states)
    stack = lambda outs, i: jnp.stack([o[i] for o in outs])
    p_h = stack(p_out, 0)
    p_kv_shape = (DEPTH, pb, TC, ATT_KV_HEADS, ATT_D)

    nb = x_sample.shape[0]
    xs = x_sample.reshape(nb, D_MODEL)
    h0r = state_ssm_re.reshape(DEPTH, nb, SSM_LANES)
    h0i = state_ssm_im.reshape(DEPTH, nb, SSM_LANES)
    s_all = state_ret.reshape(DEPTH, nb, RET_WIDTH, RET_D)
    kc_all = cache_win_k.reshape(DEPTH, nb, WINDOW, LANES)
    vc_all = cache_win_v.reshape(DEPTH, nb, WINDOW, LANES)
    prev = None
    s_h = []
    for layer in range(DEPTH):
        xs, h_re, h_im, prev = _sample_layer(xs, p, layer, h0r, h0i, s_all, kc_all, vc_all, prev)
        s_h.append((h_re, h_im))
    s_ret, s_k, s_v = prev
    ssm_shape = (DEPTH, nb, SSM_GROUPS, SSM_STATE)
    s_kv_shape = (DEPTH, nb, WINDOW, ATT_KV_HEADS, ATT_D)

    return (xp, xs.reshape(nb, 1, D_MODEL),
            p_h[:, :, :SSM_LANES].reshape(DEPTH, pb, SSM_GROUPS, SSM_STATE),
            p_h[:, :, SSM_LANES:].reshape(DEPTH, pb, SSM_GROUPS, SSM_STATE),
            stack(p_out, 1), stack(p_out, 2).reshape(p_kv_shape), stack(p_out, 3).reshape(p_kv_shape),
            stack(s_h, 0).reshape(ssm_shape), stack(s_h, 1).reshape(ssm_shape),
            s_ret.reshape(DEPTH, nb, RET_HEADS, RET_D, RET_D), s_k.reshape(s_kv_shape), s_v.reshape(s_kv_shape))
```

```python
import functools
import math

import numpy as np
import jax
import jax.numpy as jnp
from jax import lax
from jax.experimental import pallas as pl
from jax.experimental.pallas import tpu as pltpu

F32 = jnp.float32
BF16 = jnp.bfloat16

D_MODEL = 1024
DEPTH = 2
SSM_GROUPS = 24
SSM_CH = 16
SSM_STATE = 64
SSM_WIDTH = SSM_GROUPS * SSM_CH
SSM_LANES = SSM_GROUPS * SSM_STATE
RET_HEADS = 4
RET_D = 64
RET_WIDTH = RET_HEADS * RET_D
ATT_Q_HEADS = 6
ATT_KV_HEADS = 2
ATT_GROUP = ATT_Q_HEADS // ATT_KV_HEADS
ATT_D = 64
ATT_WIDTH = ATT_Q_HEADS * ATT_D
WINDOW = 128
FFN_HIDDEN = 2816
IN_WIDTH = 2048
ALPHA = (2 * DEPTH) ** 0.25
LN_EPS = 1e-5
NEG_INF = -1e30

COL_RET = SSM_WIDTH
COL_ATT = SSM_WIDTH + 4 * RET_WIDTH
RET_COLS = 4 * RET_WIDTH
ATT_COLS = ATT_WIDTH + 2 * ATT_KV_HEADS * ATT_D

TC = 128
LANES = 128
N_TILES = SSM_LANES // LANES
HALF = 64
ROW_TILE = 512
SAMPLE_ROWS = 16
VMEM_LIMIT = 56 * 2 ** 20


def _alibi_slopes(n):
    def pow2(k):
        start = 2.0 ** (-8.0 / k)
        return [start ** (i + 1) for i in range(k)]
    if n & (n - 1) == 0:
        return pow2(n)
    c = 2 ** int(math.floor(math.log2(n)))
    return pow2(c) + pow2(2 * c)[0::2][: n - c]


def _cparams(n_axes):
    return pltpu.CompilerParams(dimension_semantics=("arbitrary",) * n_axes, vmem_limit_bytes=VMEM_LIMIT)


def _const_spec(shape):
    nd = len(shape)
    return pl.BlockSpec(shape, lambda *_: (0,) * nd, pipeline_mode=pl.Buffered(1))


def _layer_spec(shape, layer):
    nd = len(shape)
    return pl.BlockSpec((None,) + tuple(shape), lambda *_: (layer,) + (0,) * nd, pipeline_mode=pl.Buffered(1))


def _dot(a, b):
    return jnp.dot(a, b, preferred_element_type=F32)


def _dot_nt(a, b):
    return lax.dot_general(a, b, (((1,), (1,)), ((), ())), preferred_element_type=F32)


def _dot_tn(a, b):
    return lax.dot_general(a, b, (((0,), (0,)), ((), ())), preferred_element_type=F32)


def _layernorm(v, g, b):
    mu = jnp.mean(v, -1, keepdims=True)
    d = v - mu
    var = jnp.mean(d * d, -1, keepdims=True)
    return d * lax.rsqrt(var + LN_EPS) * g + b


def _gelu_tanh(x):
    return 0.5 * x * (1.0 + jnp.tanh(math.sqrt(2.0 / math.pi) * (x + 0.044715 * (x * x * x))))


def _sigmoid(x):
    return 1.0 / (1.0 + jnp.exp(-x))


def _inproj_kernel(x_ref, w_ref, pu_ref, pr_ref, pa_ref):
    xb = x_ref[...].astype(BF16)
    pu_ref[...] = _dot(xb, w_ref[:, 0:COL_RET])
    pr_ref[...] = _dot(xb, w_ref[:, COL_RET:COL_ATT])
    pa_ref[...] = _dot(xb, w_ref[:, COL_ATT:IN_WIDTH])


def _inproj(x2, p, layer):
    t = x2.shape[0]
    tm = min(ROW_TILE, t)
    row = lambda c: pl.BlockSpec((tm, c), lambda i: (i, 0))
    return pl.pallas_call(
        _inproj_kernel,
        grid=(t // tm,),
        in_specs=[row(D_MODEL), _layer_spec((D_MODEL, IN_WIDTH), layer)],
        out_specs=[row(SSM_WIDTH), row(RET_COLS), row(ATT_COLS)],
        out_shape=[jax.ShapeDtypeStruct((t, SSM_WIDTH), F32),
                   jax.ShapeDtypeStruct((t, RET_COLS), F32),
                   jax.ShapeDtypeStruct((t, ATT_COLS), F32)],
        compiler_params=_cparams(1),
        name="inproj",
    )(x2, p["w_in"])


def _s5_bu(ub, bre_ref, bim_ref):
    res, ims = [], []
    for i in range(3):
        ui = ub[:, LANES * i:LANES * (i + 1)]
        res.append(_dot(ui, bre_ref[i]))
        ims.append(_dot(ui, bim_ref[i]))
    return res, ims


def _s5_out(u, h_re, h_im, cre_ref, cim_ref, d_ref, gw_ref, gb_ref):
    ys = []
    for i in range(3):
        ys.append(_dot(h_re[i].astype(BF16), cre_ref[i]) + _dot(h_im[i].astype(BF16), cim_ref[i]))
    y = jnp.concatenate(ys, axis=1) + d_ref[...] * u
    z = _gelu_tanh(y)
    gate = _sigmoid(_dot(z.astype(BF16), gw_ref[...]) + gb_ref[...])
    return z * gate


def _s5_prompt_kernel(u_ref, bre_ref, bim_ref, cre_ref, cim_ref, ar_ref, ai_ref, d_ref, gw_ref, gb_ref,
                      z_ref, hlast_ref, h_scr, hc_scr, bt_scr, tm_scr):
    n = pl.program_id(0)
    nb = u_ref.shape[0]
    rows = nb * TC

    @pl.when(n == 0)
    def _():
        hc_scr[...] = jnp.zeros_like(hc_scr)

    u_bt = u_ref[...].reshape(rows, SSM_WIDTH)
    for c in range(3):
        bt_scr[c] = u_bt[:, LANES * c:LANES * (c + 1)]

    def to_time_major(t, carry):
        r0 = pl.multiple_of(t * nb, nb)
        for c in range(3):
            tm_scr[c, pl.ds(r0, nb), :] = bt_scr[c, pl.ds(t, nb, stride=TC), :]
        return carry

    lax.fori_loop(0, TC, to_time_major, 0, unroll=4)
    u = jnp.concatenate([tm_scr[c] for c in range(3)], axis=1)

    res, ims = _s5_bu(u.astype(BF16), bre_ref, bim_ref)
    for i in range(3):
        for k in range(4):
            h_scr[4 * i + k] = res[i][:, LANES * k:LANES * (k + 1)]
            h_scr[N_TILES + 4 * i + k] = ims[i][:, LANES * k:LANES * (k + 1)]

    def step(t, carry):
        r0 = pl.multiple_of(t * nb, nb)
        out = []
        for c in range(N_TILES):
            sl = slice(LANES * c, LANES * (c + 1))
            ar, ai = ar_ref[:, sl], ai_ref[:, sl]
            hr, hi = carry[c], carry[N_TILES + c]
            nr = ar * hr - ai * hi + h_scr[c, pl.ds(r0, nb), :]
            ni = ar * hi + ai * hr + h_scr[N_TILES + c, pl.ds(r0, nb), :]
            h_scr[c, pl.ds(r0, nb), :] = nr
            h_scr[N_TILES + c, pl.ds(r0, nb), :] = ni
            out.append((nr, ni))
        return tuple(o[0] for o in out) + tuple(o[1] for o in out)

    init = tuple(hc_scr[:, LANES * c:LANES * (c + 1)] for c in range(2 * N_TILES))
    fin = lax.fori_loop(0, TC, step, init, unroll=2)
    for c in range(2 * N_TILES):
        hc_scr[:, LANES * c:LANES * (c + 1)] = fin[c]
        hlast_ref[:, LANES * c:LANES * (c + 1)] = fin[c]

    h_re = [jnp.concatenate([h_scr[4 * i + k] for k in range(4)], axis=1) for i in range(3)]
    h_im = [jnp.concatenate([h_scr[N_TILES + 4 * i + k] for k in range(4)], axis=1) for i in range(3)]
    zz = _s5_out(u, h_re, h_im, cre_ref, cim_ref, d_ref, gw_ref, gb_ref)
    for c in range(3):
        tm_scr[c] = zz[:, LANES * c:LANES * (c + 1)]

    def to_batch_major(t, carry):
        r0 = pl.multiple_of(t * nb, nb)
        for c in range(3):
            bt_scr[c, pl.ds(t, nb, stride=TC), :] = tm_scr[c, pl.ds(r0, nb), :]
        return carry

    lax.fori_loop(0, TC, to_batch_major, 0, unroll=4)
    z_bt = jnp.concatenate([bt_scr[c] for c in range(3)], axis=1)
    z_ref[...] = z_bt.astype(BF16).reshape(nb, TC, SSM_WIDTH)


def _s5_weight_specs(layer):
    ls = lambda *shape: _layer_spec(shape, layer)
    return [ls(3, LANES, 512), ls(3, LANES, 512), ls(3, 512, LANES), ls(3, 512, LANES),
            ls(1, SSM_LANES), ls(1, SSM_LANES), ls(1, SSM_WIDTH), ls(SSM_WIDTH, SSM_WIDTH), ls(1, SSM_WIDTH)]


def _s5_weights(p):
    return (p["bre"], p["bim"], p["cre"], p["cim"], p["ar"], p["ai"], p["ssm_d"], p["glu_w"], p["glu_b"])


def _s5_prompt(u3, p, layer):
    nb, length, _ = u3.shape
    return pl.pallas_call(
        _s5_prompt_kernel,
        grid=(length // TC,),
        in_specs=[pl.BlockSpec((nb, TC, SSM_WIDTH), lambda n: (0, n, 0))] + _s5_weight_specs(layer),
        out_specs=[pl.BlockSpec((nb, TC, SSM_WIDTH), lambda n: (0, n, 0)),
                   pl.BlockSpec((nb, 2 * SSM_LANES), lambda n: (0, 0))],
        out_shape=[jax.ShapeDtypeStruct((nb, length, SSM_WIDTH), BF16),
                   jax.ShapeDtypeStruct((nb, 2 * SSM_LANES), F32)],
        scratch_shapes=[pltpu.VMEM((2 * N_TILES, nb * TC, LANES), F32), pltpu.VMEM((nb, 2 * SSM_LANES), F32),
                        pltpu.VMEM((3, nb * TC, LANES), F32), pltpu.VMEM((3, nb * TC, LANES), F32)],
        compiler_params=_cparams(1),
        name="s5_prompt",
    )(u3, *_s5_weights(p))


def _s5_sample_kernel(u_ref, h0r_ref, h0i_ref, bre_ref, bim_ref, cre_ref, cim_ref, ar_ref, ai_ref, d_ref,
                      gw_ref, gb_ref, z_ref, hr_ref, hi_ref):
    u = u_ref[...]
    res, ims = _s5_bu(u.astype(BF16), bre_ref, bim_ref)
    h_re, h_im = [], []
    for i in range(3):
        sl = slice(512 * i, 512 * (i + 1))
        ar, ai = ar_ref[:, sl], ai_ref[:, sl]
        h0r, h0i = h0r_ref[:, sl], h0i_ref[:, sl]
        nr = ar * h0r - ai * h0i + res[i]
        ni = ar * h0i + ai * h0r + ims[i]
        hr_ref[:, sl] = nr
        hi_ref[:, sl] = ni
        h_re.append(nr)
        h_im.append(ni)
    z_ref[...] = _s5_out(u, h_re, h_im, cre_ref, cim_ref, d_ref, gw_ref, gb_ref).astype(BF16)


def _s5_sample(u, h0r, h0i, p, layer):
    nb = u.shape[0]
    full = lambda c: pl.BlockSpec((nb, c), lambda i: (0, 0))
    state = pl.BlockSpec((None, nb, SSM_LANES), lambda i: (layer, 0, 0))
    return pl.pallas_call(
        _s5_sample_kernel,
        grid=(1,),
        in_specs=[full(SSM_WIDTH), state, state] + _s5_weight_specs(layer),
        out_specs=[full(SSM_WIDTH), full(SSM_LANES), full(SSM_LANES)],
        out_shape=[jax.ShapeDtypeStruct((nb, SSM_WIDTH), BF16),
                   jax.ShapeDtypeStruct((nb, SSM_LANES), F32),
                   jax.ShapeDtypeStruct((nb, SSM_LANES), F32)],
        compiler_params=_cparams(1),
        name="s5_sample",
    )(u, h0r, h0i, *_s5_weights(p))


def _ret_gammas():
    return [1.0 - 2.0 ** (-5.0 - h) for h in range(RET_HEADS)]


def _ret_consts():
    log_g = np.log1p(-(2.0 ** (-5.0 - np.arange(RET_HEADS, dtype=np.float64))))
    idx = np.arange(TC)
    diff = idx[:, None] - idx[None, :]
    dmask = np.where(diff >= 0, np.exp(log_g[:, None, None] * np.maximum(diff, 0)), 0.0)
    inner = np.exp(log_g[:, None] * (idx + 1))
    kvd = np.exp(log_g[:, None] * (TC - 1 - idx))
    cd = np.exp(log_g * TC)
    indec = np.zeros((2, TC, LANES))
    kvdec = np.zeros((2, TC, LANES))
    cdec = np.zeros((2, LANES, LANES))
    blk = np.zeros((LANES, LANES))
    for j in range(2):
        for hh in range(2):
            sl = slice(HALF * hh, HALF * (hh + 1))
            indec[j, :, sl] = inner[2 * j + hh][:, None]
            kvdec[j, :, sl] = kvd[2 * j + hh][:, None]
            cdec[j, sl, sl] = cd[2 * j + hh]
            blk[sl, sl] = 1.0
    as32 = lambda a: jnp.asarray(a, F32)
    return as32(dmask), as32(kvdec), as32(indec), as32(cdec), as32(blk)


def _halfnorm(o, lo):
    zero = jnp.zeros_like(o)
    s_lo = jnp.sum(jnp.where(lo, o, zero), -1, keepdims=True)
    s_hi = jnp.sum(jnp.where(lo, zero, o), -1, keepdims=True)
    d = o - jnp.where(lo, s_lo, s_hi) * (1.0 / HALF)
    d2 = d * d
    v_lo = jnp.sum(jnp.where(lo, d2, zero), -1, keepdims=True)
    v_hi = jnp.sum(jnp.where(lo, zero, d2), -1, keepdims=True)
    var = jnp.where(lo, v_lo, v_hi) * (1.0 / HALF)
    return d * lax.rsqrt(var + LN_EPS)


def _ret_prompt_kernel(pr_ref, dmask_ref, kvdec_ref, indec_ref, cdec_ref, blk_ref, gng_ref, gnb_ref,
                       o_ref, slast_ref, s_scr):
    n = pl.program_id(0)
    nb = pr_ref.shape[0]

    @pl.when(n == 0)
    def _():
        s_scr[...] = jnp.zeros_like(s_scr)

    lo = lax.broadcasted_iota(jnp.int32, (TC, LANES), 1) < HALF
    zero = jnp.zeros((TC, LANES), F32)
    for b in range(nb):
        for j in range(2):
            q2 = pr_ref[b, :, LANES * j:LANES * (j + 1)]
            k2 = pr_ref[b, :, 256 + LANES * j:256 + LANES * (j + 1)]
            v2 = pr_ref[b, :, 512 + LANES * j:512 + LANES * (j + 1)]
            g2 = pr_ref[b, :, 768 + LANES * j:768 + LANES * (j + 1)]
            k2b = k2.astype(BF16)
            v2b = v2.astype(BF16)
            state = s_scr[b, j]
            halves = []
            for hh in range(2):
                qm = (jnp.where(lo, q2, zero) if hh == 0 else jnp.where(lo, zero, q2)).astype(BF16)
                sc = _dot_nt(qm, k2b) * dmask_ref[2 * j + hh]
                halves.append(_dot(sc.astype(BF16), v2b))
            o = jnp.where(lo, halves[0], halves[1])
            o = o + _dot(q2.astype(BF16), state.astype(BF16)) * indec_ref[j]
            kd = (k2 * kvdec_ref[j]).astype(BF16)
            s_new = cdec_ref[j] * state + blk_ref[...] * _dot_tn(kd, v2b)
            s_scr[b, j] = s_new
            slast_ref[b, j] = s_new
            sl = slice(LANES * j, LANES * (j + 1))
            o = _halfnorm(o, lo) * gng_ref[:, sl] + gnb_ref[:, sl]
            o_ref[b, :, sl] = (g2 * _sigmoid(g2) * o).astype(BF16)


def _ret_prompt(pr3, p, layer):
    nb, length, _ = pr3.shape
    dmask, kvdec, indec, cdec, blk = _ret_consts()
    return pl.pallas_call(
        _ret_prompt_kernel,
        grid=(length // TC,),
        in_specs=[pl.BlockSpec((nb, TC, RET_COLS), lambda n: (0, n, 0)),
                  _const_spec((RET_HEADS, TC, TC)), _const_spec((2, TC, LANES)), _const_spec((2, TC, LANES)),
                  _const_spec((2, LANES, LANES)), _const_spec((LANES, LANES)),
                  _layer_spec((1, RET_WIDTH), layer), _layer_spec((1, RET_WIDTH), layer)],
        out_specs=[pl.BlockSpec((nb, TC, RET_WIDTH), lambda n: (0, n, 0)),
                   pl.BlockSpec((nb, 2, LANES, LANES), lambda n: (0, 0, 0, 0))],
        out_shape=[jax.ShapeDtypeStruct((nb, length, RET_WIDTH), BF16),
                   jax.ShapeDtypeStruct((nb, 2, LANES, LANES), F32)],
        scratch_shapes=[pltpu.VMEM((nb, 2, LANES, LANES), F32)],
        compiler_params=_cparams(1),
        name="ret_prompt",
    )(pr3, dmask, kvdec, indec, cdec, blk, p["gn_g"], p["gn_b"])


def _att_bias():
    slopes = _alibi_slopes(ATT_Q_HEADS)
    qpos = np.arange(TC)[:, None] + TC
    kpos = np.arange(2 * TC)[None, :]
    dist = qpos - kpos
    valid = (dist >= 0) & (dist <= WINDOW)
    prev_cur = np.stack([np.where(valid, -slopes[h] * dist, NEG_INF) for h in range(ATT_Q_HEADS)])
    cur_prev = np.concatenate([prev_cur[:, :, TC:], prev_cur[:, :, :TC]], axis=2)
    first = np.where(np.arange(2 * TC) >= TC, NEG_INF, 0.0)[None, :]
    return jnp.asarray(cur_prev, F32), jnp.asarray(prev_cur, F32), jnp.asarray(first, F32)


def _att_prompt_kernel(sink_ref, pa_ref, bias_cp_ref, bias_pc_ref, first_ref, o_ref, kout_ref, vout_ref,
                       kn_scr, ks_scr, vn_scr, vs_scr, bias_scr, *, layer):
    n = pl.program_id(0)
    nb = pa_ref.shape[0]

    @pl.when(n == 0)
    def _():
        for k_scr, v_scr in ((kn_scr, vn_scr), (ks_scr, vs_scr)):
            k_scr[...] = jnp.zeros_like(k_scr)
            v_scr[:, :, 0:LANES] = jnp.zeros((nb, 2 * TC, LANES), BF16)
            v_scr[:, :, LANES:2 * LANES] = jnp.ones((nb, 2 * TC, LANES), BF16)

    even = n % 2 == 0
    row0 = pl.multiple_of((n % 2) * TC, TC)
    first = jnp.where(n == 0, first_ref[...], jnp.zeros_like(first_ref))
    for head in range(ATT_Q_HEADS):
        bias_scr[head] = jnp.where(even, bias_cp_ref[head], bias_pc_ref[head]) + first

    lo = lax.broadcasted_iota(jnp.int32, (TC, LANES), 1) < HALF
    zero = jnp.zeros((TC, LANES), F32)
    for b in range(nb):
        kcur = pa_ref[b, :, ATT_WIDTH:ATT_WIDTH + LANES]
        vcur = pa_ref[b, :, ATT_WIDTH + LANES:ATT_WIDTH + 2 * LANES]
        kout_ref[b] = kcur
        vout_ref[b] = vcur
        kn_scr[b, pl.ds(row0, TC), :] = kcur.astype(BF16)
        ks_scr[b, pl.ds(row0, TC), :] = pltpu.roll(kcur, HALF, 1).astype(BF16)
        vn_scr[b, pl.ds(row0, TC), 0:LANES] = vcur.astype(BF16)
        vs_scr[b, pl.ds(row0, TC), 0:LANES] = pltpu.roll(vcur, HALF, 1).astype(BF16)
        for c in range(3):
            qcol = pa_ref[b, :, LANES * c:LANES * (c + 1)]
            halves = []
            for hh in range(2):
                head = 2 * c + hh
                swapped = hh != head // ATT_GROUP
                kc = (ks_scr if swapped else kn_scr)[b]
                va = (vs_scr if swapped else vn_scr)[b]
                sink = sink_ref[layer, head]
                qm = (jnp.where(lo, qcol, zero) if hh == 0 else jnp.where(lo, zero, qcol)).astype(BF16)
                s = _dot_nt(qm, kc) + bias_scr[head]
                m = jnp.maximum(jnp.max(s, -1, keepdims=True), sink)
                e = jnp.exp(s - m)
                r = _dot(e.astype(BF16), va)
                den = r[:, LANES:2 * LANES] + jnp.exp(sink - m)
                halves.append(r[:, 0:LANES] / den)
            o_ref[b, :, LANES * c:LANES * (c + 1)] = jnp.where(lo, halves[0], halves[1]).astype(BF16)


def _att_prompt(pa3, p, layer):
    nb, length, _ = pa3.shape
    bias_cp, bias_pc, first = _att_bias()
    kv_scr = lambda width: pltpu.VMEM((nb, 2 * TC, width), BF16)
    return pl.pallas_call(
        functools.partial(_att_prompt_kernel, layer=layer),
        grid=(length // TC,),
        in_specs=[pl.BlockSpec(memory_space=pltpu.SMEM),
                  pl.BlockSpec((nb, TC, ATT_COLS), lambda n: (0, n, 0)),
                  _const_spec((ATT_Q_HEADS, TC, 2 * TC)), _const_spec((ATT_Q_HEADS, TC, 2 * TC)),
                  _const_spec((1, 2 * TC))],
        out_specs=[pl.BlockSpec((nb, TC, ATT_WIDTH), lambda n: (0, n, 0)),
                   pl.BlockSpec((nb, TC, LANES), lambda n: (0, 0, 0)),
                   pl.BlockSpec((nb, TC, LANES), lambda n: (0, 0, 0))],
        out_shape=[jax.ShapeDtypeStruct((nb, length, ATT_WIDTH), BF16),
                   jax.ShapeDtypeStruct((nb, TC, LANES), F32),
                   jax.ShapeDtypeStruct((nb, TC, LANES), F32)],
        scratch_shapes=[kv_scr(LANES), kv_scr(LANES), kv_scr(2 * LANES), kv_scr(2 * LANES),
                        pltpu.VMEM((ATT_Q_HEADS, TC, 2 * TC), F32)],
        compiler_params=_cparams(1),
        name="att_prompt",
    )(p["sinks"], pa3, bias_cp, bias_pc, first)


def _mix_sample_kernel(qt_ref, kt_ref, v4_ref, g4_ref, s_ref, qb_ref, kn_ref, vn_ref, kc_ref, vc_ref,
                       gng_ref, gnb_ref, bias_ref, sink_ref, *rest, layer):
    or_ref, oa_ref, sn_all, kout_all, vout_all = rest[-5:]
    if layer:
        for prev_ref, out_all in zip(rest[:3], (sn_all, kout_all, vout_all)):
            out_all[0:layer] = prev_ref[...]
    sn_ref, kout_ref, vout_ref = sn_all.at[layer], kout_all.at[layer], vout_all.at[layer]
    gammas = _ret_gammas()
    sink = sink_ref[:, 0:1]
    bias = bias_ref[...]

    for i in range(SAMPLE_ROWS):
        qt = qt_ref[i]
        kt = kt_ref[i]
        v4 = v4_ref[i]
        g4 = g4_ref[i]
        rows = []
        for h in range(RET_HEADS):
            sl = slice(RET_D * h, RET_D * (h + 1))
            sn = gammas[h] * s_ref[i, sl, :] + kt[:, h:h + 1] * v4[h:h + 1, :]
            sn_ref[i, sl, :] = sn
            rows.append(jnp.sum(qt[:, h:h + 1] * sn, axis=0, keepdims=True))
        o4 = jnp.concatenate(rows, axis=0)
        mu = jnp.mean(o4, -1, keepdims=True)
        d = o4 - mu
        var = jnp.mean(d * d, -1, keepdims=True)
        o4 = d * lax.rsqrt(var + LN_EPS) * gng_ref[...] + gnb_ref[...]
        or_ref[i] = g4 * _sigmoid(g4) * o4

        kb = kc_ref[i]
        vb = vc_ref[i]
        q8 = qb_ref[i]
        knew = kn_ref[i]
        vnew = vn_ref[i]
        s = _dot_nt(q8.astype(BF16), kb.astype(BF16)) + bias
        s_self = jnp.sum(q8 * knew, -1, keepdims=True)
        m = jnp.maximum(jnp.maximum(jnp.max(s, -1, keepdims=True), s_self), sink)
        e = jnp.exp(s - m)
        e_self = jnp.exp(s_self - m)
        den = jnp.sum(e, -1, keepdims=True) + e_self + jnp.exp(sink - m)
        oa_ref[i] = (_dot(e.astype(BF16), vb.astype(BF16)) + e_self * vnew) / den

        kout_ref[i, 0:WINDOW - 1, :] = kc_ref[i, 1:WINDOW, :]
        kout_ref[i, WINDOW - 1:WINDOW, :] = knew
        vout_ref[i, 0:WINDOW - 1, :] = vc_ref[i, 1:WINDOW, :]
        vout_ref[i, WINDOW - 1:WINDOW, :] = vnew


def _mix_sample(qt, kt, v4, g4, s_all, qb, kn, vn, kc_all, vc_all, p, bias8, layer, prev):
    nb = qt.shape[0]
    bb = SAMPLE_ROWS
    blk = lambda *dims: pl.BlockSpec((bb,) + dims, lambda i: (i,) + (0,) * len(dims))
    lblk = lambda *dims: pl.BlockSpec((None, bb) + dims, lambda i: (layer, i) + (0,) * len(dims))
    in_specs = [blk(RET_D, RET_HEADS), blk(RET_D, RET_HEADS), blk(RET_HEADS, RET_D), blk(RET_HEADS, RET_D),
                lblk(RET_WIDTH, RET_D), blk(8, LANES), blk(1, LANES), blk(1, LANES),
                lblk(WINDOW, LANES), lblk(WINDOW, LANES),
                _layer_spec((RET_HEADS, RET_D), layer), _layer_spec((RET_HEADS, RET_D), layer),
                _const_spec((8, LANES)), _layer_spec((8, LANES), layer)]
    args = [qt, kt, v4, g4, s_all, qb, kn, vn, kc_all, vc_all, p["gn_g4"], p["gn_b4"], bias8, p["sink8"]]
    stacked = lambda n_layers, *dims: pl.BlockSpec((n_layers, bb) + dims, lambda i: (0, i) + (0,) * len(dims))
    state_dims = ((RET_WIDTH, RET_D), (WINDOW, LANES), (WINDOW, LANES))
    if layer:
        in_specs += [stacked(layer, *dims) for dims in state_dims]
        args += list(prev)
    return pl.pallas_call(
        functools.partial(_mix_sample_kernel, layer=layer),
        grid=(nb // bb,),
        in_specs=in_specs,
        out_specs=[blk(RET_HEADS, RET_D), blk(8, LANES)] + [stacked(layer + 1, *dims) for dims in state_dims],
        out_shape=[jax.ShapeDtypeStruct((nb, RET_HEADS, RET_D), F32),
                   jax.ShapeDtypeStruct((nb, 8, LANES), F32)]
                  + [jax.ShapeDtypeStruct((layer + 1, nb) + dims, F32) for dims in state_dims],
        compiler_params=_cparams(1),
        name="mix_sample",
    )(*args)


def _outffn_kernel(z_ref, r_ref, a_ref, x_ref, wo_ref, g1_ref, b1_ref, w1_ref, w3_ref, w2_ref, g2_ref, b2_ref, o_ref):
    c1, c2 = SSM_WIDTH, SSM_WIDTH + RET_WIDTH
    mix = _dot(z_ref[...], wo_ref[0:c1]) + _dot(r_ref[...], wo_ref[c1:c2]) + _dot(a_ref[...], wo_ref[c2:D_MODEL])
    x = _layernorm(ALPHA * x_ref[...] + mix, g1_ref[...], b1_ref[...])
    xb = x.astype(BF16)
    h1 = _dot(xb, w1_ref[...])
    h3 = _dot(xb, w3_ref[...])
    hid = (h1 * _sigmoid(h1) * h3).astype(BF16)
    o_ref[...] = _layernorm(ALPHA * x + _dot(hid, w2_ref[...]), g2_ref[...], b2_ref[...])


def _outffn(z, r, a, x2, p, layer):
    t = x2.shape[0]
    tm = min(ROW_TILE, t)
    row = lambda c: pl.BlockSpec((tm, c), lambda i: (i, 0))
    vec = _layer_spec((1, D_MODEL), layer)
    return pl.pallas_call(
        _outffn_kernel,
        grid=(t // tm,),
        in_specs=[row(SSM_WIDTH), row(RET_WIDTH), row(ATT_WIDTH), row(D_MODEL),
                  _layer_spec((D_MODEL, D_MODEL), layer), vec, vec,
                  _layer_spec((D_MODEL, FFN_HIDDEN), layer), _layer_spec((D_MODEL, FFN_HIDDEN), layer),
                  _layer_spec((FFN_HIDDEN, D_MODEL), layer), vec, vec],
        out_specs=row(D_MODEL),
        out_shape=jax.ShapeDtypeStruct((t, D_MODEL), F32),
        compiler_params=_cparams(1),
        name="outffn",
    )(z, r, a, x2, p["w_out"], p["ln1_g"], p["ln1_b"], p["w1"], p["w3"], p["w2"], p["ln2_g"], p["ln2_b"])


def _prep_params(w_in, lam_re, lam_im, log_step, b_re, b_im, c_re, c_im, ssm_d, glu_w, glu_b,
                 gn_g, gn_b, sinks, w_out, ln1_g, ln1_b, w1, w3, w2, ln2_g, ln2_b):
    col_scale = np.ones((IN_WIDTH,), np.float32)
    col_scale[COL_RET + RET_WIDTH:COL_RET + 2 * RET_WIDTH] = RET_D ** -0.5
    col_scale[COL_ATT:COL_ATT + ATT_WIDTH] = ATT_D ** -0.5

    step = jnp.exp(log_step)[..., None]
    mag = jnp.exp(lam_re * step)
    a_re, a_im = mag * jnp.cos(lam_im * step), mag * jnp.sin(lam_im * step)
    den = lam_re * lam_re + lam_im * lam_im
    k_re = ((a_re - 1.0) * lam_re + a_im * lam_im) / den
    k_im = (a_im * lam_re - (a_re - 1.0) * lam_im) / den
    bb_re = k_re[..., None] * b_re - k_im[..., None] * b_im
    bb_im = k_re[..., None] * b_im + k_im[..., None] * b_re
    eye8 = jnp.eye(8, dtype=F32)

    def b_blocks(m):
        m = m.reshape(DEPTH, 3, 8, SSM_STATE, SSM_CH)
        return jnp.einsum("ligph,gk->lighkp", m, eye8).reshape(DEPTH, 3, LANES, 512).astype(BF16)

    def c_blocks(m):
        m = m.reshape(DEPTH, 3, 8, SSM_CH, SSM_STATE)
        return jnp.einsum("lighp,gk->likpgh", m, eye8).reshape(DEPTH, 3, 512, LANES).astype(BF16)

    row = lambda a: a.reshape(DEPTH, 1, -1)
    sink8 = jnp.concatenate([sinks, jnp.zeros((DEPTH, 2), F32)], axis=1)
    return dict(
        w_in=(w_in * col_scale).astype(BF16),
        bre=b_blocks(bb_re), bim=b_blocks(bb_im), cre=c_blocks(c_re), cim=c_blocks(-c_im),
        ar=row(a_re), ai=row(a_im),
        ssm_d=row(ssm_d), glu_w=glu_w.astype(BF16), glu_b=row(glu_b),
        gn_g=row(gn_g), gn_b=row(gn_b),
        gn_g4=gn_g.reshape(DEPTH, RET_HEADS, RET_D), gn_b4=gn_b.reshape(DEPTH, RET_HEADS, RET_D),
        sinks=sinks, sink8=jnp.broadcast_to(sink8[:, :, None], (DEPTH, 8, LANES)),
        w_out=w_out.astype(BF16), ln1_g=row(ln1_g), ln1_b=row(ln1_b),
        w1=w1.astype(BF16), w3=w3.astype(BF16), w2=w2.astype(BF16), ln2_g=row(ln2_g), ln2_b=row(ln2_b),
    )


def _prompt_layer(x3, p, layer):
    nb, length, _ = x3.shape
    t = nb * length
    x2 = x3.reshape(t, D_MODEL)
    pu, pr, pa = _inproj(x2, p, layer)
    z, hlast = _s5_prompt(pu.reshape(nb, length, SSM_WIDTH), p, layer)
    o_r, sblk = _ret_prompt(pr.reshape(nb, length, RET_COLS), p, layer)
    o_a, klast, vlast = _att_prompt(pa.reshape(nb, length, ATT_COLS), p, layer)
    y = _outffn(z.reshape(t, SSM_WIDTH), o_r.reshape(t, RET_WIDTH), o_a.reshape(t, ATT_WIDTH), x2, p, layer)
    y = y.reshape(nb, length, D_MODEL)
    s4 = jnp.stack([sblk[:, h // 2, HALF * (h % 2):HALF * (h % 2 + 1), HALF * (h % 2):HALF * (h % 2 + 1)]
                    for h in range(RET_HEADS)], axis=1)
    return y, hlast, s4, klast, vlast


def _sample_bias8():
    slopes = _alibi_slopes(ATT_Q_HEADS)
    dist = WINDOW - np.arange(WINDOW)
    bias = np.zeros((8, WINDOW))
    for h in range(ATT_Q_HEADS):
        bias[h] = -slopes[h] * dist
    return jnp.asarray(bias, F32)


def _sample_layer(x2, p, layer, h0r, h0i, s_all, kc_all, vc_all, prev):
    nb = x2.shape[0]
    pu, pr, pa = _inproj(x2, p, layer)
    z, h_re, h_im = _s5_sample(pu, h0r, h0i, p, layer)
    heads = lambda a: a.reshape(nb, RET_HEADS, RET_D)
    q4, k4, v4, g4 = (heads(pr[:, RET_WIDTH * i:RET_WIDTH * (i + 1)]) for i in range(4))
    q6 = pa[:, 0:ATT_WIDTH].reshape(nb, ATT_Q_HEADS, ATT_D)
    kv_of = np.arange(ATT_Q_HEADS) // ATT_GROUP
    qb = jnp.zeros((nb, 8, ATT_KV_HEADS, ATT_D), F32).at[:, np.arange(ATT_Q_HEADS), kv_of].set(q6)
    kn = pa[:, ATT_WIDTH:ATT_WIDTH + LANES].reshape(nb, 1, LANES)
    vn = pa[:, ATT_WIDTH + LANES:ATT_WIDTH + 2 * LANES].reshape(nb, 1, LANES)
    o_r, o8, s_new, k_out, v_out = _mix_sample(
        q4.transpose(0, 2, 1), k4.transpose(0, 2, 1), v4, g4, s_all, qb.reshape(nb, 8, LANES), kn, vn,
        kc_all, vc_all, p, _sample_bias8(), layer, prev)
    o_r = o_r.reshape(nb, RET_WIDTH).astype(BF16)
    o_a = o8.reshape(nb, 8, ATT_KV_HEADS, ATT_D)[:, np.arange(ATT_Q_HEADS), kv_of].reshape(nb, ATT_WIDTH).astype(BF16)
    y = _outffn(z, o_r, o_a, x2, p, layer)
    return y, h_re, h_im, (s_new, k_out, v_out)


def kernel(x_prompt, x_sample, state_ssm_re, state_ssm_im, state_ret, cache_win_k, cache_win_v, w_in, ssm_lambda_re, ssm_lambda_im, ssm_log_step, ssm_b_re, ssm_b_im, ssm_c_re, ssm_c_im, ssm_d, ssm_glu_w, ssm_glu_b, ret_gn_g, ret_gn_b, attn_sinks, w_out, ln1_g, ln1_b, ffn_w1, ffn_w3, ffn_w2, ln2_g, ln2_b):
    p = _prep_params(w_in, ssm_lambda_re, ssm_lambda_im, ssm_log_step, ssm_b_re, ssm_b_im, ssm_c_re, ssm_c_im,
                     ssm_d, ssm_glu_w, ssm_glu_b, ret_gn_g, ret_gn_b, attn_sinks, w_out, ln1_g, ln1_b,
                     ffn_w1, ffn_w3, ffn_w2, ln2_g, ln2_b)

    xp = x_prompt
    pb = x_prompt.shape[0]
    p_out = []
    for layer in range(DEPTH):
        xp, *states = _prompt_layer(xp, p, layer)
        p_out.append(states)
    stack = lambda outs, i: jnp.stack([o[i] for o in outs])
    p_h = stack(p_out, 0)
    p_kv_shape = (DEPTH, pb, TC, ATT_KV_HEADS, ATT_D)

    nb = x_sample.shape[0]
    xs = x_sample.reshape(nb, D_MODEL)
    h0r = state_ssm_re.reshape(DEPTH, nb, SSM_LANES)
    h0i = state_ssm_im.reshape(DEPTH, nb, SSM_LANES)
    s_all = state_ret.reshape(DEPTH, nb, RET_WIDTH, RET_D)
    kc_all = cache_win_k.reshape(DEPTH, nb, WINDOW, LANES)
    vc_all = cache_win_v.reshape(DEPTH, nb, WINDOW, LANES)
    prev = None
    s_h = []
    for layer in range(DEPTH):
        xs, h_re, h_im, prev = _sample_layer(xs, p, layer, h0r, h0i, s_all, kc_all, vc_all, prev)
        s_h.append((h_re, h_im))
    s_ret, s_k, s_v = prev
    ssm_shape = (DEPTH, nb, SSM_GROUPS, SSM_STATE)
    s_kv_shape = (DEPTH, nb, WINDOW, ATT_KV_HEADS, ATT_D)

    return (xp, xs.reshape(nb, 1, D_MODEL),
            p_h[:, :, :SSM_LANES].reshape(DEPTH, pb, SSM_GROUPS, SSM_STATE),
            p_h[:, :, SSM_LANES:].reshape(DEPTH, pb, SSM_GROUPS, SSM_STATE),
            stack(p_out, 1), stack(p_out, 2).reshape(p_kv_shape), stack(p_out, 3).reshape(p_kv_shape),
            stack(s_h, 0).reshape(ssm_shape), stack(s_h, 1).reshape(ssm_shape),
            s_ret.reshape(DEPTH, nb, RET_HEADS, RET_D, RET_D), s_k.reshape(s_kv_shape), s_v.reshape(s_kv_shape))
```

```python
import functools
import math

import numpy as np
import jax
import jax.numpy as jnp
from jax import lax
from jax.experimental import pallas as pl
from jax.experimental.pallas import tpu as pltpu

F32 = jnp.float32
BF16 = jnp.bfloat16

D_MODEL = 1024
DEPTH = 2
SSM_GROUPS = 24
SSM_CH = 16
SSM_STATE = 64
SSM_WIDTH = SSM_GROUPS * SSM_CH
SSM_LANES = SSM_GROUPS * SSM_STATE
RET_HEADS = 4
RET_D = 64
RET_WIDTH = RET_HEADS * RET_D
ATT_Q_HEADS = 6
ATT_KV_HEADS = 2
ATT_GROUP = ATT_Q_HEADS // ATT_KV_HEADS
ATT_D = 64
ATT_WIDTH = ATT_Q_HEADS * ATT_D
WINDOW = 128
FFN_HIDDEN = 2816
IN_WIDTH = 2048
ALPHA = (2 * DEPTH) ** 0.25
LN_EPS = 1e-5
NEG_INF = -1e30

COL_RET = SSM_WIDTH
COL_ATT = SSM_WIDTH + 4 * RET_WIDTH
RET_COLS = 4 * RET_WIDTH
ATT_COLS = ATT_WIDTH + 2 * ATT_KV_HEADS * ATT_D

TC = 128
LANES = 128
N_TILES = SSM_LANES // LANES
HALF = 64
ROW_TILE = 512
SAMPLE_ROWS = 16
VMEM_LIMIT = 56 * 2 ** 20


def _alibi_slopes(n):
    def pow2(k):
        start = 2.0 ** (-8.0 / k)
        return [start ** (i + 1) for i in range(k)]
    if n & (n - 1) == 0:
        return pow2(n)
    c = 2 ** int(math.floor(math.log2(n)))
    return pow2(c) + pow2(2 * c)[0::2][: n - c]


def _cparams(n_axes):
    return pltpu.CompilerParams(dimension_semantics=("arbitrary",) * n_axes, vmem_limit_bytes=VMEM_LIMIT)


def _const_spec(shape):
    nd = len(shape)
    return pl.BlockSpec(shape, lambda *_: (0,) * nd, pipeline_mode=pl.Buffered(1))


def _layer_spec(shape, layer):
    nd = len(shape)
    return pl.BlockSpec((None,) + tuple(shape), lambda *_: (layer,) + (0,) * nd, pipeline_mode=pl.Buffered(1))


def _dot(a, b):
    return jnp.dot(a, b, preferred_element_type=F32)


def _dot_nt(a, b):
    return lax.dot_general(a, b, (((1,), (1,)), ((), ())), preferred_element_type=F32)


def _dot_tn(a, b):
    return lax.dot_general(a, b, (((0,), (0,)), ((), ())), preferred_element_type=F32)


def _layernorm(v, g, b):
    mu = jnp.mean(v, -1, keepdims=True)
    d = v - mu
    var = jnp.mean(d * d, -1, keepdims=True)
    return d * lax.rsqrt(var + LN_EPS) * g + b


def _gelu_tanh(x):
    return 0.5 * x * (1.0 + jnp.tanh(math.sqrt(2.0 / math.pi) * (x + 0.044715 * (x * x * x))))


def _sigmoid(x):
    return 1.0 / (1.0 + jnp.exp(-x))


def _inproj_kernel(x_ref, w_ref, pu_ref, pr_ref, pa_ref):
    xb = x_ref[...].astype(BF16)
    pu_ref[...] = _dot(xb, w_ref[:, 0:COL_RET])
    pr_ref[...] = _dot(xb, w_ref[:, COL_RET:COL_ATT])
    pa_ref[...] = _dot(xb, w_ref[:, COL_ATT:IN_WIDTH])


def _inproj(x2, p, layer):
    t = x2.shape[0]
    tm = min(ROW_TILE, t)
    row = lambda c: pl.BlockSpec((tm, c), lambda i: (i, 0))
    return pl.pallas_call(
        _inproj_kernel,
        grid=(t // tm,),
        in_specs=[row(D_MODEL), _layer_spec((D_MODEL, IN_WIDTH), layer)],
        out_specs=[row(SSM_WIDTH), row(RET_COLS), row(ATT_COLS)],
        out_shape=[jax.ShapeDtypeStruct((t, SSM_WIDTH), F32),
                   jax.ShapeDtypeStruct((t, RET_COLS), F32),
                   jax.ShapeDtypeStruct((t, ATT_COLS), F32)],
        compiler_params=_cparams(1),
        name="inproj",
    )(x2, p["w_in"])


def _s5_bu(ub, bre_ref, bim_ref):
    res, ims = [], []
    for i in range(3):
        ui = ub[:, LANES * i:LANES * (i + 1)]
        res.append(_dot(ui, bre_ref[i]))
        ims.append(_dot(ui, bim_ref[i]))
    return res, ims


def _s5_out(u, h_re, h_im, cre_ref, cim_ref, d_ref, gw_ref, gb_ref):
    ys = []
    for i in range(3):
        ys.append(_dot(h_re[i].astype(BF16), cre_ref[i]) + _dot(h_im[i].astype(BF16), cim_ref[i]))
    y = jnp.concatenate(ys, axis=1) + d_ref[...] * u
    z = _gelu_tanh(y)
    gate = _sigmoid(_dot(z.astype(BF16), gw_ref[...]) + gb_ref[...])
    return z * gate


def _s5_prompt_kernel(u_ref, bre_ref, bim_ref, cre_ref, cim_ref, ar_ref, ai_ref, d_ref, gw_ref, gb_ref,
                      z_ref, hlast_ref, h_scr, hc_scr, bt_scr, tm_scr):
    n = pl.program_id(0)
    nb = u_ref.shape[0]
    rows = nb * TC

    @pl.when(n == 0)
    def _():
        hc_scr[...] = jnp.zeros_like(hc_scr)

    u_bt = u_ref[...].reshape(rows, SSM_WIDTH)
    for c in range(3):
        bt_scr[c] = u_bt[:, LANES * c:LANES * (c + 1)]

    def to_time_major(t, carry):
        r0 = pl.multiple_of(t * nb, nb)
        for c in range(3):
            tm_scr[c, pl.ds(r0, nb), :] = bt_scr[c, pl.ds(t, nb, stride=TC), :]
        return carry

    lax.fori_loop(0, TC, to_time_major, 0, unroll=4)
    u = jnp.concatenate([tm_scr[c] for c in range(3)], axis=1)

    res, ims = _s5_bu(u.astype(BF16), bre_ref, bim_ref)
    for i in range(3):
        for k in range(4):
            h_scr[4 * i + k] = res[i][:, LANES * k:LANES * (k + 1)]
            h_scr[N_TILES + 4 * i + k] = ims[i][:, LANES * k:LANES * (k + 1)]

    def step(t, carry):
        r0 = pl.multiple_of(t * nb, nb)
        out = []
        for c in range(N_TILES):
            sl = slice(LANES * c, LANES * (c + 1))
            ar, ai = ar_ref[:, sl], ai_ref[:, sl]
            hr, hi = carry[c], carry[N_TILES + c]
            nr = ar * hr - ai * hi + h_scr[c, pl.ds(r0, nb), :]
            ni = ar * hi + ai * hr + h_scr[N_TILES + c, pl.ds(r0, nb), :]
            h_scr[c, pl.ds(r0, nb), :] = nr
            h_scr[N_TILES + c, pl.ds(r0, nb), :] = ni
            out.append((nr, ni))
        return tuple(o[0] for o in out) + tuple(o[1] for o in out)

    init = tuple(hc_scr[:, LANES * c:LANES * (c + 1)] for c in range(2 * N_TILES))
    fin = lax.fori_loop(0, TC, step, init, unroll=2)
    for c in range(2 * N_TILES):
        hc_scr[:, LANES * c:LANES * (c + 1)] = fin[c]
        hlast_ref[:, LANES * c:LANES * (c + 1)] = fin[c]

    h_re = [jnp.concatenate([h_scr[4 * i + k] for k in range(4)], axis=1) for i in range(3)]
    h_im = [jnp.concatenate([h_scr[N_TILES + 4 * i + k] for k in range(4)], axis=1) for i in range(3)]
    zz = _s5_out(u, h_re, h_im, cre_ref, cim_ref, d_ref, gw_ref, gb_ref)
    for c in range(3):
        tm_scr[c] = zz[:, LANES * c:LANES * (c + 1)]

    def to_batch_major(t, carry):
        r0 = pl.multiple_of(t * nb, nb)
        for c in range(3):
            bt_scr[c, pl.ds(t, nb, stride=TC), :] = tm_scr[c, pl.ds(r0, nb), :]
        return carry

    lax.fori_loop(0, TC, to_batch_major, 0, unroll=4)
    z_bt = jnp.concatenate([bt_scr[c] for c in range(3)], axis=1)
    z_ref[...] = z_bt.astype(BF16).reshape(nb, TC, SSM_WIDTH)


def _s5_weight_specs(layer):
    ls = lambda *shape: _layer_spec(shape, layer)
    return [ls(3, LANES, 512), ls(3, LANES, 512), ls(3, 512, LANES), ls(3, 512, LANES),
            ls(1, SSM_LANES), ls(1, SSM_LANES), ls(1, SSM_WIDTH), ls(SSM_WIDTH, SSM_WIDTH), ls(1, SSM_WIDTH)]


def _s5_weights(p):
    return (p["bre"], p["bim"], p["cre"], p["cim"], p["ar"], p["ai"], p["ssm_d"], p["glu_w"], p["glu_b"])


def _s5_prompt(u3, p, layer):
    nb, length, _ = u3.shape
    return pl.pallas_call(
        _s5_prompt_kernel,
        grid=(length // TC,),
        in_specs=[pl.BlockSpec((nb, TC, SSM_WIDTH), lambda n: (0, n, 0))] + _s5_weight_specs(layer),
        out_specs=[pl.BlockSpec((nb, TC, SSM_WIDTH), lambda n: (0, n, 0)),
                   pl.BlockSpec((nb, 2 * SSM_LANES), lambda n: (0, 0))],
        out_shape=[jax.ShapeDtypeStruct((nb, length, SSM_WIDTH), BF16),
                   jax.ShapeDtypeStruct((nb, 2 * SSM_LANES), F32)],
        scratch_shapes=[pltpu.VMEM((2 * N_TILES, nb * TC, LANES), F32), pltpu.VMEM((nb, 2 * SSM_LANES), F32),
                        pltpu.VMEM((3, nb * TC, LANES), F32), pltpu.VMEM((3, nb * TC, LANES), F32)],
        compiler_params=_cparams(1),
        name="s5_prompt",
    )(u3, *_s5_weights(p))


def _s5_sample_kernel(u_ref, h0r_ref, h0i_ref, bre_ref, bim_ref, cre_ref, cim_ref, ar_ref, ai_ref, d_ref,
                      gw_ref, gb_ref, z_ref, hr_ref, hi_ref):
    u = u_ref[...]
    res, ims = _s5_bu(u.astype(BF16), bre_ref, bim_ref)
    h_re, h_im = [], []
    for i in range(3):
        sl = slice(512 * i, 512 * (i + 1))
        ar, ai = ar_ref[:, sl], ai_ref[:, sl]
        h0r, h0i = h0r_ref[:, sl], h0i_ref[:, sl]
        nr = ar * h0r - ai * h0i + res[i]
        ni = ar * h0i + ai * h0r + ims[i]
        hr_ref[:, sl] = nr
        hi_ref[:, sl] = ni
        h_re.append(nr)
        h_im.append(ni)
    z_ref[...] = _s5_out(u, h_re, h_im, cre_ref, cim_ref, d_ref, gw_ref, gb_ref).astype(BF16)


def _s5_sample(u, h0r, h0i, p, layer):
    nb = u.shape[0]
    full = lambda c: pl.BlockSpec((nb, c), lambda i: (0, 0))
    state = pl.BlockSpec((None, nb, SSM_LANES), lambda i: (layer, 0, 0))
    return pl.pallas_call(
        _s5_sample_kernel,
        grid=(1,),
        in_specs=[full(SSM_WIDTH), state, state] + _s5_weight_specs(layer),
        out_specs=[full(SSM_WIDTH), full(SSM_LANES), full(SSM_LANES)],
        out_shape=[jax.ShapeDtypeStruct((nb, SSM_WIDTH), BF16),
                   jax.ShapeDtypeStruct((nb, SSM_LANES), F32),
                   jax.ShapeDtypeStruct((nb, SSM_LANES), F32)],
        compiler_params=_cparams(1),
        name="s5_sample",
    )(u, h0r, h0i, *_s5_weights(p))


def _ret_gammas():
    return [1.0 - 2.0 ** (-5.0 - h) for h in range(RET_HEADS)]


def _ret_consts():
    log_g = np.log1p(-(2.0 ** (-5.0 - np.arange(RET_HEADS, dtype=np.float64))))
    idx = np.arange(TC)
    diff = idx[:, None] - idx[None, :]
    dmask = np.where(diff >= 0, np.exp(log_g[:, None, None] * np.maximum(diff, 0)), 0.0)
    inner = np.exp(log_g[:, None] * (idx + 1))
    kvd = np.exp(log_g[:, None] * (TC - 1 - idx))
    cd = np.exp(log_g * TC)
    indec = np.zeros((2, TC, LANES))
    kvdec = np.zeros((2, TC, LANES))
    cdec = np.zeros((2, LANES, LANES))
    blk = np.zeros((LANES, LANES))
    for j in range(2):
        for hh in range(2):
            sl = slice(HALF * hh, HALF * (hh + 1))
            indec[j, :, sl] = inner[2 * j + hh][:, None]
            kvdec[j, :, sl] = kvd[2 * j + hh][:, None]
            cdec[j, sl, sl] = cd[2 * j + hh]
            blk[sl, sl] = 1.0
    as32 = lambda a: jnp.asarray(a, F32)
    return as32(dmask), as32(kvdec), as32(indec), as32(cdec), as32(blk)


def _halfnorm(o, lo):
    zero = jnp.zeros_like(o)
    s_lo = jnp.sum(jnp.where(lo, o, zero), -1, keepdims=True)
    s_hi = jnp.sum(jnp.where(lo, zero, o), -1, keepdims=True)
    d = o - jnp.where(lo, s_lo, s_hi) * (1.0 / HALF)
    d2 = d * d
    v_lo = jnp.sum(jnp.where(lo, d2, zero), -1, keepdims=True)
    v_hi = jnp.sum(jnp.where(lo, zero, d2), -1, keepdims=True)
    var = jnp.where(lo, v_lo, v_hi) * (1.0 / HALF)
    return d * lax.rsqrt(var + LN_EPS)


def _ret_prompt_kernel(pr_ref, dmask_ref, kvdec_ref, indec_ref, cdec_ref, blk_ref, gng_ref, gnb_ref,
                       o_ref, slast_ref, s_scr):
    n = pl.program_id(0)
    nb = pr_ref.shape[0]

    @pl.when(n == 0)
    def _():
        s_scr[...] = jnp.zeros_like(s_scr)

    lo = lax.broadcasted_iota(jnp.int32, (TC, LANES), 1) < HALF
    zero = jnp.zeros((TC, LANES), F32)
    for b in range(nb):
        for j in range(2):
            q2 = pr_ref[b, :, LANES * j:LANES * (j + 1)]
            k2 = pr_ref[b, :, 256 + LANES * j:256 + LANES * (j + 1)]
            v2 = pr_ref[b, :, 512 + LANES * j:512 + LANES * (j + 1)]
            g2 = pr_ref[b, :, 768 + LANES * j:768 + LANES * (j + 1)]
            k2b = k2.astype(BF16)
            v2b = v2.astype(BF16)
            state = s_scr[b, j]
            qs = jnp.concatenate([jnp.where(lo, q2, zero), jnp.where(lo, zero, q2)], axis=0).astype(BF16)
            sc = _dot_nt(qs, k2b)
            qd = (q2 * indec_ref[j]).astype(BF16)
            lhs = jnp.concatenate(
                [jnp.concatenate([(sc[hh * TC:(hh + 1) * TC] * dmask_ref[2 * j + hh]).astype(BF16), qd], axis=1)
                 for hh in range(2)], axis=0)
            o2 = _dot(lhs, jnp.concatenate([v2b, state.astype(BF16)], axis=0))
            o = jnp.where(lo, o2[0:TC], o2[TC:2 * TC])
            kd = (k2 * kvdec_ref[j]).astype(BF16)
            s_new = cdec_ref[j] * state + blk_ref[...] * _dot_tn(kd, v2b)
            s_scr[b, j] = s_new
            slast_ref[b, j] = s_new
            sl = slice(LANES * j, LANES * (j + 1))
            o = _halfnorm(o, lo) * gng_ref[:, sl] + gnb_ref[:, sl]
            o_ref[b, :, sl] = (g2 * _sigmoid(g2) * o).astype(BF16)


def _ret_prompt(pr3, p, layer):
    nb, length, _ = pr3.shape
    dmask, kvdec, indec, cdec, blk = _ret_consts()
    return pl.pallas_call(
        _ret_prompt_kernel,
        grid=(length // TC,),
        in_specs=[pl.BlockSpec((nb, TC, RET_COLS), lambda n: (0, n, 0)),
                  _const_spec((RET_HEADS, TC, TC)), _const_spec((2, TC, LANES)), _const_spec((2, TC, LANES)),
                  _const_spec((2, LANES, LANES)), _const_spec((LANES, LANES)),
                  _layer_spec((1, RET_WIDTH), layer), _layer_spec((1, RET_WIDTH), layer)],
        out_specs=[pl.BlockSpec((nb, TC, RET_WIDTH), lambda n: (0, n, 0)),
                   pl.BlockSpec((nb, 2, LANES, LANES), lambda n: (0, 0, 0, 0))],
        out_shape=[jax.ShapeDtypeStruct((nb, length, RET_WIDTH), BF16),
                   jax.ShapeDtypeStruct((nb, 2, LANES, LANES), F32)],
        scratch_shapes=[pltpu.VMEM((nb, 2, LANES, LANES), F32)],
        compiler_params=_cparams(1),
        name="ret_prompt",
    )(pr3, dmask, kvdec, indec, cdec, blk, p["gn_g"], p["gn_b"])


def _att_bias():
    slopes = _alibi_slopes(ATT_Q_HEADS)
    qpos = np.arange(TC)[:, None] + TC
    kpos = np.arange(2 * TC)[None, :]
    dist = qpos - kpos
    valid = (dist >= 0) & (dist <= WINDOW)
    prev_cur = np.stack([np.where(valid, -slopes[h] * dist, NEG_INF) for h in range(ATT_Q_HEADS)])
    cur_prev = np.concatenate([prev_cur[:, :, TC:], prev_cur[:, :, :TC]], axis=2)
    first = np.where(np.arange(2 * TC) >= TC, NEG_INF, 0.0)[None, :]
    return jnp.asarray(cur_prev, F32), jnp.asarray(prev_cur, F32), jnp.asarray(first, F32)


def _att_prompt_kernel(sink_ref, pa_ref, bias_cp_ref, bias_pc_ref, first_ref, o_ref, kout_ref, vout_ref,
                       k_scr, v_scr, bias_scr, *, layer):
    n = pl.program_id(0)
    nb = pa_ref.shape[0]

    @pl.when(n == 0)
    def _():
        k_scr[...] = jnp.zeros_like(k_scr)
        v_scr[:, :, 0:LANES] = jnp.zeros((nb, 2 * TC, LANES), BF16)
        v_scr[:, :, LANES:2 * LANES] = jnp.ones((nb, 2 * TC, LANES), BF16)

    even = n % 2 == 0
    row0 = pl.multiple_of((n % 2) * TC, TC)
    first = jnp.where(n == 0, first_ref[...], jnp.zeros_like(first_ref))
    for head in range(ATT_Q_HEADS):
        bias_scr[head] = jnp.where(even, bias_cp_ref[head], bias_pc_ref[head]) + first

    lo = lax.broadcasted_iota(jnp.int32, (TC, LANES), 1) < HALF
    zero = jnp.zeros((TC, LANES), F32)
    swap = lambda a: pltpu.roll(a, HALF, 1)
    for b in range(nb):
        kcur = pa_ref[b, :, ATT_WIDTH:ATT_WIDTH + LANES]
        vcur = pa_ref[b, :, ATT_WIDTH + LANES:ATT_WIDTH + 2 * LANES]
        kout_ref[b] = kcur
        vout_ref[b] = vcur
        k_scr[b, pl.ds(row0, TC), :] = kcur.astype(BF16)
        v_scr[b, pl.ds(row0, TC), 0:LANES] = vcur.astype(BF16)
        rows, moved = [], []
        for head in range(ATT_Q_HEADS):
            qcol = pa_ref[b, :, LANES * (head // 2):LANES * (head // 2 + 1)]
            kv = head // ATT_GROUP
            moved.append(head % 2 != kv)
            if moved[-1]:
                qcol = swap(qcol)
            rows.append(jnp.where(lo, qcol, zero) if kv == 0 else jnp.where(lo, zero, qcol))
        qs = jnp.concatenate(rows, axis=0).astype(BF16)
        s_all = _dot_nt(qs, k_scr[b])
        es, ms = [], []
        for head in range(ATT_Q_HEADS):
            s = s_all[head * TC:(head + 1) * TC] + bias_scr[head]
            m = jnp.maximum(jnp.max(s, -1, keepdims=True), sink_ref[layer, head])
            es.append(jnp.exp(s - m).astype(BF16))
            ms.append(m)
        r_all = _dot(jnp.concatenate(es, axis=0), v_scr[b])
        outs = []
        for head in range(ATT_Q_HEADS):
            r = r_all[head * TC:(head + 1) * TC]
            den = r[:, LANES:2 * LANES] + jnp.exp(sink_ref[layer, head] - ms[head])
            out = r[:, 0:LANES] / den
            outs.append(swap(out) if moved[head] else out)
        for c in range(3):
            o_ref[b, :, LANES * c:LANES * (c + 1)] = jnp.where(lo, outs[2 * c], outs[2 * c + 1]).astype(BF16)


def _att_prompt(pa3, p, layer):
    nb, length, _ = pa3.shape
    bias_cp, bias_pc, first = _att_bias()
    kv_scr = lambda width: pltpu.VMEM((nb, 2 * TC, width), BF16)
    return pl.pallas_call(
        functools.partial(_att_prompt_kernel, layer=layer),
        grid=(length // TC,),
        in_specs=[pl.BlockSpec(memory_space=pltpu.SMEM),
                  pl.BlockSpec((nb, TC, ATT_COLS), lambda n: (0, n, 0)),
                  _const_spec((ATT_Q_HEADS, TC, 2 * TC)), _const_spec((ATT_Q_HEADS, TC, 2 * TC)),
                  _const_spec((1, 2 * TC))],
        out_specs=[pl.BlockSpec((nb, TC, ATT_WIDTH), lambda n: (0, n, 0)),
                   pl.BlockSpec((nb, TC, LANES), lambda n: (0, 0, 0)),
                   pl.BlockSpec((nb, TC, LANES), lambda n: (0, 0, 0))],
        out_shape=[jax.ShapeDtypeStruct((nb, length, ATT_WIDTH), BF16),
                   jax.ShapeDtypeStruct((nb, TC, LANES), F32),
                   jax.ShapeDtypeStruct((nb, TC, LANES), F32)],
        scratch_shapes=[kv_scr(LANES), kv_scr(2 * LANES),
                        pltpu.VMEM((ATT_Q_HEADS, TC, 2 * TC), F32)],
        compiler_params=_cparams(1),
        name="att_prompt",
    )(p["sinks"], pa3, bias_cp, bias_pc, first)


def _mix_sample_kernel(qt_ref, kt_ref, v4_ref, g4_ref, s_ref, qb_ref, kn_ref, vn_ref, kc_ref, vc_ref,
                       gng_ref, gnb_ref, bias_ref, sink_ref, *rest, layer):
    or_ref, oa_ref, sn_all, kout_all, vout_all = rest[-5:]
    if layer:
        for prev_ref, out_all in zip(rest[:3], (sn_all, kout_all, vout_all)):
            out_all[0:layer] = prev_ref[...]
    sn_ref, kout_ref, vout_ref = sn_all.at[layer], kout_all.at[layer], vout_all.at[layer]
    gammas = _ret_gammas()
    sink = sink_ref[:, 0:1]
    bias = bias_ref[...]

    for i in range(SAMPLE_ROWS):
        qt = qt_ref[i]
        kt = kt_ref[i]
        v4 = v4_ref[i]
        g4 = g4_ref[i]
        rows = []
        for h in range(RET_HEADS):
            sl = slice(RET_D * h, RET_D * (h + 1))
            sn = gammas[h] * s_ref[i, sl, :] + kt[:, h:h + 1] * v4[h:h + 1, :]
            sn_ref[i, sl, :] = sn
            rows.append(jnp.sum(qt[:, h:h + 1] * sn, axis=0, keepdims=True))
        o4 = jnp.concatenate(rows, axis=0)
        mu = jnp.mean(o4, -1, keepdims=True)
        d = o4 - mu
        var = jnp.mean(d * d, -1, keepdims=True)
        o4 = d * lax.rsqrt(var + LN_EPS) * gng_ref[...] + gnb_ref[...]
        or_ref[i] = g4 * _sigmoid(g4) * o4

        kb = kc_ref[i]
        vb = vc_ref[i]
        q8 = qb_ref[i]
        knew = kn_ref[i]
        vnew = vn_ref[i]
        s = _dot_nt(q8.astype(BF16), kb.astype(BF16)) + bias
        s_self = jnp.sum(q8 * knew, -1, keepdims=True)
        m = jnp.maximum(jnp.maximum(jnp.max(s, -1, keepdims=True), s_self), sink)
        e = jnp.exp(s - m)
        e_self = jnp.exp(s_self - m)
        den = jnp.sum(e, -1, keepdims=True) + e_self + jnp.exp(sink - m)
        oa_ref[i] = (_dot(e.astype(BF16), vb.astype(BF16)) + e_self * vnew) / den

        kout_ref[i, 0:WINDOW - 1, :] = kc_ref[i, 1:WINDOW, :]
        kout_ref[i, WINDOW - 1:WINDOW, :] = knew
        vout_ref[i, 0:WINDOW - 1, :] = vc_ref[i, 1:WINDOW, :]
        vout_ref[i, WINDOW - 1:WINDOW, :] = vnew


def _mix_sample(qt, kt, v4, g4, s_all, qb, kn, vn, kc_all, vc_all, p, bias8, layer, prev):
    nb = qt.shape[0]
    bb = SAMPLE_ROWS
    blk = lambda *dims: pl.BlockSpec((bb,) + dims, lambda i: (i,) + (0,) * len(dims))
    lblk = lambda *dims: pl.BlockSpec((None, bb) + dims, lambda i: (layer, i) + (0,) * len(dims))
    in_specs = [blk(RET_D, RET_HEADS), blk(RET_D, RET_HEADS), blk(RET_HEADS, RET_D), blk(RET_HEADS, RET_D),
                lblk(RET_WIDTH, RET_D), blk(8, LANES), blk(1, LANES), blk(1, LANES),
                lblk(WINDOW, LANES), lblk(WINDOW, LANES),
                _layer_spec((RET_HEADS, RET_D), layer), _layer_spec((RET_HEADS, RET_D), layer),
                _const_spec((8, LANES)), _layer_spec((8, LANES), layer)]
    args = [qt, kt, v4, g4, s_all, qb, kn, vn, kc_all, vc_all, p["gn_g4"], p["gn_b4"], bias8, p["sink8"]]
    stacked = lambda n_layers, *dims: pl.BlockSpec((n_layers, bb) + dims, lambda i: (0, i) + (0,) * len(dims))
    state_dims = ((RET_WIDTH, RET_D), (WINDOW, LANES), (WINDOW, LANES))
    if layer:
        in_specs += [stacked(layer, *dims) for dims in state_dims]
        args += list(prev)
    return pl.pallas_call(
        functools.partial(_mix_sample_kernel, layer=layer),
        grid=(nb // bb,),
        in_specs=in_specs,
        out_specs=[blk(RET_HEADS, RET_D), blk(8, LANES)] + [stacked(layer + 1, *dims) for dims in state_dims],
        out_shape=[jax.ShapeDtypeStruct((nb, RET_HEADS, RET_D), F32),
                   jax.ShapeDtypeStruct((nb, 8, LANES), F32)]
                  + [jax.ShapeDtypeStruct((layer + 1, nb) + dims, F32) for dims in state_dims],
        compiler_params=_cparams(1),
        name="mix_sample",
    )(*args)


def _outffn_kernel(z_ref, r_ref, a_ref, x_ref, wo_ref, g1_ref, b1_ref, w1_ref, w3_ref, w2_ref, g2_ref, b2_ref, o_ref):
    c1, c2 = SSM_WIDTH, SSM_WIDTH + RET_WIDTH
    mix = _dot(z_ref[...], wo_ref[0:c1]) + _dot(r_ref[...], wo_ref[c1:c2]) + _dot(a_ref[...], wo_ref[c2:D_MODEL])
    x = _layernorm(ALPHA * x_ref[...] + mix, g1_ref[...], b1_ref[...])
    xb = x.astype(BF16)
    h1 = _dot(xb, w1_ref[...])
    h3 = _dot(xb, w3_ref[...])
    hid = (h1 * _sigmoid(h1) * h3).astype(BF16)
    o_ref[...] = _layernorm(ALPHA * x + _dot(hid, w2_ref[...]), g2_ref[...], b2_ref[...])


def _outffn(z, r, a, x2, p, layer):
    t = x2.shape[0]
    tm = min(ROW_TILE, t)
    row = lambda c: pl.BlockSpec((tm, c), lambda i: (i, 0))
    vec = _layer_spec((1, D_MODEL), layer)
    return pl.pallas_call(
        _outffn_kernel,
        grid=(t // tm,),
        in_specs=[row(SSM_WIDTH), row(RET_WIDTH), row(ATT_WIDTH), row(D_MODEL),
                  _layer_spec((D_MODEL, D_MODEL), layer), vec, vec,
                  _layer_spec((D_MODEL, FFN_HIDDEN), layer), _layer_spec((D_MODEL, FFN_HIDDEN), layer),
                  _layer_spec((FFN_HIDDEN, D_MODEL), layer), vec, vec],
        out_specs=row(D_MODEL),
        out_shape=jax.ShapeDtypeStruct((t, D_MODEL), F32),
        compiler_params=_cparams(1),
        name="outffn",
    )(z, r, a, x2, p["w_out"], p["ln1_g"], p["ln1_b"], p["w1"], p["w3"], p["w2"], p["ln2_g"], p["ln2_b"])


def _prep_params(w_in, lam_re, lam_im, log_step, b_re, b_im, c_re, c_im, ssm_d, glu_w, glu_b,
                 gn_g, gn_b, sinks, w_out, ln1_g, ln1_b, w1, w3, w2, ln2_g, ln2_b):
    col_scale = np.ones((IN_WIDTH,), np.float32)
    col_scale[COL_RET + RET_WIDTH:COL_RET + 2 * RET_WIDTH] = RET_D ** -0.5
    col_scale[COL_ATT:COL_ATT + ATT_WIDTH] = ATT_D ** -0.5

    step = jnp.exp(log_step)[..., None]
    mag = jnp.exp(lam_re * step)
    a_re, a_im = mag * jnp.cos(lam_im * step), mag * jnp.sin(lam_im * step)
    den = lam_re * lam_re + lam_im * lam_im
    k_re = ((a_re - 1.0) * lam_re + a_im * lam_im) / den
    k_im = (a_im * lam_re - (a_re - 1.0) * lam_im) / den
    bb_re = k_re[..., None] * b_re - k_im[..., None] * b_im
    bb_im = k_re[..., None] * b_im + k_im[..., None] * b_re
    eye8 = jnp.eye(8, dtype=F32)

    def b_blocks(m):
        m = m.reshape(DEPTH, 3, 8, SSM_STATE, SSM_CH)
        return jnp.einsum("ligph,gk->lighkp", m, eye8).reshape(DEPTH, 3, LANES, 512).astype(BF16)

    def c_blocks(m):
        m = m.reshape(DEPTH, 3, 8, SSM_CH, SSM_STATE)
        return jnp.einsum("lighp,gk->likpgh", m, eye8).reshape(DEPTH, 3, 512, LANES).astype(BF16)

    row = lambda a: a.reshape(DEPTH, 1, -1)
    sink8 = jnp.concatenate([sinks, jnp.zeros((DEPTH, 2), F32)], axis=1)
    return dict(
        w_in=(w_in * col_scale).astype(BF16),
        bre=b_blocks(bb_re), bim=b_blocks(bb_im), cre=c_blocks(c_re), cim=c_blocks(-c_im),
        ar=row(a_re), ai=row(a_im),
        ssm_d=row(ssm_d), glu_w=glu_w.astype(BF16), glu_b=row(glu_b),
        gn_g=row(gn_g), gn_b=row(gn_b),
        gn_g4=gn_g.reshape(DEPTH, RET_HEADS, RET_D), gn_b4=gn_b.reshape(DEPTH, RET_HEADS, RET_D),
        sinks=sinks, sink8=jnp.broadcast_to(sink8[:, :, None], (DEPTH, 8, LANES)),
        w_out=w_out.astype(BF16), ln1_g=row(ln1_g), ln1_b=row(ln1_b),
        w1=w1.astype(BF16), w3=w3.astype(BF16), w2=w2.astype(BF16), ln2_g=row(ln2_g), ln2_b=row(ln2_b),
    )


def _prompt_layer(x3, p, layer):
    nb, length, _ = x3.shape
    t = nb * length
    x2 = x3.reshape(t, D_MODEL)
    pu, pr, pa = _inproj(x2, p, layer)
    z, hlast = _s5_prompt(pu.reshape(nb, length, SSM_WIDTH), p, layer)
    o_r, sblk = _ret_prompt(pr.reshape(nb, length, RET_COLS), p, layer)
    o_a, klast, vlast = _att_prompt(pa.reshape(nb, length, ATT_COLS), p, layer)
    y = _outffn(z.reshape(t, SSM_WIDTH), o_r.reshape(t, RET_WIDTH), o_a.reshape(t, ATT_WIDTH), x2, p, layer)
    y = y.reshape(nb, length, D_MODEL)
    s4 = jnp.stack([sblk[:, h // 2, HALF * (h % 2):HALF * (h % 2 + 1), HALF * (h % 2):HALF * (h % 2 + 1)]
                    for h in range(RET_HEADS)], axis=1)
    return y, hlast, s4, klast, vlast


def _sample_bias8():
    slopes = _alibi_slopes(ATT_Q_HEADS)
    dist = WINDOW - np.arange(WINDOW)
    bias = np.zeros((8, WINDOW))
    for h in range(ATT_Q_HEADS):
        bias[h] = -slopes[h] * dist
    return jnp.asarray(bias, F32)


def _sample_layer(x2, p, layer, h0r, h0i, s_all, kc_all, vc_all, prev):
    nb = x2.shape[0]
    pu, pr, pa = _inproj(x2, p, layer)
    z, h_re, h_im = _s5_sample(pu, h0r, h0i, p, layer)
    heads = lambda a: a.reshape(nb, RET_HEADS, RET_D)
    q4, k4, v4, g4 = (heads(pr[:, RET_WIDTH * i:RET_WIDTH * (i + 1)]) for i in range(4))
    q6 = pa[:, 0:ATT_WIDTH].reshape(nb, ATT_Q_HEADS, ATT_D)
    kv_of = np.arange(ATT_Q_HEADS) // ATT_GROUP
    qb = jnp.zeros((nb, 8, ATT_KV_HEADS, ATT_D), F32).at[:, np.arange(ATT_Q_HEADS), kv_of].set(q6)
    kn = pa[:, ATT_WIDTH:ATT_WIDTH + LANES].reshape(nb, 1, LANES)
    vn = pa[:, ATT_WIDTH + LANES:ATT_WIDTH + 2 * LANES].reshape(nb, 1, LANES)
    o_r, o8, s_new, k_out, v_out = _mix_sample(
        q4.transpose(0, 2, 1), k4.transpose(0, 2, 1), v4, g4, s_all, qb.reshape(nb, 8, LANES), kn, vn,
        kc_all, vc_all, p, _sample_bias8(), layer, prev)
    o_r = o_r.reshape(nb, RET_WIDTH).astype(BF16)
    o_a = o8.reshape(nb, 8, ATT_KV_HEADS, ATT_D)[:, np.arange(ATT_Q_HEADS), kv_of].reshape(nb, ATT_WIDTH).astype(BF16)
    y = _outffn(z, o_r, o_a, x2, p, layer)
    return y, h_re, h_im, (s_new, k_out, v_out)


def kernel(x_prompt, x_sample, state_ssm_re, state_ssm_im, state_ret, cache_win_k, cache_win_v, w_in, ssm_lambda_re, ssm_lambda_im, ssm_log_step, ssm_b_re, ssm_b_im, ssm_c_re, ssm_c_im, ssm_d, ssm_glu_w, ssm_glu_b, ret_gn_g, ret_gn_b, attn_sinks, w_out, ln1_g, ln1_b, ffn_w1, ffn_w3, ffn_w2, ln2_g, ln2_b):
    p = _prep_params(w_in, ssm_lambda_re, ssm_lambda_im, ssm_log_step, ssm_b_re, ssm_b_im, ssm_c_re, ssm_c_im,
                     ssm_d, ssm_glu_w, ssm_glu_b, ret_gn_g, ret_gn_b, attn_sinks, w_out, ln1_g, ln1_b,
                     ffn_w1, ffn_w3, ffn_w2, ln2_g, ln2_b)

    xp = x_prompt
    pb = x_prompt.shape[0]
    p_out = []
    for layer in range(DEPTH):
        xp, *states = _prompt_layer(xp, p, layer)
        p_out.append(states)
    stack = lambda outs, i: jnp.stack([o[i] for o in outs])
    p_h = stack(p_out, 0)
    p_kv_shape = (DEPTH, pb, TC, ATT_KV_HEADS, ATT_D)

    nb = x_sample.shape[0]
    xs = x_sample.reshape(nb, D_MODEL)
    h0r = state_ssm_re.reshape(DEPTH, nb, SSM_LANES)
    h0i = state_ssm_im.reshape(DEPTH, nb, SSM_LANES)
    s_all = state_ret.reshape(DEPTH, nb, RET_WIDTH, RET_D)
    kc_all = cache_win_k.reshape(DEPTH, nb, WINDOW, LANES)
    vc_all = cache_win_v.reshape(DEPTH, nb, WINDOW, LANES)
    prev = None
    s_h = []
    for layer in range(DEPTH):
        xs, h_re, h_im, prev = _sample_layer(xs, p, layer, h0r, h0i, s_all, kc_all, vc_all, prev)
        s_h.append((h_re, h_im))
    s_ret, s_k, s_v = prev
    ssm_shape = (DEPTH, nb, SSM_GROUPS, SSM_STATE)
    s_kv_shape = (DEPTH, nb, WINDOW, ATT_KV_HEADS, ATT_D)

    return (xp, xs.reshape(nb, 1, D_MODEL),
            p_h[:, :, :SSM_LANES].reshape(DEPTH, pb, SSM_GROUPS, SSM_STATE),
            p_h[:, :, SSM_LANES:].reshape(DEPTH, pb, SSM_GROUPS, SSM_STATE),
            stack(p_out, 1), stack(p_out, 2).reshape(p_kv_shape), stack(p_out, 3).reshape(p_kv_shape),
            stack(s_h, 0).reshape(ssm_shape), stack(s_h, 1).reshape(ssm_shape),
            s_ret.reshape(DEPTH, nb, RET_HEADS, RET_D, RET_D), s_k.reshape(s_kv_shape), s_v.reshape(s_kv_shape))
```

```python
import functools
import math

import numpy as np
import jax
import jax.numpy as jnp
from jax import lax
from jax.experimental import pallas as pl
from jax.experimental.pallas import tpu as pltpu

F32 = jnp.float32
BF16 = jnp.bfloat16

D_MODEL = 1024
DEPTH = 2
SSM_GROUPS = 24
SSM_CH = 16
SSM_STATE = 64
SSM_WIDTH = SSM_GROUPS * SSM_CH
SSM_LANES = SSM_GROUPS * SSM_STATE
RET_HEADS = 4
RET_D = 64
RET_WIDTH = RET_HEADS * RET_D
ATT_Q_HEADS = 6
ATT_KV_HEADS = 2
ATT_GROUP = ATT_Q_HEADS // ATT_KV_HEADS
ATT_D = 64
ATT_WIDTH = ATT_Q_HEADS * ATT_D
WINDOW = 128
FFN_HIDDEN = 2816
IN_WIDTH = 2048
ALPHA = (2 * DEPTH) ** 0.25
LN_EPS = 1e-5
NEG_INF = -1e30

COL_RET = SSM_WIDTH
COL_ATT = SSM_WIDTH + 4 * RET_WIDTH
RET_COLS = 4 * RET_WIDTH
ATT_COLS = ATT_WIDTH + 2 * ATT_KV_HEADS * ATT_D

TC = 128
LANES = 128
N_TILES = SSM_LANES // LANES
HALF = 64
ROW_TILE = 512
SAMPLE_ROWS = 16
VMEM_LIMIT = 56 * 2 ** 20


def _alibi_slopes(n):
    def pow2(k):
        start = 2.0 ** (-8.0 / k)
        return [start ** (i + 1) for i in range(k)]
    if n & (n - 1) == 0:
        return pow2(n)
    c = 2 ** int(math.floor(math.log2(n)))
    return pow2(c) + pow2(2 * c)[0::2][: n - c]


def _cparams(n_axes):
    return pltpu.CompilerParams(dimension_semantics=("arbitrary",) * n_axes, vmem_limit_bytes=VMEM_LIMIT)


def _const_spec(shape):
    nd = len(shape)
    return pl.BlockSpec(shape, lambda *_: (0,) * nd, pipeline_mode=pl.Buffered(1))


def _layer_spec(shape, layer):
    nd = len(shape)
    return pl.BlockSpec((None,) + tuple(shape), lambda *_: (layer,) + (0,) * nd, pipeline_mode=pl.Buffered(1))


def _dot(a, b):
    return jnp.dot(a, b, preferred_element_type=F32)


def _dot_nt(a, b):
    return lax.dot_general(a, b, (((1,), (1,)), ((), ())), preferred_element_type=F32)


def _dot_tn(a, b):
    return lax.dot_general(a, b, (((0,), (0,)), ((), ())), preferred_element_type=F32)


def _layernorm(v, g, b):
    mu = jnp.mean(v, -1, keepdims=True)
    d = v - mu
    var = jnp.mean(d * d, -1, keepdims=True)
    return d * lax.rsqrt(var + LN_EPS) * g + b


def _gelu_tanh(x):
    return 0.5 * x * (1.0 + jnp.tanh(math.sqrt(2.0 / math.pi) * (x + 0.044715 * (x * x * x))))


def _sigmoid(x):
    return 1.0 / (1.0 + jnp.exp(-x))


def _inproj_kernel(x_ref, w_ref, pu_ref, pr_ref, pa_ref):
    xb = x_ref[...].astype(BF16)
    pu_ref[...] = _dot(xb, w_ref[:, 0:COL_RET])
    pr_ref[...] = _dot(xb, w_ref[:, COL_RET:COL_ATT])
    pa_ref[...] = _dot(xb, w_ref[:, COL_ATT:IN_WIDTH])


def _inproj(x2, p, layer):
    t = x2.shape[0]
    tm = min(ROW_TILE, t)
    row = lambda c: pl.BlockSpec((tm, c), lambda i: (i, 0))
    return pl.pallas_call(
        _inproj_kernel,
        grid=(t // tm,),
        in_specs=[row(D_MODEL), _layer_spec((D_MODEL, IN_WIDTH), layer)],
        out_specs=[row(SSM_WIDTH), row(RET_COLS), row(ATT_COLS)],
        out_shape=[jax.ShapeDtypeStruct((t, SSM_WIDTH), F32),
                   jax.ShapeDtypeStruct((t, RET_COLS), F32),
                   jax.ShapeDtypeStruct((t, ATT_COLS), F32)],
        compiler_params=_cparams(1),
        name="inproj",
    )(x2, p["w_in"])


def _s5_bu(ub, bre_ref, bim_ref):
    res, ims = [], []
    for i in range(3):
        ui = ub[:, LANES * i:LANES * (i + 1)]
        res.append(_dot(ui, bre_ref[i]))
        ims.append(_dot(ui, bim_ref[i]))
    return res, ims


def _s5_out(u, h_re, h_im, cre_ref, cim_ref, d_ref, gw_ref, gb_ref):
    ys = []
    for i in range(3):
        ys.append(_dot(h_re[i].astype(BF16), cre_ref[i]) + _dot(h_im[i].astype(BF16), cim_ref[i]))
    y = jnp.concatenate(ys, axis=1) + d_ref[...] * u
    z = _gelu_tanh(y)
    gate = _sigmoid(_dot(z.astype(BF16), gw_ref[...]) + gb_ref[...])
    return z * gate


def _s5_prompt_kernel(u_ref, bre_ref, bim_ref, cre_ref, cim_ref, ar_ref, ai_ref, d_ref, gw_ref, gb_ref,
                      z_ref, hlast_ref, h_scr, hc_scr):
    n = pl.program_id(0)
    nb = u_ref.shape[0]
    rows = nb * TC

    @pl.when(n == 0)
    def _():
        hc_scr[...] = jnp.zeros_like(hc_scr)

    u = pltpu.einshape("btc->(tb)c", u_ref[...])

    res, ims = _s5_bu(u.astype(BF16), bre_ref, bim_ref)
    for i in range(3):
        for k in range(4):
            h_scr[4 * i + k] = res[i][:, LANES * k:LANES * (k + 1)]
            h_scr[N_TILES + 4 * i + k] = ims[i][:, LANES * k:LANES * (k + 1)]

    def step(t, carry):
        r0 = pl.multiple_of(t * nb, nb)
        out = []
        for c in range(N_TILES):
            sl = slice(LANES * c, LANES * (c + 1))
            ar, ai = ar_ref[:, sl], ai_ref[:, sl]
            hr, hi = carry[c], carry[N_TILES + c]
            nr = ar * hr - ai * hi + h_scr[c, pl.ds(r0, nb), :]
            ni = ar * hi + ai * hr + h_scr[N_TILES + c, pl.ds(r0, nb), :]
            h_scr[c, pl.ds(r0, nb), :] = nr
            h_scr[N_TILES + c, pl.ds(r0, nb), :] = ni
            out.append((nr, ni))
        return tuple(o[0] for o in out) + tuple(o[1] for o in out)

    init = tuple(hc_scr[:, LANES * c:LANES * (c + 1)] for c in range(2 * N_TILES))
    fin = lax.fori_loop(0, TC, step, init, unroll=2)
    for c in range(2 * N_TILES):
        hc_scr[:, LANES * c:LANES * (c + 1)] = fin[c]
        hlast_ref[:, LANES * c:LANES * (c + 1)] = fin[c]

    h_re = [jnp.concatenate([h_scr[4 * i + k] for k in range(4)], axis=1) for i in range(3)]
    h_im = [jnp.concatenate([h_scr[N_TILES + 4 * i + k] for k in range(4)], axis=1) for i in range(3)]
    zz = _s5_out(u, h_re, h_im, cre_ref, cim_ref, d_ref, gw_ref, gb_ref)
    z_ref[...] = pltpu.einshape("(tb)c->btc", zz, b=nb).astype(BF16)


def _s5_weight_specs(layer):
    ls = lambda *shape: _layer_spec(shape, layer)
    return [ls(3, LANES, 512), ls(3, LANES, 512), ls(3, 512, LANES), ls(3, 512, LANES),
            ls(1, SSM_LANES), ls(1, SSM_LANES), ls(1, SSM_WIDTH), ls(SSM_WIDTH, SSM_WIDTH), ls(1, SSM_WIDTH)]


def _s5_weights(p):
    return (p["bre"], p["bim"], p["cre"], p["cim"], p["ar"], p["ai"], p["ssm_d"], p["glu_w"], p["glu_b"])


def _s5_prompt(u3, p, layer):
    nb, length, _ = u3.shape
    return pl.pallas_call(
        _s5_prompt_kernel,
        grid=(length // TC,),
        in_specs=[pl.BlockSpec((nb, TC, SSM_WIDTH), lambda n: (0, n, 0))] + _s5_weight_specs(layer),
        out_specs=[pl.BlockSpec((nb, TC, SSM_WIDTH), lambda n: (0, n, 0)),
                   pl.BlockSpec((nb, 2 * SSM_LANES), lambda n: (0, 0))],
        out_shape=[jax.ShapeDtypeStruct((nb, length, SSM_WIDTH), BF16),
                   jax.ShapeDtypeStruct((nb, 2 * SSM_LANES), F32)],
        scratch_shapes=[pltpu.VMEM((2 * N_TILES, nb * TC, LANES), F32), pltpu.VMEM((nb, 2 * SSM_LANES), F32)],
        compiler_params=_cparams(1),
        name="s5_prompt",
    )(u3, *_s5_weights(p))


def _s5_sample_kernel(u_ref, h0r_ref, h0i_ref, bre_ref, bim_ref, cre_ref, cim_ref, ar_ref, ai_ref, d_ref,
                      gw_ref, gb_ref, z_ref, hr_ref, hi_ref):
    u = u_ref[...]
    res, ims = _s5_bu(u.astype(BF16), bre_ref, bim_ref)
    h_re, h_im = [], []
    for i in range(3):
        sl = slice(512 * i, 512 * (i + 1))
        ar, ai = ar_ref[:, sl], ai_ref[:, sl]
        h0r, h0i = h0r_ref[:, sl], h0i_ref[:, sl]
        nr = ar * h0r - ai * h0i + res[i]
        ni = ar * h0i + ai * h0r + ims[i]
        hr_ref[:, sl] = nr
        hi_ref[:, sl] = ni
        h_re.append(nr)
        h_im.append(ni)
    z_ref[...] = _s5_out(u, h_re, h_im, cre_ref, cim_ref, d_ref, gw_ref, gb_ref).astype(BF16)


def _s5_sample(u, h0r, h0i, p, layer):
    nb = u.shape[0]
    full = lambda c: pl.BlockSpec((nb, c), lambda i: (0, 0))
    state = pl.BlockSpec((None, nb, SSM_LANES), lambda i: (layer, 0, 0))
    return pl.pallas_call(
        _s5_sample_kernel,
        grid=(1,),
        in_specs=[full(SSM_WIDTH), state, state] + _s5_weight_specs(layer),
        out_specs=[full(SSM_WIDTH), full(SSM_LANES), full(SSM_LANES)],
        out_shape=[jax.ShapeDtypeStruct((nb, SSM_WIDTH), BF16),
                   jax.ShapeDtypeStruct((nb, SSM_LANES), F32),
                   jax.ShapeDtypeStruct((nb, SSM_LANES), F32)],
        compiler_params=_cparams(1),
        name="s5_sample",
    )(u, h0r, h0i, *_s5_weights(p))


def _ret_gammas():
    return [1.0 - 2.0 ** (-5.0 - h) for h in range(RET_HEADS)]


def _ret_consts():
    log_g = np.log1p(-(2.0 ** (-5.0 - np.arange(RET_HEADS, dtype=np.float64))))
    idx = np.arange(TC)
    diff = idx[:, None] - idx[None, :]
    dmask = np.where(diff >= 0, np.exp(log_g[:, None, None] * np.maximum(diff, 0)), 0.0)
    inner = np.exp(log_g[:, None] * (idx + 1))
    kvd = np.exp(log_g[:, None] * (TC - 1 - idx))
    cd = np.exp(log_g * TC)
    indec = np.zeros((2, TC, LANES))
    kvdec = np.zeros((2, TC, LANES))
    cdec = np.zeros((2, LANES, LANES))
    blk = np.zeros((LANES, LANES))
    for j in range(2):
        for hh in range(2):
            sl = slice(HALF * hh, HALF * (hh + 1))
            indec[j, :, sl] = inner[2 * j + hh][:, None]
            kvdec[j, :, sl] = kvd[2 * j + hh][:, None]
            cdec[j, sl, sl] = cd[2 * j + hh]
            blk[sl, sl] = 1.0
    as32 = lambda a: jnp.asarray(a, F32)
    return as32(dmask), as32(kvdec), as32(indec), as32(cdec), as32(blk)


def _halfnorm(o, lo):
    zero = jnp.zeros_like(o)
    s_lo = jnp.sum(jnp.where(lo, o, zero), -1, keepdims=True)
    s_hi = jnp.sum(jnp.where(lo, zero, o), -1, keepdims=True)
    d = o - jnp.where(lo, s_lo, s_hi) * (1.0 / HALF)
    d2 = d * d
    v_lo = jnp.sum(jnp.where(lo, d2, zero), -1, keepdims=True)
    v_hi = jnp.sum(jnp.where(lo, zero, d2), -1, keepdims=True)
    var = jnp.where(lo, v_lo, v_hi) * (1.0 / HALF)
    return d * lax.rsqrt(var + LN_EPS)


def _ret_prompt_kernel(pr_ref, dmask_ref, kvdec_ref, indec_ref, cdec_ref, blk_ref, gng_ref, gnb_ref,
                       o_ref, slast_ref, s_scr):
    n = pl.program_id(0)
    nb = pr_ref.shape[0]

    @pl.when(n == 0)
    def _():
        s_scr[...] = jnp.zeros_like(s_scr)

    lo = lax.broadcasted_iota(jnp.int32, (TC, LANES), 1) < HALF
    zero = jnp.zeros((TC, LANES), F32)
    for b in range(nb):
        for j in range(2):
            q2 = pr_ref[b, :, LANES * j:LANES * (j + 1)]
            k2 = pr_ref[b, :, 256 + LANES * j:256 + LANES * (j + 1)]
            v2 = pr_ref[b, :, 512 + LANES * j:512 + LANES * (j + 1)]
            g2 = pr_ref[b, :, 768 + LANES * j:768 + LANES * (j + 1)]
            k2b = k2.astype(BF16)
            v2b = v2.astype(BF16)
            state = s_scr[b, j]
            qs = jnp.concatenate([jnp.where(lo, q2, zero), jnp.where(lo, zero, q2)], axis=0).astype(BF16)
            sc = _dot_nt(qs, k2b)
            qd = (q2 * indec_ref[j]).astype(BF16)
            lhs = jnp.concatenate(
                [jnp.concatenate([(sc[hh * TC:(hh + 1) * TC] * dmask_ref[2 * j + hh]).astype(BF16), qd], axis=1)
                 for hh in range(2)], axis=0)
            o2 = _dot(lhs, jnp.concatenate([v2b, state.astype(BF16)], axis=0))
            o = jnp.where(lo, o2[0:TC], o2[TC:2 * TC])
            kd = (k2 * kvdec_ref[j]).astype(BF16)
            s_new = cdec_ref[j] * state + blk_ref[...] * _dot_tn(kd, v2b)
            s_scr[b, j] = s_new
            slast_ref[b, j] = s_new
            sl = slice(LANES * j, LANES * (j + 1))
            o = _halfnorm(o, lo) * gng_ref[:, sl] + gnb_ref[:, sl]
            o_ref[b, :, sl] = (g2 * _sigmoid(g2) * o).astype(BF16)


def _ret_prompt(pr3, p, layer):
    nb, length, _ = pr3.shape
    dmask, kvdec, indec, cdec, blk = _ret_consts()
    return pl.pallas_call(
        _ret_prompt_kernel,
        grid=(length // TC,),
        in_specs=[pl.BlockSpec((nb, TC, RET_COLS), lambda n: (0, n, 0)),
                  _const_spec((RET_HEADS, TC, TC)), _const_spec((2, TC, LANES)), _const_spec((2, TC, LANES)),
                  _const_spec((2, LANES, LANES)), _const_spec((LANES, LANES)),
                  _layer_spec((1, RET_WIDTH), layer), _layer_spec((1, RET_WIDTH), layer)],
        out_specs=[pl.BlockSpec((nb, TC, RET_WIDTH), lambda n: (0, n, 0)),
                   pl.BlockSpec((nb, 2, LANES, LANES), lambda n: (0, 0, 0, 0))],
        out_shape=[jax.ShapeDtypeStruct((nb, length, RET_WIDTH), BF16),
                   jax.ShapeDtypeStruct((nb, 2, LANES, LANES), F32)],
        scratch_shapes=[pltpu.VMEM((nb, 2, LANES, LANES), F32)],
        compiler_params=_cparams(1),
        name="ret_prompt",
    )(pr3, dmask, kvdec, indec, cdec, blk, p["gn_g"], p["gn_b"])


def _att_bias():
    slopes = _alibi_slopes(ATT_Q_HEADS)
    qpos = np.arange(TC)[:, None] + TC
    kpos = np.arange(2 * TC)[None, :]
    dist = qpos - kpos
    valid = (dist >= 0) & (dist <= WINDOW)
    prev_cur = np.stack([np.where(valid, -slopes[h] * dist, NEG_INF) for h in range(ATT_Q_HEADS)])
    cur_prev = np.concatenate([prev_cur[:, :, TC:], prev_cur[:, :, :TC]], axis=2)
    first = np.where(np.arange(2 * TC) >= TC, NEG_INF, 0.0)[None, :]
    return jnp.asarray(cur_prev, F32), jnp.asarray(prev_cur, F32), jnp.asarray(first, F32)


def _att_prompt_kernel(sink_ref, pa_ref, bias_cp_ref, bias_pc_ref, first_ref, o_ref, kout_ref, vout_ref,
                       k_scr, v_scr, bias_scr, *, layer):
    n = pl.program_id(0)
    nb = pa_ref.shape[0]

    @pl.when(n == 0)
    def _():
        k_scr[...] = jnp.zeros_like(k_scr)
        v_scr[:, :, 0:LANES] = jnp.zeros((nb, 2 * TC, LANES), BF16)
        v_scr[:, :, LANES:2 * LANES] = jnp.ones((nb, 2 * TC, LANES), BF16)

    even = n % 2 == 0
    row0 = pl.multiple_of((n % 2) * TC, TC)
    first = jnp.where(n == 0, first_ref[...], jnp.zeros_like(first_ref))
    for head in range(ATT_Q_HEADS):
        bias_scr[head] = jnp.where(even, bias_cp_ref[head], bias_pc_ref[head]) + first

    lo = lax.broadcasted_iota(jnp.int32, (TC, LANES), 1) < HALF
    zero = jnp.zeros((TC, LANES), F32)
    swap = lambda a: pltpu.roll(a, HALF, 1)
    for b in range(nb):
        kcur = pa_ref[b, :, ATT_WIDTH:ATT_WIDTH + LANES]
        vcur = pa_ref[b, :, ATT_WIDTH + LANES:ATT_WIDTH + 2 * LANES]
        kout_ref[b] = kcur
        vout_ref[b] = vcur
        k_scr[b, pl.ds(row0, TC), :] = kcur.astype(BF16)
        v_scr[b, pl.ds(row0, TC), 0:LANES] = vcur.astype(BF16)
        rows, moved = [], []
        for head in range(ATT_Q_HEADS):
            qcol = pa_ref[b, :, LANES * (head // 2):LANES * (head // 2 + 1)]
            kv = head // ATT_GROUP
            moved.append(head % 2 != kv)
            if moved[-1]:
                qcol = swap(qcol)
            rows.append(jnp.where(lo, qcol, zero) if kv == 0 else jnp.where(lo, zero, qcol))
        qs = jnp.concatenate(rows, axis=0).astype(BF16)
        s_all = _dot_nt(qs, k_scr[b])
        es, ms = [], []
        for head in range(ATT_Q_HEADS):
            s = s_all[head * TC:(head + 1) * TC] + bias_scr[head]
            m = jnp.maximum(jnp.max(s, -1, keepdims=True), sink_ref[layer, head])
            es.append(jnp.exp(s - m).astype(BF16))
            ms.append(m)
        r_all = _dot(jnp.concatenate(es, axis=0), v_scr[b])
        outs = []
        for head in range(ATT_Q_HEADS):
            r = r_all[head * TC:(head + 1) * TC]
            den = r[:, LANES:2 * LANES] + jnp.exp(sink_ref[layer, head] - ms[head])
            out = r[:, 0:LANES] / den
            outs.append(swap(out) if moved[head] else out)
        for c in range(3):
            o_ref[b, :, LANES * c:LANES * (c + 1)] = jnp.where(lo, outs[2 * c], outs[2 * c + 1]).astype(BF16)


def _att_prompt(pa3, p, layer):
    nb, length, _ = pa3.shape
    bias_cp, bias_pc, first = _att_bias()
    kv_scr = lambda width: pltpu.VMEM((nb, 2 * TC, width), BF16)
    return pl.pallas_call(
        functools.partial(_att_prompt_kernel, layer=layer),
        grid=(length // TC,),
        in_specs=[pl.BlockSpec(memory_space=pltpu.SMEM),
                  pl.BlockSpec((nb, TC, ATT_COLS), lambda n: (0, n, 0)),
                  _const_spec((ATT_Q_HEADS, TC, 2 * TC)), _const_spec((ATT_Q_HEADS, TC, 2 * TC)),
                  _const_spec((1, 2 * TC))],
        out_specs=[pl.BlockSpec((nb, TC, ATT_WIDTH), lambda n: (0, n, 0)),
                   pl.BlockSpec((nb, TC, LANES), lambda n: (0, 0, 0)),
                   pl.BlockSpec((nb, TC, LANES), lambda n: (0, 0, 0))],
        out_shape=[jax.ShapeDtypeStruct((nb, length, ATT_WIDTH), BF16),
                   jax.ShapeDtypeStruct((nb, TC, LANES), F32),
                   jax.ShapeDtypeStruct((nb, TC, LANES), F32)],
        scratch_shapes=[kv_scr(LANES), kv_scr(2 * LANES),
                        pltpu.VMEM((ATT_Q_HEADS, TC, 2 * TC), F32)],
        compiler_params=_cparams(1),
        name="att_prompt",
    )(p["sinks"], pa3, bias_cp, bias_pc, first)


def _mix_sample_kernel(qt_ref, kt_ref, v4_ref, g4_ref, s_ref, qb_ref, kn_ref, vn_ref, kc_ref, vc_ref,
                       gng_ref, gnb_ref, bias_ref, sink_ref, *rest, layer):
    or_ref, oa_ref, sn_all, kout_all, vout_all = rest[-5:]
    if layer:
        for prev_ref, out_all in zip(rest[:3], (sn_all, kout_all, vout_all)):
            out_all[0:layer] = prev_ref[...]
    sn_ref, kout_ref, vout_ref = sn_all.at[layer], kout_all.at[layer], vout_all.at[layer]
    gammas = _ret_gammas()
    sink = sink_ref[:, 0:1]
    bias = bias_ref[...]

    for i in range(SAMPLE_ROWS):
        qt = qt_ref[i]
        kt = kt_ref[i]
        v4 = v4_ref[i]
        g4 = g4_ref[i]
        rows = []
        for h in range(RET_HEADS):
            sl = slice(RET_D * h, RET_D * (h + 1))
            sn = gammas[h] * s_ref[i, sl, :] + kt[:, h:h + 1] * v4[h:h + 1, :]
            sn_ref[i, sl, :] = sn
            rows.append(jnp.sum(qt[:, h:h + 1] * sn, axis=0, keepdims=True))
        o4 = jnp.concatenate(rows, axis=0)
        mu = jnp.mean(o4, -1, keepdims=True)
        d = o4 - mu
        var = jnp.mean(d * d, -1, keepdims=True)
        o4 = d * lax.rsqrt(var + LN_EPS) * gng_ref[...] + gnb_ref[...]
        or_ref[i] = g4 * _sigmoid(g4) * o4

        kb = kc_ref[i]
        vb = vc_ref[i]
        q8 = qb_ref[i]
        knew = kn_ref[i]
        vnew = vn_ref[i]
        s = _dot_nt(q8.astype(BF16), kb.astype(BF16)) + bias
        s_self = jnp.sum(q8 * knew, -1, keepdims=True)
        m = jnp.maximum(jnp.maximum(jnp.max(s, -1, keepdims=True), s_self), sink)
        e = jnp.exp(s - m)
        e_self = jnp.exp(s_self - m)
        den = jnp.sum(e, -1, keepdims=True) + e_self + jnp.exp(sink - m)
        oa_ref[i] = (_dot(e.astype(BF16), vb.astype(BF16)) + e_self * vnew) / den

        kout_ref[i, 0:WINDOW - 1, :] = kc_ref[i, 1:WINDOW, :]
        kout_ref[i, WINDOW - 1:WINDOW, :] = knew
        vout_ref[i, 0:WINDOW - 1, :] = vc_ref[i, 1:WINDOW, :]
        vout_ref[i, WINDOW - 1:WINDOW, :] = vnew


def _mix_sample(qt, kt, v4, g4, s_all, qb, kn, vn, kc_all, vc_all, p, bias8, layer, prev):
    nb = qt.shape[0]
    bb = SAMPLE_ROWS
    blk = lambda *dims: pl.BlockSpec((bb,) + dims, lambda i: (i,) + (0,) * len(dims))
    lblk = lambda *dims: pl.BlockSpec((None, bb) + dims, lambda i: (layer, i) + (0,) * len(dims))
    in_specs = [blk(RET_D, RET_HEADS), blk(RET_D, RET_HEADS), blk(RET_HEADS, RET_D), blk(RET_HEADS, RET_D),
                lblk(RET_WIDTH, RET_D), blk(8, LANES), blk(1, LANES), blk(1, LANES),
                lblk(WINDOW, LANES), lblk(WINDOW, LANES),
                _layer_spec((RET_HEADS, RET_D), layer), _layer_spec((RET_HEADS, RET_D), layer),
                _const_spec((8, LANES)), _layer_spec((8, LANES), layer)]
    args = [qt, kt, v4, g4, s_all, qb, kn, vn, kc_all, vc_all, p["gn_g4"], p["gn_b4"], bias8, p["sink8"]]
    stacked = lambda n_layers, *dims: pl.BlockSpec((n_layers, bb) + dims, lambda i: (0, i) + (0,) * len(dims))
    state_dims = ((RET_WIDTH, RET_D), (WINDOW, LANES), (WINDOW, LANES))
    if layer:
        in_specs += [stacked(layer, *dims) for dims in state_dims]
        args += list(prev)
    return pl.pallas_call(
        functools.partial(_mix_sample_kernel, layer=layer),
        grid=(nb // bb,),
        in_specs=in_specs,
        out_specs=[blk(RET_HEADS, RET_D), blk(8, LANES)] + [stacked(layer + 1, *dims) for dims in state_dims],
        out_shape=[jax.ShapeDtypeStruct((nb, RET_HEADS, RET_D), F32),
                   jax.ShapeDtypeStruct((nb, 8, LANES), F32)]
                  + [jax.ShapeDtypeStruct((layer + 1, nb) + dims, F32) for dims in state_dims],
        compiler_params=_cparams(1),
        name="mix_sample",
    )(*args)


def _outffn_kernel(z_ref, r_ref, a_ref, x_ref, wo_ref, g1_ref, b1_ref, w1_ref, w3_ref, w2_ref, g2_ref, b2_ref, o_ref):
    c1, c2 = SSM_WIDTH, SSM_WIDTH + RET_WIDTH
    mix = _dot(z_ref[...], wo_ref[0:c1]) + _dot(r_ref[...], wo_ref[c1:c2]) + _dot(a_ref[...], wo_ref[c2:D_MODEL])
    x = _layernorm(ALPHA * x_ref[...] + mix, g1_ref[...], b1_ref[...])
    xb = x.astype(BF16)
    h1 = _dot(xb, w1_ref[...])
    h3 = _dot(xb, w3_ref[...])
    hid = (h1 * _sigmoid(h1) * h3).astype(BF16)
    o_ref[...] = _layernorm(ALPHA * x + _dot(hid, w2_ref[...]), g2_ref[...], b2_ref[...])


def _outffn(z, r, a, x2, p, layer):
    t = x2.shape[0]
    tm = min(ROW_TILE, t)
    row = lambda c: pl.BlockSpec((tm, c), lambda i: (i, 0))
    vec = _layer_spec((1, D_MODEL), layer)
    return pl.pallas_call(
        _outffn_kernel,
        grid=(t // tm,),
        in_specs=[row(SSM_WIDTH), row(RET_WIDTH), row(ATT_WIDTH), row(D_MODEL),
                  _layer_spec((D_MODEL, D_MODEL), layer), vec, vec,
                  _layer_spec((D_MODEL, FFN_HIDDEN), layer), _layer_spec((D_MODEL, FFN_HIDDEN), layer),
                  _layer_spec((FFN_HIDDEN, D_MODEL), layer), vec, vec],
        out_specs=row(D_MODEL),
        out_shape=jax.ShapeDtypeStruct((t, D_MODEL), F32),
        compiler_params=_cparams(1),
        name="outffn",
    )(z, r, a, x2, p["w_out"], p["ln1_g"], p["ln1_b"], p["w1"], p["w3"], p["w2"], p["ln2_g"], p["ln2_b"])


def _prep_params(w_in, lam_re, lam_im, log_step, b_re, b_im, c_re, c_im, ssm_d, glu_w, glu_b,
                 gn_g, gn_b, sinks, w_out, ln1_g, ln1_b, w1, w3, w2, ln2_g, ln2_b):
    col_scale = np.ones((IN_WIDTH,), np.float32)
    col_scale[COL_RET + RET_WIDTH:COL_RET + 2 * RET_WIDTH] = RET_D ** -0.5
    col_scale[COL_ATT:COL_ATT + ATT_WIDTH] = ATT_D ** -0.5

    step = jnp.exp(log_step)[..., None]
    mag = jnp.exp(lam_re * step)
    a_re, a_im = mag * jnp.cos(lam_im * step), mag * jnp.sin(lam_im * step)
    den = lam_re * lam_re + lam_im * lam_im
    k_re = ((a_re - 1.0) * lam_re + a_im * lam_im) / den
    k_im = (a_im * lam_re - (a_re - 1.0) * lam_im) / den
    bb_re = k_re[..., None] * b_re - k_im[..., None] * b_im
    bb_im = k_re[..., None] * b_im + k_im[..., None] * b_re
    eye8 = jnp.eye(8, dtype=F32)

    def b_blocks(m):
        m = m.reshape(DEPTH, 3, 8, SSM_STATE, SSM_CH)
        return jnp.einsum("ligph,gk->lighkp", m, eye8).reshape(DEPTH, 3, LANES, 512).astype(BF16)

    def c_blocks(m):
        m = m.reshape(DEPTH, 3, 8, SSM_CH, SSM_STATE)
        return jnp.einsum("lighp,gk->likpgh", m, eye8).reshape(DEPTH, 3, 512, LANES).astype(BF16)

    row = lambda a: a.reshape(DEPTH, 1, -1)
    sink8 = jnp.concatenate([sinks, jnp.zeros((DEPTH, 2), F32)], axis=1)
    return dict(
        w_in=(w_in * col_scale).astype(BF16),
        bre=b_blocks(bb_re), bim=b_blocks(bb_im), cre=c_blocks(c_re), cim=c_blocks(-c_im),
        ar=row(a_re), ai=row(a_im),
        ssm_d=row(ssm_d), glu_w=glu_w.astype(BF16), glu_b=row(glu_b),
        gn_g=row(gn_g), gn_b=row(gn_b),
        gn_g4=gn_g.reshape(DEPTH, RET_HEADS, RET_D), gn_b4=gn_b.reshape(DEPTH, RET_HEADS, RET_D),
        sinks=sinks, sink8=jnp.broadcast_to(sink8[:, :, None], (DEPTH, 8, LANES)),
        w_out=w_out.astype(BF16), ln1_g=row(ln1_g), ln1_b=row(ln1_b),
        w1=w1.astype(BF16), w3=w3.astype(BF16), w2=w2.astype(BF16), ln2_g=row(ln2_g), ln2_b=row(ln2_b),
    )


def _prompt_layer(x3, p, layer):
    nb, length, _ = x3.shape
    t = nb * length
    x2 = x3.reshape(t, D_MODEL)
    pu, pr, pa = _inproj(x2, p, layer)
    z, hlast = _s5_prompt(pu.reshape(nb, length, SSM_WIDTH), p, layer)
    o_r, sblk = _ret_prompt(pr.reshape(nb, length, RET_COLS), p, layer)
    o_a, klast, vlast = _att_prompt(pa.reshape(nb, length, ATT_COLS), p, layer)
    y = _outffn(z.reshape(t, SSM_WIDTH), o_r.reshape(t, RET_WIDTH), o_a.reshape(t, ATT_WIDTH), x2, p, layer)
    y = y.reshape(nb, length, D_MODEL)
    s4 = jnp.stack([sblk[:, h // 2, HALF * (h % 2):HALF * (h % 2 + 1), HALF * (h % 2):HALF * (h % 2 + 1)]
                    for h in range(RET_HEADS)], axis=1)
    return y, hlast, s4, klast, vlast


def _sample_bias8():
    slopes = _alibi_slopes(ATT_Q_HEADS)
    dist = WINDOW - np.arange(WINDOW)
    bias = np.zeros((8, WINDOW))
    for h in range(ATT_Q_HEADS):
        bias[h] = -slopes[h] * dist
    return jnp.asarray(bias, F32)


def _sample_layer(x2, p, layer, h0r, h0i, s_all, kc_all, vc_all, prev):
    nb = x2.shape[0]
    pu, pr, pa = _inproj(x2, p, layer)
    z, h_re, h_im = _s5_sample(pu, h0r, h0i, p, layer)
    heads = lambda a: a.reshape(nb, RET_HEADS, RET_D)
    q4, k4, v4, g4 = (heads(pr[:, RET_WIDTH * i:RET_WIDTH * (i + 1)]) for i in range(4))
    q6 = pa[:, 0:ATT_WIDTH].reshape(nb, ATT_Q_HEADS, ATT_D)
    kv_of = np.arange(ATT_Q_HEADS) // ATT_GROUP
    qb = jnp.zeros((nb, 8, ATT_KV_HEADS, ATT_D), F32).at[:, np.arange(ATT_Q_HEADS), kv_of].set(q6)
    kn = pa[:, ATT_WIDTH:ATT_WIDTH + LANES].reshape(nb, 1, LANES)
    vn = pa[:, ATT_WIDTH + LANES:ATT_WIDTH + 2 * LANES].reshape(nb, 1, LANES)
    o_r, o8, s_new, k_out, v_out = _mix_sample(
        q4.transpose(0, 2, 1), k4.transpose(0, 2, 1), v4, g4, s_all, qb.reshape(nb, 8, LANES), kn, vn,
        kc_all, vc_all, p, _sample_bias8(), layer, prev)
    o_r = o_r.reshape(nb, RET_WIDTH).astype(BF16)
    o_a = o8.reshape(nb, 8, ATT_KV_HEADS, ATT_D)[:, np.arange(ATT_Q_HEADS), kv_of].reshape(nb, ATT_WIDTH).astype(BF16)
    y = _outffn(z, o_r, o_a, x2, p, layer)
    return y, h_re, h_im, (s_new, k_out, v_out)


def kernel(x_prompt, x_sample, state_ssm_re, state_ssm_im, state_ret, cache_win_k, cache_win_v, w_in, ssm_lambda_re, ssm_lambda_im, ssm_log_step, ssm_b_re, ssm_b_im, ssm_c_re, ssm_c_im, ssm_d, ssm_glu_w, ssm_glu_b, ret_gn_g, ret_gn_b, attn_sinks, w_out, ln1_g, ln1_b, ffn_w1, ffn_w3, ffn_w2, ln2_g, ln2_b):
    p = _prep_params(w_in, ssm_lambda_re, ssm_lambda_im, ssm_log_step, ssm_b_re, ssm_b_im, ssm_c_re, ssm_c_im,
                     ssm_d, ssm_glu_w, ssm_glu_b, ret_gn_g, ret_gn_b, attn_sinks, w_out, ln1_g, ln1_b,
                     ffn_w1, ffn_w3, ffn_w2, ln2_g, ln2_b)

    xp = x_prompt
    pb = x_prompt.shape[0]
    p_out = []
    for layer in range(DEPTH):
        xp, *states = _prompt_layer(xp, p, layer)
        p_out.append(states)
    stack = lambda outs, i: jnp.stack([o[i] for o in outs])
    p_h = stack(p_out, 0)
    p_kv_shape = (DEPTH, pb, TC, ATT_KV_HEADS, ATT_D)

    nb = x_sample.shape[0]
    xs = x_sample.reshape(nb, D_MODEL)
    h0r = state_ssm_re.reshape(DEPTH, nb, SSM_LANES)
    h0i = state_ssm_im.reshape(DEPTH, nb, SSM_LANES)
    s_all = state_ret.reshape(DEPTH, nb, RET_WIDTH, RET_D)
    kc_all = cache_win_k.reshape(DEPTH, nb, WINDOW, LANES)
    vc_all = cache_win_v.reshape(DEPTH, nb, WINDOW, LANES)
    prev = None
    s_h = []
    for layer in range(DEPTH):
        xs, h_re, h_im, prev = _sample_layer(xs, p, layer, h0r, h0i, s_all, kc_all, vc_all, prev)
        s_h.append((h_re, h_im))
    s_ret, s_k, s_v = prev
    ssm_shape = (DEPTH, nb, SSM_GROUPS, SSM_STATE)
    s_kv_shape = (DEPTH, nb, WINDOW, ATT_KV_HEADS, ATT_D)

    return (xp, xs.reshape(nb, 1, D_MODEL),
            p_h[:, :, :SSM_LANES].reshape(DEPTH, pb, SSM_GROUPS, SSM_STATE),
            p_h[:, :, SSM_LANES:].reshape(DEPTH, pb, SSM_GROUPS, SSM_STATE),
            stack(p_out, 1), stack(p_out, 2).reshape(p_kv_shape), stack(p_out, 3).reshape(p_kv_shape),
            stack(s_h, 0).reshape(ssm_shape), stack(s_h, 1).reshape(ssm_shape),
            s_ret.reshape(DEPTH, nb, RET_HEADS, RET_D, RET_D), s_k.reshape(s_kv_shape), s_v.reshape(s_kv_shape))
```

```python
import functools
import math

import numpy as np
import jax
import jax.numpy as jnp
from jax import lax
from jax.experimental import pallas as pl
from jax.experimental.pallas import tpu as pltpu

F32 = jnp.float32
BF16 = jnp.bfloat16

D_MODEL = 1024
DEPTH = 2
SSM_GROUPS = 24
SSM_CH = 16
SSM_STATE = 64
SSM_WIDTH = SSM_GROUPS * SSM_CH
SSM_LANES = SSM_GROUPS * SSM_STATE
RET_HEADS = 4
RET_D = 64
RET_WIDTH = RET_HEADS * RET_D
ATT_Q_HEADS = 6
ATT_KV_HEADS = 2
ATT_GROUP = ATT_Q_HEADS // ATT_KV_HEADS
ATT_D = 64
ATT_WIDTH = ATT_Q_HEADS * ATT_D
WINDOW = 128
FFN_HIDDEN = 2816
IN_WIDTH = 2048
ALPHA = (2 * DEPTH) ** 0.25
LN_EPS = 1e-5
NEG_INF = -1e30

COL_RET = SSM_WIDTH
COL_ATT = SSM_WIDTH + 4 * RET_WIDTH
RET_COLS = 4 * RET_WIDTH
ATT_COLS = ATT_WIDTH + 2 * ATT_KV_HEADS * ATT_D

TC = 128
LANES = 128
N_TILES = SSM_LANES // LANES
HALF = 64
ROW_TILE = 512
SAMPLE_ROWS = 16
VMEM_LIMIT = 56 * 2 ** 20


def _alibi_slopes(n):
    def pow2(k):
        start = 2.0 ** (-8.0 / k)
        return [start ** (i + 1) for i in range(k)]
    if n & (n - 1) == 0:
        return pow2(n)
    c = 2 ** int(math.floor(math.log2(n)))
    return pow2(c) + pow2(2 * c)[0::2][: n - c]


def _cparams(n_axes):
    return pltpu.CompilerParams(dimension_semantics=("arbitrary",) * n_axes, vmem_limit_bytes=VMEM_LIMIT)


def _const_spec(shape):
    nd = len(shape)
    return pl.BlockSpec(shape, lambda *_: (0,) * nd, pipeline_mode=pl.Buffered(1))


def _layer_spec(shape, layer):
    nd = len(shape)
    return pl.BlockSpec((None,) + tuple(shape), lambda *_: (layer,) + (0,) * nd, pipeline_mode=pl.Buffered(1))


def _dot(a, b):
    return jnp.dot(a, b, preferred_element_type=F32)


def _dot_nt(a, b):
    return lax.dot_general(a, b, (((1,), (1,)), ((), ())), preferred_element_type=F32)


def _dot_tn(a, b):
    return lax.dot_general(a, b, (((0,), (0,)), ((), ())), preferred_element_type=F32)


def _layernorm(v, g, b):
    mu = jnp.mean(v, -1, keepdims=True)
    d = v - mu
    var = jnp.mean(d * d, -1, keepdims=True)
    return d * lax.rsqrt(var + LN_EPS) * g + b


def _gelu_tanh(x):
    return 0.5 * x * (1.0 + jnp.tanh(math.sqrt(2.0 / math.pi) * (x + 0.044715 * (x * x * x))))


def _sigmoid(x):
    return 1.0 / (1.0 + jnp.exp(-x))


def _inproj_kernel(x_ref, w_ref, pu_ref, pr_ref, pa_ref):
    xb = x_ref[...].astype(BF16)
    pu_ref[...] = _dot(xb, w_ref[:, 0:COL_RET])
    pr_ref[...] = _dot(xb, w_ref[:, COL_RET:COL_ATT])
    pa_ref[...] = _dot(xb, w_ref[:, COL_ATT:IN_WIDTH])


def _inproj(x2, p, layer):
    t = x2.shape[0]
    tm = min(ROW_TILE, t)
    row = lambda c: pl.BlockSpec((tm, c), lambda i: (i, 0))
    return pl.pallas_call(
        _inproj_kernel,
        grid=(t // tm,),
        in_specs=[row(D_MODEL), _layer_spec((D_MODEL, IN_WIDTH), layer)],
        out_specs=[row(SSM_WIDTH), row(RET_COLS), row(ATT_COLS)],
        out_shape=[jax.ShapeDtypeStruct((t, SSM_WIDTH), F32),
                   jax.ShapeDtypeStruct((t, RET_COLS), F32),
                   jax.ShapeDtypeStruct((t, ATT_COLS), F32)],
        compiler_params=_cparams(1),
        name="inproj",
    )(x2, p["w_in"])


def _s5_bu(ub, bre_ref, bim_ref):
    res, ims = [], []
    for i in range(3):
        ui = ub[:, LANES * i:LANES * (i + 1)]
        res.append(_dot(ui, bre_ref[i]))
        ims.append(_dot(ui, bim_ref[i]))
    return res, ims


def _s5_out(u, h_re, h_im, cre_ref, cim_ref, d_ref, gw_ref, gb_ref):
    ys = []
    for i in range(3):
        ys.append(_dot(h_re[i].astype(BF16), cre_ref[i]) + _dot(h_im[i].astype(BF16), cim_ref[i]))
    y = jnp.concatenate(ys, axis=1) + d_ref[...] * u
    z = _gelu_tanh(y)
    gate = _sigmoid(_dot(z.astype(BF16), gw_ref[...]) + gb_ref[...])
    return z * gate


def _s5_prompt_body(n, u3, bre_ref, bim_ref, cre_ref, cim_ref, ar_ref, ai_ref, d_ref, gw_ref, gb_ref,
                    z_ref, hlast_ref, h_scr, hc_scr):
    nb = u3.shape[0]

    @pl.when(n == 0)
    def _():
        hc_scr[...] = jnp.zeros_like(hc_scr)

    u = pltpu.einshape("btc->(tb)c", u3)

    res, ims = _s5_bu(u.astype(BF16), bre_ref, bim_ref)
    for i in range(3):
        for k in range(4):
            h_scr[4 * i + k] = res[i][:, LANES * k:LANES * (k + 1)]
            h_scr[N_TILES + 4 * i + k] = ims[i][:, LANES * k:LANES * (k + 1)]

    def step(t, carry):
        r0 = pl.multiple_of(t * nb, nb)
        out = []
        for c in range(N_TILES):
            sl = slice(LANES * c, LANES * (c + 1))
            ar, ai = ar_ref[:, sl], ai_ref[:, sl]
            hr, hi = carry[c], carry[N_TILES + c]
            nr = ar * hr - ai * hi + h_scr[c, pl.ds(r0, nb), :]
            ni = ar * hi + ai * hr + h_scr[N_TILES + c, pl.ds(r0, nb), :]
            h_scr[c, pl.ds(r0, nb), :] = nr
            h_scr[N_TILES + c, pl.ds(r0, nb), :] = ni
            out.append((nr, ni))
        return tuple(o[0] for o in out) + tuple(o[1] for o in out)

    init = tuple(hc_scr[:, LANES * c:LANES * (c + 1)] for c in range(2 * N_TILES))
    fin = lax.fori_loop(0, TC, step, init, unroll=2)
    for c in range(2 * N_TILES):
        hc_scr[:, LANES * c:LANES * (c + 1)] = fin[c]
        hlast_ref[:, LANES * c:LANES * (c + 1)] = fin[c]

    h_re = [jnp.concatenate([h_scr[4 * i + k] for k in range(4)], axis=1) for i in range(3)]
    h_im = [jnp.concatenate([h_scr[N_TILES + 4 * i + k] for k in range(4)], axis=1) for i in range(3)]
    zz = _s5_out(u, h_re, h_im, cre_ref, cim_ref, d_ref, gw_ref, gb_ref)
    z_ref[...] = pltpu.einshape("(tb)c->btc", zz, b=nb).astype(BF16)


def _s5_weight_specs(layer):
    ls = lambda *shape: _layer_spec(shape, layer)
    return [ls(3, LANES, 512), ls(3, LANES, 512), ls(3, 512, LANES), ls(3, 512, LANES),
            ls(1, SSM_LANES), ls(1, SSM_LANES), ls(1, SSM_WIDTH), ls(SSM_WIDTH, SSM_WIDTH), ls(1, SSM_WIDTH)]


def _s5_weights(p):
    return (p["bre"], p["bim"], p["cre"], p["cim"], p["ar"], p["ai"], p["ssm_d"], p["glu_w"], p["glu_b"])


def _s5_sample_kernel(u_ref, h0r_ref, h0i_ref, bre_ref, bim_ref, cre_ref, cim_ref, ar_ref, ai_ref, d_ref,
                      gw_ref, gb_ref, z_ref, hr_ref, hi_ref):
    u = u_ref[...]
    res, ims = _s5_bu(u.astype(BF16), bre_ref, bim_ref)
    h_re, h_im = [], []
    for i in range(3):
        sl = slice(512 * i, 512 * (i + 1))
        ar, ai = ar_ref[:, sl], ai_ref[:, sl]
        h0r, h0i = h0r_ref[:, sl], h0i_ref[:, sl]
        nr = ar * h0r - ai * h0i + res[i]
        ni = ar * h0i + ai * h0r + ims[i]
        hr_ref[:, sl] = nr
        hi_ref[:, sl] = ni
        h_re.append(nr)
        h_im.append(ni)
    z_ref[...] = _s5_out(u, h_re, h_im, cre_ref, cim_ref, d_ref, gw_ref, gb_ref).astype(BF16)


def _s5_sample(u, h0r, h0i, p, layer):
    nb = u.shape[0]
    full = lambda c: pl.BlockSpec((nb, c), lambda i: (0, 0))
    state = pl.BlockSpec((None, nb, SSM_LANES), lambda i: (layer, 0, 0))
    return pl.pallas_call(
        _s5_sample_kernel,
        grid=(1,),
        in_specs=[full(SSM_WIDTH), state, state] + _s5_weight_specs(layer),
        out_specs=[full(SSM_WIDTH), full(SSM_LANES), full(SSM_LANES)],
        out_shape=[jax.ShapeDtypeStruct((nb, SSM_WIDTH), BF16),
                   jax.ShapeDtypeStruct((nb, SSM_LANES), F32),
                   jax.ShapeDtypeStruct((nb, SSM_LANES), F32)],
        compiler_params=_cparams(1),
        name="s5_sample",
    )(u, h0r, h0i, *_s5_weights(p))


def _ret_gammas():
    return [1.0 - 2.0 ** (-5.0 - h) for h in range(RET_HEADS)]


def _ret_consts():
    log_g = np.log1p(-(2.0 ** (-5.0 - np.arange(RET_HEADS, dtype=np.float64))))
    idx = np.arange(TC)
    diff = idx[:, None] - idx[None, :]
    dmask = np.where(diff >= 0, np.exp(log_g[:, None, None] * np.maximum(diff, 0)), 0.0)
    inner = np.exp(log_g[:, None] * (idx + 1))
    kvd = np.exp(log_g[:, None] * (TC - 1 - idx))
    cd = np.exp(log_g * TC)
    indec = np.zeros((2, TC, LANES))
    kvdec = np.zeros((2, TC, LANES))
    cdec = np.zeros((2, LANES, LANES))
    blk = np.zeros((LANES, LANES))
    for j in range(2):
        for hh in range(2):
            sl = slice(HALF * hh, HALF * (hh + 1))
            indec[j, :, sl] = inner[2 * j + hh][:, None]
            kvdec[j, :, sl] = kvd[2 * j + hh][:, None]
            cdec[j, sl, sl] = cd[2 * j + hh]
            blk[sl, sl] = 1.0
    as32 = lambda a: jnp.asarray(a, F32)
    return as32(dmask), as32(kvdec), as32(indec), as32(cdec), as32(blk)


def _halfnorm(o, lo):
    zero = jnp.zeros_like(o)
    s_lo = jnp.sum(jnp.where(lo, o, zero), -1, keepdims=True)
    s_hi = jnp.sum(jnp.where(lo, zero, o), -1, keepdims=True)
    d = o - jnp.where(lo, s_lo, s_hi) * (1.0 / HALF)
    d2 = d * d
    v_lo = jnp.sum(jnp.where(lo, d2, zero), -1, keepdims=True)
    v_hi = jnp.sum(jnp.where(lo, zero, d2), -1, keepdims=True)
    var = jnp.where(lo, v_lo, v_hi) * (1.0 / HALF)
    return d * lax.rsqrt(var + LN_EPS)


def _ret_prompt_body(n, pr_ref, dmask_ref, kvdec_ref, indec_ref, cdec_ref, blk_ref, gng_ref, gnb_ref,
                     o_ref, slast_ref, s_scr):
    nb = pr_ref.shape[0]

    @pl.when(n == 0)
    def _():
        s_scr[...] = jnp.zeros_like(s_scr)

    lo = lax.broadcasted_iota(jnp.int32, (TC, LANES), 1) < HALF
    zero = jnp.zeros((TC, LANES), F32)
    for b in range(nb):
        for j in range(2):
            q2 = pr_ref[b, :, LANES * j:LANES * (j + 1)]
            k2 = pr_ref[b, :, 256 + LANES * j:256 + LANES * (j + 1)]
            v2 = pr_ref[b, :, 512 + LANES * j:512 + LANES * (j + 1)]
            g2 = pr_ref[b, :, 768 + LANES * j:768 + LANES * (j + 1)]
            k2b = k2.astype(BF16)
            v2b = v2.astype(BF16)
            state = s_scr[b, j]
            qs = jnp.concatenate([jnp.where(lo, q2, zero), jnp.where(lo, zero, q2)], axis=0).astype(BF16)
            sc = _dot_nt(qs, k2b)
            qd = (q2 * indec_ref[j]).astype(BF16)
            lhs = jnp.concatenate(
                [jnp.concatenate([(sc[hh * TC:(hh + 1) * TC] * dmask_ref[2 * j + hh]).astype(BF16), qd], axis=1)
                 for hh in range(2)], axis=0)
            o2 = _dot(lhs, jnp.concatenate([v2b, state.astype(BF16)], axis=0))
            o = jnp.where(lo, o2[0:TC], o2[TC:2 * TC])
            kd = (k2 * kvdec_ref[j]).astype(BF16)
            s_new = cdec_ref[j] * state + blk_ref[...] * _dot_tn(kd, v2b)
            s_scr[b, j] = s_new
            slast_ref[b, j] = s_new
            sl = slice(LANES * j, LANES * (j + 1))
            o = _halfnorm(o, lo) * gng_ref[:, sl] + gnb_ref[:, sl]
            o_ref[b, :, sl] = (g2 * _sigmoid(g2) * o).astype(BF16)


def _att_bias():
    slopes = _alibi_slopes(ATT_Q_HEADS)
    qpos = np.arange(TC)[:, None] + TC
    kpos = np.arange(2 * TC)[None, :]
    dist = qpos - kpos
    valid = (dist >= 0) & (dist <= WINDOW)
    prev_cur = np.stack([np.where(valid, -slopes[h] * dist, NEG_INF) for h in range(ATT_Q_HEADS)])
    cur_prev = np.concatenate([prev_cur[:, :, TC:], prev_cur[:, :, :TC]], axis=2)
    first = np.where(np.arange(2 * TC) >= TC, NEG_INF, 0.0)[None, :]
    return jnp.asarray(cur_prev, F32), jnp.asarray(prev_cur, F32), jnp.asarray(first, F32)


def _att_prompt_body(n, layer, sink_ref, pa_ref, bias_cp_ref, bias_pc_ref, first_ref, o_ref, kout_ref, vout_ref,
                     k_scr, v_scr, bias_scr):
    nb = pa_ref.shape[0]

    @pl.when(n == 0)
    def _():
        k_scr[...] = jnp.zeros_like(k_scr)
        v_scr[:, :, 0:LANES] = jnp.zeros((nb, 2 * TC, LANES), BF16)
        v_scr[:, :, LANES:2 * LANES] = jnp.ones((nb, 2 * TC, LANES), BF16)

    even = n % 2 == 0
    row0 = pl.multiple_of((n % 2) * TC, TC)
    first = jnp.where(n == 0, first_ref[...], jnp.zeros_like(first_ref))
    for head in range(ATT_Q_HEADS):
        bias_scr[head] = jnp.where(even, bias_cp_ref[head], bias_pc_ref[head]) + first

    lo = lax.broadcasted_iota(jnp.int32, (TC, LANES), 1) < HALF
    zero = jnp.zeros((TC, LANES), F32)
    swap = lambda a: pltpu.roll(a, HALF, 1)
    for b in range(nb):
        kcur = pa_ref[b, :, ATT_WIDTH:ATT_WIDTH + LANES]
        vcur = pa_ref[b, :, ATT_WIDTH + LANES:ATT_WIDTH + 2 * LANES]
        kout_ref[b] = kcur
        vout_ref[b] = vcur
        k_scr[b, pl.ds(row0, TC), :] = kcur.astype(BF16)
        v_scr[b, pl.ds(row0, TC), 0:LANES] = vcur.astype(BF16)
        rows, moved = [], []
        for head in range(ATT_Q_HEADS):
            qcol = pa_ref[b, :, LANES * (head // 2):LANES * (head // 2 + 1)]
            kv = head // ATT_GROUP
            moved.append(head % 2 != kv)
            if moved[-1]:
                qcol = swap(qcol)
            rows.append(jnp.where(lo, qcol, zero) if kv == 0 else jnp.where(lo, zero, qcol))
        qs = jnp.concatenate(rows, axis=0).astype(BF16)
        s_all = _dot_nt(qs, k_scr[b])
        es, ms = [], []
        for head in range(ATT_Q_HEADS):
            s = s_all[head * TC:(head + 1) * TC] + bias_scr[head]
            m = jnp.maximum(jnp.max(s, -1, keepdims=True), sink_ref[layer, head])
            es.append(jnp.exp(s - m).astype(BF16))
            ms.append(m)
        r_all = _dot(jnp.concatenate(es, axis=0), v_scr[b])
        outs = []
        for head in range(ATT_Q_HEADS):
            r = r_all[head * TC:(head + 1) * TC]
            den = r[:, LANES:2 * LANES] + jnp.exp(sink_ref[layer, head] - ms[head])
            out = r[:, 0:LANES] / den
            outs.append(swap(out) if moved[head] else out)
        for c in range(3):
            o_ref[b, :, LANES * c:LANES * (c + 1)] = jnp.where(lo, outs[2 * c], outs[2 * c + 1]).astype(BF16)


def _mixer_kernel(sink_ref, x_ref, w_ref,
                  bre_ref, bim_ref, cre_ref, cim_ref, ar_ref, ai_ref, d_ref, gw_ref, gb_ref,
                  dmask_ref, kvdec_ref, indec_ref, cdec_ref, blk_ref, gng_ref, gnb_ref,
                  bias_cp_ref, bias_pc_ref, first_ref,
                  z_ref, or_ref, oa_ref, hlast_ref, slast_ref, kout_ref, vout_ref,
                  pr_scr, pa_scr, h_scr, hc_scr, s_scr, k_scr, v_scr, bias_scr, *, layer):
    n = pl.program_id(0)
    nb = x_ref.shape[0]
    xb = x_ref[...].reshape(nb * TC, D_MODEL).astype(BF16)
    u3 = _dot(xb, w_ref[:, 0:COL_RET]).reshape(nb, TC, SSM_WIDTH)
    pr_scr[...] = _dot(xb, w_ref[:, COL_RET:COL_ATT]).reshape(nb, TC, RET_COLS)
    pa_scr[...] = _dot(xb, w_ref[:, COL_ATT:IN_WIDTH]).reshape(nb, TC, ATT_COLS)
    _s5_prompt_body(n, u3, bre_ref, bim_ref, cre_ref, cim_ref, ar_ref, ai_ref, d_ref, gw_ref, gb_ref,
                    z_ref, hlast_ref, h_scr, hc_scr)
    _ret_prompt_body(n, pr_scr, dmask_ref, kvdec_ref, indec_ref, cdec_ref, blk_ref, gng_ref, gnb_ref,
                     or_ref, slast_ref, s_scr)
    _att_prompt_body(n, layer, sink_ref, pa_scr, bias_cp_ref, bias_pc_ref, first_ref, oa_ref, kout_ref, vout_ref,
                     k_scr, v_scr, bias_scr)


def _mixer_prompt(x3, p, layer):
    nb, length, _ = x3.shape
    chunk = lambda c: pl.BlockSpec((nb, TC, c), lambda n: (0, n, 0))
    fixed = lambda *shape: pl.BlockSpec(shape, lambda n: (0,) * len(shape))
    ret_consts = _ret_consts()
    att_consts = _att_bias()
    return pl.pallas_call(
        functools.partial(_mixer_kernel, layer=layer),
        grid=(length // TC,),
        in_specs=[pl.BlockSpec(memory_space=pltpu.SMEM), chunk(D_MODEL), _layer_spec((D_MODEL, IN_WIDTH), layer)]
                 + _s5_weight_specs(layer)
                 + [_const_spec(c.shape) for c in ret_consts]
                 + [_layer_spec((1, RET_WIDTH), layer), _layer_spec((1, RET_WIDTH), layer)]
                 + [_const_spec(c.shape) for c in att_consts],
        out_specs=[chunk(SSM_WIDTH), chunk(RET_WIDTH), chunk(ATT_WIDTH),
                   fixed(nb, 2 * SSM_LANES), fixed(nb, 2, LANES, LANES), fixed(nb, TC, LANES), fixed(nb, TC, LANES)],
        out_shape=[jax.ShapeDtypeStruct((nb, length, SSM_WIDTH), BF16),
                   jax.ShapeDtypeStruct((nb, length, RET_WIDTH), BF16),
                   jax.ShapeDtypeStruct((nb, length, ATT_WIDTH), BF16),
                   jax.ShapeDtypeStruct((nb, 2 * SSM_LANES), F32),
                   jax.ShapeDtypeStruct((nb, 2, LANES, LANES), F32),
                   jax.ShapeDtypeStruct((nb, TC, LANES), F32),
                   jax.ShapeDtypeStruct((nb, TC, LANES), F32)],
        scratch_shapes=[pltpu.VMEM((nb, TC, RET_COLS), F32), pltpu.VMEM((nb, TC, ATT_COLS), F32),
                        pltpu.VMEM((2 * N_TILES, nb * TC, LANES), F32), pltpu.VMEM((nb, 2 * SSM_LANES), F32),
                        pltpu.VMEM((nb, 2, LANES, LANES), F32),
                        pltpu.VMEM((nb, 2 * TC, LANES), BF16), pltpu.VMEM((nb, 2 * TC, 2 * LANES), BF16),
                        pltpu.VMEM((ATT_Q_HEADS, TC, 2 * TC), F32)],
        compiler_params=_cparams(1),
        name="mixer",
    )(p["sinks"], x3, p["w_in"], *_s5_weights(p), *ret_consts, p["gn_g"], p["gn_b"], *att_consts)


def _mix_sample_kernel(qt_ref, kt_ref, v4_ref, g4_ref, s_ref, qb_ref, kn_ref, vn_ref, kc_ref, vc_ref,
                       gng_ref, gnb_ref, bias_ref, sink_ref, *rest, layer):
    or_ref, oa_ref, sn_all, kout_all, vout_all = rest[-5:]
    if layer:
        for prev_ref, out_all in zip(rest[:3], (sn_all, kout_all, vout_all)):
            out_all[0:layer] = prev_ref[...]
    sn_ref, kout_ref, vout_ref = sn_all.at[layer], kout_all.at[layer], vout_all.at[layer]
    gammas = _ret_gammas()
    sink = sink_ref[:, 0:1]
    bias = bias_ref[...]

    for i in range(SAMPLE_ROWS):
        qt = qt_ref[i]
        kt = kt_ref[i]
        v4 = v4_ref[i]
        g4 = g4_ref[i]
        rows = []
        for h in range(RET_HEADS):
            sl = slice(RET_D * h, RET_D * (h + 1))
            sn = gammas[h] * s_ref[i, sl, :] + kt[:, h:h + 1] * v4[h:h + 1, :]
            sn_ref[i, sl, :] = sn
            rows.append(jnp.sum(qt[:, h:h + 1] * sn, axis=0, keepdims=True))
        o4 = jnp.concatenate(rows, axis=0)
        mu = jnp.mean(o4, -1, keepdims=True)
        d = o4 - mu
        var = jnp.mean(d * d, -1, keepdims=True)
        o4 = d * lax.rsqrt(var + LN_EPS) * gng_ref[...] + gnb_ref[...]
        or_ref[i] = g4 * _sigmoid(g4) * o4

        kb = kc_ref[i]
        vb = vc_ref[i]
        q8 = qb_ref[i]
        knew = kn_ref[i]
        vnew = vn_ref[i]
        s = _dot_nt(q8.astype(BF16), kb.astype(BF16)) + bias
        s_self = jnp.sum(q8 * knew, -1, keepdims=True)
        m = jnp.maximum(jnp.maximum(jnp.max(s, -1, keepdims=True), s_self), sink)
        e = jnp.exp(s - m)
        e_self = jnp.exp(s_self - m)
        den = jnp.sum(e, -1, keepdims=True) + e_self + jnp.exp(sink - m)
        oa_ref[i] = (_dot(e.astype(BF16), vb.astype(BF16)) + e_self * vnew) / den

        kout_ref[i, 0:WINDOW - 1, :] = kc_ref[i, 1:WINDOW, :]
        kout_ref[i, WINDOW - 1:WINDOW, :] = knew
        vout_ref[i, 0:WINDOW - 1, :] = vc_ref[i, 1:WINDOW, :]
        vout_ref[i, WINDOW - 1:WINDOW, :] = vnew


def _mix_sample(qt, kt, v4, g4, s_all, qb, kn, vn, kc_all, vc_all, p, bias8, layer, prev):
    nb = qt.shape[0]
    bb = SAMPLE_ROWS
    blk = lambda *dims: pl.BlockSpec((bb,) + dims, lambda i: (i,) + (0,) * len(dims))
    lblk = lambda *dims: pl.BlockSpec((None, bb) + dims, lambda i: (layer, i) + (0,) * len(dims))
    in_specs = [blk(RET_D, RET_HEADS), blk(RET_D, RET_HEADS), blk(RET_HEADS, RET_D), blk(RET_HEADS, RET_D),
                lblk(RET_WIDTH, RET_D), blk(8, LANES), blk(1, LANES), blk(1, LANES),
                lblk(WINDOW, LANES), lblk(WINDOW, LANES),
                _layer_spec((RET_HEADS, RET_D), layer), _layer_spec((RET_HEADS, RET_D), layer),
                _const_spec((8, LANES)), _layer_spec((8, LANES), layer)]
    args = [qt, kt, v4, g4, s_all, qb, kn, vn, kc_all, vc_all, p["gn_g4"], p["gn_b4"], bias8, p["sink8"]]
    stacked = lambda n_layers, *dims: pl.BlockSpec((n_layers, bb) + dims, lambda i: (0, i) + (0,) * len(dims))
    state_dims = ((RET_WIDTH, RET_D), (WINDOW, LANES), (WINDOW, LANES))
    if layer:
        in_specs += [stacked(layer, *dims) for dims in state_dims]
        args += list(prev)
    return pl.pallas_call(
        functools.partial(_mix_sample_kernel, layer=layer),
        grid=(nb // bb,),
        in_specs=in_specs,
        out_specs=[blk(RET_HEADS, RET_D), blk(8, LANES)] + [stacked(layer + 1, *dims) for dims in state_dims],
        out_shape=[jax.ShapeDtypeStruct((nb, RET_HEADS, RET_D), F32),
                   jax.ShapeDtypeStruct((nb, 8, LANES), F32)]
                  + [jax.ShapeDtypeStruct((layer + 1, nb) + dims, F32) for dims in state_dims],
        compiler_params=_cparams(1),
        name="mix_sample",
    )(*args)


def _outffn_kernel(z_ref, r_ref, a_ref, x_ref, wo_ref, g1_ref, b1_ref, w1_ref, w3_ref, w2_ref, g2_ref, b2_ref, o_ref):
    c1, c2 = SSM_WIDTH, SSM_WIDTH + RET_WIDTH
    mix = _dot(z_ref[...], wo_ref[0:c1]) + _dot(r_ref[...], wo_ref[c1:c2]) + _dot(a_ref[...], wo_ref[c2:D_MODEL])
    x = _layernorm(ALPHA * x_ref[...] + mix, g1_ref[...], b1_ref[...])
    xb = x.astype(BF16)
    h1 = _dot(xb, w1_ref[...])
    h3 = _dot(xb, w3_ref[...])
    hid = (h1 * _sigmoid(h1) * h3).astype(BF16)
    o_ref[...] = _layernorm(ALPHA * x + _dot(hid, w2_ref[...]), g2_ref[...], b2_ref[...])


def _outffn(z, r, a, x2, p, layer):
    t = x2.shape[0]
    tm = min(ROW_TILE, t)
    row = lambda c: pl.BlockSpec((tm, c), lambda i: (i, 0))
    vec = _layer_spec((1, D_MODEL), layer)
    return pl.pallas_call(
        _outffn_kernel,
        grid=(t // tm,),
        in_specs=[row(SSM_WIDTH), row(RET_WIDTH), row(ATT_WIDTH), row(D_MODEL),
                  _layer_spec((D_MODEL, D_MODEL), layer), vec, vec,
                  _layer_spec((D_MODEL, FFN_HIDDEN), layer), _layer_spec((D_MODEL, FFN_HIDDEN), layer),
                  _layer_spec((FFN_HIDDEN, D_MODEL), layer), vec, vec],
        out_specs=row(D_MODEL),
        out_shape=jax.ShapeDtypeStruct((t, D_MODEL), F32),
        compiler_params=_cparams(1),
        name="outffn",
    )(z, r, a, x2, p["w_out"], p["ln1_g"], p["ln1_b"], p["w1"], p["w3"], p["w2"], p["ln2_g"], p["ln2_b"])


def _prep_params(w_in, lam_re, lam_im, log_step, b_re, b_im, c_re, c_im, ssm_d, glu_w, glu_b,
                 gn_g, gn_b, sinks, w_out, ln1_g, ln1_b, w1, w3, w2, ln2_g, ln2_b):
    col_scale = np.ones((IN_WIDTH,), np.float32)
    col_scale[COL_RET + RET_WIDTH:COL_RET + 2 * RET_WIDTH] = RET_D ** -0.5
    col_scale[COL_ATT:COL_ATT + ATT_WIDTH] = ATT_D ** -0.5

    step = jnp.exp(log_step)[..., None]
    mag = jnp.exp(lam_re * step)
    a_re, a_im = mag * jnp.cos(lam_im * step), mag * jnp.sin(lam_im * step)
    den = lam_re * lam_re + lam_im * lam_im
    k_re = ((a_re - 1.0) * lam_re + a_im * lam_im) / den
    k_im = (a_im * lam_re - (a_re - 1.0) * lam_im) / den
    bb_re = k_re[..., None] * b_re - k_im[..., None] * b_im
    bb_im = k_re[..., None] * b_im + k_im[..., None] * b_re
    eye8 = jnp.eye(8, dtype=F32)

    def b_blocks(m):
        m = m.reshape(DEPTH, 3, 8, SSM_STATE, SSM_CH)
        return jnp.einsum("ligph,gk->lighkp", m, eye8).reshape(DEPTH, 3, LANES, 512).astype(BF16)

    def c_blocks(m):
        m = m.reshape(DEPTH, 3, 8, SSM_CH, SSM_STATE)
        return jnp.einsum("lighp,gk->likpgh", m, eye8).reshape(DEPTH, 3, 512, LANES).astype(BF16)

    row = lambda a: a.reshape(DEPTH, 1, -1)
    sink8 = jnp.concatenate([sinks, jnp.zeros((DEPTH, 2), F32)], axis=1)
    return dict(
        w_in=(w_in * col_scale).astype(BF16),
        bre=b_blocks(bb_re), bim=b_blocks(bb_im), cre=c_blocks(c_re), cim=c_blocks(-c_im),
        ar=row(a_re), ai=row(a_im),
        ssm_d=row(ssm_d), glu_w=glu_w.astype(BF16), glu_b=row(glu_b),
        gn_g=row(gn_g), gn_b=row(gn_b),
        gn_g4=gn_g.reshape(DEPTH, RET_HEADS, RET_D), gn_b4=gn_b.reshape(DEPTH, RET_HEADS, RET_D),
        sinks=sinks, sink8=jnp.broadcast_to(sink8[:, :, None], (DEPTH, 8, LANES)),
        w_out=w_out.astype(BF16), ln1_g=row(ln1_g), ln1_b=row(ln1_b),
        w1=w1.astype(BF16), w3=w3.astype(BF16), w2=w2.astype(BF16), ln2_g=row(ln2_g), ln2_b=row(ln2_b),
    )


def _prompt_layer(x3, p, layer):
    nb, length, _ = x3.shape
    t = nb * length
    x2 = x3.reshape(t, D_MODEL)
    z, o_r, o_a, hlast, sblk, klast, vlast = _mixer_prompt(x3, p, layer)
    y = _outffn(z.reshape(t, SSM_WIDTH), o_r.reshape(t, RET_WIDTH), o_a.reshape(t, ATT_WIDTH), x2, p, layer)
    y = y.reshape(nb, length, D_MODEL)
    s4 = jnp.stack([sblk[:, h // 2, HALF * (h % 2):HALF * (h % 2 + 1), HALF * (h % 2):HALF * (h % 2 + 1)]
                    for h in range(RET_HEADS)], axis=1)
    return y, hlast, s4, klast, vlast


def _sample_bias8():
    slopes = _alibi_slopes(ATT_Q_HEADS)
    dist = WINDOW - np.arange(WINDOW)
    bias = np.zeros((8, WINDOW))
    for h in range(ATT_Q_HEADS):
        bias[h] = -slopes[h] * dist
    return jnp.asarray(bias, F32)


def _sample_layer(x2, p, layer, h0r, h0i, s_all, kc_all, vc_all, prev):
    nb = x2.shape[0]
    pu, pr, pa = _inproj(x2, p, layer)
    z, h_re, h_im = _s5_sample(pu, h0r, h0i, p, layer)
    heads = lambda a: a.reshape(nb, RET_HEADS, RET_D)
    q4, k4, v4, g4 = (heads(pr[:, RET_WIDTH * i:RET_WIDTH * (i + 1)]) for i in range(4))
    q6 = pa[:, 0:ATT_WIDTH].reshape(nb, ATT_Q_HEADS, ATT_D)
    kv_of = np.arange(ATT_Q_HEADS) // ATT_GROUP
    qb = jnp.zeros((nb, 8, ATT_KV_HEADS, ATT_D), F32).at[:, np.arange(ATT_Q_HEADS), kv_of].set(q6)
    kn = pa[:, ATT_WIDTH:ATT_WIDTH + LANES].reshape(nb, 1, LANES)
    vn = pa[:, ATT_WIDTH + LANES:ATT_WIDTH + 2 * LANES].reshape(nb, 1, LANES)
    o_r, o8, s_new, k_out, v_out = _mix_sample(
        q4.transpose(0, 2, 1), k4.transpose(0, 2, 1), v4, g4, s_all, qb.reshape(nb, 8, LANES), kn, vn,
        kc_all, vc_all, p, _sample_bias8(), layer, prev)
    o_r = o_r.reshape(nb, RET_WIDTH).astype(BF16)
    o_a = o8.reshape(nb, 8, ATT_KV_HEADS, ATT_D)[:, np.arange(ATT_Q_HEADS), kv_of].reshape(nb, ATT_WIDTH).astype(BF16)
    y = _outffn(z, o_r, o_a, x2, p, layer)
    return y, h_re, h_im, (s_new, k_out, v_out)


def kernel(x_prompt, x_sample, state_ssm_re, state_ssm_im, state_ret, cache_win_k, cache_win_v, w_in, ssm_lambda_re, ssm_lambda_im, ssm_log_step, ssm_b_re, ssm_b_im, ssm_c_re, ssm_c_im, ssm_d, ssm_glu_w, ssm_glu_b, ret_gn_g, ret_gn_b, attn_sinks, w_out, ln1_g, ln1_b, ffn_w1, ffn_w3, ffn_w2, ln2_g, ln2_b):
    p = _prep_params(w_in, ssm_lambda_re, ssm_lambda_im, ssm_log_step, ssm_b_re, ssm_b_im, ssm_c_re, ssm_c_im,
                     ssm_d, ssm_glu_w, ssm_glu_b, ret_gn_g, ret_gn_b, attn_sinks, w_out, ln1_g, ln1_b,
                     ffn_w1, ffn_w3, ffn_w2, ln2_g, ln2_b)

    xp = x_prompt
    pb = x_prompt.shape[0]
    p_out = []
    for layer in range(DEPTH):
        xp, *states = _prompt_layer(xp, p, layer)
        p_out.append(states)
    stack = lambda outs, i: jnp.stack([o[i] for o in outs])
    p_h = stack(p_out, 0)
    p_kv_shape = (DEPTH, pb, TC, ATT_KV_HEADS, ATT_D)

    nb = x_sample.shape[0]
    xs = x_sample.reshape(nb, D_MODEL)
    h0r = state_ssm_re.reshape(DEPTH, nb, SSM_LANES)
    h0i = state_ssm_im.reshape(DEPTH, nb, SSM_LANES)
    s_all = state_ret.reshape(DEPTH, nb, RET_WIDTH, RET_D)
    kc_all = cache_win_k.reshape(DEPTH, nb, WINDOW, LANES)
    vc_all = cache_win_v.reshape(DEPTH, nb, WINDOW, LANES)
    prev = None
    s_h = []
    for layer in range(DEPTH):
        xs, h_re, h_im, prev = _sample_layer(xs, p, layer, h0r, h0i, s_all, kc_all, vc_all, prev)
        s_h.append((h_re, h_im))
    s_ret, s_k, s_v = prev
    ssm_shape = (DEPTH, nb, SSM_GROUPS, SSM_STATE)
    s_kv_shape = (DEPTH, nb, WINDOW, ATT_KV_HEADS, ATT_D)

    return (xp, xs.reshape(nb, 1, D_MODEL),
            p_h[:, :, :SSM_LANES].reshape(DEPTH, pb, SSM_GROUPS, SSM_STATE),
            p_h[:, :, SSM_LANES:].reshape(DEPTH, pb, SSM_GROUPS, SSM_STATE),
            stack(p_out, 1), stack(p_out, 2).reshape(p_kv_shape), stack(p_out, 3).reshape(p_kv_shape),
            stack(s_h, 0).reshape(ssm_shape), stack(s_h, 1).reshape(ssm_shape),
            s_ret.reshape(DEPTH, nb, RET_HEADS, RET_D, RET_D), s_k.reshape(s_kv_shape), s_v.reshape(s_kv_shape))
```

```python
import functools
import math

import numpy as np
import jax
import jax.numpy as jnp
from jax import lax
from jax.experimental import pallas as pl
from jax.experimental.pallas import tpu as pltpu

F32 = jnp.float32
BF16 = jnp.bfloat16

D_MODEL = 1024
DEPTH = 2
SSM_GROUPS = 24
SSM_CH = 16
SSM_STATE = 64
SSM_WIDTH = SSM_GROUPS * SSM_CH
SSM_LANES = SSM_GROUPS * SSM_STATE
RET_HEADS = 4
RET_D = 64
RET_WIDTH = RET_HEADS * RET_D
ATT_Q_HEADS = 6
ATT_KV_HEADS = 2
ATT_GROUP = ATT_Q_HEADS // ATT_KV_HEADS
ATT_D = 64
ATT_WIDTH = ATT_Q_HEADS * ATT_D
WINDOW = 128
FFN_HIDDEN = 2816
IN_WIDTH = 2048
ALPHA = (2 * DEPTH) ** 0.25
LN_EPS = 1e-5
NEG_INF = -1e30

COL_RET = SSM_WIDTH
COL_ATT = SSM_WIDTH + 4 * RET_WIDTH
RET_COLS = 4 * RET_WIDTH
ATT_COLS = ATT_WIDTH + 2 * ATT_KV_HEADS * ATT_D

TC = 128
LANES = 128
N_TILES = SSM_LANES // LANES
HALF = 64
MXU_COLS = 256
ROW_TILE = 512
SAMPLE_ROWS = 16
VMEM_LIMIT = 56 * 2 ** 20


def _alibi_slopes(n):
    def pow2(k):
        start = 2.0 ** (-8.0 / k)
        return [start ** (i + 1) for i in range(k)]
    if n & (n - 1) == 0:
        return pow2(n)
    c = 2 ** int(math.floor(math.log2(n)))
    return pow2(c) + pow2(2 * c)[0::2][: n - c]


def _cparams(n_axes):
    return pltpu.CompilerParams(dimension_semantics=("arbitrary",) * n_axes, vmem_limit_bytes=VMEM_LIMIT)


def _const_spec(shape):
    nd = len(shape)
    return pl.BlockSpec(shape, lambda *_: (0,) * nd, pipeline_mode=pl.Buffered(1))


def _layer_spec(shape, layer):
    nd = len(shape)
    return pl.BlockSpec((None,) + tuple(shape), lambda *_: (layer,) + (0,) * nd, pipeline_mode=pl.Buffered(1))


def _dot(a, b):
    return jnp.dot(a, b, preferred_element_type=F32)


def _dot_nt(a, b):
    return lax.dot_general(a, b, (((1,), (1,)), ((), ())), preferred_element_type=F32)


def _dot_tn(a, b):
    return lax.dot_general(a, b, (((0,), (0,)), ((), ())), preferred_element_type=F32)


def _layernorm(v, g, b):
    mu = jnp.mean(v, -1, keepdims=True)
    d = v - mu
    var = jnp.mean(d * d, -1, keepdims=True)
    return d * lax.rsqrt(var + LN_EPS) * g + b


def _gelu_tanh(x):
    return 0.5 * x * (1.0 + jnp.tanh(math.sqrt(2.0 / math.pi) * (x + 0.044715 * (x * x * x))))


def _sigmoid(x):
    return 1.0 / (1.0 + jnp.exp(-x))


def _inproj_kernel(x_ref, w_ref, pu_ref, pr_ref, pa_ref):
    xb = x_ref[...].astype(BF16)
    pu_ref[...] = _dot(xb, w_ref[:, 0:COL_RET])
    pr_ref[...] = _dot(xb, w_ref[:, COL_RET:COL_ATT])
    pa_ref[...] = _dot(xb, w_ref[:, COL_ATT:IN_WIDTH])


def _inproj(x2, p, layer):
    t = x2.shape[0]
    tm = min(ROW_TILE, t)
    row = lambda c: pl.BlockSpec((tm, c), lambda i: (i, 0))
    return pl.pallas_call(
        _inproj_kernel,
        grid=(t // tm,),
        in_specs=[row(D_MODEL), _layer_spec((D_MODEL, IN_WIDTH), layer)],
        out_specs=[row(SSM_WIDTH), row(RET_COLS), row(ATT_COLS)],
        out_shape=[jax.ShapeDtypeStruct((t, SSM_WIDTH), F32),
                   jax.ShapeDtypeStruct((t, RET_COLS), F32),
                   jax.ShapeDtypeStruct((t, ATT_COLS), F32)],
        compiler_params=_cparams(1),
        name="inproj",
    )(x2, p["w_in"])


def _s5_bu(ub, bre_ref, bim_ref):
    res, ims = [], []
    for i in range(3):
        ui = ub[:, LANES * i:LANES * (i + 1)]
        res.append(_dot(ui, bre_ref[i]))
        ims.append(_dot(ui, bim_ref[i]))
    return res, ims


def _s5_out(u, h_re, h_im, cre_ref, cim_ref, d_ref, gw_ref, gb_ref):
    ys = []
    for i in range(3):
        ys.append(_dot(h_re[i].astype(BF16), cre_ref[i]) + _dot(h_im[i].astype(BF16), cim_ref[i]))
    y = jnp.concatenate(ys, axis=1) + d_ref[...] * u
    z = _gelu_tanh(y)
    gate = _sigmoid(_dot(z.astype(BF16), gw_ref[...]) + gb_ref[...])
    return z * gate


def _s5_prompt_body(n, u3, bre_ref, bim_ref, cre_ref, cim_ref, ar_ref, ai_ref, d_ref, gw_ref, gb_ref,
                    z_ref, hlast_ref, h_scr, hc_scr, fillers=()):
    nb = u3.shape[0]

    @pl.when(n == 0)
    def _():
        hc_scr[...] = jnp.zeros_like(hc_scr)

    u = pltpu.einshape("btc->(tb)c", u3)

    res, ims = _s5_bu(u.astype(BF16), bre_ref, bim_ref)
    for i in range(3):
        for k in range(4):
            h_scr[4 * i + k] = res[i][:, LANES * k:LANES * (k + 1)]
            h_scr[N_TILES + 4 * i + k] = ims[i][:, LANES * k:LANES * (k + 1)]

    def step(t, carry):
        r0 = t * nb
        out = []
        for c in range(N_TILES):
            sl = slice(LANES * c, LANES * (c + 1))
            ar, ai = ar_ref[:, sl], ai_ref[:, sl]
            hr, hi = carry[c], carry[N_TILES + c]
            nr = ar * hr - ai * hi + h_scr[c, pl.ds(r0, nb), :]
            ni = ar * hi + ai * hr + h_scr[N_TILES + c, pl.ds(r0, nb), :]
            h_scr[c, pl.ds(r0, nb), :] = nr
            h_scr[N_TILES + c, pl.ds(r0, nb), :] = ni
            out.append((nr, ni))
        return tuple(o[0] for o in out) + tuple(o[1] for o in out)

    fin = tuple(hc_scr[:, LANES * c:LANES * (c + 1)] for c in range(2 * N_TILES))
    groups = len(fillers) + 1
    for g in range(groups):
        for t in range(TC * g // groups, TC * (g + 1) // groups):
            fin = step(t, fin)
        if g < len(fillers):
            fillers[g]()
    for c in range(2 * N_TILES):
        hc_scr[:, LANES * c:LANES * (c + 1)] = fin[c]
        hlast_ref[:, LANES * c:LANES * (c + 1)] = fin[c]

    ys = []

    def out_piece(i):
        def emit():
            h_re = jnp.concatenate([h_scr[4 * i + k] for k in range(4)], axis=1).astype(BF16)
            h_im = jnp.concatenate([h_scr[N_TILES + 4 * i + k] for k in range(4)], axis=1).astype(BF16)
            ys.append(_dot(h_re, cre_ref[i]) + _dot(h_im, cim_ref[i]))
        return emit

    def finish():
        z = _gelu_tanh(jnp.concatenate(ys, axis=1) + d_ref[...] * u)
        gate = _sigmoid(_dot(z.astype(BF16), gw_ref[...]) + gb_ref[...])
        z_ref[...] = pltpu.einshape("(tb)c->btc", z * gate, b=nb).astype(BF16)

    return [out_piece(i) for i in range(3)] + [finish]


def _s5_weight_specs(layer):
    ls = lambda *shape: _layer_spec(shape, layer)
    return [ls(3, LANES, 512), ls(3, LANES, 512), ls(3, 512, LANES), ls(3, 512, LANES),
            ls(1, SSM_LANES), ls(1, SSM_LANES), ls(1, SSM_WIDTH), ls(SSM_WIDTH, SSM_WIDTH), ls(1, SSM_WIDTH)]


def _s5_weights(p):
    return (p["bre"], p["bim"], p["cre"], p["cim"], p["ar"], p["ai"], p["ssm_d"], p["glu_w"], p["glu_b"])


def _s5_sample_kernel(u_ref, h0r_ref, h0i_ref, bre_ref, bim_ref, cre_ref, cim_ref, ar_ref, ai_ref, d_ref,
                      gw_ref, gb_ref, z_ref, hr_ref, hi_ref):
    u = u_ref[...]
    res, ims = _s5_bu(u.astype(BF16), bre_ref, bim_ref)
    h_re, h_im = [], []
    for i in range(3):
        sl = slice(512 * i, 512 * (i + 1))
        ar, ai = ar_ref[:, sl], ai_ref[:, sl]
        h0r, h0i = h0r_ref[:, sl], h0i_ref[:, sl]
        nr = ar * h0r - ai * h0i + res[i]
        ni = ar * h0i + ai * h0r + ims[i]
        hr_ref[:, sl] = nr
        hi_ref[:, sl] = ni
        h_re.append(nr)
        h_im.append(ni)
    z_ref[...] = _s5_out(u, h_re, h_im, cre_ref, cim_ref, d_ref, gw_ref, gb_ref).astype(BF16)


def _s5_sample(u, h0r, h0i, p, layer):
    nb = u.shape[0]
    full = lambda c: pl.BlockSpec((nb, c), lambda i: (0, 0))
    state = pl.BlockSpec((None, nb, SSM_LANES), lambda i: (layer, 0, 0))
    return pl.pallas_call(
        _s5_sample_kernel,
        grid=(1,),
        in_specs=[full(SSM_WIDTH), state, state] + _s5_weight_specs(layer),
        out_specs=[full(SSM_WIDTH), full(SSM_LANES), full(SSM_LANES)],
        out_shape=[jax.ShapeDtypeStruct((nb, SSM_WIDTH), BF16),
                   jax.ShapeDtypeStruct((nb, SSM_LANES), F32),
                   jax.ShapeDtypeStruct((nb, SSM_LANES), F32)],
        compiler_params=_cparams(1),
        name="s5_sample",
    )(u, h0r, h0i, *_s5_weights(p))


def _ret_gammas():
    return [1.0 - 2.0 ** (-5.0 - h) for h in range(RET_HEADS)]


def _ret_consts():
    log_g = np.log1p(-(2.0 ** (-5.0 - np.arange(RET_HEADS, dtype=np.float64))))
    idx = np.arange(TC)
    diff = idx[:, None] - idx[None, :]
    dmask = np.where(diff >= 0, np.exp(log_g[:, None, None] * np.maximum(diff, 0)), 0.0)
    inner = np.exp(log_g[:, None] * (idx + 1))
    kvd = np.exp(log_g[:, None] * (TC - 1 - idx))
    cd = np.exp(log_g * TC)
    indec = np.zeros((2, TC, LANES))
    kvdec = np.zeros((2, TC, LANES))
    cdec = np.zeros((2, LANES, LANES))
    blk = np.zeros((LANES, LANES))
    for j in range(2):
        for hh in range(2):
            sl = slice(HALF * hh, HALF * (hh + 1))
            indec[j, :, sl] = inner[2 * j + hh][:, None]
            kvdec[j, :, sl] = kvd[2 * j + hh][:, None]
            cdec[j, sl, sl] = cd[2 * j + hh]
            blk[sl, sl] = 1.0
    as32 = lambda a: jnp.asarray(a, F32)
    return as32(dmask), as32(kvdec), as32(indec), as32(cdec), as32(blk)


def _halfnorm(o, lo):
    zero = jnp.zeros_like(o)
    s_lo = jnp.sum(jnp.where(lo, o, zero), -1, keepdims=True)
    s_hi = jnp.sum(jnp.where(lo, zero, o), -1, keepdims=True)
    d = o - jnp.where(lo, s_lo, s_hi) * (1.0 / HALF)
    d2 = d * d
    v_lo = jnp.sum(jnp.where(lo, d2, zero), -1, keepdims=True)
    v_hi = jnp.sum(jnp.where(lo, zero, d2), -1, keepdims=True)
    var = jnp.where(lo, v_lo, v_hi) * (1.0 / HALF)
    return d * lax.rsqrt(var + LN_EPS)


def _ret_prompt_body(n, pr_ref, dmask_ref, kvdec_ref, indec_ref, cdec_ref, blk_ref, gng_ref, gnb_ref,
                     o_ref, slast_ref, s_scr, fillers=()):
    nb = pr_ref.shape[0]
    fillers = list(fillers)

    @pl.when(n == 0)
    def _():
        s_scr[...] = jnp.zeros_like(s_scr)

    lo = lax.broadcasted_iota(jnp.int32, (TC, LANES), 1) < HALF
    zero = jnp.zeros((TC, LANES), F32)
    for b in range(nb):
        for j in range(2):
            q2 = pr_ref[b, :, LANES * j:LANES * (j + 1)]
            k2 = pr_ref[b, :, 256 + LANES * j:256 + LANES * (j + 1)]
            v2 = pr_ref[b, :, 512 + LANES * j:512 + LANES * (j + 1)]
            g2 = pr_ref[b, :, 768 + LANES * j:768 + LANES * (j + 1)]
            k2b = k2.astype(BF16)
            v2b = v2.astype(BF16)
            state = s_scr[b, j]
            qs = jnp.concatenate([jnp.where(lo, q2, zero), jnp.where(lo, zero, q2)], axis=0).astype(BF16)
            sc = _dot_nt(qs, k2b)
            qd = (q2 * indec_ref[j]).astype(BF16)
            lhs = jnp.concatenate(
                [jnp.concatenate([(sc[hh * TC:(hh + 1) * TC] * dmask_ref[2 * j + hh]).astype(BF16), qd], axis=1)
                 for hh in range(2)], axis=0)
            o2 = _dot(lhs, jnp.concatenate([v2b, state.astype(BF16)], axis=0))
            o = jnp.where(lo, o2[0:TC], o2[TC:2 * TC])
            kd = (k2 * kvdec_ref[j]).astype(BF16)
            s_new = cdec_ref[j] * state + blk_ref[...] * _dot_tn(kd, v2b)
            s_scr[b, j] = s_new
            slast_ref[b, j] = s_new
            sl = slice(LANES * j, LANES * (j + 1))
            o = _halfnorm(o, lo) * gng_ref[:, sl] + gnb_ref[:, sl]
            o_ref[b, :, sl] = (g2 * _sigmoid(g2) * o).astype(BF16)
        if fillers:
            fillers.pop(0)()
    while fillers:
        fillers.pop(0)()


def _att_bias():
    slopes = _alibi_slopes(ATT_Q_HEADS)
    qpos = np.arange(TC)[:, None] + TC
    kpos = np.arange(2 * TC)[None, :]
    dist = qpos - kpos
    valid = (dist >= 0) & (dist <= WINDOW)
    prev_cur = np.stack([np.where(valid, -slopes[h] * dist, NEG_INF) for h in range(ATT_Q_HEADS)])
    cur_prev = np.concatenate([prev_cur[:, :, TC:], prev_cur[:, :, :TC]], axis=2)
    first = np.where(np.arange(2 * TC) >= TC, NEG_INF, 0.0)[None, :]
    return jnp.asarray(cur_prev, F32), jnp.asarray(prev_cur, F32), jnp.asarray(first, F32)


def _att_prompt_body(n, layer, sink_ref, pa_ref, bias_cp_ref, bias_pc_ref, first_ref, o_ref, kout_ref, vout_ref,
                     k_scr, v_scr, bias_scr):
    nb = pa_ref.shape[0]

    @pl.when(n == 0)
    def _():
        k_scr[...] = jnp.zeros_like(k_scr)
        v_scr[:, :, 0:LANES] = jnp.zeros((nb, 2 * TC, LANES), BF16)
        v_scr[:, :, LANES:2 * LANES] = jnp.ones((nb, 2 * TC, LANES), BF16)

    even = n % 2 == 0
    row0 = pl.multiple_of((n % 2) * TC, TC)
    first = jnp.where(n == 0, first_ref[...], jnp.zeros_like(first_ref))
    for head in range(ATT_Q_HEADS):
        bias_scr[head] = jnp.where(even, bias_cp_ref[head], bias_pc_ref[head]) + first

    lo = lax.broadcasted_iota(jnp.int32, (TC, LANES), 1) < HALF
    zero = jnp.zeros((TC, LANES), F32)
    swap = lambda a: pltpu.roll(a, HALF, 1)
    for b in range(nb):
        kcur = pa_ref[b, :, ATT_WIDTH:ATT_WIDTH + LANES]
        vcur = pa_ref[b, :, ATT_WIDTH + LANES:ATT_WIDTH + 2 * LANES]
        kout_ref[b] = kcur
        vout_ref[b] = vcur
        k_scr[b, pl.ds(row0, TC), :] = kcur.astype(BF16)
        v_scr[b, pl.ds(row0, TC), 0:LANES] = vcur.astype(BF16)
        rows, moved = [], []
        for head in range(ATT_Q_HEADS):
            qcol = pa_ref[b, :, LANES * (head // 2):LANES * (head // 2 + 1)]
            kv = head // ATT_GROUP
            moved.append(head % 2 != kv)
            if moved[-1]:
                qcol = swap(qcol)
            rows.append(jnp.where(lo, qcol, zero) if kv == 0 else jnp.where(lo, zero, qcol))
        qs = jnp.concatenate(rows, axis=0).astype(BF16)
        s_all = _dot_nt(qs, k_scr[b])
        es, ms = [], []
        for head in range(ATT_Q_HEADS):
            s = s_all[head * TC:(head + 1) * TC] + bias_scr[head]
            m = jnp.maximum(jnp.max(s, -1, keepdims=True), sink_ref[layer, head])
            es.append(jnp.exp(s - m).astype(BF16))
            ms.append(m)
        r_all = _dot(jnp.concatenate(es, axis=0), v_scr[b])
        outs = []
        for head in range(ATT_Q_HEADS):
            r = r_all[head * TC:(head + 1) * TC]
            den = r[:, LANES:2 * LANES] + jnp.exp(sink_ref[layer, head] - ms[head])
            out = r[:, 0:LANES] / den
            outs.append(swap(out) if moved[head] else out)
        for c in range(3):
            o_ref[b, :, LANES * c:LANES * (c + 1)] = jnp.where(lo, outs[2 * c], outs[2 * c + 1]).astype(BF16)


def _mixer_kernel(sink_ref, x_ref, w_ref,
                  bre_ref, bim_ref, cre_ref, cim_ref, ar_ref, ai_ref, d_ref, gw_ref, gb_ref,
                  dmask_ref, kvdec_ref, indec_ref, cdec_ref, blk_ref, gng_ref, gnb_ref,
                  bias_cp_ref, bias_pc_ref, first_ref,
                  z_ref, or_ref, oa_ref, hlast_ref, slast_ref, kout_ref, vout_ref,
                  pr_scr, pa_scr, h_scr, hc_scr, s_scr, k_scr, v_scr, bias_scr, *, layer):
    n = pl.program_id(0)
    nb = x_ref.shape[0]
    xb = x_ref[...].reshape(nb * TC, D_MODEL).astype(BF16)
    u3 = _dot(xb, w_ref[:, 0:COL_RET]).reshape(nb, TC, SSM_WIDTH)

    def project(dst, col0, c0, c1):
        def emit():
            dst[:, :, c0:c1] = _dot(xb, w_ref[:, col0 + c0:col0 + c1]).reshape(nb, TC, c1 - c0)
        return emit

    pieces = [project(pr_scr, COL_RET, c, c + MXU_COLS) for c in range(0, RET_COLS, MXU_COLS)]
    pieces += [project(pa_scr, COL_ATT, c, min(c + MXU_COLS, ATT_COLS)) for c in range(0, ATT_COLS, MXU_COLS)]
    s5_out = _s5_prompt_body(n, u3, bre_ref, bim_ref, cre_ref, cim_ref, ar_ref, ai_ref, d_ref, gw_ref, gb_ref,
                             z_ref, hlast_ref, h_scr, hc_scr, fillers=pieces)
    _ret_prompt_body(n, pr_scr, dmask_ref, kvdec_ref, indec_ref, cdec_ref, blk_ref, gng_ref, gnb_ref,
                     or_ref, slast_ref, s_scr, fillers=s5_out)
    _att_prompt_body(n, layer, sink_ref, pa_scr, bias_cp_ref, bias_pc_ref, first_ref, oa_ref, kout_ref, vout_ref,
                     k_scr, v_scr, bias_scr)


def _mixer_prompt(x3, p, layer):
    nb, length, _ = x3.shape
    chunk = lambda c: pl.BlockSpec((nb, TC, c), lambda n: (0, n, 0))
    fixed = lambda *shape: pl.BlockSpec(shape, lambda n: (0,) * len(shape))
    ret_consts = _ret_consts()
    att_consts = _att_bias()
    return pl.pallas_call(
        functools.partial(_mixer_kernel, layer=layer),
        grid=(length // TC,),
        in_specs=[pl.BlockSpec(memory_space=pltpu.SMEM), chunk(D_MODEL), _layer_spec((D_MODEL, IN_WIDTH), layer)]
                 + _s5_weight_specs(layer)
                 + [_const_spec(c.shape) for c in ret_consts]
                 + [_layer_spec((1, RET_WIDTH), layer), _layer_spec((1, RET_WIDTH), layer)]
                 + [_const_spec(c.shape) for c in att_consts],
        out_specs=[chunk(SSM_WIDTH), chunk(RET_WIDTH), chunk(ATT_WIDTH),
                   fixed(nb, 2 * SSM_LANES), fixed(nb, 2, LANES, LANES), fixed(nb, TC, LANES), fixed(nb, TC, LANES)],
        out_shape=[jax.ShapeDtypeStruct((nb, length, SSM_WIDTH), BF16),
                   jax.ShapeDtypeStruct((nb, length, RET_WIDTH), BF16),
                   jax.ShapeDtypeStruct((nb, length, ATT_WIDTH), BF16),
                   jax.ShapeDtypeStruct((nb, 2 * SSM_LANES), F32),
                   jax.ShapeDtypeStruct((nb, 2, LANES, LANES), F32),
                   jax.ShapeDtypeStruct((nb, TC, LANES), F32),
                   jax.ShapeDtypeStruct((nb, TC, LANES), F32)],
        scratch_shapes=[pltpu.VMEM((nb, TC, RET_COLS), F32), pltpu.VMEM((nb, TC, ATT_COLS), F32),
                        pltpu.VMEM((2 * N_TILES, nb * TC, LANES), F32), pltpu.VMEM((nb, 2 * SSM_LANES), F32),
                        pltpu.VMEM((nb, 2, LANES, LANES), F32),
                        pltpu.VMEM((nb, 2 * TC, LANES), BF16), pltpu.VMEM((nb, 2 * TC, 2 * LANES), BF16),
                        pltpu.VMEM((ATT_Q_HEADS, TC, 2 * TC), F32)],
        compiler_params=_cparams(1),
        name="mixer",
    )(p["sinks"], x3, p["w_in"], *_s5_weights(p), *ret_consts, p["gn_g"], p["gn_b"], *att_consts)


def _mix_sample_kernel(qt_ref, kt_ref, v4_ref, g4_ref, s_ref, qb_ref, kn_ref, vn_ref, kc_ref, vc_ref,
                       gng_ref, gnb_ref, bias_ref, sink_ref, *rest, layer):
    or_ref, oa_ref, sn_all, kout_all, vout_all = rest[-5:]
    if layer:
        for prev_ref, out_all in zip(rest[:3], (sn_all, kout_all, vout_all)):
            out_all[0:layer] = prev_ref[...]
    sn_ref, kout_ref, vout_ref = sn_all.at[layer], kout_all.at[layer], vout_all.at[layer]
    gammas = _ret_gammas()
    sink = sink_ref[:, 0:1]
    bias = bias_ref[...]

    for i in range(SAMPLE_ROWS):
        qt = qt_ref[i]
        kt = kt_ref[i]
        v4 = v4_ref[i]
        g4 = g4_ref[i]
        rows = []
        for h in range(RET_HEADS):
            sl = slice(RET_D * h, RET_D * (h + 1))
            sn = gammas[h] * s_ref[i, sl, :] + kt[:, h:h + 1] * v4[h:h + 1, :]
            sn_ref[i, sl, :] = sn
            rows.append(jnp.sum(qt[:, h:h + 1] * sn, axis=0, keepdims=True))
        o4 = jnp.concatenate(rows, axis=0)
        mu = jnp.mean(o4, -1, keepdims=True)
        d = o4 - mu
        var = jnp.mean(d * d, -1, keepdims=True)
        o4 = d * lax.rsqrt(var + LN_EPS) * gng_ref[...] + gnb_ref[...]
        or_ref[i] = g4 * _sigmoid(g4) * o4

        kb = kc_ref[i]
        vb = vc_ref[i]
        q8 = qb_ref[i]
        knew = kn_ref[i]
        vnew = vn_ref[i]
        s = _dot_nt(q8.astype(BF16), kb.astype(BF16)) + bias
        s_self = jnp.sum(q8 * knew, -1, keepdims=True)
        m = jnp.maximum(jnp.maximum(jnp.max(s, -1, keepdims=True), s_self), sink)
        e = jnp.exp(s - m)
        e_self = jnp.exp(s_self - m)
        den = jnp.sum(e, -1, keepdims=True) + e_self + jnp.exp(sink - m)
        oa_ref[i] = (_dot(e.astype(BF16), vb.astype(BF16)) + e_self * vnew) / den

        kout_ref[i, 0:WINDOW - 1, :] = kc_ref[i, 1:WINDOW, :]
        kout_ref[i, WINDOW - 1:WINDOW, :] = knew
        vout_ref[i, 0:WINDOW - 1, :] = vc_ref[i, 1:WINDOW, :]
        vout_ref[i, WINDOW - 1:WINDOW, :] = vnew


def _mix_sample(qt, kt, v4, g4, s_all, qb, kn, vn, kc_all, vc_all, p, bias8, layer, prev):
    nb = qt.shape[0]
    bb = SAMPLE_ROWS
    blk = lambda *dims: pl.BlockSpec((bb,) + dims, lambda i: (i,) + (0,) * len(dims))
    lblk = lambda *dims: pl.BlockSpec((None, bb) + dims, lambda i: (layer, i) + (0,) * len(dims))
    in_specs = [blk(RET_D, RET_HEADS), blk(RET_D, RET_HEADS), blk(RET_HEADS, RET_D), blk(RET_HEADS, RET_D),
                lblk(RET_WIDTH, RET_D), blk(8, LANES), blk(1, LANES), blk(1, LANES),
                lblk(WINDOW, LANES), lblk(WINDOW, LANES),
                _layer_spec((RET_HEADS, RET_D), layer), _layer_spec((RET_HEADS, RET_D), layer),
                _const_spec((8, LANES)), _layer_spec((8, LANES), layer)]
    args = [qt, kt, v4, g4, s_all, qb, kn, vn, kc_all, vc_all, p["gn_g4"], p["gn_b4"], bias8, p["sink8"]]
    stacked = lambda n_layers, *dims: pl.BlockSpec((n_layers, bb) + dims, lambda i: (0, i) + (0,) * len(dims))
    state_dims = ((RET_WIDTH, RET_D), (WINDOW, LANES), (WINDOW, LANES))
    if layer:
        in_specs += [stacked(layer, *dims) for dims in state_dims]
        args += list(prev)
    return pl.pallas_call(
        functools.partial(_mix_sample_kernel, layer=layer),
        grid=(nb // bb,),
        in_specs=in_specs,
        out_specs=[blk(RET_HEADS, RET_D), blk(8, LANES)] + [stacked(layer + 1, *dims) for dims in state_dims],
        out_shape=[jax.ShapeDtypeStruct((nb, RET_HEADS, RET_D), F32),
                   jax.ShapeDtypeStruct((nb, 8, LANES), F32)]
                  + [jax.ShapeDtypeStruct((layer + 1, nb) + dims, F32) for dims in state_dims],
        compiler_params=_cparams(1),
        name="mix_sample",
    )(*args)


def _outffn_kernel(z_ref, r_ref, a_ref, x_ref, wo_ref, g1_ref, b1_ref, w1_ref, w3_ref, w2_ref, g2_ref, b2_ref, o_ref):
    c1, c2 = SSM_WIDTH, SSM_WIDTH + RET_WIDTH
    mix = _dot(z_ref[...], wo_ref[0:c1]) + _dot(r_ref[...], wo_ref[c1:c2]) + _dot(a_ref[...], wo_ref[c2:D_MODEL])
    x = _layernorm(ALPHA * x_ref[...] + mix, g1_ref[...], b1_ref[...])
    xb = x.astype(BF16)
    h1 = _dot(xb, w1_ref[...])
    h3 = _dot(xb, w3_ref[...])
    hid = (h1 * _sigmoid(h1) * h3).astype(BF16)
    o_ref[...] = _layernorm(ALPHA * x + _dot(hid, w2_ref[...]), g2_ref[...], b2_ref[...])


def _outffn(z, r, a, x2, p, layer):
    t = x2.shape[0]
    tm = min(ROW_TILE, t)
    row = lambda c: pl.BlockSpec((tm, c), lambda i: (i, 0))
    vec = _layer_spec((1, D_MODEL), layer)
    return pl.pallas_call(
        _outffn_kernel,
        grid=(t // tm,),
        in_specs=[row(SSM_WIDTH), row(RET_WIDTH), row(ATT_WIDTH), row(D_MODEL),
                  _layer_spec((D_MODEL, D_MODEL), layer), vec, vec,
                  _layer_spec((D_MODEL, FFN_HIDDEN), layer), _layer_spec((D_MODEL, FFN_HIDDEN), layer),
                  _layer_spec((FFN_HIDDEN, D_MODEL), layer), vec, vec],
        out_specs=row(D_MODEL),
        out_shape=jax.ShapeDtypeStruct((t, D_MODEL), F32),
        compiler_params=_cparams(1),
        name="outffn",
    )(z, r, a, x2, p["w_out"], p["ln1_g"], p["ln1_b"], p["w1"], p["w3"], p["w2"], p["ln2_g"], p["ln2_b"])


def _prep_params(w_in, lam_re, lam_im, log_step, b_re, b_im, c_re, c_im, ssm_d, glu_w, glu_b,
                 gn_g, gn_b, sinks, w_out, ln1_g, ln1_b, w1, w3, w2, ln2_g, ln2_b):
    col_scale = np.ones((IN_WIDTH,), np.float32)
    col_scale[COL_RET + RET_WIDTH:COL_RET + 2 * RET_WIDTH] = RET_D ** -0.5
    col_scale[COL_ATT:COL_ATT + ATT_WIDTH] = ATT_D ** -0.5

    step = jnp.exp(log_step)[..., None]
    mag = jnp.exp(lam_re * step)
    a_re, a_im = mag * jnp.cos(lam_im * step), mag * jnp.sin(lam_im * step)
    den = lam_re * lam_re + lam_im * lam_im
    k_re = ((a_re - 1.0) * lam_re + a_im * lam_im) / den
    k_im = (a_im * lam_re - (a_re - 1.0) * lam_im) / den
    bb_re = k_re[..., None] * b_re - k_im[..., None] * b_im
    bb_im = k_re[..., None] * b_im + k_im[..., None] * b_re
    eye8 = jnp.eye(8, dtype=F32)

    def b_blocks(m):
        m = m.reshape(DEPTH, 3, 8, SSM_STATE, SSM_CH)
        return jnp.einsum("ligph,gk->lighkp", m, eye8).reshape(DEPTH, 3, LANES, 512).astype(BF16)

    def c_blocks(m):
        m = m.reshape(DEPTH, 3, 8, SSM_CH, SSM_STATE)
        return jnp.einsum("lighp,gk->likpgh", m, eye8).reshape(DEPTH, 3, 512, LANES).astype(BF16)

    row = lambda a: a.reshape(DEPTH, 1, -1)
    sink8 = jnp.concatenate([sinks, jnp.zeros((DEPTH, 2), F32)], axis=1)
    return dict(
        w_in=(w_in * col_scale).astype(BF16),
        bre=b_blocks(bb_re), bim=b_blocks(bb_im), cre=c_blocks(c_re), cim=c_blocks(-c_im),
        ar=row(a_re), ai=row(a_im),
        ssm_d=row(ssm_d), glu_w=glu_w.astype(BF16), glu_b=row(glu_b),
        gn_g=row(gn_g), gn_b=row(gn_b),
        gn_g4=gn_g.reshape(DEPTH, RET_HEADS, RET_D), gn_b4=gn_b.reshape(DEPTH, RET_HEADS, RET_D),
        sinks=sinks, sink8=jnp.broadcast_to(sink8[:, :, None], (DEPTH, 8, LANES)),
        w_out=w_out.astype(BF16), ln1_g=row(ln1_g), ln1_b=row(ln1_b),
        w1=w1.astype(BF16), w3=w3.astype(BF16), w2=w2.astype(BF16), ln2_g=row(ln2_g), ln2_b=row(ln2_b),
    )


def _prompt_layer(x3, p, layer):
    nb, length, _ = x3.shape
    t = nb * length
    x2 = x3.reshape(t, D_MODEL)
    z, o_r, o_a, hlast, sblk, klast, vlast = _mixer_prompt(x3, p, layer)
    y = _outffn(z.reshape(t, SSM_WIDTH), o_r.reshape(t, RET_WIDTH), o_a.reshape(t, ATT_WIDTH), x2, p, layer)
    y = y.reshape(nb, length, D_MODEL)
    s4 = jnp.stack([sblk[:, h // 2, HALF * (h % 2):HALF * (h % 2 + 1), HALF * (h % 2):HALF * (h % 2 + 1)]
                    for h in range(RET_HEADS)], axis=1)
    return y, hlast, s4, klast, vlast


def _sample_bias8():
    slopes = _alibi_slopes(ATT_Q_HEADS)
    dist = WINDOW - np.arange(WINDOW)
    bias = np.zeros((8, WINDOW))
    for h in range(ATT_Q_HEADS):
        bias[h] = -slopes[h] * dist
    return jnp.asarray(bias, F32)


def _sample_layer(x2, p, layer, h0r, h0i, s_all, kc_all, vc_all, prev):
    nb = x2.shape[0]
    pu, pr, pa = _inproj(x2, p, layer)
    z, h_re, h_im = _s5_sample(pu, h0r, h0i, p, layer)
    heads = lambda a: a.reshape(nb, RET_HEADS, RET_D)
    q4, k4, v4, g4 = (heads(pr[:, RET_WIDTH * i:RET_WIDTH * (i + 1)]) for i in range(4))
    q6 = pa[:, 0:ATT_WIDTH].reshape(nb, ATT_Q_HEADS, ATT_D)
    kv_of = np.arange(ATT_Q_HEADS) // ATT_GROUP
    qb = jnp.zeros((nb, 8, ATT_KV_HEADS, ATT_D), F32).at[:, np.arange(ATT_Q_HEADS), kv_of].set(q6)
    kn = pa[:, ATT_WIDTH:ATT_WIDTH + LANES].reshape(nb, 1, LANES)
    vn = pa[:, ATT_WIDTH + LANES:ATT_WIDTH + 2 * LANES].reshape(nb, 1, LANES)
    o_r, o8, s_new, k_out, v_out = _mix_sample(
        q4.transpose(0, 2, 1), k4.transpose(0, 2, 1), v4, g4, s_all, qb.reshape(nb, 8, LANES), kn, vn,
        kc_all, vc_all, p, _sample_bias8(), layer, prev)
    o_r = o_r.reshape(nb, RET_WIDTH).astype(BF16)
    o_a = o8.reshape(nb, 8, ATT_KV_HEADS, ATT_D)[:, np.arange(ATT_Q_HEADS), kv_of].reshape(nb, ATT_WIDTH).astype(BF16)
    y = _outffn(z, o_r, o_a, x2, p, layer)
    return y, h_re, h_im, (s_new, k_out, v_out)


def kernel(x_prompt, x_sample, state_ssm_re, state_ssm_im, state_ret, cache_win_k, cache_win_v, w_in, ssm_lambda_re, ssm_lambda_im, ssm_log_step, ssm_b_re, ssm_b_im, ssm_c_re, ssm_c_im, ssm_d, ssm_glu_w, ssm_glu_b, ret_gn_g, ret_gn_b, attn_sinks, w_out, ln1_g, ln1_b, ffn_w1, ffn_w3, ffn_w2, ln2_g, ln2_b):
    p = _prep_params(w_in, ssm_lambda_re, ssm_lambda_im, ssm_log_step, ssm_b_re, ssm_b_im, ssm_c_re, ssm_c_im,
                     ssm_d, ssm_glu_w, ssm_glu_b, ret_gn_g, ret_gn_b, attn_sinks, w_out, ln1_g, ln1_b,
                     ffn_w1, ffn_w3, ffn_w2, ln2_g, ln2_b)

    xp = x_prompt
    pb = x_prompt.shape[0]
    p_out = []
    for layer in range(DEPTH):
        xp, *states = _prompt_layer(xp, p, layer)
        p_out.append(states)
    stack = lambda outs, i: jnp.stack([o[i] for o in outs])
    p_h = stack(p_out, 0)
    p_kv_shape = (DEPTH, pb, TC, ATT_KV_HEADS, ATT_D)

    nb = x_sample.shape[0]
    xs = x_sample.reshape(nb, D_MODEL)
    h0r = state_ssm_re.reshape(DEPTH, nb, SSM_LANES)
    h0i = state_ssm_im.reshape(DEPTH, nb, SSM_LANES)
    s_all = state_ret.reshape(DEPTH, nb, RET_WIDTH, RET_D)
    kc_all = cache_win_k.reshape(DEPTH, nb, WINDOW, LANES)
    vc_all = cache_win_v.reshape(DEPTH, nb, WINDOW, LANES)
    prev = None
    s_h = []
    for layer in range(DEPTH):
        xs, h_re, h_im, prev = _sample_layer(xs, p, layer, h0r, h0i, s_all, kc_all, vc_all, prev)
        s_h.append((h_re, h_im))
    s_ret, s_k, s_v = prev
    ssm_shape = (DEPTH, nb, SSM_GROUPS, SSM_STATE)
    s_kv_shape = (DEPTH, nb, WINDOW, ATT_KV_HEADS, ATT_D)

    return (xp, xs.reshape(nb, 1, D_MODEL),
            p_h[:, :, :SSM_LANES].reshape(DEPTH, pb, SSM_GROUPS, SSM_STATE),
            p_h[:, :, SSM_LANES:].reshape(DEPTH, pb, SSM_GROUPS, SSM_STATE),
            stack(p_out, 1), stack(p_out, 2).reshape(p_kv_shape), stack(p_out, 3).reshape(p_kv_shape),
            stack(s_h, 0).reshape(ssm_shape), stack(s_h, 1).reshape(ssm_shape),
            s_ret.reshape(DEPTH, nb, RET_HEADS, RET_D, RET_D), s_k.reshape(s_kv_shape), s_v.reshape(s_kv_shape))
```

```python
import functools
import math

import numpy as np
import jax
import jax.numpy as jnp
from jax import lax
from jax.experimental import pallas as pl
from jax.experimental.pallas import tpu as pltpu

F32 = jnp.float32
BF16 = jnp.bfloat16

D_MODEL = 1024
DEPTH = 2
SSM_GROUPS = 24
SSM_CH = 16
SSM_STATE = 64
SSM_WIDTH = SSM_GROUPS * SSM_CH
SSM_LANES = SSM_GROUPS * SSM_STATE
RET_HEADS = 4
RET_D = 64
RET_WIDTH = RET_HEADS * RET_D
ATT_Q_HEADS = 6
ATT_KV_HEADS = 2
ATT_GROUP = ATT_Q_HEADS // ATT_KV_HEADS
ATT_D = 64
ATT_WIDTH = ATT_Q_HEADS * ATT_D
WINDOW = 128
FFN_HIDDEN = 2816
IN_WIDTH = 2048
ALPHA = (2 * DEPTH) ** 0.25
LN_EPS = 1e-5
NEG_INF = -1e30

COL_RET = SSM_WIDTH
COL_ATT = SSM_WIDTH + 4 * RET_WIDTH
RET_COLS = 4 * RET_WIDTH
ATT_COLS = ATT_WIDTH + 2 * ATT_KV_HEADS * ATT_D

TC = 128
LANES = 128
N_TILES = SSM_LANES // LANES
HALF = 64
MXU_COLS = 256
SCAN_PIECES = 4
ROW_TILE = 512
SAMPLE_ROWS = 16
VMEM_LIMIT = 56 * 2 ** 20


def _alibi_slopes(n):
    def pow2(k):
        start = 2.0 ** (-8.0 / k)
        return [start ** (i + 1) for i in range(k)]
    if n & (n - 1) == 0:
        return pow2(n)
    c = 2 ** int(math.floor(math.log2(n)))
    return pow2(c) + pow2(2 * c)[0::2][: n - c]


def _cparams(n_axes):
    return pltpu.CompilerParams(dimension_semantics=("arbitrary",) * n_axes, vmem_limit_bytes=VMEM_LIMIT)


def _const_spec(shape):
    nd = len(shape)
    return pl.BlockSpec(shape, lambda *_: (0,) * nd, pipeline_mode=pl.Buffered(1))


def _layer_spec(shape, layer):
    nd = len(shape)
    return pl.BlockSpec((None,) + tuple(shape), lambda *_: (layer,) + (0,) * nd, pipeline_mode=pl.Buffered(1))


def _dot(a, b):
    return jnp.dot(a, b, preferred_element_type=F32)


def _dot_nt(a, b):
    return lax.dot_general(a, b, (((1,), (1,)), ((), ())), preferred_element_type=F32)


def _dot_tn(a, b):
    return lax.dot_general(a, b, (((0,), (0,)), ((), ())), preferred_element_type=F32)


def _layernorm(v, g, b):
    mu = jnp.mean(v, -1, keepdims=True)
    d = v - mu
    var = jnp.mean(d * d, -1, keepdims=True)
    return d * lax.rsqrt(var + LN_EPS) * g + b


def _gelu_tanh(x):
    return 0.5 * x * (1.0 + jnp.tanh(math.sqrt(2.0 / math.pi) * (x + 0.044715 * (x * x * x))))


def _sigmoid(x):
    return 1.0 / (1.0 + jnp.exp(-x))


def _inproj_kernel(x_ref, w_ref, pu_ref, pr_ref, pa_ref):
    xb = x_ref[...].astype(BF16)
    pu_ref[...] = _dot(xb, w_ref[:, 0:COL_RET])
    pr_ref[...] = _dot(xb, w_ref[:, COL_RET:COL_ATT])
    pa_ref[...] = _dot(xb, w_ref[:, COL_ATT:IN_WIDTH])


def _inproj(x2, p, layer):
    t = x2.shape[0]
    tm = min(ROW_TILE, t)
    row = lambda c: pl.BlockSpec((tm, c), lambda i: (i, 0))
    return pl.pallas_call(
        _inproj_kernel,
        grid=(t // tm,),
        in_specs=[row(D_MODEL), _layer_spec((D_MODEL, IN_WIDTH), layer)],
        out_specs=[row(SSM_WIDTH), row(RET_COLS), row(ATT_COLS)],
        out_shape=[jax.ShapeDtypeStruct((t, SSM_WIDTH), F32),
                   jax.ShapeDtypeStruct((t, RET_COLS), F32),
                   jax.ShapeDtypeStruct((t, ATT_COLS), F32)],
        compiler_params=_cparams(1),
        name="inproj",
    )(x2, p["w_in"])


def _s5_bu(ub, bre_ref, bim_ref):
    res, ims = [], []
    for i in range(3):
        ui = ub[:, LANES * i:LANES * (i + 1)]
        res.append(_dot(ui, bre_ref[i]))
        ims.append(_dot(ui, bim_ref[i]))
    return res, ims


def _s5_out(u, h_re, h_im, cre_ref, cim_ref, d_ref, gw_ref, gb_ref):
    ys = []
    for i in range(3):
        ys.append(_dot(h_re[i].astype(BF16), cre_ref[i]) + _dot(h_im[i].astype(BF16), cim_ref[i]))
    y = jnp.concatenate(ys, axis=1) + d_ref[...] * u
    z = _gelu_tanh(y)
    gate = _sigmoid(_dot(z.astype(BF16), gw_ref[...]) + gb_ref[...])
    return z * gate


def _s5_prompt_body(n, u3, bre_ref, bim_ref, cre_ref, cim_ref, ar_ref, ai_ref, d_ref, gw_ref, gb_ref,
                    z_ref, hlast_ref, h_scr, hc_scr, fillers=()):
    nb = u3.shape[0]

    u = pltpu.einshape("btc->(tb)c", u3)

    res, ims = _s5_bu(u.astype(BF16), bre_ref, bim_ref)
    for i in range(3):
        for k in range(4):
            h_scr[4 * i + k] = res[i][:, LANES * k:LANES * (k + 1)]
            h_scr[N_TILES + 4 * i + k] = ims[i][:, LANES * k:LANES * (k + 1)]

    def step(t, carry):
        r0 = t * nb
        out = []
        for c in range(N_TILES):
            sl = slice(LANES * c, LANES * (c + 1))
            ar, ai = ar_ref[:, sl], ai_ref[:, sl]
            hr, hi = carry[c], carry[N_TILES + c]
            nr = ar * hr - ai * hi + h_scr[c, pl.ds(r0, nb), :]
            ni = ar * hi + ai * hr + h_scr[N_TILES + c, pl.ds(r0, nb), :]
            h_scr[c, pl.ds(r0, nb), :] = nr
            h_scr[N_TILES + c, pl.ds(r0, nb), :] = ni
            out.append((nr, ni))
        return tuple(o[0] for o in out) + tuple(o[1] for o in out)

    fin = tuple(hc_scr[:, LANES * c:LANES * (c + 1)] for c in range(2 * N_TILES))
    groups = len(fillers) + 1
    for g in range(groups):
        for t in range(TC * g // groups, TC * (g + 1) // groups):
            fin = step(t, fin)
        if g < len(fillers):
            fillers[g]()
    for c in range(2 * N_TILES):
        hc_scr[:, LANES * c:LANES * (c + 1)] = fin[c]
        hlast_ref[:, LANES * c:LANES * (c + 1)] = fin[c]

    ys = []

    def out_piece(i):
        def emit():
            h_re = jnp.concatenate([h_scr[4 * i + k] for k in range(4)], axis=1).astype(BF16)
            h_im = jnp.concatenate([h_scr[N_TILES + 4 * i + k] for k in range(4)], axis=1).astype(BF16)
            ys.append(_dot(h_re, cre_ref[i]) + _dot(h_im, cim_ref[i]))
        return emit

    def finish(fillers=()):
        fillers = list(fillers)
        y = jnp.concatenate(ys, axis=1) + d_ref[...] * u
        if fillers:
            fillers.pop(0)()
        z = _gelu_tanh(y)
        if fillers:
            fillers.pop(0)()
        gate = _sigmoid(_dot(z.astype(BF16), gw_ref[...]) + gb_ref[...])
        while fillers:
            fillers.pop(0)()
        z_ref[...] = pltpu.einshape("(tb)c->btc", z * gate, b=nb).astype(BF16)

    return [out_piece(i) for i in range(3)] + [finish]


def _s5_weight_specs(layer):
    ls = lambda *shape: _layer_spec(shape, layer)
    return [ls(3, LANES, 512), ls(3, LANES, 512), ls(3, 512, LANES), ls(3, 512, LANES),
            ls(1, SSM_LANES), ls(1, SSM_LANES), ls(1, SSM_WIDTH), ls(SSM_WIDTH, SSM_WIDTH), ls(1, SSM_WIDTH)]


def _s5_weights(p):
    return (p["bre"], p["bim"], p["cre"], p["cim"], p["ar"], p["ai"], p["ssm_d"], p["glu_w"], p["glu_b"])


def _s5_sample_kernel(u_ref, h0r_ref, h0i_ref, bre_ref, bim_ref, cre_ref, cim_ref, ar_ref, ai_ref, d_ref,
                      gw_ref, gb_ref, z_ref, hr_ref, hi_ref):
    u = u_ref[...]
    res, ims = _s5_bu(u.astype(BF16), bre_ref, bim_ref)
    h_re, h_im = [], []
    for i in range(3):
        sl = slice(512 * i, 512 * (i + 1))
        ar, ai = ar_ref[:, sl], ai_ref[:, sl]
        h0r, h0i = h0r_ref[:, sl], h0i_ref[:, sl]
        nr = ar * h0r - ai * h0i + res[i]
        ni = ar * h0i + ai * h0r + ims[i]
        hr_ref[:, sl] = nr
        hi_ref[:, sl] = ni
        h_re.append(nr)
        h_im.append(ni)
    z_ref[...] = _s5_out(u, h_re, h_im, cre_ref, cim_ref, d_ref, gw_ref, gb_ref).astype(BF16)


def _s5_sample(u, h0r, h0i, p, layer):
    nb = u.shape[0]
    full = lambda c: pl.BlockSpec((nb, c), lambda i: (0, 0))
    state = pl.BlockSpec((None, nb, SSM_LANES), lambda i: (layer, 0, 0))
    return pl.pallas_call(
        _s5_sample_kernel,
        grid=(1,),
        in_specs=[full(SSM_WIDTH), state, state] + _s5_weight_specs(layer),
        out_specs=[full(SSM_WIDTH), full(SSM_LANES), full(SSM_LANES)],
        out_shape=[jax.ShapeDtypeStruct((nb, SSM_WIDTH), BF16),
                   jax.ShapeDtypeStruct((nb, SSM_LANES), F32),
                   jax.ShapeDtypeStruct((nb, SSM_LANES), F32)],
        compiler_params=_cparams(1),
        name="s5_sample",
    )(u, h0r, h0i, *_s5_weights(p))


def _ret_gammas():
    return [1.0 - 2.0 ** (-5.0 - h) for h in range(RET_HEADS)]


def _ret_consts():
    log_g = np.log1p(-(2.0 ** (-5.0 - np.arange(RET_HEADS, dtype=np.float64))))
    idx = np.arange(TC)
    diff = idx[:, None] - idx[None, :]
    dmask = np.where(diff >= 0, np.exp(log_g[:, None, None] * np.maximum(diff, 0)), 0.0)
    inner = np.exp(log_g[:, None] * (idx + 1))
    kvd = np.exp(log_g[:, None] * (TC - 1 - idx))
    cd = np.exp(log_g * TC)
    indec = np.zeros((2, TC, LANES))
    kvdec = np.zeros((2, TC, LANES))
    cdec = np.zeros((2, LANES, LANES))
    blk = np.zeros((LANES, LANES))
    for j in range(2):
        for hh in range(2):
            sl = slice(HALF * hh, HALF * (hh + 1))
            indec[j, :, sl] = inner[2 * j + hh][:, None]
            kvdec[j, :, sl] = kvd[2 * j + hh][:, None]
            cdec[j, sl, sl] = cd[2 * j + hh]
            blk[sl, sl] = 1.0
    as32 = lambda a: jnp.asarray(a, F32)
    return as32(dmask), as32(kvdec), as32(indec), as32(cdec), as32(blk)


def _halfnorm(o, lo):
    zero = jnp.zeros_like(o)
    s_lo = jnp.sum(jnp.where(lo, o, zero), -1, keepdims=True)
    s_hi = jnp.sum(jnp.where(lo, zero, o), -1, keepdims=True)
    d = o - jnp.where(lo, s_lo, s_hi) * (1.0 / HALF)
    d2 = d * d
    v_lo = jnp.sum(jnp.where(lo, d2, zero), -1, keepdims=True)
    v_hi = jnp.sum(jnp.where(lo, zero, d2), -1, keepdims=True)
    var = jnp.where(lo, v_lo, v_hi) * (1.0 / HALF)
    return d * lax.rsqrt(var + LN_EPS)


def _ret_prompt_body(n, pr_ref, dmask_ref, kvdec_ref, indec_ref, cdec_ref, blk_ref, gng_ref, gnb_ref,
                     o_ref, slast_ref, s_scr):
    nb = pr_ref.shape[0]

    lo = lax.broadcasted_iota(jnp.int32, (TC, LANES), 1) < HALF
    zero = jnp.zeros((TC, LANES), F32)
    for b in range(nb):
        for j in range(2):
            q2 = pr_ref[b, :, LANES * j:LANES * (j + 1)]
            k2 = pr_ref[b, :, 256 + LANES * j:256 + LANES * (j + 1)]
            v2 = pr_ref[b, :, 512 + LANES * j:512 + LANES * (j + 1)]
            g2 = pr_ref[b, :, 768 + LANES * j:768 + LANES * (j + 1)]
            k2b = k2.astype(BF16)
            v2b = v2.astype(BF16)
            state = s_scr[b, j]
            qs = jnp.concatenate([jnp.where(lo, q2, zero), jnp.where(lo, zero, q2)], axis=0).astype(BF16)
            sc = _dot_nt(qs, k2b)
            qd = (q2 * indec_ref[j]).astype(BF16)
            lhs = jnp.concatenate(
                [jnp.concatenate([(sc[hh * TC:(hh + 1) * TC] * dmask_ref[2 * j + hh]).astype(BF16), qd], axis=1)
                 for hh in range(2)], axis=0)
            o2 = _dot(lhs, jnp.concatenate([v2b, state.astype(BF16)], axis=0))
            o = jnp.where(lo, o2[0:TC], o2[TC:2 * TC])
            kd = (k2 * kvdec_ref[j]).astype(BF16)
            s_new = cdec_ref[j] * state + blk_ref[...] * _dot_tn(kd, v2b)
            s_scr[b, j] = s_new
            slast_ref[b, j] = s_new
            sl = slice(LANES * j, LANES * (j + 1))
            o = _halfnorm(o, lo) * gng_ref[:, sl] + gnb_ref[:, sl]
            o_ref[b, :, sl] = (g2 * _sigmoid(g2) * o).astype(BF16)


def _att_bias():
    slopes = _alibi_slopes(ATT_Q_HEADS)
    qpos = np.arange(TC)[:, None] + TC
    kpos = np.arange(2 * TC)[None, :]
    dist = qpos - kpos
    valid = (dist >= 0) & (dist <= WINDOW)
    prev_cur = np.stack([np.where(valid, -slopes[h] * dist, NEG_INF) for h in range(ATT_Q_HEADS)])
    cur_prev = np.concatenate([prev_cur[:, :, TC:], prev_cur[:, :, :TC]], axis=2)
    first = np.where(np.arange(2 * TC) >= TC, NEG_INF, 0.0)[None, :]
    return jnp.asarray(cur_prev, F32), jnp.asarray(prev_cur, F32), jnp.asarray(first, F32)


def _att_prompt_body(n, layer, sink_ref, pa_ref, bias_cp_ref, bias_pc_ref, first_ref, o_ref, kout_ref, vout_ref,
                     k_scr, v_scr, bias_scr):
    nb = pa_ref.shape[0]

    even = n % 2 == 0
    row0 = pl.multiple_of((n % 2) * TC, TC)
    first = jnp.where(n == 0, first_ref[...], jnp.zeros_like(first_ref))
    for head in range(ATT_Q_HEADS):
        bias_scr[head] = jnp.where(even, bias_cp_ref[head], bias_pc_ref[head]) + first

    lo = lax.broadcasted_iota(jnp.int32, (TC, LANES), 1) < HALF
    zero = jnp.zeros((TC, LANES), F32)
    swap = lambda a: pltpu.roll(a, HALF, 1)
    moved = [head % 2 != head // ATT_GROUP for head in range(ATT_Q_HEADS)]

    def scores(b):
        kcur = pa_ref[b, :, ATT_WIDTH:ATT_WIDTH + LANES]
        vcur = pa_ref[b, :, ATT_WIDTH + LANES:ATT_WIDTH + 2 * LANES]
        kout_ref[b] = kcur
        vout_ref[b] = vcur
        k_scr[b, pl.ds(row0, TC), :] = kcur.astype(BF16)
        v_scr[b, pl.ds(row0, TC), 0:LANES] = vcur.astype(BF16)
        rows = []
        for head in range(ATT_Q_HEADS):
            qcol = pa_ref[b, :, LANES * (head // 2):LANES * (head // 2 + 1)]
            if moved[head]:
                qcol = swap(qcol)
            rows.append(jnp.where(lo, qcol, zero) if head // ATT_GROUP == 0 else jnp.where(lo, zero, qcol))
        qs = jnp.concatenate(rows, axis=0).astype(BF16)
        return _dot_nt(qs, k_scr[b])

    def softmax(s_all):
        es, ms = [], []
        for head in range(ATT_Q_HEADS):
            s = s_all[head * TC:(head + 1) * TC] + bias_scr[head]
            m = jnp.maximum(jnp.max(s, -1, keepdims=True), sink_ref[layer, head])
            es.append(jnp.exp(s - m).astype(BF16))
            ms.append(m)
        return jnp.concatenate(es, axis=0), ms

    def values(b, e_all, ms):
        r_all = _dot(e_all, v_scr[b])
        outs = []
        for head in range(ATT_Q_HEADS):
            r = r_all[head * TC:(head + 1) * TC]
            den = r[:, LANES:2 * LANES] + jnp.exp(sink_ref[layer, head] - ms[head])
            out = r[:, 0:LANES] / den
            outs.append(swap(out) if moved[head] else out)
        for c in range(3):
            o_ref[b, :, LANES * c:LANES * (c + 1)] = jnp.where(lo, outs[2 * c], outs[2 * c + 1]).astype(BF16)

    s_all, soft = {}, {}
    for k in range(nb + 2):
        if k < nb:
            s_all[k] = scores(k)
        if 0 <= k - 1 < nb:
            soft[k - 1] = softmax(s_all.pop(k - 1))
        if 0 <= k - 2 < nb:
            values(k - 2, *soft.pop(k - 2))


def _mixer_kernel(sink_ref, x_ref, w_ref,
                  bre_ref, bim_ref, cre_ref, cim_ref, ar_ref, ai_ref, d_ref, gw_ref, gb_ref,
                  dmask_ref, kvdec_ref, indec_ref, cdec_ref, blk_ref, gng_ref, gnb_ref,
                  bias_cp_ref, bias_pc_ref, first_ref,
                  z_ref, or_ref, oa_ref, hlast_ref, slast_ref, kout_ref, vout_ref,
                  pr_scr, pa_scr, h_scr, hc_scr, s_scr, k_scr, v_scr, bias_scr, *, layer):
    n = pl.program_id(0)
    nb = x_ref.shape[0]

    @pl.when(n == 0)
    def _():
        hc_scr[...] = jnp.zeros_like(hc_scr)
        s_scr[...] = jnp.zeros_like(s_scr)
        k_scr[...] = jnp.zeros_like(k_scr)
        v_scr[:, :, 0:LANES] = jnp.zeros((nb, 2 * TC, LANES), BF16)
        v_scr[:, :, LANES:2 * LANES] = jnp.ones((nb, 2 * TC, LANES), BF16)

    xb = x_ref[...].reshape(nb * TC, D_MODEL).astype(BF16)
    u3 = _dot(xb, w_ref[:, 0:COL_RET]).reshape(nb, TC, SSM_WIDTH)

    def project(dst, col0, c0, c1):
        def emit():
            dst[:, :, c0:c1] = _dot(xb, w_ref[:, col0 + c0:col0 + c1]).reshape(nb, TC, c1 - c0)
        return emit

    pieces = [project(pr_scr, COL_RET, c, c + MXU_COLS) for c in range(0, RET_COLS, MXU_COLS)]
    pieces += [project(pa_scr, COL_ATT, c, min(c + MXU_COLS, ATT_COLS)) for c in range(0, ATT_COLS, MXU_COLS)]
    *s5_maps, s5_finish = _s5_prompt_body(n, u3, bre_ref, bim_ref, cre_ref, cim_ref, ar_ref, ai_ref, d_ref, gw_ref,
                                          gb_ref, z_ref, hlast_ref, h_scr, hc_scr, fillers=pieces[:SCAN_PIECES])
    for emit in s5_maps:
        emit()
    s5_finish(pieces[SCAN_PIECES:])
    _ret_prompt_body(n, pr_scr, dmask_ref, kvdec_ref, indec_ref, cdec_ref, blk_ref, gng_ref, gnb_ref,
                     or_ref, slast_ref, s_scr)
    _att_prompt_body(n, layer, sink_ref, pa_scr, bias_cp_ref, bias_pc_ref, first_ref, oa_ref, kout_ref, vout_ref,
                     k_scr, v_scr, bias_scr)


def _mixer_prompt(x3, p, layer):
    nb, length, _ = x3.shape
    chunk = lambda c: pl.BlockSpec((nb, TC, c), lambda n: (0, n, 0))
    fixed = lambda *shape: pl.BlockSpec(shape, lambda n: (0,) * len(shape))
    ret_consts = _ret_consts()
    att_consts = _att_bias()
    return pl.pallas_call(
        functools.partial(_mixer_kernel, layer=layer),
        grid=(length // TC,),
        in_specs=[pl.BlockSpec(memory_space=pltpu.SMEM), chunk(D_MODEL), _layer_spec((D_MODEL, IN_WIDTH), layer)]
                 + _s5_weight_specs(layer)
                 + [_const_spec(c.shape) for c in ret_consts]
                 + [_layer_spec((1, RET_WIDTH), layer), _layer_spec((1, RET_WIDTH), layer)]
                 + [_const_spec(c.shape) for c in att_consts],
        out_specs=[chunk(SSM_WIDTH), chunk(RET_WIDTH), chunk(ATT_WIDTH),
                   fixed(nb, 2 * SSM_LANES), fixed(nb, 2, LANES, LANES), fixed(nb, TC, LANES), fixed(nb, TC, LANES)],
        out_shape=[jax.ShapeDtypeStruct((nb, length, SSM_WIDTH), BF16),
                   jax.ShapeDtypeStruct((nb, length, RET_WIDTH), BF16),
                   jax.ShapeDtypeStruct((nb, length, ATT_WIDTH), BF16),
                   jax.ShapeDtypeStruct((nb, 2 * SSM_LANES), F32),
                   jax.ShapeDtypeStruct((nb, 2, LANES, LANES), F32),
                   jax.ShapeDtypeStruct((nb, TC, LANES), F32),
                   jax.ShapeDtypeStruct((nb, TC, LANES), F32)],
        scratch_shapes=[pltpu.VMEM((nb, TC, RET_COLS), F32), pltpu.VMEM((nb, TC, ATT_COLS), F32),
                        pltpu.VMEM((2 * N_TILES, nb * TC, LANES), F32), pltpu.VMEM((nb, 2 * SSM_LANES), F32),
                        pltpu.VMEM((nb, 2, LANES, LANES), F32),
                        pltpu.VMEM((nb, 2 * TC, LANES), BF16), pltpu.VMEM((nb, 2 * TC, 2 * LANES), BF16),
                        pltpu.VMEM((ATT_Q_HEADS, TC, 2 * TC), F32)],
        compiler_params=_cparams(1),
        name="mixer",
    )(p["sinks"], x3, p["w_in"], *_s5_weights(p), *ret_consts, p["gn_g"], p["gn_b"], *att_consts)


def _mix_sample_kernel(qt_ref, kt_ref, v4_ref, g4_ref, s_ref, qb_ref, kn_ref, vn_ref, kc_ref, vc_ref,
                       gng_ref, gnb_ref, bias_ref, sink_ref, *rest, layer):
    or_ref, oa_ref, sn_all, kout_all, vout_all = rest[-5:]
    if layer:
        for prev_ref, out_all in zip(rest[:3], (sn_all, kout_all, vout_all)):
            out_all[0:layer] = prev_ref[...]
    sn_ref, kout_ref, vout_ref = sn_all.at[layer], kout_all.at[layer], vout_all.at[layer]
    gammas = _ret_gammas()
    sink = sink_ref[:, 0:1]
    bias = bias_ref[...]

    for i in range(SAMPLE_ROWS):
        qt = qt_ref[i]
        kt = kt_ref[i]
        v4 = v4_ref[i]
        g4 = g4_ref[i]
        rows = []
        for h in range(RET_HEADS):
            sl = slice(RET_D * h, RET_D * (h + 1))
            sn = gammas[h] * s_ref[i, sl, :] + kt[:, h:h + 1] * v4[h:h + 1, :]
            sn_ref[i, sl, :] = sn
            rows.append(jnp.sum(qt[:, h:h + 1] * sn, axis=0, keepdims=True))
        o4 = jnp.concatenate(rows, axis=0)
        mu = jnp.mean(o4, -1, keepdims=True)
        d = o4 - mu
        var = jnp.mean(d * d, -1, keepdims=True)
        o4 = d * lax.rsqrt(var + LN_EPS) * gng_ref[...] + gnb_ref[...]
        or_ref[i] = g4 * _sigmoid(g4) * o4

        kb = kc_ref[i]
        vb = vc_ref[i]
        q8 = qb_ref[i]
        knew = kn_ref[i]
        vnew = vn_ref[i]
        s = _dot_nt(q8.astype(BF16), kb.astype(BF16)) + bias
        s_self = jnp.sum(q8 * knew, -1, keepdims=True)
        m = jnp.maximum(jnp.maximum(jnp.max(s, -1, keepdims=True), s_self), sink)
        e = jnp.exp(s - m)
        e_self = jnp.exp(s_self - m)
        den = jnp.sum(e, -1, keepdims=True) + e_self + jnp.exp(sink - m)
        oa_ref[i] = (_dot(e.astype(BF16), vb.astype(BF16)) + e_self * vnew) / den

        kout_ref[i, 0:WINDOW - 1, :] = kc_ref[i, 1:WINDOW, :]
        kout_ref[i, WINDOW - 1:WINDOW, :] = knew
        vout_ref[i, 0:WINDOW - 1, :] = vc_ref[i, 1:WINDOW, :]
        vout_ref[i, WINDOW - 1:WINDOW, :] = vnew


def _mix_sample(qt, kt, v4, g4, s_all, qb, kn, vn, kc_all, vc_all, p, bias8, layer, prev):
    nb = qt.shape[0]
    bb = SAMPLE_ROWS
    blk = lambda *dims: pl.BlockSpec((bb,) + dims, lambda i: (i,) + (0,) * len(dims))
    lblk = lambda *dims: pl.BlockSpec((None, bb) + dims, lambda i: (layer, i) + (0,) * len(dims))
    in_specs = [blk(RET_D, RET_HEADS), blk(RET_D, RET_HEADS), blk(RET_HEADS, RET_D), blk(RET_HEADS, RET_D),
                lblk(RET_WIDTH, RET_D), blk(8, LANES), blk(1, LANES), blk(1, LANES),
                lblk(WINDOW, LANES), lblk(WINDOW, LANES),
                _layer_spec((RET_HEADS, RET_D), layer), _layer_spec((RET_HEADS, RET_D), layer),
                _const_spec((8, LANES)), _layer_spec((8, LANES), layer)]
    args = [qt, kt, v4, g4, s_all, qb, kn, vn, kc_all, vc_all, p["gn_g4"], p["gn_b4"], bias8, p["sink8"]]
    stacked = lambda n_layers, *dims: pl.BlockSpec((n_layers, bb) + dims, lambda i: (0, i) + (0,) * len(dims))
    state_dims = ((RET_WIDTH, RET_D), (WINDOW, LANES), (WINDOW, LANES))
    if layer:
        in_specs += [stacked(layer, *dims) for dims in state_dims]
        args += list(prev)
    return pl.pallas_call(
        functools.partial(_mix_sample_kernel, layer=layer),
        grid=(nb // bb,),
        in_specs=in_specs,
        out_specs=[blk(RET_HEADS, RET_D), blk(8, LANES)] + [stacked(layer + 1, *dims) for dims in state_dims],
        out_shape=[jax.ShapeDtypeStruct((nb, RET_HEADS, RET_D), F32),
                   jax.ShapeDtypeStruct((nb, 8, LANES), F32)]
                  + [jax.ShapeDtypeStruct((layer + 1, nb) + dims, F32) for dims in state_dims],
        compiler_params=_cparams(1),
        name="mix_sample",
    )(*args)


def _outffn_kernel(z_ref, r_ref, a_ref, x_ref, wo_ref, g1_ref, b1_ref, w1_ref, w3_ref, w2_ref, g2_ref, b2_ref, o_ref):
    c1, c2 = SSM_WIDTH, SSM_WIDTH + RET_WIDTH
    mix = _dot(z_ref[...], wo_ref[0:c1]) + _dot(r_ref[...], wo_ref[c1:c2]) + _dot(a_ref[...], wo_ref[c2:D_MODEL])
    x = _layernorm(ALPHA * x_ref[...] + mix, g1_ref[...], b1_ref[...])
    xb = x.astype(BF16)
    h1 = _dot(xb, w1_ref[...])
    h3 = _dot(xb, w3_ref[...])
    hid = (h1 * _sigmoid(h1) * h3).astype(BF16)
    o_ref[...] = _layernorm(ALPHA * x + _dot(hid, w2_ref[...]), g2_ref[...], b2_ref[...])


def _outffn(z, r, a, x2, p, layer):
    t = x2.shape[0]
    tm = min(ROW_TILE, t)
    row = lambda c: pl.BlockSpec((tm, c), lambda i: (i, 0))
    vec = _layer_spec((1, D_MODEL), layer)
    return pl.pallas_call(
        _outffn_kernel,
        grid=(t // tm,),
        in_specs=[row(SSM_WIDTH), row(RET_WIDTH), row(ATT_WIDTH), row(D_MODEL),
                  _layer_spec((D_MODEL, D_MODEL), layer), vec, vec,
                  _layer_spec((D_MODEL, FFN_HIDDEN), layer), _layer_spec((D_MODEL, FFN_HIDDEN), layer),
                  _layer_spec((FFN_HIDDEN, D_MODEL), layer), vec, vec],
        out_specs=row(D_MODEL),
        out_shape=jax.ShapeDtypeStruct((t, D_MODEL), F32),
        compiler_params=_cparams(1),
        name="outffn",
    )(z, r, a, x2, p["w_out"], p["ln1_g"], p["ln1_b"], p["w1"], p["w3"], p["w2"], p["ln2_g"], p["ln2_b"])


def _prep_params(w_in, lam_re, lam_im, log_step, b_re, b_im, c_re, c_im, ssm_d, glu_w, glu_b,
                 gn_g, gn_b, sinks, w_out, ln1_g, ln1_b, w1, w3, w2, ln2_g, ln2_b):
    col_scale = np.ones((IN_WIDTH,), np.float32)
    col_scale[COL_RET + RET_WIDTH:COL_RET + 2 * RET_WIDTH] = RET_D ** -0.5
    col_scale[COL_ATT:COL_ATT + ATT_WIDTH] = ATT_D ** -0.5

    step = jnp.exp(log_step)[..., None]
    mag = jnp.exp(lam_re * step)
    a_re, a_im = mag * jnp.cos(lam_im * step), mag * jnp.sin(lam_im * step)
    den = lam_re * lam_re + lam_im * lam_im
    k_re = ((a_re - 1.0) * lam_re + a_im * lam_im) / den
    k_im = (a_im * lam_re - (a_re - 1.0) * lam_im) / den
    bb_re = k_re[..., None] * b_re - k_im[..., None] * b_im
    bb_im = k_re[..., None] * b_im + k_im[..., None] * b_re
    eye8 = jnp.eye(8, dtype=F32)

    def b_blocks(m):
        m = m.reshape(DEPTH, 3, 8, SSM_STATE, SSM_CH)
        return jnp.einsum("ligph,gk->lighkp", m, eye8).reshape(DEPTH, 3, LANES, 512).astype(BF16)

    def c_blocks(m):
        m = m.reshape(DEPTH, 3, 8, SSM_CH, SSM_STATE)
        return jnp.einsum("lighp,gk->likpgh", m, eye8).reshape(DEPTH, 3, 512, LANES).astype(BF16)

    row = lambda a: a.reshape(DEPTH, 1, -1)
    sink8 = jnp.concatenate([sinks, jnp.zeros((DEPTH, 2), F32)], axis=1)
    return dict(
        w_in=(w_in * col_scale).astype(BF16),
        bre=b_blocks(bb_re), bim=b_blocks(bb_im), cre=c_blocks(c_re), cim=c_blocks(-c_im),
        ar=row(a_re), ai=row(a_im),
        ssm_d=row(ssm_d), glu_w=glu_w.astype(BF16), glu_b=row(glu_b),
        gn_g=row(gn_g), gn_b=row(gn_b),
        gn_g4=gn_g.reshape(DEPTH, RET_HEADS, RET_D), gn_b4=gn_b.reshape(DEPTH, RET_HEADS, RET_D),
        sinks=sinks, sink8=jnp.broadcast_to(sink8[:, :, None], (DEPTH, 8, LANES)),
        w_out=w_out.astype(BF16), ln1_g=row(ln1_g), ln1_b=row(ln1_b),
        w1=w1.astype(BF16), w3=w3.astype(BF16), w2=w2.astype(BF16), ln2_g=row(ln2_g), ln2_b=row(ln2_b),
    )


def _prompt_layer(x3, p, layer):
    nb, length, _ = x3.shape
    t = nb * length
    x2 = x3.reshape(t, D_MODEL)
    z, o_r, o_a, hlast, sblk, klast, vlast = _mixer_prompt(x3, p, layer)
    y = _outffn(z.reshape(t, SSM_WIDTH), o_r.reshape(t, RET_WIDTH), o_a.reshape(t, ATT_WIDTH), x2, p, layer)
    y = y.reshape(nb, length, D_MODEL)
    s4 = jnp.stack([sblk[:, h // 2, HALF * (h % 2):HALF * (h % 2 + 1), HALF * (h % 2):HALF * (h % 2 + 1)]
                    for h in range(RET_HEADS)], axis=1)
    return y, hlast, s4, klast, vlast


def _sample_bias8():
    slopes = _alibi_slopes(ATT_Q_HEADS)
    dist = WINDOW - np.arange(WINDOW)
    bias = np.zeros((8, WINDOW))
    for h in range(ATT_Q_HEADS):
        bias[h] = -slopes[h] * dist
    return jnp.asarray(bias, F32)


def _sample_layer(x2, p, layer, h0r, h0i, s_all, kc_all, vc_all, prev):
    nb = x2.shape[0]
    pu, pr, pa = _inproj(x2, p, layer)
    z, h_re, h_im = _s5_sample(pu, h0r, h0i, p, layer)
    heads = lambda a: a.reshape(nb, RET_HEADS, RET_D)
    q4, k4, v4, g4 = (heads(pr[:, RET_WIDTH * i:RET_WIDTH * (i + 1)]) for i in range(4))
    q6 = pa[:, 0:ATT_WIDTH].reshape(nb, ATT_Q_HEADS, ATT_D)
    kv_of = np.arange(ATT_Q_HEADS) // ATT_GROUP
    qb = jnp.zeros((nb, 8, ATT_KV_HEADS, ATT_D), F32).at[:, np.arange(ATT_Q_HEADS), kv_of].set(q6)
    kn = pa[:, ATT_WIDTH:ATT_WIDTH + LANES].reshape(nb, 1, LANES)
    vn = pa[:, ATT_WIDTH + LANES:ATT_WIDTH + 2 * LANES].reshape(nb, 1, LANES)
    o_r, o8, s_new, k_out, v_out = _mix_sample(
        q4.transpose(0, 2, 1), k4.transpose(0, 2, 1), v4, g4, s_all, qb.reshape(nb, 8, LANES), kn, vn,
        kc_all, vc_all, p, _sample_bias8(), layer, prev)
    o_r = o_r.reshape(nb, RET_WIDTH).astype(BF16)
    o_a = o8.reshape(nb, 8, ATT_KV_HEADS, ATT_D)[:, np.arange(ATT_Q_HEADS), kv_of].reshape(nb, ATT_WIDTH).astype(BF16)
    y = _outffn(z, o_r, o_a, x2, p, layer)
    return y, h_re, h_im, (s_new, k_out, v_out)


def kernel(x_prompt, x_sample, state_ssm_re, state_ssm_im, state_ret, cache_win_k, cache_win_v, w_in, ssm_lambda_re, ssm_lambda_im, ssm_log_step, ssm_b_re, ssm_b_im, ssm_c_re, ssm_c_im, ssm_d, ssm_glu_w, ssm_glu_b, ret_gn_g, ret_gn_b, attn_sinks, w_out, ln1_g, ln1_b, ffn_w1, ffn_w3, ffn_w2, ln2_g, ln2_b):
    p = _prep_params(w_in, ssm_lambda_re, ssm_lambda_im, ssm_log_step, ssm_b_re, ssm_b_im, ssm_c_re, ssm_c_im,
                     ssm_d, ssm_glu_w, ssm_glu_b, ret_gn_g, ret_gn_b, attn_sinks, w_out, ln1_g, ln1_b,
                     ffn_w1, ffn_w3, ffn_w2, ln2_g, ln2_b)

    xp = x_prompt
    pb = x_prompt.shape[0]
    p_out = []
    for layer in range(DEPTH):
        xp, *states = _prompt_layer(xp, p, layer)
        p_out.append(states)
    stack = lambda outs, i: jnp.stack([o[i] for o in outs])
    p_h = stack(p_out, 0)
    p_kv_shape = (DEPTH, pb, TC, ATT_KV_HEADS, ATT_D)

    nb = x_sample.shape[0]
    xs = x_sample.reshape(nb, D_MODEL)
    h0r = state_ssm_re.reshape(DEPTH, nb, SSM_LANES)
    h0i = state_ssm_im.reshape(DEPTH, nb, SSM_LANES)
    s_all = state_ret.reshape(DEPTH, nb, RET_WIDTH, RET_D)
    kc_all = cache_win_k.reshape(DEPTH, nb, WINDOW, LANES)
    vc_all = cache_win_v.reshape(DEPTH, nb, WINDOW, LANES)
    prev = None
    s_h = []
    for layer in range(DEPTH):
        xs, h_re, h_im, prev = _sample_layer(xs, p, layer, h0r, h0i, s_all, kc_all, vc_all, prev)
        s_h.append((h_re, h_im))
    s_ret, s_k, s_v = prev
    ssm_shape = (DEPTH, nb, SSM_GROUPS, SSM_STATE)
    s_kv_shape = (DEPTH, nb, WINDOW, ATT_KV_HEADS, ATT_D)

    return (xp, xs.reshape(nb, 1, D_MODEL),
            p_h[:, :, :SSM_LANES].reshape(DEPTH, pb, SSM_GROUPS, SSM_STATE),
            p_h[:, :, SSM_LANES:].reshape(DEPTH, pb, SSM_GROUPS, SSM_STATE),
            stack(p_out, 1), stack(p_out, 2).reshape(p_kv_shape), stack(p_out, 3).reshape(p_kv_shape),
            stack(s_h, 0).reshape(ssm_shape), stack(s_h, 1).reshape(ssm_shape),
            s_ret.reshape(DEPTH, nb, RET_HEADS, RET_D, RET_D), s_k.reshape(s_kv_shape), s_v.reshape(s_kv_shape))
```

```python
import functools
import math

import numpy as np
import jax
import jax.numpy as jnp
from jax import lax
from jax.experimental import pallas as pl
from jax.experimental.pallas import tpu as pltpu

F32 = jnp.float32
BF16 = jnp.bfloat16

D_MODEL = 1024
DEPTH = 2
SSM_GROUPS = 24
SSM_CH = 16
SSM_STATE = 64
SSM_WIDTH = SSM_GROUPS * SSM_CH
SSM_LANES = SSM_GROUPS * SSM_STATE
RET_HEADS = 4
RET_D = 64
RET_WIDTH = RET_HEADS * RET_D
ATT_Q_HEADS = 6
ATT_KV_HEADS = 2
ATT_GROUP = ATT_Q_HEADS // ATT_KV_HEADS
ATT_D = 64
ATT_WIDTH = ATT_Q_HEADS * ATT_D
WINDOW = 128
FFN_HIDDEN = 2816
IN_WIDTH = 2048
ALPHA = (2 * DEPTH) ** 0.25
LN_EPS = 1e-5
NEG_INF = -1e30

COL_RET = SSM_WIDTH
COL_ATT = SSM_WIDTH + 4 * RET_WIDTH
RET_COLS = 4 * RET_WIDTH
ATT_COLS = ATT_WIDTH + 2 * ATT_KV_HEADS * ATT_D

TC = 128
LANES = 128
N_TILES = SSM_LANES // LANES
HALF = 64
MXU_COLS = 256
SCAN_PIECES = 4
ROW_TILE = 512
FFN_PARTS = 2
SAMPLE_ROWS = 16
VMEM_LIMIT = 56 * 2 ** 20


def _alibi_slopes(n):
    def pow2(k):
        start = 2.0 ** (-8.0 / k)
        return [start ** (i + 1) for i in range(k)]
    if n & (n - 1) == 0:
        return pow2(n)
    c = 2 ** int(math.floor(math.log2(n)))
    return pow2(c) + pow2(2 * c)[0::2][: n - c]


def _cparams(n_axes):
    return pltpu.CompilerParams(dimension_semantics=("arbitrary",) * n_axes, vmem_limit_bytes=VMEM_LIMIT)


def _const_spec(shape):
    nd = len(shape)
    return pl.BlockSpec(shape, lambda *_: (0,) * nd, pipeline_mode=pl.Buffered(1))


def _layer_spec(shape, layer):
    nd = len(shape)
    return pl.BlockSpec((None,) + tuple(shape), lambda *_: (layer,) + (0,) * nd, pipeline_mode=pl.Buffered(1))


def _dot(a, b):
    return jnp.dot(a, b, preferred_element_type=F32)


def _dot_nt(a, b):
    return lax.dot_general(a, b, (((1,), (1,)), ((), ())), preferred_element_type=F32)


def _dot_tn(a, b):
    return lax.dot_general(a, b, (((0,), (0,)), ((), ())), preferred_element_type=F32)


def _layernorm(v, g, b):
    mu = jnp.mean(v, -1, keepdims=True)
    d = v - mu
    var = jnp.mean(d * d, -1, keepdims=True)
    return d * lax.rsqrt(var + LN_EPS) * g + b


def _gelu_tanh(x):
    return 0.5 * x * (1.0 + jnp.tanh(math.sqrt(2.0 / math.pi) * (x + 0.044715 * (x * x * x))))


def _sigmoid(x):
    return 1.0 / (1.0 + jnp.exp(-x))


def _inproj_kernel(x_ref, w_ref, pu_ref, pr_ref, pa_ref):
    xb = x_ref[...].astype(BF16)
    pu_ref[...] = _dot(xb, w_ref[:, 0:COL_RET])
    pr_ref[...] = _dot(xb, w_ref[:, COL_RET:COL_ATT])
    pa_ref[...] = _dot(xb, w_ref[:, COL_ATT:IN_WIDTH])


def _inproj(x2, p, layer):
    t = x2.shape[0]
    tm = min(ROW_TILE, t)
    row = lambda c: pl.BlockSpec((tm, c), lambda i: (i, 0))
    return pl.pallas_call(
        _inproj_kernel,
        grid=(t // tm,),
        in_specs=[row(D_MODEL), _layer_spec((D_MODEL, IN_WIDTH), layer)],
        out_specs=[row(SSM_WIDTH), row(RET_COLS), row(ATT_COLS)],
        out_shape=[jax.ShapeDtypeStruct((t, SSM_WIDTH), F32),
                   jax.ShapeDtypeStruct((t, RET_COLS), F32),
                   jax.ShapeDtypeStruct((t, ATT_COLS), F32)],
        compiler_params=_cparams(1),
        name="inproj",
    )(x2, p["w_in"])


def _s5_bu(ub, bre_ref, bim_ref):
    res, ims = [], []
    for i in range(3):
        ui = ub[:, LANES * i:LANES * (i + 1)]
        res.append(_dot(ui, bre_ref[i]))
        ims.append(_dot(ui, bim_ref[i]))
    return res, ims


def _s5_out(u, h_re, h_im, cre_ref, cim_ref, d_ref, gw_ref, gb_ref):
    ys = []
    for i in range(3):
        ys.append(_dot(h_re[i].astype(BF16), cre_ref[i]) + _dot(h_im[i].astype(BF16), cim_ref[i]))
    y = jnp.concatenate(ys, axis=1) + d_ref[...] * u
    z = _gelu_tanh(y)
    gate = _sigmoid(_dot(z.astype(BF16), gw_ref[...]) + gb_ref[...])
    return z * gate


def _s5_prompt_body(n, u3, bre_ref, bim_ref, cre_ref, cim_ref, ar_ref, ai_ref, d_ref, gw_ref, gb_ref,
                    z_ref, hlast_ref, h_scr, hc_scr, fillers=()):
    nb = u3.shape[0]

    u = pltpu.einshape("btc->(tb)c", u3)

    res, ims = _s5_bu(u.astype(BF16), bre_ref, bim_ref)
    for i in range(3):
        for k in range(4):
            h_scr[4 * i + k] = res[i][:, LANES * k:LANES * (k + 1)]
            h_scr[N_TILES + 4 * i + k] = ims[i][:, LANES * k:LANES * (k + 1)]

    def step(t, carry):
        r0 = t * nb
        out = []
        for c in range(N_TILES):
            sl = slice(LANES * c, LANES * (c + 1))
            ar, ai = ar_ref[:, sl], ai_ref[:, sl]
            hr, hi = carry[c], carry[N_TILES + c]
            nr = ar * hr - ai * hi + h_scr[c, pl.ds(r0, nb), :]
            ni = ar * hi + ai * hr + h_scr[N_TILES + c, pl.ds(r0, nb), :]
            h_scr[c, pl.ds(r0, nb), :] = nr
            h_scr[N_TILES + c, pl.ds(r0, nb), :] = ni
            out.append((nr, ni))
        return tuple(o[0] for o in out) + tuple(o[1] for o in out)

    fin = tuple(hc_scr[:, LANES * c:LANES * (c + 1)] for c in range(2 * N_TILES))
    groups = len(fillers) + 1
    for g in range(groups):
        for t in range(TC * g // groups, TC * (g + 1) // groups):
            fin = step(t, fin)
        if g < len(fillers):
            fillers[g]()
    for c in range(2 * N_TILES):
        hc_scr[:, LANES * c:LANES * (c + 1)] = fin[c]
        hlast_ref[:, LANES * c:LANES * (c + 1)] = fin[c]

    ys = []

    def out_piece(i):
        def emit():
            h_re = jnp.concatenate([h_scr[4 * i + k] for k in range(4)], axis=1).astype(BF16)
            h_im = jnp.concatenate([h_scr[N_TILES + 4 * i + k] for k in range(4)], axis=1).astype(BF16)
            ys.append(_dot(h_re, cre_ref[i]) + _dot(h_im, cim_ref[i]))
        return emit

    def finish(fillers=()):
        fillers = list(fillers)
        y = jnp.concatenate(ys, axis=1) + d_ref[...] * u
        if fillers:
            fillers.pop(0)()
        z = _gelu_tanh(y)
        if fillers:
            fillers.pop(0)()
        gate = _sigmoid(_dot(z.astype(BF16), gw_ref[...]) + gb_ref[...])
        while fillers:
            fillers.pop(0)()
        z_ref[...] = pltpu.einshape("(tb)c->btc", z * gate, b=nb).astype(BF16)

    return [out_piece(i) for i in range(3)] + [finish]


def _s5_weight_specs(layer):
    ls = lambda *shape: _layer_spec(shape, layer)
    return [ls(3, LANES, 512), ls(3, LANES, 512), ls(3, 512, LANES), ls(3, 512, LANES),
            ls(1, SSM_LANES), ls(1, SSM_LANES), ls(1, SSM_WIDTH), ls(SSM_WIDTH, SSM_WIDTH), ls(1, SSM_WIDTH)]


def _s5_weights(p):
    return (p["bre"], p["bim"], p["cre"], p["cim"], p["ar"], p["ai"], p["ssm_d"], p["glu_w"], p["glu_b"])


def _s5_sample_kernel(u_ref, h0r_ref, h0i_ref, bre_ref, bim_ref, cre_ref, cim_ref, ar_ref, ai_ref, d_ref,
                      gw_ref, gb_ref, z_ref, hr_ref, hi_ref):
    u = u_ref[...]
    res, ims = _s5_bu(u.astype(BF16), bre_ref, bim_ref)
    h_re, h_im = [], []
    for i in range(3):
        sl = slice(512 * i, 512 * (i + 1))
        ar, ai = ar_ref[:, sl], ai_ref[:, sl]
        h0r, h0i = h0r_ref[:, sl], h0i_ref[:, sl]
        nr = ar * h0r - ai * h0i + res[i]
        ni = ar * h0i + ai * h0r + ims[i]
        hr_ref[:, sl] = nr
        hi_ref[:, sl] = ni
        h_re.append(nr)
        h_im.append(ni)
    z_ref[...] = _s5_out(u, h_re, h_im, cre_ref, cim_ref, d_ref, gw_ref, gb_ref).astype(BF16)


def _s5_sample(u, h0r, h0i, p, layer):
    nb = u.shape[0]
    full = lambda c: pl.BlockSpec((nb, c), lambda i: (0, 0))
    state = pl.BlockSpec((None, nb, SSM_LANES), lambda i: (layer, 0, 0))
    return pl.pallas_call(
        _s5_sample_kernel,
        grid=(1,),
        in_specs=[full(SSM_WIDTH), state, state] + _s5_weight_specs(layer),
        out_specs=[full(SSM_WIDTH), full(SSM_LANES), full(SSM_LANES)],
        out_shape=[jax.ShapeDtypeStruct((nb, SSM_WIDTH), BF16),
                   jax.ShapeDtypeStruct((nb, SSM_LANES), F32),
                   jax.ShapeDtypeStruct((nb, SSM_LANES), F32)],
        compiler_params=_cparams(1),
        name="s5_sample",
    )(u, h0r, h0i, *_s5_weights(p))


def _ret_gammas():
    return [1.0 - 2.0 ** (-5.0 - h) for h in range(RET_HEADS)]


def _ret_consts():
    log_g = np.log1p(-(2.0 ** (-5.0 - np.arange(RET_HEADS, dtype=np.float64))))
    idx = np.arange(TC)
    diff = idx[:, None] - idx[None, :]
    dmask = np.where(diff >= 0, np.exp(log_g[:, None, None] * np.maximum(diff, 0)), 0.0)
    inner = np.exp(log_g[:, None] * (idx + 1))
    kvd = np.exp(log_g[:, None] * (TC - 1 - idx))
    cd = np.exp(log_g * TC)
    indec = np.zeros((2, TC, LANES))
    kvdec = np.zeros((2, TC, LANES))
    cdec = np.zeros((2, LANES, LANES))
    blk = np.zeros((LANES, LANES))
    for j in range(2):
        for hh in range(2):
            sl = slice(HALF * hh, HALF * (hh + 1))
            indec[j, :, sl] = inner[2 * j + hh][:, None]
            kvdec[j, :, sl] = kvd[2 * j + hh][:, None]
            cdec[j, sl, sl] = cd[2 * j + hh]
            blk[sl, sl] = 1.0
    as32 = lambda a: jnp.asarray(a, F32)
    return as32(dmask), as32(kvdec), as32(indec), as32(cdec), as32(blk)


def _halfnorm(o, lo):
    zero = jnp.zeros_like(o)
    s_lo = jnp.sum(jnp.where(lo, o, zero), -1, keepdims=True)
    s_hi = jnp.sum(jnp.where(lo, zero, o), -1, keepdims=True)
    d = o - jnp.where(lo, s_lo, s_hi) * (1.0 / HALF)
    d2 = d * d
    v_lo = jnp.sum(jnp.where(lo, d2, zero), -1, keepdims=True)
    v_hi = jnp.sum(jnp.where(lo, zero, d2), -1, keepdims=True)
    var = jnp.where(lo, v_lo, v_hi) * (1.0 / HALF)
    return d * lax.rsqrt(var + LN_EPS)


def _ret_prompt_body(n, pr_ref, dmask_ref, kvdec_ref, indec_ref, cdec_ref, blk_ref, gng_ref, gnb_ref,
                     o_ref, slast_ref, s_scr):
    nb = pr_ref.shape[0]

    lo = lax.broadcasted_iota(jnp.int32, (TC, LANES), 1) < HALF
    zero = jnp.zeros((TC, LANES), F32)
    for b in range(nb):
        for j in range(2):
            q2 = pr_ref[b, :, LANES * j:LANES * (j + 1)]
            k2 = pr_ref[b, :, 256 + LANES * j:256 + LANES * (j + 1)]
            v2 = pr_ref[b, :, 512 + LANES * j:512 + LANES * (j + 1)]
            g2 = pr_ref[b, :, 768 + LANES * j:768 + LANES * (j + 1)]
            k2b = k2.astype(BF16)
            v2b = v2.astype(BF16)
            state = s_scr[b, j]
            qs = jnp.concatenate([jnp.where(lo, q2, zero), jnp.where(lo, zero, q2)], axis=0).astype(BF16)
            sc = _dot_nt(qs, k2b)
            qd = (q2 * indec_ref[j]).astype(BF16)
            lhs = jnp.concatenate(
                [jnp.concatenate([(sc[hh * TC:(hh + 1) * TC] * dmask_ref[2 * j + hh]).astype(BF16), qd], axis=1)
                 for hh in range(2)], axis=0)
            o2 = _dot(lhs, jnp.concatenate([v2b, state.astype(BF16)], axis=0))
            o = jnp.where(lo, o2[0:TC], o2[TC:2 * TC])
            kd = (k2 * kvdec_ref[j]).astype(BF16)
            s_new = cdec_ref[j] * state + blk_ref[...] * _dot_tn(kd, v2b)
            s_scr[b, j] = s_new
            slast_ref[b, j] = s_new
            sl = slice(LANES * j, LANES * (j + 1))
            o = _halfnorm(o, lo) * gng_ref[:, sl] + gnb_ref[:, sl]
            o_ref[b, :, sl] = (g2 * _sigmoid(g2) * o).astype(BF16)


def _att_bias():
    slopes = _alibi_slopes(ATT_Q_HEADS)
    qpos = np.arange(TC)[:, None] + TC
    kpos = np.arange(2 * TC)[None, :]
    dist = qpos - kpos
    valid = (dist >= 0) & (dist <= WINDOW)
    prev_cur = np.stack([np.where(valid, -slopes[h] * dist, NEG_INF) for h in range(ATT_Q_HEADS)])
    cur_prev = np.concatenate([prev_cur[:, :, TC:], prev_cur[:, :, :TC]], axis=2)
    first = np.where(np.arange(2 * TC) >= TC, NEG_INF, 0.0)[None, :]
    return jnp.asarray(cur_prev, F32), jnp.asarray(prev_cur, F32), jnp.asarray(first, F32)


def _att_prompt_body(n, layer, sink_ref, pa_ref, bias_cp_ref, bias_pc_ref, first_ref, o_ref, kout_ref, vout_ref,
                     k_scr, v_scr, bias_scr):
    nb = pa_ref.shape[0]

    even = n % 2 == 0
    row0 = pl.multiple_of((n % 2) * TC, TC)
    first = jnp.where(n == 0, first_ref[...], jnp.zeros_like(first_ref))
    for head in range(ATT_Q_HEADS):
        bias_scr[head] = jnp.where(even, bias_cp_ref[head], bias_pc_ref[head]) + first

    lo = lax.broadcasted_iota(jnp.int32, (TC, LANES), 1) < HALF
    zero = jnp.zeros((TC, LANES), F32)
    swap = lambda a: pltpu.roll(a, HALF, 1)
    moved = [head % 2 != head // ATT_GROUP for head in range(ATT_Q_HEADS)]

    def scores(b):
        kcur = pa_ref[b, :, ATT_WIDTH:ATT_WIDTH + LANES]
        vcur = pa_ref[b, :, ATT_WIDTH + LANES:ATT_WIDTH + 2 * LANES]
        kout_ref[b] = kcur
        vout_ref[b] = vcur
        k_scr[b, pl.ds(row0, TC), :] = kcur.astype(BF16)
        v_scr[b, pl.ds(row0, TC), 0:LANES] = vcur.astype(BF16)
        rows = []
        for head in range(ATT_Q_HEADS):
            qcol = pa_ref[b, :, LANES * (head // 2):LANES * (head // 2 + 1)]
            if moved[head]:
                qcol = swap(qcol)
            rows.append(jnp.where(lo, qcol, zero) if head // ATT_GROUP == 0 else jnp.where(lo, zero, qcol))
        qs = jnp.concatenate(rows, axis=0).astype(BF16)
        return _dot_nt(qs, k_scr[b])

    def softmax(s_all):
        es, ms = [], []
        for head in range(ATT_Q_HEADS):
            s = s_all[head * TC:(head + 1) * TC] + bias_scr[head]
            m = jnp.maximum(jnp.max(s, -1, keepdims=True), sink_ref[layer, head])
            es.append(jnp.exp(s - m).astype(BF16))
            ms.append(m)
        return jnp.concatenate(es, axis=0), ms

    def values(b, e_all, ms):
        r_all = _dot(e_all, v_scr[b])
        outs = []
        for head in range(ATT_Q_HEADS):
            r = r_all[head * TC:(head + 1) * TC]
            den = r[:, LANES:2 * LANES] + jnp.exp(sink_ref[layer, head] - ms[head])
            out = r[:, 0:LANES] / den
            outs.append(swap(out) if moved[head] else out)
        for c in range(3):
            o_ref[b, :, LANES * c:LANES * (c + 1)] = jnp.where(lo, outs[2 * c], outs[2 * c + 1]).astype(BF16)

    s_all, soft = {}, {}
    for k in range(nb + 2):
        if k < nb:
            s_all[k] = scores(k)
        if 0 <= k - 1 < nb:
            soft[k - 1] = softmax(s_all.pop(k - 1))
        if 0 <= k - 2 < nb:
            values(k - 2, *soft.pop(k - 2))


def _mixer_kernel(sink_ref, x_ref, w_ref,
                  bre_ref, bim_ref, cre_ref, cim_ref, ar_ref, ai_ref, d_ref, gw_ref, gb_ref,
                  dmask_ref, kvdec_ref, indec_ref, cdec_ref, blk_ref, gng_ref, gnb_ref,
                  bias_cp_ref, bias_pc_ref, first_ref,
                  z_ref, or_ref, oa_ref, hlast_ref, slast_ref, kout_ref, vout_ref,
                  pr_scr, pa_scr, h_scr, hc_scr, s_scr, k_scr, v_scr, bias_scr, *, layer):
    n = pl.program_id(0)
    nb = x_ref.shape[0]

    @pl.when(n == 0)
    def _():
        hc_scr[...] = jnp.zeros_like(hc_scr)
        s_scr[...] = jnp.zeros_like(s_scr)
        k_scr[...] = jnp.zeros_like(k_scr)
        v_scr[:, :, 0:LANES] = jnp.zeros((nb, 2 * TC, LANES), BF16)
        v_scr[:, :, LANES:2 * LANES] = jnp.ones((nb, 2 * TC, LANES), BF16)

    xb = x_ref[...].reshape(nb * TC, D_MODEL).astype(BF16)
    u3 = _dot(xb, w_ref[:, 0:COL_RET]).reshape(nb, TC, SSM_WIDTH)

    def project(dst, col0, c0, c1):
        def emit():
            dst[:, :, c0:c1] = _dot(xb, w_ref[:, col0 + c0:col0 + c1]).reshape(nb, TC, c1 - c0)
        return emit

    pieces = [project(pr_scr, COL_RET, c, c + MXU_COLS) for c in range(0, RET_COLS, MXU_COLS)]
    pieces += [project(pa_scr, COL_ATT, c, min(c + MXU_COLS, ATT_COLS)) for c in range(0, ATT_COLS, MXU_COLS)]
    *s5_maps, s5_finish = _s5_prompt_body(n, u3, bre_ref, bim_ref, cre_ref, cim_ref, ar_ref, ai_ref, d_ref, gw_ref,
                                          gb_ref, z_ref, hlast_ref, h_scr, hc_scr, fillers=pieces[:SCAN_PIECES])
    for emit in s5_maps:
        emit()
    s5_finish(pieces[SCAN_PIECES:])
    _ret_prompt_body(n, pr_scr, dmask_ref, kvdec_ref, indec_ref, cdec_ref, blk_ref, gng_ref, gnb_ref,
                     or_ref, slast_ref, s_scr)
    _att_prompt_body(n, layer, sink_ref, pa_scr, bias_cp_ref, bias_pc_ref, first_ref, oa_ref, kout_ref, vout_ref,
                     k_scr, v_scr, bias_scr)


def _mixer_prompt(x3, p, layer):
    nb, length, _ = x3.shape
    chunk = lambda c: pl.BlockSpec((nb, TC, c), lambda n: (0, n, 0))
    fixed = lambda *shape: pl.BlockSpec(shape, lambda n: (0,) * len(shape))
    ret_consts = _ret_consts()
    att_consts = _att_bias()
    return pl.pallas_call(
        functools.partial(_mixer_kernel, layer=layer),
        grid=(length // TC,),
        in_specs=[pl.BlockSpec(memory_space=pltpu.SMEM), chunk(D_MODEL), _layer_spec((D_MODEL, IN_WIDTH), layer)]
                 + _s5_weight_specs(layer)
                 + [_const_spec(c.shape) for c in ret_consts]
                 + [_layer_spec((1, RET_WIDTH), layer), _layer_spec((1, RET_WIDTH), layer)]
                 + [_const_spec(c.shape) for c in att_consts],
        out_specs=[chunk(SSM_WIDTH), chunk(RET_WIDTH), chunk(ATT_WIDTH),
                   fixed(nb, 2 * SSM_LANES), fixed(nb, 2, LANES, LANES), fixed(nb, TC, LANES), fixed(nb, TC, LANES)],
        out_shape=[jax.ShapeDtypeStruct((nb, length, SSM_WIDTH), BF16),
                   jax.ShapeDtypeStruct((nb, length, RET_WIDTH), BF16),
                   jax.ShapeDtypeStruct((nb, length, ATT_WIDTH), BF16),
                   jax.ShapeDtypeStruct((nb, 2 * SSM_LANES), F32),
                   jax.ShapeDtypeStruct((nb, 2, LANES, LANES), F32),
                   jax.ShapeDtypeStruct((nb, TC, LANES), F32),
                   jax.ShapeDtypeStruct((nb, TC, LANES), F32)],
        scratch_shapes=[pltpu.VMEM((nb, TC, RET_COLS), F32), pltpu.VMEM((nb, TC, ATT_COLS), F32),
                        pltpu.VMEM((2 * N_TILES, nb * TC, LANES), F32), pltpu.VMEM((nb, 2 * SSM_LANES), F32),
                        pltpu.VMEM((nb, 2, LANES, LANES), F32),
                        pltpu.VMEM((nb, 2 * TC, LANES), BF16), pltpu.VMEM((nb, 2 * TC, 2 * LANES), BF16),
                        pltpu.VMEM((ATT_Q_HEADS, TC, 2 * TC), F32)],
        compiler_params=_cparams(1),
        name="mixer",
    )(p["sinks"], x3, p["w_in"], *_s5_weights(p), *ret_consts, p["gn_g"], p["gn_b"], *att_consts)


def _mix_sample_kernel(qt_ref, kt_ref, v4_ref, g4_ref, s_ref, qb_ref, kn_ref, vn_ref, kc_ref, vc_ref,
                       gng_ref, gnb_ref, bias_ref, sink_ref, *rest, layer):
    or_ref, oa_ref, sn_all, kout_all, vout_all = rest[-5:]
    if layer:
        for prev_ref, out_all in zip(rest[:3], (sn_all, kout_all, vout_all)):
            out_all[0:layer] = prev_ref[...]
    sn_ref, kout_ref, vout_ref = sn_all.at[layer], kout_all.at[layer], vout_all.at[layer]
    gammas = _ret_gammas()
    sink = sink_ref[:, 0:1]
    bias = bias_ref[...]

    rows = range(SAMPLE_ROWS)
    qt, kt, v4, g4 = qt_ref[...], kt_ref[...], v4_ref[...], g4_ref[...]
    for h in range(RET_HEADS):
        sl = slice(RET_D * h, RET_D * (h + 1))
        sn = gammas[h] * s_ref[:, sl, :] + kt[:, :, h:h + 1] * v4[:, h:h + 1, :]
        sn_ref[:, sl, :] = sn
        o = jnp.sum(qt[:, :, h:h + 1] * sn, axis=1, keepdims=True)
        d = o - jnp.mean(o, -1, keepdims=True)
        var = jnp.mean(d * d, -1, keepdims=True)
        o = d * lax.rsqrt(var + LN_EPS) * gng_ref[h:h + 1, :] + gnb_ref[h:h + 1, :]
        gate = g4[:, h:h + 1, :]
        or_ref[:, h:h + 1, :] = gate * _sigmoid(gate) * o

    q_all = jnp.concatenate([qb_ref[i] for i in rows], axis=0)
    knew = jnp.concatenate([jnp.broadcast_to(kn_ref[i], (8, LANES)) for i in rows], axis=0)
    vnew = jnp.concatenate([jnp.broadcast_to(vn_ref[i], (8, LANES)) for i in rows], axis=0)
    sink_all = jnp.concatenate([sink] * SAMPLE_ROWS, axis=0)
    s = jnp.concatenate([_dot_nt(qb_ref[i].astype(BF16), kc_ref[i].astype(BF16)) + bias for i in rows], axis=0)
    s_self = jnp.sum(q_all * knew, -1, keepdims=True)
    m = jnp.maximum(jnp.maximum(jnp.max(s, -1, keepdims=True), s_self), sink_all)
    e = jnp.exp(s - m)
    e_self = jnp.exp(s_self - m)
    den = jnp.sum(e, -1, keepdims=True) + e_self + jnp.exp(sink_all - m)
    pv = jnp.concatenate([_dot(e[8 * i:8 * (i + 1)].astype(BF16), vc_ref[i].astype(BF16)) for i in rows], axis=0)
    o = (pv + e_self * vnew) / den
    for i in rows:
        oa_ref[i] = o[8 * i:8 * (i + 1)]

    kout_ref[:, 0:WINDOW - 1, :] = kc_ref[:, 1:WINDOW, :]
    kout_ref[:, WINDOW - 1:WINDOW, :] = kn_ref[...]
    vout_ref[:, 0:WINDOW - 1, :] = vc_ref[:, 1:WINDOW, :]
    vout_ref[:, WINDOW - 1:WINDOW, :] = vn_ref[...]


def _mix_sample(qt, kt, v4, g4, s_all, qb, kn, vn, kc_all, vc_all, p, bias8, layer, prev):
    nb = qt.shape[0]
    bb = SAMPLE_ROWS
    blk = lambda *dims: pl.BlockSpec((bb,) + dims, lambda i: (i,) + (0,) * len(dims))
    lblk = lambda *dims: pl.BlockSpec((None, bb) + dims, lambda i: (layer, i) + (0,) * len(dims))
    in_specs = [blk(RET_D, RET_HEADS), blk(RET_D, RET_HEADS), blk(RET_HEADS, RET_D), blk(RET_HEADS, RET_D),
                lblk(RET_WIDTH, RET_D), blk(8, LANES), blk(1, LANES), blk(1, LANES),
                lblk(WINDOW, LANES), lblk(WINDOW, LANES),
                _layer_spec((RET_HEADS, RET_D), layer), _layer_spec((RET_HEADS, RET_D), layer),
                _const_spec((8, LANES)), _layer_spec((8, LANES), layer)]
    args = [qt, kt, v4, g4, s_all, qb, kn, vn, kc_all, vc_all, p["gn_g4"], p["gn_b4"], bias8, p["sink8"]]
    stacked = lambda n_layers, *dims: pl.BlockSpec((n_layers, bb) + dims, lambda i: (0, i) + (0,) * len(dims))
    state_dims = ((RET_WIDTH, RET_D), (WINDOW, LANES), (WINDOW, LANES))
    if layer:
        in_specs += [stacked(layer, *dims) for dims in state_dims]
        args += list(prev)
    return pl.pallas_call(
        functools.partial(_mix_sample_kernel, layer=layer),
        grid=(nb // bb,),
        in_specs=in_specs,
        out_specs=[blk(RET_HEADS, RET_D), blk(8, LANES)] + [stacked(layer + 1, *dims) for dims in state_dims],
        out_shape=[jax.ShapeDtypeStruct((nb, RET_HEADS, RET_D), F32),
                   jax.ShapeDtypeStruct((nb, 8, LANES), F32)]
                  + [jax.ShapeDtypeStruct((layer + 1, nb) + dims, F32) for dims in state_dims],
        compiler_params=_cparams(1),
        name="mix_sample",
    )(*args)


def _outffn_kernel(z_ref, r_ref, a_ref, x_ref, wo_ref, g1_ref, b1_ref, w1_ref, w3_ref, w2_ref, g2_ref, b2_ref, o_ref):
    c1, c2 = SSM_WIDTH, SSM_WIDTH + RET_WIDTH
    tm = x_ref.shape[0]
    parts = FFN_PARTS if tm % (16 * FFN_PARTS) == 0 else 1
    rows = [slice(tm // parts * k, tm // parts * (k + 1)) for k in range(parts)]

    def mixed(sl):
        mix = _dot(z_ref[sl], wo_ref[0:c1]) + _dot(r_ref[sl], wo_ref[c1:c2]) + _dot(a_ref[sl], wo_ref[c2:D_MODEL])
        return _layernorm(ALPHA * x_ref[sl] + mix, g1_ref[...], b1_ref[...])

    def hidden(x):
        xb = x.astype(BF16)
        h1 = _dot(xb, w1_ref[...])
        h3 = _dot(xb, w3_ref[...])
        return (h1 * _sigmoid(h1) * h3).astype(BF16)

    def out(sl, x, hid):
        o_ref[sl] = _layernorm(ALPHA * x + _dot(hid, w2_ref[...]), g2_ref[...], b2_ref[...])

    xs, hs = {}, {}
    for k in range(parts + 2):
        if k < parts:
            xs[k] = mixed(rows[k])
        if 0 <= k - 1 < parts:
            hs[k - 1] = hidden(xs[k - 1])
        if 0 <= k - 2 < parts:
            out(rows[k - 2], xs.pop(k - 2), hs.pop(k - 2))


def _outffn(z, r, a, x2, p, layer):
    t = x2.shape[0]
    tm = min(ROW_TILE, t)
    row = lambda c: pl.BlockSpec((tm, c), lambda i: (i, 0))
    vec = _layer_spec((1, D_MODEL), layer)
    return pl.pallas_call(
        _outffn_kernel,
        grid=(t // tm,),
        in_specs=[row(SSM_WIDTH), row(RET_WIDTH), row(ATT_WIDTH), row(D_MODEL),
                  _layer_spec((D_MODEL, D_MODEL), layer), vec, vec,
                  _layer_spec((D_MODEL, FFN_HIDDEN), layer), _layer_spec((D_MODEL, FFN_HIDDEN), layer),
                  _layer_spec((FFN_HIDDEN, D_MODEL), layer), vec, vec],
        out_specs=row(D_MODEL),
        out_shape=jax.ShapeDtypeStruct((t, D_MODEL), F32),
        compiler_params=_cparams(1),
        name="outffn",
    )(z, r, a, x2, p["w_out"], p["ln1_g"], p["ln1_b"], p["w1"], p["w3"], p["w2"], p["ln2_g"], p["ln2_b"])


def _prep_params(w_in, lam_re, lam_im, log_step, b_re, b_im, c_re, c_im, ssm_d, glu_w, glu_b,
                 gn_g, gn_b, sinks, w_out, ln1_g, ln1_b, w1, w3, w2, ln2_g, ln2_b):
    col_scale = np.ones((IN_WIDTH,), np.float32)
    col_scale[COL_RET + RET_WIDTH:COL_RET + 2 * RET_WIDTH] = RET_D ** -0.5
    col_scale[COL_ATT:COL_ATT + ATT_WIDTH] = ATT_D ** -0.5

    step = jnp.exp(log_step)[..., None]
    mag = jnp.exp(lam_re * step)
    a_re, a_im = mag * jnp.cos(lam_im * step), mag * jnp.sin(lam_im * step)
    den = lam_re * lam_re + lam_im * lam_im
    k_re = ((a_re - 1.0) * lam_re + a_im * lam_im) / den
    k_im = (a_im * lam_re - (a_re - 1.0) * lam_im) / den
    bb_re = k_re[..., None] * b_re - k_im[..., None] * b_im
    bb_im = k_re[..., None] * b_im + k_im[..., None] * b_re
    eye8 = jnp.eye(8, dtype=F32)

    def b_blocks(m):
        m = m.reshape(DEPTH, 3, 8, SSM_STATE, SSM_CH)
        return jnp.einsum("ligph,gk->lighkp", m, eye8).reshape(DEPTH, 3, LANES, 512).astype(BF16)

    def c_blocks(m):
        m = m.reshape(DEPTH, 3, 8, SSM_CH, SSM_STATE)
        return jnp.einsum("lighp,gk->likpgh", m, eye8).reshape(DEPTH, 3, 512, LANES).astype(BF16)

    row = lambda a: a.reshape(DEPTH, 1, -1)
    sink8 = jnp.concatenate([sinks, jnp.zeros((DEPTH, 2), F32)], axis=1)
    return dict(
        w_in=(w_in * col_scale).astype(BF16),
        bre=b_blocks(bb_re), bim=b_blocks(bb_im), cre=c_blocks(c_re), cim=c_blocks(-c_im),
        ar=row(a_re), ai=row(a_im),
        ssm_d=row(ssm_d), glu_w=glu_w.astype(BF16), glu_b=row(glu_b),
        gn_g=row(gn_g), gn_b=row(gn_b),
        gn_g4=gn_g.reshape(DEPTH, RET_HEADS, RET_D), gn_b4=gn_b.reshape(DEPTH, RET_HEADS, RET_D),
        sinks=sinks, sink8=jnp.broadcast_to(sink8[:, :, None], (DEPTH, 8, LANES)),
        w_out=w_out.astype(BF16), ln1_g=row(ln1_g), ln1_b=row(ln1_b),
        w1=w1.astype(BF16), w3=w3.astype(BF16), w2=w2.astype(BF16), ln2_g=row(ln2_g), ln2_b=row(ln2_b),
    )


def _prompt_layer(x3, p, layer):
    nb, length, _ = x3.shape
    t = nb * length
    x2 = x3.reshape(t, D_MODEL)
    z, o_r, o_a, hlast, sblk, klast, vlast = _mixer_prompt(x3, p, layer)
    y = _outffn(z.reshape(t, SSM_WIDTH), o_r.reshape(t, RET_WIDTH), o_a.reshape(t, ATT_WIDTH), x2, p, layer)
    y = y.reshape(nb, length, D_MODEL)
    s4 = jnp.stack([sblk[:, h // 2, HALF * (h % 2):HALF * (h % 2 + 1), HALF * (h % 2):HALF * (h % 2 + 1)]
                    for h in range(RET_HEADS)], axis=1)
    return y, hlast, s4, klast, vlast


def _sample_bias8():
    slopes = _alibi_slopes(ATT_Q_HEADS)
    dist = WINDOW - np.arange(WINDOW)
    bias = np.zeros((8, WINDOW))
    for h in range(ATT_Q_HEADS):
        bias[h] = -slopes[h] * dist
    return jnp.asarray(bias, F32)


def _sample_layer(x2, p, layer, h0r, h0i, s_all, kc_all, vc_all, prev):
    nb = x2.shape[0]
    pu, pr, pa = _inproj(x2, p, layer)
    z, h_re, h_im = _s5_sample(pu, h0r, h0i, p, layer)
    heads = lambda a: a.reshape(nb, RET_HEADS, RET_D)
    q4, k4, v4, g4 = (heads(pr[:, RET_WIDTH * i:RET_WIDTH * (i + 1)]) for i in range(4))
    q6 = pa[:, 0:ATT_WIDTH].reshape(nb, ATT_Q_HEADS, ATT_D)
    kv_of = np.arange(ATT_Q_HEADS) // ATT_GROUP
    qb = jnp.zeros((nb, 8, ATT_KV_HEADS, ATT_D), F32).at[:, np.arange(ATT_Q_HEADS), kv_of].set(q6)
    kn = pa[:, ATT_WIDTH:ATT_WIDTH + LANES].reshape(nb, 1, LANES)
    vn = pa[:, ATT_WIDTH + LANES:ATT_WIDTH + 2 * LANES].reshape(nb, 1, LANES)
    o_r, o8, s_new, k_out, v_out = _mix_sample(
        q4.transpose(0, 2, 1), k4.transpose(0, 2, 1), v4, g4, s_all, qb.reshape(nb, 8, LANES), kn, vn,
        kc_all, vc_all, p, _sample_bias8(), layer, prev)
    o_r = o_r.reshape(nb, RET_WIDTH).astype(BF16)
    o_a = o8.reshape(nb, 8, ATT_KV_HEADS, ATT_D)[:, np.arange(ATT_Q_HEADS), kv_of].reshape(nb, ATT_WIDTH).astype(BF16)
    y = _outffn(z, o_r, o_a, x2, p, layer)
    return y, h_re, h_im, (s_new, k_out, v_out)


def kernel(x_prompt, x_sample, state_ssm_re, state_ssm_im, state_ret, cache_win_k, cache_win_v, w_in, ssm_lambda_re, ssm_lambda_im, ssm_log_step, ssm_b_re, ssm_b_im, ssm_c_re, ssm_c_im, ssm_d, ssm_glu_w, ssm_glu_b, ret_gn_g, ret_gn_b, attn_sinks, w_out, ln1_g, ln1_b, ffn_w1, ffn_w3, ffn_w2, ln2_g, ln2_b):
    p = _prep_params(w_in, ssm_lambda_re, ssm_lambda_im, ssm_log_step, ssm_b_re, ssm_b_im, ssm_c_re, ssm_c_im,
                     ssm_d, ssm_glu_w, ssm_glu_b, ret_gn_g, ret_gn_b, attn_sinks, w_out, ln1_g, ln1_b,
                     ffn_w1, ffn_w3, ffn_w2, ln2_g, ln2_b)

    xp = x_prompt
    pb = x_prompt.shape[0]
    p_out = []
    for layer in range(DEPTH):
        xp, *states = _prompt_layer(xp, p, layer)
        p_out.append(states)
    stack = lambda outs, i: jnp.stack([o[i] for o in outs])
    p_h = stack(p_out, 0)
    p_kv_shape = (DEPTH, pb, TC, ATT_KV_HEADS, ATT_D)

    nb = x_sample.shape[0]
    xs = x_sample.reshape(nb, D_MODEL)
    h0r = state_ssm_re.reshape(DEPTH, nb, SSM_LANES)
    h0i = state_ssm_im.reshape(DEPTH, nb, SSM_LANES)
    s_all = state_ret.reshape(DEPTH, nb, RET_WIDTH, RET_D)
    kc_all = cache_win_k.reshape(DEPTH, nb, WINDOW, LANES)
    vc_all = cache_win_v.reshape(DEPTH, nb, WINDOW, LANES)
    prev = None
    s_h = []
    for layer in range(DEPTH):
        xs, h_re, h_im, prev = _sample_layer(xs, p, layer, h0r, h0i, s_all, kc_all, vc_all, prev)
        s_h.append((h_re, h_im))
    s_ret, s_k, s_v = prev
    ssm_shape = (DEPTH, nb, SSM_GROUPS, SSM_STATE)
    s_kv_shape = (DEPTH, nb, WINDOW, ATT_KV_HEADS, ATT_D)

    return (xp, xs.reshape(nb, 1, D_MODEL),
            p_h[:, :, :SSM_LANES].reshape(DEPTH, pb, SSM_GROUPS, SSM_STATE),
            p_h[:, :, SSM_LANES:].reshape(DEPTH, pb, SSM_GROUPS, SSM_STATE),
            stack(p_out, 1), stack(p_out, 2).reshape(p_kv_shape), stack(p_out, 3).reshape(p_kv_shape),
            stack(s_h, 0).reshape(ssm_shape), stack(s_h, 1).reshape(ssm_shape),
            s_ret.reshape(DEPTH, nb, RET_HEADS, RET_D, RET_D), s_k.reshape(s_kv_shape), s_v.reshape(s_kv_shape))
```

```python
import functools
import math

import numpy as np
import jax
import jax.numpy as jnp
from jax import lax
from jax.experimental import pallas as pl
from jax.experimental.pallas import tpu as pltpu

F32 = jnp.float32
BF16 = jnp.bfloat16

D_MODEL = 1024
DEPTH = 2
SSM_GROUPS = 24
SSM_CH = 16
SSM_STATE = 64
SSM_WIDTH = SSM_GROUPS * SSM_CH
SSM_LANES = SSM_GROUPS * SSM_STATE
RET_HEADS = 4
RET_D = 64
RET_WIDTH = RET_HEADS * RET_D
ATT_Q_HEADS = 6
ATT_KV_HEADS = 2
ATT_GROUP = ATT_Q_HEADS // ATT_KV_HEADS
ATT_D = 64
ATT_WIDTH = ATT_Q_HEADS * ATT_D
WINDOW = 128
FFN_HIDDEN = 2816
IN_WIDTH = 2048
ALPHA = (2 * DEPTH) ** 0.25
LN_EPS = 1e-5
NEG_INF = -1e30

COL_RET = SSM_WIDTH
COL_ATT = SSM_WIDTH + 4 * RET_WIDTH
RET_COLS = 4 * RET_WIDTH
ATT_COLS = ATT_WIDTH + 2 * ATT_KV_HEADS * ATT_D

TC = 128
LANES = 128
N_TILES = SSM_LANES // LANES
HALF = 64
MXU_COLS = 256
SCAN_PIECES = 4
ROW_TILE = 512
FFN_PARTS = 2
SAMPLE_ROWS = 16
VMEM_LIMIT = 56 * 2 ** 20


def _alibi_slopes(n):
    def pow2(k):
        start = 2.0 ** (-8.0 / k)
        return [start ** (i + 1) for i in range(k)]
    if n & (n - 1) == 0:
        return pow2(n)
    c = 2 ** int(math.floor(math.log2(n)))
    return pow2(c) + pow2(2 * c)[0::2][: n - c]


def _cparams(n_axes):
    return pltpu.CompilerParams(dimension_semantics=("arbitrary",) * n_axes, vmem_limit_bytes=VMEM_LIMIT)


def _const_spec(shape):
    nd = len(shape)
    return pl.BlockSpec(shape, lambda *_: (0,) * nd, pipeline_mode=pl.Buffered(1))


def _layer_spec(shape, layer):
    nd = len(shape)
    return pl.BlockSpec((None,) + tuple(shape), lambda *_: (layer,) + (0,) * nd, pipeline_mode=pl.Buffered(1))


def _dot(a, b):
    return jnp.dot(a, b, preferred_element_type=F32)


def _dot_nt(a, b):
    return lax.dot_general(a, b, (((1,), (1,)), ((), ())), preferred_element_type=F32)


def _dot_tn(a, b):
    return lax.dot_general(a, b, (((0,), (0,)), ((), ())), preferred_element_type=F32)


def _layernorm(v, g, b):
    mu = jnp.mean(v, -1, keepdims=True)
    d = v - mu
    var = jnp.mean(d * d, -1, keepdims=True)
    return d * lax.rsqrt(var + LN_EPS) * g + b


def _gelu_tanh(x):
    return 0.5 * x * (1.0 + jnp.tanh(math.sqrt(2.0 / math.pi) * (x + 0.044715 * (x * x * x))))


def _sigmoid(x):
    return 1.0 / (1.0 + jnp.exp(-x))


def _inproj_kernel(x_ref, w_ref, pu_ref, pr_ref, pa_ref):
    xb = x_ref[...].astype(BF16)
    pu_ref[...] = _dot(xb, w_ref[:, 0:COL_RET])
    pr_ref[...] = _dot(xb, w_ref[:, COL_RET:COL_ATT])
    pa_ref[...] = _dot(xb, w_ref[:, COL_ATT:IN_WIDTH])


def _inproj(x2, p, layer):
    t = x2.shape[0]
    tm = min(ROW_TILE, t)
    row = lambda c: pl.BlockSpec((tm, c), lambda i: (i, 0))
    return pl.pallas_call(
        _inproj_kernel,
        grid=(t // tm,),
        in_specs=[row(D_MODEL), _layer_spec((D_MODEL, IN_WIDTH), layer)],
        out_specs=[row(SSM_WIDTH), row(RET_COLS), row(ATT_COLS)],
        out_shape=[jax.ShapeDtypeStruct((t, SSM_WIDTH), F32),
                   jax.ShapeDtypeStruct((t, RET_COLS), F32),
                   jax.ShapeDtypeStruct((t, ATT_COLS), F32)],
        compiler_params=_cparams(1),
        name="inproj",
    )(x2, p["w_in"])


def _s5_bu(ub, bre_ref, bim_ref):
    res, ims = [], []
    for i in range(3):
        ui = ub[:, LANES * i:LANES * (i + 1)]
        res.append(_dot(ui, bre_ref[i]))
        ims.append(_dot(ui, bim_ref[i]))
    return res, ims


def _s5_out(u, h_re, h_im, cre_ref, cim_ref, d_ref, gw_ref, gb_ref):
    ys = []
    for i in range(3):
        ys.append(_dot(h_re[i].astype(BF16), cre_ref[i]) + _dot(h_im[i].astype(BF16), cim_ref[i]))
    y = jnp.concatenate(ys, axis=1) + d_ref[...] * u
    z = _gelu_tanh(y)
    gate = _sigmoid(_dot(z.astype(BF16), gw_ref[...]) + gb_ref[...])
    return z * gate


def _s5_prompt_body(n, u3, bre_ref, bim_ref, cre_ref, cim_ref, ar_ref, ai_ref, d_ref, gw_ref, gb_ref,
                    z_ref, hlast_ref, h_scr, hc_scr, fillers=()):
    nb = u3.shape[0]

    u = pltpu.einshape("btc->(tb)c", u3)

    res, ims = _s5_bu(u.astype(BF16), bre_ref, bim_ref)
    for i in range(3):
        for k in range(4):
            h_scr[4 * i + k] = res[i][:, LANES * k:LANES * (k + 1)]
            h_scr[N_TILES + 4 * i + k] = ims[i][:, LANES * k:LANES * (k + 1)]

    def step(t, carry):
        r0 = t * nb
        out = []
        for c in range(N_TILES):
            sl = slice(LANES * c, LANES * (c + 1))
            ar, ai = ar_ref[:, sl], ai_ref[:, sl]
            hr, hi = carry[c], carry[N_TILES + c]
            nr = ar * hr - ai * hi + h_scr[c, pl.ds(r0, nb), :]
            ni = ar * hi + ai * hr + h_scr[N_TILES + c, pl.ds(r0, nb), :]
            h_scr[c, pl.ds(r0, nb), :] = nr
            h_scr[N_TILES + c, pl.ds(r0, nb), :] = ni
            out.append((nr, ni))
        return tuple(o[0] for o in out) + tuple(o[1] for o in out)

    fin = tuple(hc_scr[:, LANES * c:LANES * (c + 1)] for c in range(2 * N_TILES))
    groups = len(fillers) + 1
    for g in range(groups):
        for t in range(TC * g // groups, TC * (g + 1) // groups):
            fin = step(t, fin)
        if g < len(fillers):
            fillers[g]()
    for c in range(2 * N_TILES):
        hc_scr[:, LANES * c:LANES * (c + 1)] = fin[c]
        hlast_ref[:, LANES * c:LANES * (c + 1)] = fin[c]

    ys = []

    def out_piece(i):
        def emit():
            h_re = jnp.concatenate([h_scr[4 * i + k] for k in range(4)], axis=1).astype(BF16)
            h_im = jnp.concatenate([h_scr[N_TILES + 4 * i + k] for k in range(4)], axis=1).astype(BF16)
            ys.append(_dot(h_re, cre_ref[i]) + _dot(h_im, cim_ref[i]))
        return emit

    def finish(fillers=()):
        fillers = list(fillers)
        y = jnp.concatenate(ys, axis=1) + d_ref[...] * u
        if fillers:
            fillers.pop(0)()
        z = _gelu_tanh(y)
        if fillers:
            fillers.pop(0)()
        gate = _sigmoid(_dot(z.astype(BF16), gw_ref[...]) + gb_ref[...])
        while fillers:
            fillers.pop(0)()
        z_ref[...] = pltpu.einshape("(tb)c->btc", z * gate, b=nb).astype(BF16)

    return [out_piece(i) for i in range(3)] + [finish]


def _s5_weight_specs(layer):
    ls = lambda *shape: _layer_spec(shape, layer)
    return [ls(3, LANES, 512), ls(3, LANES, 512), ls(3, 512, LANES), ls(3, 512, LANES),
            ls(1, SSM_LANES), ls(1, SSM_LANES), ls(1, SSM_WIDTH), ls(SSM_WIDTH, SSM_WIDTH), ls(1, SSM_WIDTH)]


def _s5_weights(p):
    return (p["bre"], p["bim"], p["cre"], p["cim"], p["ar"], p["ai"], p["ssm_d"], p["glu_w"], p["glu_b"])


def _s5_sample_kernel(u_ref, h0r_ref, h0i_ref, bre_ref, bim_ref, cre_ref, cim_ref, ar_ref, ai_ref, d_ref,
                      gw_ref, gb_ref, z_ref, hr_ref, hi_ref):
    u = u_ref[...]
    res, ims = _s5_bu(u.astype(BF16), bre_ref, bim_ref)
    h_re, h_im = [], []
    for i in range(3):
        sl = slice(512 * i, 512 * (i + 1))
        ar, ai = ar_ref[:, sl], ai_ref[:, sl]
        h0r, h0i = h0r_ref[:, sl], h0i_ref[:, sl]
        nr = ar * h0r - ai * h0i + res[i]
        ni = ar * h0i + ai * h0r + ims[i]
        hr_ref[:, sl] = nr
        hi_ref[:, sl] = ni
        h_re.append(nr)
        h_im.append(ni)
    z_ref[...] = _s5_out(u, h_re, h_im, cre_ref, cim_ref, d_ref, gw_ref, gb_ref).astype(BF16)


def _s5_sample(u, h0r, h0i, p, layer):
    nb = u.shape[0]
    full = lambda c: pl.BlockSpec((nb, c), lambda i: (0, 0))
    state = pl.BlockSpec((None, nb, SSM_LANES), lambda i: (layer, 0, 0))
    return pl.pallas_call(
        _s5_sample_kernel,
        grid=(1,),
        in_specs=[full(SSM_WIDTH), state, state] + _s5_weight_specs(layer),
        out_specs=[full(SSM_WIDTH), full(SSM_LANES), full(SSM_LANES)],
        out_shape=[jax.ShapeDtypeStruct((nb, SSM_WIDTH), BF16),
                   jax.ShapeDtypeStruct((nb, SSM_LANES), F32),
                   jax.ShapeDtypeStruct((nb, SSM_LANES), F32)],
        compiler_params=_cparams(1),
        name="s5_sample",
    )(u, h0r, h0i, *_s5_weights(p))


def _ret_gammas():
    return [1.0 - 2.0 ** (-5.0 - h) for h in range(RET_HEADS)]


def _ret_consts():
    log_g = np.log1p(-(2.0 ** (-5.0 - np.arange(RET_HEADS, dtype=np.float64))))
    idx = np.arange(TC)
    diff = idx[:, None] - idx[None, :]
    dmask = np.where(diff >= 0, np.exp(log_g[:, None, None] * np.maximum(diff, 0)), 0.0)
    inner = np.exp(log_g[:, None] * (idx + 1))
    kvd = np.exp(log_g[:, None] * (TC - 1 - idx))
    cd = np.exp(log_g * TC)
    indec = np.zeros((2, TC, LANES))
    kvdec = np.zeros((2, TC, LANES))
    cdec = np.zeros((2, LANES, LANES))
    blk = np.zeros((LANES, LANES))
    for j in range(2):
        for hh in range(2):
            sl = slice(HALF * hh, HALF * (hh + 1))
            indec[j, :, sl] = inner[2 * j + hh][:, None]
            kvdec[j, :, sl] = kvd[2 * j + hh][:, None]
            cdec[j, sl, sl] = cd[2 * j + hh]
            blk[sl, sl] = 1.0
    as32 = lambda a: jnp.asarray(a, F32)
    return as32(dmask), as32(kvdec), as32(indec), as32(cdec), as32(blk)


def _halfnorm(o, lo):
    zero = jnp.zeros_like(o)
    s_lo = jnp.sum(jnp.where(lo, o, zero), -1, keepdims=True)
    s_hi = jnp.sum(jnp.where(lo, zero, o), -1, keepdims=True)
    d = o - jnp.where(lo, s_lo, s_hi) * (1.0 / HALF)
    d2 = d * d
    v_lo = jnp.sum(jnp.where(lo, d2, zero), -1, keepdims=True)
    v_hi = jnp.sum(jnp.where(lo, zero, d2), -1, keepdims=True)
    var = jnp.where(lo, v_lo, v_hi) * (1.0 / HALF)
    return d * lax.rsqrt(var + LN_EPS)


def _ret_prompt_body(n, pr_ref, dmask_ref, kvdec_ref, indec_ref, cdec_ref, blk_ref, gng_ref, gnb_ref,
                     o_ref, slast_ref, s_scr):
    nb = pr_ref.shape[0]

    lo = lax.broadcasted_iota(jnp.int32, (TC, LANES), 1) < HALF
    zero = jnp.zeros((TC, LANES), F32)
    for b in range(nb):
        for j in range(2):
            q2 = pr_ref[b, :, LANES * j:LANES * (j + 1)]
            k2 = pr_ref[b, :, 256 + LANES * j:256 + LANES * (j + 1)]
            v2 = pr_ref[b, :, 512 + LANES * j:512 + LANES * (j + 1)]
            g2 = pr_ref[b, :, 768 + LANES * j:768 + LANES * (j + 1)]
            k2b = k2.astype(BF16)
            v2b = v2.astype(BF16)
            state = s_scr[b, j]
            qs = jnp.concatenate([jnp.where(lo, q2, zero), jnp.where(lo, zero, q2)], axis=0).astype(BF16)
            sc = _dot_nt(qs, k2b)
            qd = (q2 * indec_ref[j]).astype(BF16)
            lhs = jnp.concatenate(
                [jnp.concatenate([(sc[hh * TC:(hh + 1) * TC] * dmask_ref[2 * j + hh]).astype(BF16), qd], axis=1)
                 for hh in range(2)], axis=0)
            o2 = _dot(lhs, jnp.concatenate([v2b, state.astype(BF16)], axis=0))
            o = jnp.where(lo, o2[0:TC], o2[TC:2 * TC])
            kd = (k2 * kvdec_ref[j]).astype(BF16)
            s_new = cdec_ref[j] * state + blk_ref[...] * _dot_tn(kd, v2b)
            s_scr[b, j] = s_new
            slast_ref[b, j] = s_new
            sl = slice(LANES * j, LANES * (j + 1))
            o = _halfnorm(o, lo) * gng_ref[:, sl] + gnb_ref[:, sl]
            o_ref[b, :, sl] = (g2 * _sigmoid(g2) * o).astype(BF16)


def _att_bias():
    slopes = _alibi_slopes(ATT_Q_HEADS)
    qpos = np.arange(TC)[:, None] + TC
    kpos = np.arange(2 * TC)[None, :]
    dist = qpos - kpos
    valid = (dist >= 0) & (dist <= WINDOW)
    prev_cur = np.stack([np.where(valid, -slopes[h] * dist, NEG_INF) for h in range(ATT_Q_HEADS)])
    cur_prev = np.concatenate([prev_cur[:, :, TC:], prev_cur[:, :, :TC]], axis=2)
    first = np.where(np.arange(2 * TC) >= TC, NEG_INF, 0.0)[None, :]
    return jnp.asarray(cur_prev, F32), jnp.asarray(prev_cur, F32), jnp.asarray(first, F32)


def _att_prompt_body(n, layer, sink_ref, pa_ref, bias_cp_ref, bias_pc_ref, first_ref, o_ref, kout_ref, vout_ref,
                     k_scr, v_scr, bias_scr):
    nb = pa_ref.shape[0]

    even = n % 2 == 0
    row0 = pl.multiple_of((n % 2) * TC, TC)
    first = jnp.where(n == 0, first_ref[...], jnp.zeros_like(first_ref))
    for head in range(ATT_Q_HEADS):
        bias_scr[head] = jnp.where(even, bias_cp_ref[head], bias_pc_ref[head]) + first

    lo = lax.broadcasted_iota(jnp.int32, (TC, LANES), 1) < HALF
    zero = jnp.zeros((TC, LANES), F32)
    swap = lambda a: pltpu.roll(a, HALF, 1)
    moved = [head % 2 != head // ATT_GROUP for head in range(ATT_Q_HEADS)]

    def scores(b):
        kcur = pa_ref[b, :, ATT_WIDTH:ATT_WIDTH + LANES]
        vcur = pa_ref[b, :, ATT_WIDTH + LANES:ATT_WIDTH + 2 * LANES]
        kout_ref[b] = kcur
        vout_ref[b] = vcur
        k_scr[b, pl.ds(row0, TC), :] = kcur.astype(BF16)
        v_scr[b, pl.ds(row0, TC), 0:LANES] = vcur.astype(BF16)
        rows = []
        for head in range(ATT_Q_HEADS):
            qcol = pa_ref[b, :, LANES * (head // 2):LANES * (head // 2 + 1)]
            if moved[head]:
                qcol = swap(qcol)
            rows.append(jnp.where(lo, qcol, zero) if head // ATT_GROUP == 0 else jnp.where(lo, zero, qcol))
        qs = jnp.concatenate(rows, axis=0).astype(BF16)
        return _dot_nt(qs, k_scr[b])

    def softmax(s_all):
        es, ms = [], []
        for head in range(ATT_Q_HEADS):
            s = s_all[head * TC:(head + 1) * TC] + bias_scr[head]
            m = jnp.maximum(jnp.max(s, -1, keepdims=True), sink_ref[layer, head])
            es.append(jnp.exp(s - m).astype(BF16))
            ms.append(m)
        return jnp.concatenate(es, axis=0), ms

    def values(b, e_all, ms):
        r_all = _dot(e_all, v_scr[b])
        outs = []
        for head in range(ATT_Q_HEADS):
            r = r_all[head * TC:(head + 1) * TC]
            den = r[:, LANES:2 * LANES] + jnp.exp(sink_ref[layer, head] - ms[head])
            out = r[:, 0:LANES] / den
            outs.append(swap(out) if moved[head] else out)
        for c in range(3):
            o_ref[b, :, LANES * c:LANES * (c + 1)] = jnp.where(lo, outs[2 * c], outs[2 * c + 1]).astype(BF16)

    s_all, soft = {}, {}
    for k in range(nb + 2):
        if k < nb:
            s_all[k] = scores(k)
        if 0 <= k - 1 < nb:
            soft[k - 1] = softmax(s_all.pop(k - 1))
        if 0 <= k - 2 < nb:
            values(k - 2, *soft.pop(k - 2))


def _mixer_kernel(sink_ref, x_ref, w_ref,
                  bre_ref, bim_ref, cre_ref, cim_ref, ar_ref, ai_ref, d_ref, gw_ref, gb_ref,
                  dmask_ref, kvdec_ref, indec_ref, cdec_ref, blk_ref, gng_ref, gnb_ref,
                  bias_cp_ref, bias_pc_ref, first_ref,
                  z_ref, or_ref, oa_ref, hlast_ref, slast_ref, kout_ref, vout_ref,
                  pr_scr, pa_scr, h_scr, hc_scr, s_scr, k_scr, v_scr, bias_scr, *, layer):
    n = pl.program_id(0)
    nb = x_ref.shape[0]

    @pl.when(n == 0)
    def _():
        hc_scr[...] = jnp.zeros_like(hc_scr)
        s_scr[...] = jnp.zeros_like(s_scr)
        k_scr[...] = jnp.zeros_like(k_scr)
        v_scr[:, :, 0:LANES] = jnp.zeros((nb, 2 * TC, LANES), BF16)
        v_scr[:, :, LANES:2 * LANES] = jnp.ones((nb, 2 * TC, LANES), BF16)

    xb = x_ref[...].reshape(nb * TC, D_MODEL).astype(BF16)
    u3 = _dot(xb, w_ref[:, 0:COL_RET]).reshape(nb, TC, SSM_WIDTH)

    def project(dst, col0, c0, c1):
        def emit():
            dst[:, :, c0:c1] = _dot(xb, w_ref[:, col0 + c0:col0 + c1]).reshape(nb, TC, c1 - c0)
        return emit

    pieces = [project(pr_scr, COL_RET, c, c + MXU_COLS) for c in range(0, RET_COLS, MXU_COLS)]
    pieces += [project(pa_scr, COL_ATT, c, min(c + MXU_COLS, ATT_COLS)) for c in range(0, ATT_COLS, MXU_COLS)]
    *s5_maps, s5_finish = _s5_prompt_body(n, u3, bre_ref, bim_ref, cre_ref, cim_ref, ar_ref, ai_ref, d_ref, gw_ref,
                                          gb_ref, z_ref, hlast_ref, h_scr, hc_scr, fillers=pieces[:SCAN_PIECES])
    for emit in s5_maps:
        emit()
    s5_finish(pieces[SCAN_PIECES:])
    _ret_prompt_body(n, pr_scr, dmask_ref, kvdec_ref, indec_ref, cdec_ref, blk_ref, gng_ref, gnb_ref,
                     or_ref, slast_ref, s_scr)
    _att_prompt_body(n, layer, sink_ref, pa_scr, bias_cp_ref, bias_pc_ref, first_ref, oa_ref, kout_ref, vout_ref,
                     k_scr, v_scr, bias_scr)


def _mixer_prompt(x3, p, layer):
    nb, length, _ = x3.shape
    chunk = lambda c: pl.BlockSpec((nb, TC, c), lambda n: (0, n, 0))
    fixed = lambda *shape: pl.BlockSpec(shape, lambda n: (0,) * len(shape))
    ret_consts = _ret_consts()
    att_consts = _att_bias()
    return pl.pallas_call(
        functools.partial(_mixer_kernel, layer=layer),
        grid=(length // TC,),
        in_specs=[pl.BlockSpec(memory_space=pltpu.SMEM), chunk(D_MODEL), _layer_spec((D_MODEL, IN_WIDTH), layer)]
                 + _s5_weight_specs(layer)
                 + [_const_spec(c.shape) for c in ret_consts]
                 + [_layer_spec((1, RET_WIDTH), layer), _layer_spec((1, RET_WIDTH), layer)]
                 + [_const_spec(c.shape) for c in att_consts],
        out_specs=[chunk(SSM_WIDTH), chunk(RET_WIDTH), chunk(ATT_WIDTH),
                   fixed(nb, 2 * SSM_LANES), fixed(nb, 2, LANES, LANES), fixed(nb, TC, LANES), fixed(nb, TC, LANES)],
        out_shape=[jax.ShapeDtypeStruct((nb, length, SSM_WIDTH), BF16),
                   jax.ShapeDtypeStruct((nb, length, RET_WIDTH), BF16),
                   jax.ShapeDtypeStruct((nb, length, ATT_WIDTH), BF16),
                   jax.ShapeDtypeStruct((nb, 2 * SSM_LANES), F32),
                   jax.ShapeDtypeStruct((nb, 2, LANES, LANES), F32),
                   jax.ShapeDtypeStruct((nb, TC, LANES), F32),
                   jax.ShapeDtypeStruct((nb, TC, LANES), F32)],
        scratch_shapes=[pltpu.VMEM((nb, TC, RET_COLS), F32), pltpu.VMEM((nb, TC, ATT_COLS), F32),
                        pltpu.VMEM((2 * N_TILES, nb * TC, LANES), F32), pltpu.VMEM((nb, 2 * SSM_LANES), F32),
                        pltpu.VMEM((nb, 2, LANES, LANES), F32),
                        pltpu.VMEM((nb, 2 * TC, LANES), BF16), pltpu.VMEM((nb, 2 * TC, 2 * LANES), BF16),
                        pltpu.VMEM((ATT_Q_HEADS, TC, 2 * TC), F32)],
        compiler_params=_cparams(1),
        name="mixer",
    )(p["sinks"], x3, p["w_in"], *_s5_weights(p), *ret_consts, p["gn_g"], p["gn_b"], *att_consts)


def _mix_sample_kernel(qt_ref, kt_ref, v4_ref, g4_ref, s_ref, qb_ref, kn_ref, vn_ref, kc_ref, vc_ref,
                       gng_ref, gnb_ref, bias_ref, sink_ref, or_ref, oa_ref, sn_ref, kout_ref, vout_ref):
    gammas = _ret_gammas()
    sink = sink_ref[:, 0:1]
    bias = bias_ref[...]

    rows = range(SAMPLE_ROWS)
    qt, kt, v4, g4 = qt_ref[...], kt_ref[...], v4_ref[...], g4_ref[...]
    for h in range(RET_HEADS):
        sl = slice(RET_D * h, RET_D * (h + 1))
        sn = gammas[h] * s_ref[:, sl, :] + kt[:, :, h:h + 1] * v4[:, h:h + 1, :]
        sn_ref[:, sl, :] = sn
        o = jnp.sum(qt[:, :, h:h + 1] * sn, axis=1, keepdims=True)
        d = o - jnp.mean(o, -1, keepdims=True)
        var = jnp.mean(d * d, -1, keepdims=True)
        o = d * lax.rsqrt(var + LN_EPS) * gng_ref[h:h + 1, :] + gnb_ref[h:h + 1, :]
        gate = g4[:, h:h + 1, :]
        or_ref[:, h:h + 1, :] = gate * _sigmoid(gate) * o

    q_all = jnp.concatenate([qb_ref[i] for i in rows], axis=0)
    knew = jnp.concatenate([jnp.broadcast_to(kn_ref[i], (8, LANES)) for i in rows], axis=0)
    vnew = jnp.concatenate([jnp.broadcast_to(vn_ref[i], (8, LANES)) for i in rows], axis=0)
    sink_all = jnp.concatenate([sink] * SAMPLE_ROWS, axis=0)
    s = jnp.concatenate([_dot_nt(qb_ref[i].astype(BF16), kc_ref[i].astype(BF16)) + bias for i in rows], axis=0)
    s_self = jnp.sum(q_all * knew, -1, keepdims=True)
    m = jnp.maximum(jnp.maximum(jnp.max(s, -1, keepdims=True), s_self), sink_all)
    e = jnp.exp(s - m)
    e_self = jnp.exp(s_self - m)
    den = jnp.sum(e, -1, keepdims=True) + e_self + jnp.exp(sink_all - m)
    pv = jnp.concatenate([_dot(e[8 * i:8 * (i + 1)].astype(BF16), vc_ref[i].astype(BF16)) for i in rows], axis=0)
    o = (pv + e_self * vnew) / den
    for i in rows:
        oa_ref[i] = o[8 * i:8 * (i + 1)]

    kout_ref[:, 0:WINDOW - 1, :] = kc_ref[:, 1:WINDOW, :]
    kout_ref[:, WINDOW - 1:WINDOW, :] = kn_ref[...]
    vout_ref[:, 0:WINDOW - 1, :] = vc_ref[:, 1:WINDOW, :]
    vout_ref[:, WINDOW - 1:WINDOW, :] = vn_ref[...]


def _mix_sample(qt, kt, v4, g4, s_all, qb, kn, vn, kc_all, vc_all, p, bias8, layer):
    nb = qt.shape[0]
    bb = SAMPLE_ROWS
    blk = lambda *dims: pl.BlockSpec((bb,) + dims, lambda i: (i,) + (0,) * len(dims))
    lblk = lambda *dims: pl.BlockSpec((None, bb) + dims, lambda i: (layer, i) + (0,) * len(dims))
    in_specs = [blk(RET_D, RET_HEADS), blk(RET_D, RET_HEADS), blk(RET_HEADS, RET_D), blk(RET_HEADS, RET_D),
                lblk(RET_WIDTH, RET_D), blk(8, LANES), blk(1, LANES), blk(1, LANES),
                lblk(WINDOW, LANES), lblk(WINDOW, LANES),
                _layer_spec((RET_HEADS, RET_D), layer), _layer_spec((RET_HEADS, RET_D), layer),
                _const_spec((8, LANES)), _layer_spec((8, LANES), layer)]
    args = [qt, kt, v4, g4, s_all, qb, kn, vn, kc_all, vc_all, p["gn_g4"], p["gn_b4"], bias8, p["sink8"]]
    state_dims = ((RET_WIDTH, RET_D), (WINDOW, LANES), (WINDOW, LANES))
    return pl.pallas_call(
        _mix_sample_kernel,
        grid=(nb // bb,),
        in_specs=in_specs,
        out_specs=[blk(RET_HEADS, RET_D), blk(8, LANES)] + [blk(*dims) for dims in state_dims],
        out_shape=[jax.ShapeDtypeStruct((nb, RET_HEADS, RET_D), F32),
                   jax.ShapeDtypeStruct((nb, 8, LANES), F32)]
                  + [jax.ShapeDtypeStruct((nb,) + dims, F32) for dims in state_dims],
        compiler_params=_cparams(1),
        name="mix_sample",
    )(*args)


def _outffn_kernel(z_ref, r_ref, a_ref, x_ref, wo_ref, g1_ref, b1_ref, w1_ref, w3_ref, w2_ref, g2_ref, b2_ref, o_ref):
    c1, c2 = SSM_WIDTH, SSM_WIDTH + RET_WIDTH
    tm = x_ref.shape[0]
    parts = FFN_PARTS if tm % (16 * FFN_PARTS) == 0 else 1
    rows = [slice(tm // parts * k, tm // parts * (k + 1)) for k in range(parts)]

    def mixed(sl):
        mix = _dot(z_ref[sl], wo_ref[0:c1]) + _dot(r_ref[sl], wo_ref[c1:c2]) + _dot(a_ref[sl], wo_ref[c2:D_MODEL])
        return _layernorm(ALPHA * x_ref[sl] + mix, g1_ref[...], b1_ref[...])

    def hidden(x):
        xb = x.astype(BF16)
        h1 = _dot(xb, w1_ref[...])
        h3 = _dot(xb, w3_ref[...])
        return (h1 * _sigmoid(h1) * h3).astype(BF16)

    def out(sl, x, hid):
        o_ref[sl] = _layernorm(ALPHA * x + _dot(hid, w2_ref[...]), g2_ref[...], b2_ref[...])

    xs, hs = {}, {}
    for k in range(parts + 2):
        if k < parts:
            xs[k] = mixed(rows[k])
        if 0 <= k - 1 < parts:
            hs[k - 1] = hidden(xs[k - 1])
        if 0 <= k - 2 < parts:
            out(rows[k - 2], xs.pop(k - 2), hs.pop(k - 2))


def _outffn(z, r, a, x2, p, layer):
    t = x2.shape[0]
    tm = min(ROW_TILE, t)
    row = lambda c: pl.BlockSpec((tm, c), lambda i: (i, 0))
    vec = _layer_spec((1, D_MODEL), layer)
    return pl.pallas_call(
        _outffn_kernel,
        grid=(t // tm,),
        in_specs=[row(SSM_WIDTH), row(RET_WIDTH), row(ATT_WIDTH), row(D_MODEL),
                  _layer_spec((D_MODEL, D_MODEL), layer), vec, vec,
                  _layer_spec((D_MODEL, FFN_HIDDEN), layer), _layer_spec((D_MODEL, FFN_HIDDEN), layer),
                  _layer_spec((FFN_HIDDEN, D_MODEL), layer), vec, vec],
        out_specs=row(D_MODEL),
        out_shape=jax.ShapeDtypeStruct((t, D_MODEL), F32),
        compiler_params=_cparams(1),
        name="outffn",
    )(z, r, a, x2, p["w_out"], p["ln1_g"], p["ln1_b"], p["w1"], p["w3"], p["w2"], p["ln2_g"], p["ln2_b"])


def _prep_params(w_in, lam_re, lam_im, log_step, b_re, b_im, c_re, c_im, ssm_d, glu_w, glu_b,
                 gn_g, gn_b, sinks, w_out, ln1_g, ln1_b, w1, w3, w2, ln2_g, ln2_b):
    col_scale = np.ones((IN_WIDTH,), np.float32)
    col_scale[COL_RET + RET_WIDTH:COL_RET + 2 * RET_WIDTH] = RET_D ** -0.5
    col_scale[COL_ATT:COL_ATT + ATT_WIDTH] = ATT_D ** -0.5

    step = jnp.exp(log_step)[..., None]
    mag = jnp.exp(lam_re * step)
    a_re, a_im = mag * jnp.cos(lam_im * step), mag * jnp.sin(lam_im * step)
    den = lam_re * lam_re + lam_im * lam_im
    k_re = ((a_re - 1.0) * lam_re + a_im * lam_im) / den
    k_im = (a_im * lam_re - (a_re - 1.0) * lam_im) / den
    bb_re = k_re[..., None] * b_re - k_im[..., None] * b_im
    bb_im = k_re[..., None] * b_im + k_im[..., None] * b_re
    eye8 = jnp.eye(8, dtype=F32)

    def b_blocks(m):
        m = m.reshape(DEPTH, 3, 8, SSM_STATE, SSM_CH)
        return jnp.einsum("ligph,gk->lighkp", m, eye8).reshape(DEPTH, 3, LANES, 512).astype(BF16)

    def c_blocks(m):
        m = m.reshape(DEPTH, 3, 8, SSM_CH, SSM_STATE)
        return jnp.einsum("lighp,gk->likpgh", m, eye8).reshape(DEPTH, 3, 512, LANES).astype(BF16)

    row = lambda a: a.reshape(DEPTH, 1, -1)
    sink8 = jnp.concatenate([sinks, jnp.zeros((DEPTH, 2), F32)], axis=1)
    return dict(
        w_in=(w_in * col_scale).astype(BF16),
        bre=b_blocks(bb_re), bim=b_blocks(bb_im), cre=c_blocks(c_re), cim=c_blocks(-c_im),
        ar=row(a_re), ai=row(a_im),
        ssm_d=row(ssm_d), glu_w=glu_w.astype(BF16), glu_b=row(glu_b),
        gn_g=row(gn_g), gn_b=row(gn_b),
        gn_g4=gn_g.reshape(DEPTH, RET_HEADS, RET_D), gn_b4=gn_b.reshape(DEPTH, RET_HEADS, RET_D),
        sinks=sinks, sink8=jnp.broadcast_to(sink8[:, :, None], (DEPTH, 8, LANES)),
        w_out=w_out.astype(BF16), ln1_g=row(ln1_g), ln1_b=row(ln1_b),
        w1=w1.astype(BF16), w3=w3.astype(BF16), w2=w2.astype(BF16), ln2_g=row(ln2_g), ln2_b=row(ln2_b),
    )


def _prompt_layer(x3, p, layer):
    nb, length, _ = x3.shape
    t = nb * length
    x2 = x3.reshape(t, D_MODEL)
    z, o_r, o_a, hlast, sblk, klast, vlast = _mixer_prompt(x3, p, layer)
    y = _outffn(z.reshape(t, SSM_WIDTH), o_r.reshape(t, RET_WIDTH), o_a.reshape(t, ATT_WIDTH), x2, p, layer)
    y = y.reshape(nb, length, D_MODEL)
    s4 = jnp.stack([sblk[:, h // 2, HALF * (h % 2):HALF * (h % 2 + 1), HALF * (h % 2):HALF * (h % 2 + 1)]
                    for h in range(RET_HEADS)], axis=1)
    return y, hlast, s4, klast, vlast


def _sample_bias8():
    slopes = _alibi_slopes(ATT_Q_HEADS)
    dist = WINDOW - np.arange(WINDOW)
    bias = np.zeros((8, WINDOW))
    for h in range(ATT_Q_HEADS):
        bias[h] = -slopes[h] * dist
    return jnp.asarray(bias, F32)


def _sample_layer(x2, p, layer, h0r, h0i, s_all, kc_all, vc_all):
    nb = x2.shape[0]
    pu, pr, pa = _inproj(x2, p, layer)
    z, h_re, h_im = _s5_sample(pu, h0r, h0i, p, layer)
    heads = lambda a: a.reshape(nb, RET_HEADS, RET_D)
    q4, k4, v4, g4 = (heads(pr[:, RET_WIDTH * i:RET_WIDTH * (i + 1)]) for i in range(4))
    q6 = pa[:, 0:ATT_WIDTH].reshape(nb, ATT_Q_HEADS, ATT_D)
    kv_of = np.arange(ATT_Q_HEADS) // ATT_GROUP
    qb = jnp.zeros((nb, 8, ATT_KV_HEADS, ATT_D), F32).at[:, np.arange(ATT_Q_HEADS), kv_of].set(q6)
    kn = pa[:, ATT_WIDTH:ATT_WIDTH + LANES].reshape(nb, 1, LANES)
    vn = pa[:, ATT_WIDTH + LANES:ATT_WIDTH + 2 * LANES].reshape(nb, 1, LANES)
    o_r, o8, s_new, k_out, v_out = _mix_sample(
        q4.transpose(0, 2, 1), k4.transpose(0, 2, 1), v4, g4, s_all, qb.reshape(nb, 8, LANES), kn, vn,
        kc_all, vc_all, p, _sample_bias8(), layer)
    o_r = o_r.reshape(nb, RET_WIDTH).astype(BF16)
    o_a = o8.reshape(nb, 8, ATT_KV_HEADS, ATT_D)[:, np.arange(ATT_Q_HEADS), kv_of].reshape(nb, ATT_WIDTH).astype(BF16)
    y = _outffn(z, o_r, o_a, x2, p, layer)
    return y, h_re, h_im, s_new, k_out, v_out


def kernel(x_prompt, x_sample, state_ssm_re, state_ssm_im, state_ret, cache_win_k, cache_win_v, w_in, ssm_lambda_re, ssm_lambda_im, ssm_log_step, ssm_b_re, ssm_b_im, ssm_c_re, ssm_c_im, ssm_d, ssm_glu_w, ssm_glu_b, ret_gn_g, ret_gn_b, attn_sinks, w_out, ln1_g, ln1_b, ffn_w1, ffn_w3, ffn_w2, ln2_g, ln2_b):
    p = _prep_params(w_in, ssm_lambda_re, ssm_lambda_im, ssm_log_step, ssm_b_re, ssm_b_im, ssm_c_re, ssm_c_im,
                     ssm_d, ssm_glu_w, ssm_glu_b, ret_gn_g, ret_gn_b, attn_sinks, w_out, ln1_g, ln1_b,
                     ffn_w1, ffn_w3, ffn_w2, ln2_g, ln2_b)

    xp = x_prompt
    pb = x_prompt.shape[0]
    p_out = []
    for layer in range(DEPTH):
        xp, *states = _prompt_layer(xp, p, layer)
        p_out.append(states)
    stack = lambda outs, i: jnp.stack([o[i] for o in outs])
    p_h = stack(p_out, 0)
    p_kv_shape = (DEPTH, pb, TC, ATT_KV_HEADS, ATT_D)

    nb = x_sample.shape[0]
    xs = x_sample.reshape(nb, D_MODEL)
    h0r = state_ssm_re.reshape(DEPTH, nb, SSM_LANES)
    h0i = state_ssm_im.reshape(DEPTH, nb, SSM_LANES)
    s_all = state_ret.reshape(DEPTH, nb, RET_WIDTH, RET_D)
    kc_all = cache_win_k.reshape(DEPTH, nb, WINDOW, LANES)
    vc_all = cache_win_v.reshape(DEPTH, nb, WINDOW, LANES)
    s_out = []
    for layer in range(DEPTH):
        xs, *states = _sample_layer(xs, p, layer, h0r, h0i, s_all, kc_all, vc_all)
        s_out.append(states)
    ssm_shape = (DEPTH, nb, SSM_GROUPS, SSM_STATE)
    s_kv_shape = (DEPTH, nb, WINDOW, ATT_KV_HEADS, ATT_D)

    return (xp, xs.reshape(nb, 1, D_MODEL),
            p_h[:, :, :SSM_LANES].reshape(DEPTH, pb, SSM_GROUPS, SSM_STATE),
            p_h[:, :, SSM_LANES:].reshape(DEPTH, pb, SSM_GROUPS, SSM_STATE),
            stack(p_out, 1), stack(p_out, 2).reshape(p_kv_shape), stack(p_out, 3).reshape(p_kv_shape),
            stack(s_out, 0).reshape(ssm_shape), stack(s_out, 1).reshape(ssm_shape),
            stack(s_out, 2).reshape(DEPTH, nb, RET_HEADS, RET_D, RET_D),
            stack(s_out, 3).reshape(s_kv_shape), stack(s_out, 4).reshape(s_kv_shape))
```

```python
import functools
import math

import numpy as np
import jax
import jax.numpy as jnp
from jax import lax
from jax.experimental import pallas as pl
from jax.experimental.pallas import tpu as pltpu

F32 = jnp.float32
BF16 = jnp.bfloat16

D_MODEL = 1024
DEPTH = 2
SSM_GROUPS = 24
SSM_CH = 16
SSM_STATE = 64
SSM_WIDTH = SSM_GROUPS * SSM_CH
SSM_LANES = SSM_GROUPS * SSM_STATE
RET_HEADS = 4
RET_D = 64
RET_WIDTH = RET_HEADS * RET_D
ATT_Q_HEADS = 6
ATT_KV_HEADS = 2
ATT_GROUP = ATT_Q_HEADS // ATT_KV_HEADS
ATT_D = 64
ATT_WIDTH = ATT_Q_HEADS * ATT_D
WINDOW = 128
FFN_HIDDEN = 2816
IN_WIDTH = 2048
ALPHA = (2 * DEPTH) ** 0.25
LN_EPS = 1e-5
NEG_INF = -1e30

COL_RET = SSM_WIDTH
COL_ATT = SSM_WIDTH + 4 * RET_WIDTH
RET_COLS = 4 * RET_WIDTH
ATT_COLS = ATT_WIDTH + 2 * ATT_KV_HEADS * ATT_D

TC = 128
LANES = 128
N_TILES = SSM_LANES // LANES
HALF = 64
MXU_COLS = 256
SCAN_PIECES = 4
ROW_TILE = 512
FFN_PARTS = 2
SAMPLE_ROWS = 16
VMEM_LIMIT = 56 * 2 ** 20


def _alibi_slopes(n):
    def pow2(k):
        start = 2.0 ** (-8.0 / k)
        return [start ** (i + 1) for i in range(k)]
    if n & (n - 1) == 0:
        return pow2(n)
    c = 2 ** int(math.floor(math.log2(n)))
    return pow2(c) + pow2(2 * c)[0::2][: n - c]


def _cparams(n_axes):
    return pltpu.CompilerParams(dimension_semantics=("arbitrary",) * n_axes, vmem_limit_bytes=VMEM_LIMIT)


def _const_spec(shape):
    nd = len(shape)
    return pl.BlockSpec(shape, lambda *_: (0,) * nd, pipeline_mode=pl.Buffered(1))


def _layer_spec(shape, layer):
    nd = len(shape)
    return pl.BlockSpec((None,) + tuple(shape), lambda *_: (layer,) + (0,) * nd, pipeline_mode=pl.Buffered(1))


def _dot(a, b):
    return jnp.dot(a, b, preferred_element_type=F32)


def _dot_nt(a, b):
    return lax.dot_general(a, b, (((1,), (1,)), ((), ())), preferred_element_type=F32)


def _dot_tn(a, b):
    return lax.dot_general(a, b, (((0,), (0,)), ((), ())), preferred_element_type=F32)


def _layernorm(v, g, b):
    mu = jnp.mean(v, -1, keepdims=True)
    d = v - mu
    var = jnp.mean(d * d, -1, keepdims=True)
    return d * lax.rsqrt(var + LN_EPS) * g + b


def _gelu_tanh(x):
    return 0.5 * x * (1.0 + jnp.tanh(math.sqrt(2.0 / math.pi) * (x + 0.044715 * (x * x * x))))


def _sigmoid(x):
    return 1.0 / (1.0 + jnp.exp(-x))


def _inproj_kernel(x_ref, w_ref, pu_ref, pr_ref, pa_ref):
    xb = x_ref[...].astype(BF16)
    pu_ref[...] = _dot(xb, w_ref[:, 0:COL_RET])
    pr_ref[...] = _dot(xb, w_ref[:, COL_RET:COL_ATT])
    pa_ref[...] = _dot(xb, w_ref[:, COL_ATT:IN_WIDTH])


def _inproj(x2, p, layer):
    t = x2.shape[0]
    tm = min(ROW_TILE, t)
    row = lambda c: pl.BlockSpec((tm, c), lambda i: (i, 0))
    return pl.pallas_call(
        _inproj_kernel,
        grid=(t // tm,),
        in_specs=[row(D_MODEL), _layer_spec((D_MODEL, IN_WIDTH), layer)],
        out_specs=[row(SSM_WIDTH), row(RET_COLS), row(ATT_COLS)],
        out_shape=[jax.ShapeDtypeStruct((t, SSM_WIDTH), F32),
                   jax.ShapeDtypeStruct((t, RET_COLS), F32),
                   jax.ShapeDtypeStruct((t, ATT_COLS), F32)],
        compiler_params=_cparams(1),
        name="inproj",
    )(x2, p["w_in"])


def _s5_bu(ub, bre_ref, bim_ref):
    res, ims = [], []
    for i in range(3):
        ui = ub[:, LANES * i:LANES * (i + 1)]
        res.append(_dot(ui, bre_ref[i]))
        ims.append(_dot(ui, bim_ref[i]))
    return res, ims


def _s5_out(u, h_re, h_im, cre_ref, cim_ref, d_ref, gw_ref, gb_ref):
    ys = []
    for i in range(3):
        ys.append(_dot(h_re[i].astype(BF16), cre_ref[i]) + _dot(h_im[i].astype(BF16), cim_ref[i]))
    y = jnp.concatenate(ys, axis=1) + d_ref[...] * u
    z = _gelu_tanh(y)
    gate = _sigmoid(_dot(z.astype(BF16), gw_ref[...]) + gb_ref[...])
    return z * gate


def _s5_prompt_body(n, u3, bre_ref, bim_ref, cre_ref, cim_ref, ar_ref, ai_ref, d_ref, gw_ref, gb_ref,
                    z_ref, hlast_ref, h_scr, hc_scr, fillers=()):
    nb = u3.shape[0]

    u = pltpu.einshape("btc->(tb)c", u3)

    res, ims = _s5_bu(u.astype(BF16), bre_ref, bim_ref)
    for i in range(3):
        for k in range(4):
            h_scr[4 * i + k] = res[i][:, LANES * k:LANES * (k + 1)]
            h_scr[N_TILES + 4 * i + k] = ims[i][:, LANES * k:LANES * (k + 1)]

    def step(t, carry):
        r0 = t * nb
        out = []
        for c in range(N_TILES):
            sl = slice(LANES * c, LANES * (c + 1))
            ar, ai = ar_ref[:, sl], ai_ref[:, sl]
            hr, hi = carry[c], carry[N_TILES + c]
            nr = ar * hr - ai * hi + h_scr[c, pl.ds(r0, nb), :]
            ni = ar * hi + ai * hr + h_scr[N_TILES + c, pl.ds(r0, nb), :]
            h_scr[c, pl.ds(r0, nb), :] = nr
            h_scr[N_TILES + c, pl.ds(r0, nb), :] = ni
            out.append((nr, ni))
        return tuple(o[0] for o in out) + tuple(o[1] for o in out)

    fin = tuple(hc_scr[:, LANES * c:LANES * (c + 1)] for c in range(2 * N_TILES))
    groups = len(fillers) + 1
    for g in range(groups):
        for t in range(TC * g // groups, TC * (g + 1) // groups):
            fin = step(t, fin)
        if g < len(fillers):
            fillers[g]()
    for c in range(2 * N_TILES):
        hc_scr[:, LANES * c:LANES * (c + 1)] = fin[c]
        hlast_ref[:, LANES * c:LANES * (c + 1)] = fin[c]

    ys = []

    def out_piece(i):
        def emit():
            h_re = jnp.concatenate([h_scr[4 * i + k] for k in range(4)], axis=1).astype(BF16)
            h_im = jnp.concatenate([h_scr[N_TILES + 4 * i + k] for k in range(4)], axis=1).astype(BF16)
            ys.append(_dot(h_re, cre_ref[i]) + _dot(h_im, cim_ref[i]))
        return emit

    def finish(fillers=()):
        fillers = list(fillers)
        y = jnp.concatenate(ys, axis=1) + d_ref[...] * u
        if fillers:
            fillers.pop(0)()
        z = _gelu_tanh(y)
        if fillers:
            fillers.pop(0)()
        gate = _sigmoid(_dot(z.astype(BF16), gw_ref[...]) + gb_ref[...])
        while fillers:
            fillers.pop(0)()
        z_ref[...] = pltpu.einshape("(tb)c->btc", z * gate, b=nb).astype(BF16)

    return [out_piece(i) for i in range(3)] + [finish]


def _s5_weight_specs(layer):
    ls = lambda *shape: _layer_spec(shape, layer)
    return [ls(3, LANES, 512), ls(3, LANES, 512), ls(3, 512, LANES), ls(3, 512, LANES),
            ls(1, SSM_LANES), ls(1, SSM_LANES), ls(1, SSM_WIDTH), ls(SSM_WIDTH, SSM_WIDTH), ls(1, SSM_WIDTH)]


def _s5_weights(p):
    return (p["bre"], p["bim"], p["cre"], p["cim"], p["ar"], p["ai"], p["ssm_d"], p["glu_w"], p["glu_b"])


def _s5_sample_kernel(u_ref, h0r_ref, h0i_ref, bre_ref, bim_ref, cre_ref, cim_ref, ar_ref, ai_ref, d_ref,
                      gw_ref, gb_ref, z_ref, hr_ref, hi_ref):
    u = u_ref[...]
    res, ims = _s5_bu(u.astype(BF16), bre_ref, bim_ref)
    h_re, h_im = [], []
    for i in range(3):
        sl = slice(512 * i, 512 * (i + 1))
        ar, ai = ar_ref[:, sl], ai_ref[:, sl]
        h0r, h0i = h0r_ref[:, sl], h0i_ref[:, sl]
        nr = ar * h0r - ai * h0i + res[i]
        ni = ar * h0i + ai * h0r + ims[i]
        hr_ref[:, sl] = nr
        hi_ref[:, sl] = ni
        h_re.append(nr)
        h_im.append(ni)
    z_ref[...] = _s5_out(u, h_re, h_im, cre_ref, cim_ref, d_ref, gw_ref, gb_ref).astype(BF16)


def _s5_sample(u, h0r, h0i, p, layer):
    nb = u.shape[0]
    full = lambda c: pl.BlockSpec((nb, c), lambda i: (0, 0))
    state = pl.BlockSpec((None, nb, SSM_LANES), lambda i: (layer, 0, 0))
    return pl.pallas_call(
        _s5_sample_kernel,
        grid=(1,),
        in_specs=[full(SSM_WIDTH), state, state] + _s5_weight_specs(layer),
        out_specs=[full(SSM_WIDTH), full(SSM_LANES), full(SSM_LANES)],
        out_shape=[jax.ShapeDtypeStruct((nb, SSM_WIDTH), BF16),
                   jax.ShapeDtypeStruct((nb, SSM_LANES), F32),
                   jax.ShapeDtypeStruct((nb, SSM_LANES), F32)],
        compiler_params=_cparams(1),
        name="s5_sample",
    )(u, h0r, h0i, *_s5_weights(p))


def _ret_gammas():
    return [1.0 - 2.0 ** (-5.0 - h) for h in range(RET_HEADS)]


def _ret_consts():
    log_g = np.log1p(-(2.0 ** (-5.0 - np.arange(RET_HEADS, dtype=np.float64))))
    idx = np.arange(TC)
    diff = idx[:, None] - idx[None, :]
    dmask = np.where(diff >= 0, np.exp(log_g[:, None, None] * np.maximum(diff, 0)), 0.0)
    inner = np.exp(log_g[:, None] * (idx + 1))
    kvd = np.exp(log_g[:, None] * (TC - 1 - idx))
    cd = np.exp(log_g * TC)
    indec = np.zeros((2, TC, LANES))
    kvdec = np.zeros((2, TC, LANES))
    cdec = np.zeros((2, LANES, LANES))
    blk = np.zeros((LANES, LANES))
    for j in range(2):
        for hh in range(2):
            sl = slice(HALF * hh, HALF * (hh + 1))
            indec[j, :, sl] = inner[2 * j + hh][:, None]
            kvdec[j, :, sl] = kvd[2 * j + hh][:, None]
            cdec[j, sl, sl] = cd[2 * j + hh]
            blk[sl, sl] = 1.0
    as32 = lambda a: jnp.asarray(a, F32)
    return as32(dmask), as32(kvdec), as32(indec), as32(cdec), as32(blk)


def _halfnorm(o, lo):
    zero = jnp.zeros_like(o)
    s_lo = jnp.sum(jnp.where(lo, o, zero), -1, keepdims=True)
    s_hi = jnp.sum(jnp.where(lo, zero, o), -1, keepdims=True)
    d = o - jnp.where(lo, s_lo, s_hi) * (1.0 / HALF)
    d2 = d * d
    v_lo = jnp.sum(jnp.where(lo, d2, zero), -1, keepdims=True)
    v_hi = jnp.sum(jnp.where(lo, zero, d2), -1, keepdims=True)
    var = jnp.where(lo, v_lo, v_hi) * (1.0 / HALF)
    return d * lax.rsqrt(var + LN_EPS)


def _ret_prompt_body(n, pr_ref, dmask_ref, kvdec_ref, indec_ref, cdec_ref, blk_ref, gng_ref, gnb_ref,
                     o_ref, slast_ref, s_scr):
    nb = pr_ref.shape[0]

    lo = lax.broadcasted_iota(jnp.int32, (TC, LANES), 1) < HALF
    zero = jnp.zeros((TC, LANES), F32)
    for b in range(nb):
        for j in range(2):
            q2 = pr_ref[b, :, LANES * j:LANES * (j + 1)]
            k2 = pr_ref[b, :, 256 + LANES * j:256 + LANES * (j + 1)]
            v2 = pr_ref[b, :, 512 + LANES * j:512 + LANES * (j + 1)]
            g2 = pr_ref[b, :, 768 + LANES * j:768 + LANES * (j + 1)]
            k2b = k2.astype(BF16)
            v2b = v2.astype(BF16)
            state = s_scr[b, j]
            qs = jnp.concatenate([jnp.where(lo, q2, zero), jnp.where(lo, zero, q2)], axis=0).astype(BF16)
            sc = _dot_nt(qs, k2b)
            qd = (q2 * indec_ref[j]).astype(BF16)
            lhs = jnp.concatenate(
                [jnp.concatenate([(sc[hh * TC:(hh + 1) * TC] * dmask_ref[2 * j + hh]).astype(BF16), qd], axis=1)
                 for hh in range(2)], axis=0)
            o2 = _dot(lhs, jnp.concatenate([v2b, state.astype(BF16)], axis=0))
            o = jnp.where(lo, o2[0:TC], o2[TC:2 * TC])
            kd = (k2 * kvdec_ref[j]).astype(BF16)
            s_new = cdec_ref[j] * state + blk_ref[...] * _dot_tn(kd, v2b)
            s_scr[b, j] = s_new
            slast_ref[b, j] = s_new
            sl = slice(LANES * j, LANES * (j + 1))
            o = _halfnorm(o, lo) * gng_ref[:, sl] + gnb_ref[:, sl]
            o_ref[b, :, sl] = (g2 * _sigmoid(g2) * o).astype(BF16)


def _att_bias():
    slopes = _alibi_slopes(ATT_Q_HEADS)
    qpos = np.arange(TC)[:, None] + TC
    kpos = np.arange(2 * TC)[None, :]
    dist = qpos - kpos
    valid = (dist >= 0) & (dist <= WINDOW)
    prev_cur = np.stack([np.where(valid, -slopes[h] * dist, NEG_INF) for h in range(ATT_Q_HEADS)])
    cur_prev = np.concatenate([prev_cur[:, :, TC:], prev_cur[:, :, :TC]], axis=2)
    first = np.where(np.arange(2 * TC) >= TC, NEG_INF, 0.0)[None, :]
    return jnp.asarray(cur_prev, F32), jnp.asarray(prev_cur, F32), jnp.asarray(first, F32)


def _att_prompt_body(n, layer, sink_ref, pa_ref, bias_cp_ref, bias_pc_ref, first_ref, o_ref, kout_ref, vout_ref,
                     k_scr, v_scr, bias_scr):
    nb = pa_ref.shape[0]

    even = n % 2 == 0
    row0 = pl.multiple_of((n % 2) * TC, TC)
    first = jnp.where(n == 0, first_ref[...], jnp.zeros_like(first_ref))
    for head in range(ATT_Q_HEADS):
        bias_scr[head] = jnp.where(even, bias_cp_ref[head], bias_pc_ref[head]) + first

    lo = lax.broadcasted_iota(jnp.int32, (TC, LANES), 1) < HALF
    zero = jnp.zeros((TC, LANES), F32)
    swap = lambda a: pltpu.roll(a, HALF, 1)
    moved = [head % 2 != head // ATT_GROUP for head in range(ATT_Q_HEADS)]

    def scores(b):
        kcur = pa_ref[b, :, ATT_WIDTH:ATT_WIDTH + LANES]
        vcur = pa_ref[b, :, ATT_WIDTH + LANES:ATT_WIDTH + 2 * LANES]
        kout_ref[b] = kcur
        vout_ref[b] = vcur
        k_scr[b, pl.ds(row0, TC), :] = kcur.astype(BF16)
        v_scr[b, pl.ds(row0, TC), 0:LANES] = vcur.astype(BF16)
        rows = []
        for head in range(ATT_Q_HEADS):
            qcol = pa_ref[b, :, LANES * (head // 2):LANES * (head // 2 + 1)]
            if moved[head]:
                qcol = swap(qcol)
            rows.append(jnp.where(lo, qcol, zero) if head // ATT_GROUP == 0 else jnp.where(lo, zero, qcol))
        qs = jnp.concatenate(rows, axis=0).astype(BF16)
        return _dot_nt(qs, k_scr[b])

    def softmax(s_all):
        es, ms = [], []
        for head in range(ATT_Q_HEADS):
            s = s_all[head * TC:(head + 1) * TC] + bias_scr[head]
            m = jnp.maximum(jnp.max(s, -1, keepdims=True), sink_ref[layer, head])
            es.append(jnp.exp(s - m).astype(BF16))
            ms.append(m)
        return jnp.concatenate(es, axis=0), ms

    def values(b, e_all, ms):
        r_all = _dot(e_all, v_scr[b])
        outs = []
        for head in range(ATT_Q_HEADS):
            r = r_all[head * TC:(head + 1) * TC]
            den = r[:, LANES:2 * LANES] + jnp.exp(sink_ref[layer, head] - ms[head])
            out = r[:, 0:LANES] / den
            outs.append(swap(out) if moved[head] else out)
        for c in range(3):
            o_ref[b, :, LANES * c:LANES * (c + 1)] = jnp.where(lo, outs[2 * c], outs[2 * c + 1]).astype(BF16)

    s_all, soft = {}, {}
    for k in range(nb + 2):
        if k < nb:
            s_all[k] = scores(k)
        if 0 <= k - 1 < nb:
            soft[k - 1] = softmax(s_all.pop(k - 1))
        if 0 <= k - 2 < nb:
            values(k - 2, *soft.pop(k - 2))


def _mixer_kernel(sink_ref, x_ref, w_ref,
                  bre_ref, bim_ref, cre_ref, cim_ref, ar_ref, ai_ref, d_ref, gw_ref, gb_ref,
                  dmask_ref, kvdec_ref, indec_ref, cdec_ref, blk_ref, gng_ref, gnb_ref,
                  bias_cp_ref, bias_pc_ref, first_ref,
                  z_ref, or_ref, oa_ref, hlast_ref, slast_ref, kout_ref, vout_ref,
                  pr_scr, pa_scr, h_scr, hc_scr, s_scr, k_scr, v_scr, bias_scr, *, layer):
    n = pl.program_id(0)
    nb = x_ref.shape[0]

    @pl.when(n == 0)
    def _():
        hc_scr[...] = jnp.zeros_like(hc_scr)
        s_scr[...] = jnp.zeros_like(s_scr)
        k_scr[...] = jnp.zeros_like(k_scr)
        v_scr[:, :, 0:LANES] = jnp.zeros((nb, 2 * TC, LANES), BF16)
        v_scr[:, :, LANES:2 * LANES] = jnp.ones((nb, 2 * TC, LANES), BF16)

    xb = x_ref[...].reshape(nb * TC, D_MODEL).astype(BF16)
    u3 = _dot(xb, w_ref[:, 0:COL_RET]).reshape(nb, TC, SSM_WIDTH)

    def project(dst, col0, c0, c1):
        def emit():
            dst[:, :, c0:c1] = _dot(xb, w_ref[:, col0 + c0:col0 + c1]).reshape(nb, TC, c1 - c0)
        return emit

    pieces = [project(pr_scr, COL_RET, c, c + MXU_COLS) for c in range(0, RET_COLS, MXU_COLS)]
    pieces += [project(pa_scr, COL_ATT, c, min(c + MXU_COLS, ATT_COLS)) for c in range(0, ATT_COLS, MXU_COLS)]
    *s5_maps, s5_finish = _s5_prompt_body(n, u3, bre_ref, bim_ref, cre_ref, cim_ref, ar_ref, ai_ref, d_ref, gw_ref,
                                          gb_ref, z_ref, hlast_ref, h_scr, hc_scr, fillers=pieces[:SCAN_PIECES])
    for emit in s5_maps:
        emit()
    s5_finish(pieces[SCAN_PIECES:])
    _ret_prompt_body(n, pr_scr, dmask_ref, kvdec_ref, indec_ref, cdec_ref, blk_ref, gng_ref, gnb_ref,
                     or_ref, slast_ref, s_scr)
    _att_prompt_body(n, layer, sink_ref, pa_scr, bias_cp_ref, bias_pc_ref, first_ref, oa_ref, kout_ref, vout_ref,
                     k_scr, v_scr, bias_scr)


def _mixer_prompt(x3, p, layer):
    nb, length, _ = x3.shape
    chunk = lambda c: pl.BlockSpec((nb, TC, c), lambda n: (0, n, 0))
    fixed = lambda *shape: pl.BlockSpec(shape, lambda n: (0,) * len(shape))
    ret_consts = _ret_consts()
    att_consts = _att_bias()
    return pl.pallas_call(
        functools.partial(_mixer_kernel, layer=layer),
        grid=(length // TC,),
        in_specs=[pl.BlockSpec(memory_space=pltpu.SMEM), chunk(D_MODEL), _layer_spec((D_MODEL, IN_WIDTH), layer)]
                 + _s5_weight_specs(layer)
                 + [_const_spec(c.shape) for c in ret_consts]
                 + [_layer_spec((1, RET_WIDTH), layer), _layer_spec((1, RET_WIDTH), layer)]
                 + [_const_spec(c.shape) for c in att_consts],
        out_specs=[chunk(SSM_WIDTH), chunk(RET_WIDTH), chunk(ATT_WIDTH),
                   fixed(nb, 2 * SSM_LANES), fixed(nb, 2, LANES, LANES), fixed(nb, TC, LANES), fixed(nb, TC, LANES)],
        out_shape=[jax.ShapeDtypeStruct((nb, length, SSM_WIDTH), BF16),
                   jax.ShapeDtypeStruct((nb, length, RET_WIDTH), BF16),
                   jax.ShapeDtypeStruct((nb, length, ATT_WIDTH), BF16),
                   jax.ShapeDtypeStruct((nb, 2 * SSM_LANES), F32),
                   jax.ShapeDtypeStruct((nb, 2, LANES, LANES), F32),
                   jax.ShapeDtypeStruct((nb, TC, LANES), F32),
                   jax.ShapeDtypeStruct((nb, TC, LANES), F32)],
        scratch_shapes=[pltpu.VMEM((nb, TC, RET_COLS), F32), pltpu.VMEM((nb, TC, ATT_COLS), F32),
                        pltpu.VMEM((2 * N_TILES, nb * TC, LANES), F32), pltpu.VMEM((nb, 2 * SSM_LANES), F32),
                        pltpu.VMEM((nb, 2, LANES, LANES), F32),
                        pltpu.VMEM((nb, 2 * TC, LANES), BF16), pltpu.VMEM((nb, 2 * TC, 2 * LANES), BF16),
                        pltpu.VMEM((ATT_Q_HEADS, TC, 2 * TC), F32)],
        compiler_params=_cparams(1),
        name="mixer",
    )(p["sinks"], x3, p["w_in"], *_s5_weights(p), *ret_consts, p["gn_g"], p["gn_b"], *att_consts)


def _mix_sample_kernel(qt_ref, kt_ref, v4_ref, g4_ref, s_ref, qb_ref, kn_ref, vn_ref, kc_ref, vc_ref,
                       gng_ref, gnb_ref, bias_ref, sink_ref, or_ref, oa_ref, sn_ref, kout_ref, vout_ref):
    gammas = _ret_gammas()
    sink = sink_ref[:, 0:1]
    bias = bias_ref[...]

    rows = range(SAMPLE_ROWS)
    qt, kt, v4, g4 = qt_ref[...], kt_ref[...], v4_ref[...], g4_ref[...]
    for h in range(RET_HEADS):
        sl = slice(RET_D * h, RET_D * (h + 1))
        sn = gammas[h] * s_ref[:, sl, :] + kt[:, :, h:h + 1] * v4[:, h:h + 1, :]
        sn_ref[:, sl, :] = sn
        o = jnp.sum(qt[:, :, h:h + 1] * sn, axis=1, keepdims=True)
        d = o - jnp.mean(o, -1, keepdims=True)
        var = jnp.mean(d * d, -1, keepdims=True)
        o = d * lax.rsqrt(var + LN_EPS) * gng_ref[h:h + 1, :] + gnb_ref[h:h + 1, :]
        gate = g4[:, h:h + 1, :]
        or_ref[:, h:h + 1, :] = gate * _sigmoid(gate) * o

    q_all = jnp.concatenate([qb_ref[i] for i in rows], axis=0)
    knew = jnp.concatenate([jnp.broadcast_to(kn_ref[i], (8, LANES)) for i in rows], axis=0)
    vnew = jnp.concatenate([jnp.broadcast_to(vn_ref[i], (8, LANES)) for i in rows], axis=0)
    sink_all = jnp.concatenate([sink] * SAMPLE_ROWS, axis=0)
    s = jnp.concatenate([_dot_nt(qb_ref[i].astype(BF16), kc_ref[i].astype(BF16)) + bias for i in rows], axis=0)
    s_self = jnp.sum(q_all * knew, -1, keepdims=True)
    m = jnp.maximum(jnp.maximum(jnp.max(s, -1, keepdims=True), s_self), sink_all)
    e = jnp.exp(s - m)
    e_self = jnp.exp(s_self - m)
    den = jnp.sum(e, -1, keepdims=True) + e_self + jnp.exp(sink_all - m)
    pv = jnp.concatenate([_dot(e[8 * i:8 * (i + 1)].astype(BF16), vc_ref[i].astype(BF16)) for i in rows], axis=0)
    o = (pv + e_self * vnew) / den
    for i in rows:
        oa_ref[i] = o[8 * i:8 * (i + 1)]

    kout_ref[:, 0:WINDOW - 1, :] = kc_ref[:, 1:WINDOW, :]
    kout_ref[:, WINDOW - 1:WINDOW, :] = kn_ref[...]
    vout_ref[:, 0:WINDOW - 1, :] = vc_ref[:, 1:WINDOW, :]
    vout_ref[:, WINDOW - 1:WINDOW, :] = vn_ref[...]


def _mix_sample(qt, kt, v4, g4, s_all, qb, kn, vn, kc_all, vc_all, p, bias8, layer):
    nb = qt.shape[0]
    bb = SAMPLE_ROWS
    blk = lambda *dims: pl.BlockSpec((bb,) + dims, lambda i: (i,) + (0,) * len(dims))
    lblk = lambda *dims: pl.BlockSpec((None, bb) + dims, lambda i: (layer, i) + (0,) * len(dims))
    in_specs = [blk(RET_D, RET_HEADS), blk(RET_D, RET_HEADS), blk(RET_HEADS, RET_D), blk(RET_HEADS, RET_D),
                lblk(RET_WIDTH, RET_D), blk(8, LANES), blk(1, LANES), blk(1, LANES),
                lblk(WINDOW, LANES), lblk(WINDOW, LANES),
                _layer_spec((RET_HEADS, RET_D), layer), _layer_spec((RET_HEADS, RET_D), layer),
                _const_spec((8, LANES)), _layer_spec((8, LANES), layer)]
    args = [qt, kt, v4, g4, s_all, qb, kn, vn, kc_all, vc_all, p["gn_g4"], p["gn_b4"], bias8, p["sink8"]]
    state_dims = ((RET_WIDTH, RET_D), (WINDOW, LANES), (WINDOW, LANES))
    return pl.pallas_call(
        _mix_sample_kernel,
        grid=(nb // bb,),
        in_specs=in_specs,
        out_specs=[blk(RET_HEADS, RET_D), blk(8, LANES)] + [blk(*dims) for dims in state_dims],
        out_shape=[jax.ShapeDtypeStruct((nb, RET_HEADS, RET_D), F32),
                   jax.ShapeDtypeStruct((nb, 8, LANES), F32)]
                  + [jax.ShapeDtypeStruct((nb,) + dims, F32) for dims in state_dims],
        compiler_params=_cparams(1),
        name="mix_sample",
    )(*args)


def _outffn_rows(z_ref, r_ref, a_ref, x_ref, wo_ref, g1_ref, b1_ref, w1_ref, w3_ref, w2_ref, g2_ref, b2_ref, o_ref):
    c1, c2 = SSM_WIDTH, SSM_WIDTH + RET_WIDTH
    tm = x_ref.shape[0]
    parts = FFN_PARTS if tm % (16 * FFN_PARTS) == 0 else 1
    rows = [slice(tm // parts * k, tm // parts * (k + 1)) for k in range(parts)]

    def mixed(sl):
        mix = _dot(z_ref[sl], wo_ref[0:c1]) + _dot(r_ref[sl], wo_ref[c1:c2]) + _dot(a_ref[sl], wo_ref[c2:D_MODEL])
        return _layernorm(ALPHA * x_ref[sl] + mix, g1_ref[...], b1_ref[...])

    def hidden(x):
        xb = x.astype(BF16)
        h1 = _dot(xb, w1_ref[...])
        h3 = _dot(xb, w3_ref[...])
        return (h1 * _sigmoid(h1) * h3).astype(BF16)

    def out(sl, x, hid):
        o_ref[sl] = _layernorm(ALPHA * x + _dot(hid, w2_ref[...]), g2_ref[...], b2_ref[...])

    xs, hs = {}, {}
    for k in range(parts + 2):
        if k < parts:
            xs[k] = mixed(rows[k])
        if 0 <= k - 1 < parts:
            hs[k - 1] = hidden(xs[k - 1])
        if 0 <= k - 2 < parts:
            out(rows[k - 2], xs.pop(k - 2), hs.pop(k - 2))


def _outffn_kernel(zp_ref, rp_ref, ap_ref, xp_ref, zs_ref, rs_ref, as_ref, xs_ref, *rest, prompt_steps):
    weights, (op_ref, os_ref) = rest[:-2], rest[-2:]
    i = pl.program_id(0)

    @pl.when(i < prompt_steps)
    def _():
        _outffn_rows(zp_ref, rp_ref, ap_ref, xp_ref, *weights, op_ref)

    @pl.when(i == prompt_steps)
    def _():
        _outffn_rows(zs_ref, rs_ref, as_ref, xs_ref, *weights, os_ref)


def _outffn(prompt, sample, p, layer):
    tp, ts = prompt[3].shape[0], sample[3].shape[0]
    tm = min(ROW_TILE, tp)
    steps = tp // tm
    tile = lambda c: pl.BlockSpec((tm, c), lambda i: (jnp.minimum(i, steps - 1), 0))
    whole = lambda c: pl.BlockSpec((ts, c), lambda i: (0, 0))
    widths = (SSM_WIDTH, RET_WIDTH, ATT_WIDTH, D_MODEL)
    vec = _layer_spec((1, D_MODEL), layer)
    return pl.pallas_call(
        functools.partial(_outffn_kernel, prompt_steps=steps),
        grid=(steps + 1,),
        in_specs=[tile(c) for c in widths] + [whole(c) for c in widths]
                 + [_layer_spec((D_MODEL, D_MODEL), layer), vec, vec,
                    _layer_spec((D_MODEL, FFN_HIDDEN), layer), _layer_spec((D_MODEL, FFN_HIDDEN), layer),
                    _layer_spec((FFN_HIDDEN, D_MODEL), layer), vec, vec],
        out_specs=[tile(D_MODEL), whole(D_MODEL)],
        out_shape=[jax.ShapeDtypeStruct((tp, D_MODEL), F32), jax.ShapeDtypeStruct((ts, D_MODEL), F32)],
        compiler_params=_cparams(1),
        name="outffn",
    )(*prompt, *sample, p["w_out"], p["ln1_g"], p["ln1_b"], p["w1"], p["w3"], p["w2"], p["ln2_g"], p["ln2_b"])


def _prep_params(w_in, lam_re, lam_im, log_step, b_re, b_im, c_re, c_im, ssm_d, glu_w, glu_b,
                 gn_g, gn_b, sinks, w_out, ln1_g, ln1_b, w1, w3, w2, ln2_g, ln2_b):
    col_scale = np.ones((IN_WIDTH,), np.float32)
    col_scale[COL_RET + RET_WIDTH:COL_RET + 2 * RET_WIDTH] = RET_D ** -0.5
    col_scale[COL_ATT:COL_ATT + ATT_WIDTH] = ATT_D ** -0.5

    step = jnp.exp(log_step)[..., None]
    mag = jnp.exp(lam_re * step)
    a_re, a_im = mag * jnp.cos(lam_im * step), mag * jnp.sin(lam_im * step)
    den = lam_re * lam_re + lam_im * lam_im
    k_re = ((a_re - 1.0) * lam_re + a_im * lam_im) / den
    k_im = (a_im * lam_re - (a_re - 1.0) * lam_im) / den
    bb_re = k_re[..., None] * b_re - k_im[..., None] * b_im
    bb_im = k_re[..., None] * b_im + k_im[..., None] * b_re
    eye8 = jnp.eye(8, dtype=F32)

    def b_blocks(m):
        m = m.reshape(DEPTH, 3, 8, SSM_STATE, SSM_CH)
        return jnp.einsum("ligph,gk->lighkp", m, eye8).reshape(DEPTH, 3, LANES, 512).astype(BF16)

    def c_blocks(m):
        m = m.reshape(DEPTH, 3, 8, SSM_CH, SSM_STATE)
        return jnp.einsum("lighp,gk->likpgh", m, eye8).reshape(DEPTH, 3, 512, LANES).astype(BF16)

    row = lambda a: a.reshape(DEPTH, 1, -1)
    sink8 = jnp.concatenate([sinks, jnp.zeros((DEPTH, 2), F32)], axis=1)
    return dict(
        w_in=(w_in * col_scale).astype(BF16),
        bre=b_blocks(bb_re), bim=b_blocks(bb_im), cre=c_blocks(c_re), cim=c_blocks(-c_im),
        ar=row(a_re), ai=row(a_im),
        ssm_d=row(ssm_d), glu_w=glu_w.astype(BF16), glu_b=row(glu_b),
        gn_g=row(gn_g), gn_b=row(gn_b),
        gn_g4=gn_g.reshape(DEPTH, RET_HEADS, RET_D), gn_b4=gn_b.reshape(DEPTH, RET_HEADS, RET_D),
        sinks=sinks, sink8=jnp.broadcast_to(sink8[:, :, None], (DEPTH, 8, LANES)),
        w_out=w_out.astype(BF16), ln1_g=row(ln1_g), ln1_b=row(ln1_b),
        w1=w1.astype(BF16), w3=w3.astype(BF16), w2=w2.astype(BF16), ln2_g=row(ln2_g), ln2_b=row(ln2_b),
    )


def _prompt_mix(x3, p, layer):
    nb, length, _ = x3.shape
    t = nb * length
    z, o_r, o_a, hlast, sblk, klast, vlast = _mixer_prompt(x3, p, layer)
    rows = (z.reshape(t, SSM_WIDTH), o_r.reshape(t, RET_WIDTH), o_a.reshape(t, ATT_WIDTH), x3.reshape(t, D_MODEL))
    s4 = jnp.stack([sblk[:, h // 2, HALF * (h % 2):HALF * (h % 2 + 1), HALF * (h % 2):HALF * (h % 2 + 1)]
                    for h in range(RET_HEADS)], axis=1)
    return rows, (hlast, s4, klast, vlast)


def _sample_bias8():
    slopes = _alibi_slopes(ATT_Q_HEADS)
    dist = WINDOW - np.arange(WINDOW)
    bias = np.zeros((8, WINDOW))
    for h in range(ATT_Q_HEADS):
        bias[h] = -slopes[h] * dist
    return jnp.asarray(bias, F32)


def _sample_mix(x2, p, layer, h0r, h0i, s_all, kc_all, vc_all):
    nb = x2.shape[0]
    pu, pr, pa = _inproj(x2, p, layer)
    z, h_re, h_im = _s5_sample(pu, h0r, h0i, p, layer)
    heads = lambda a: a.reshape(nb, RET_HEADS, RET_D)
    q4, k4, v4, g4 = (heads(pr[:, RET_WIDTH * i:RET_WIDTH * (i + 1)]) for i in range(4))
    q6 = pa[:, 0:ATT_WIDTH].reshape(nb, ATT_Q_HEADS, ATT_D)
    kv_of = np.arange(ATT_Q_HEADS) // ATT_GROUP
    qb = jnp.zeros((nb, 8, ATT_KV_HEADS, ATT_D), F32).at[:, np.arange(ATT_Q_HEADS), kv_of].set(q6)
    kn = pa[:, ATT_WIDTH:ATT_WIDTH + LANES].reshape(nb, 1, LANES)
    vn = pa[:, ATT_WIDTH + LANES:ATT_WIDTH + 2 * LANES].reshape(nb, 1, LANES)
    o_r, o8, s_new, k_out, v_out = _mix_sample(
        q4.transpose(0, 2, 1), k4.transpose(0, 2, 1), v4, g4, s_all, qb.reshape(nb, 8, LANES), kn, vn,
        kc_all, vc_all, p, _sample_bias8(), layer)
    o_r = o_r.reshape(nb, RET_WIDTH).astype(BF16)
    o_a = o8.reshape(nb, 8, ATT_KV_HEADS, ATT_D)[:, np.arange(ATT_Q_HEADS), kv_of].reshape(nb, ATT_WIDTH).astype(BF16)
    return (z, o_r, o_a, x2), (h_re, h_im, s_new, k_out, v_out)


def kernel(x_prompt, x_sample, state_ssm_re, state_ssm_im, state_ret, cache_win_k, cache_win_v, w_in, ssm_lambda_re, ssm_lambda_im, ssm_log_step, ssm_b_re, ssm_b_im, ssm_c_re, ssm_c_im, ssm_d, ssm_glu_w, ssm_glu_b, ret_gn_g, ret_gn_b, attn_sinks, w_out, ln1_g, ln1_b, ffn_w1, ffn_w3, ffn_w2, ln2_g, ln2_b):
    p = _prep_params(w_in, ssm_lambda_re, ssm_lambda_im, ssm_log_step, ssm_b_re, ssm_b_im, ssm_c_re, ssm_c_im,
                     ssm_d, ssm_glu_w, ssm_glu_b, ret_gn_g, ret_gn_b, attn_sinks, w_out, ln1_g, ln1_b,
                     ffn_w1, ffn_w3, ffn_w2, ln2_g, ln2_b)

    xp = x_prompt
    pb, length, _ = x_prompt.shape
    nb = x_sample.shape[0]
    xs = x_sample.reshape(nb, D_MODEL)
    h0r = state_ssm_re.reshape(DEPTH, nb, SSM_LANES)
    h0i = state_ssm_im.reshape(DEPTH, nb, SSM_LANES)
    s_all = state_ret.reshape(DEPTH, nb, RET_WIDTH, RET_D)
    kc_all = cache_win_k.reshape(DEPTH, nb, WINDOW, LANES)
    vc_all = cache_win_v.reshape(DEPTH, nb, WINDOW, LANES)
    p_out, s_out = [], []
    for layer in range(DEPTH):
        prompt_rows, states = _prompt_mix(xp, p, layer)
        p_out.append(states)
        sample_rows, states = _sample_mix(xs, p, layer, h0r, h0i, s_all, kc_all, vc_all)
        s_out.append(states)
        yp, xs = _outffn(prompt_rows, sample_rows, p, layer)
        xp = yp.reshape(pb, length, D_MODEL)
    stack = lambda outs, i: jnp.stack([o[i] for o in outs])
    p_h = stack(p_out, 0)
    p_kv_shape = (DEPTH, pb, TC, ATT_KV_HEADS, ATT_D)
    ssm_shape = (DEPTH, nb, SSM_GROUPS, SSM_STATE)
    s_kv_shape = (DEPTH, nb, WINDOW, ATT_KV_HEADS, ATT_D)

    return (xp, xs.reshape(nb, 1, D_MODEL),
            p_h[:, :, :SSM_LANES].reshape(DEPTH, pb, SSM_GROUPS, SSM_STATE),
            p_h[:, :, SSM_LANES:].reshape(DEPTH, pb, SSM_GROUPS, SSM_STATE),
            stack(p_out, 1), stack(p_out, 2).reshape(p_kv_shape), stack(p_out, 3).reshape(p_kv_shape),
            stack(s_out, 0).reshape(ssm_shape), stack(s_out, 1).reshape(ssm_shape),
            stack(s_out, 2).reshape(DEPTH, nb, RET_HEADS, RET_D, RET_D),
            stack(s_out, 3).reshape(s_kv_shape), stack(s_out, 4).reshape(s_kv_shape))
```

```python
import functools
import math

import numpy as np
import jax
import jax.numpy as jnp
from jax import lax
from jax.experimental import pallas as pl
from jax.experimental.pallas import tpu as pltpu

F32 = jnp.float32
BF16 = jnp.bfloat16

D_MODEL = 1024
DEPTH = 2
SSM_GROUPS = 24
SSM_CH = 16
SSM_STATE = 64
SSM_WIDTH = SSM_GROUPS * SSM_CH
SSM_LANES = SSM_GROUPS * SSM_STATE
RET_HEADS = 4
RET_D = 64
RET_WIDTH = RET_HEADS * RET_D
ATT_Q_HEADS = 6
ATT_KV_HEADS = 2
ATT_GROUP = ATT_Q_HEADS // ATT_KV_HEADS
ATT_D = 64
ATT_WIDTH = ATT_Q_HEADS * ATT_D
WINDOW = 128
FFN_HIDDEN = 2816
IN_WIDTH = 2048
ALPHA = (2 * DEPTH) ** 0.25
LN_EPS = 1e-5
NEG_INF = -1e30

COL_RET = SSM_WIDTH
COL_ATT = SSM_WIDTH + 4 * RET_WIDTH
RET_COLS = 4 * RET_WIDTH
ATT_COLS = ATT_WIDTH + 2 * ATT_KV_HEADS * ATT_D

TC = 128
LANES = 128
N_TILES = SSM_LANES // LANES
HALF = 64
MXU_COLS = 256
SCAN_PIECES = 4
ROW_TILE = 512
FFN_PARTS = 2
SAMPLE_ROWS = 16
VMEM_LIMIT = 56 * 2 ** 20


def _alibi_slopes(n):
    def pow2(k):
        start = 2.0 ** (-8.0 / k)
        return [start ** (i + 1) for i in range(k)]
    if n & (n - 1) == 0:
        return pow2(n)
    c = 2 ** int(math.floor(math.log2(n)))
    return pow2(c) + pow2(2 * c)[0::2][: n - c]


def _cparams(n_axes):
    return pltpu.CompilerParams(dimension_semantics=("arbitrary",) * n_axes, vmem_limit_bytes=VMEM_LIMIT)


def _const_spec(shape):
    nd = len(shape)
    return pl.BlockSpec(shape, lambda *_: (0,) * nd, pipeline_mode=pl.Buffered(1))


def _layer_spec(shape, layer):
    nd = len(shape)
    return pl.BlockSpec((None,) + tuple(shape), lambda *_: (layer,) + (0,) * nd, pipeline_mode=pl.Buffered(1))


def _dot(a, b):
    return jnp.dot(a, b, preferred_element_type=F32)


def _dot_nt(a, b):
    return lax.dot_general(a, b, (((1,), (1,)), ((), ())), preferred_element_type=F32)


def _dot_tn(a, b):
    return lax.dot_general(a, b, (((0,), (0,)), ((), ())), preferred_element_type=F32)


def _layernorm(v, g, b):
    mu = jnp.mean(v, -1, keepdims=True)
    d = v - mu
    var = jnp.mean(d * d, -1, keepdims=True)
    return d * lax.rsqrt(var + LN_EPS) * g + b


def _gelu_tanh(x):
    return 0.5 * x * (1.0 + jnp.tanh(math.sqrt(2.0 / math.pi) * (x + 0.044715 * (x * x * x))))


def _sigmoid(x):
    return 1.0 / (1.0 + jnp.exp(-x))


def _inproj_kernel(x_ref, w_ref, pu_ref, pr_ref, pa_ref):
    xb = x_ref[...].astype(BF16)
    pu_ref[...] = _dot(xb, w_ref[:, 0:COL_RET])
    pr_ref[...] = _dot(xb, w_ref[:, COL_RET:COL_ATT])
    pa_ref[...] = _dot(xb, w_ref[:, COL_ATT:IN_WIDTH])


def _inproj(x2, p, layer):
    t = x2.shape[0]
    tm = min(ROW_TILE, t)
    row = lambda c: pl.BlockSpec((tm, c), lambda i: (i, 0))
    return pl.pallas_call(
        _inproj_kernel,
        grid=(t // tm,),
        in_specs=[row(D_MODEL), _layer_spec((D_MODEL, IN_WIDTH), layer)],
        out_specs=[row(SSM_WIDTH), row(RET_COLS), row(ATT_COLS)],
        out_shape=[jax.ShapeDtypeStruct((t, SSM_WIDTH), F32),
                   jax.ShapeDtypeStruct((t, RET_COLS), F32),
                   jax.ShapeDtypeStruct((t, ATT_COLS), F32)],
        compiler_params=_cparams(1),
        name="inproj",
    )(x2, p["w_in"])


def _s5_bu(ub, bre_ref, bim_ref):
    res, ims = [], []
    for i in range(3):
        ui = ub[:, LANES * i:LANES * (i + 1)]
        res.append(_dot(ui, bre_ref[i]))
        ims.append(_dot(ui, bim_ref[i]))
    return res, ims


def _s5_out(u, h_re, h_im, cre_ref, cim_ref, d_ref, gw_ref, gb_ref):
    ys = []
    for i in range(3):
        ys.append(_dot(h_re[i].astype(BF16), cre_ref[i]) + _dot(h_im[i].astype(BF16), cim_ref[i]))
    y = jnp.concatenate(ys, axis=1) + d_ref[...] * u
    z = _gelu_tanh(y)
    gate = _sigmoid(_dot(z.astype(BF16), gw_ref[...]) + gb_ref[...])
    return z * gate


def _s5_prompt_body(n, u3, bre_ref, bim_ref, cre_ref, cim_ref, ar_ref, ai_ref, d_ref, gw_ref, gb_ref,
                    z_ref, hlast_ref, h_scr, hc_scr, fillers=()):
    nb = u3.shape[0]

    u = pltpu.einshape("btc->(tb)c", u3)

    res, ims = _s5_bu(u.astype(BF16), bre_ref, bim_ref)
    for i in range(3):
        for k in range(4):
            h_scr[4 * i + k] = res[i][:, LANES * k:LANES * (k + 1)]
            h_scr[N_TILES + 4 * i + k] = ims[i][:, LANES * k:LANES * (k + 1)]

    def step(t, carry):
        r0 = t * nb
        out = []
        for c in range(N_TILES):
            sl = slice(LANES * c, LANES * (c + 1))
            ar, ai = ar_ref[:, sl], ai_ref[:, sl]
            hr, hi = carry[c], carry[N_TILES + c]
            nr = ar * hr - ai * hi + h_scr[c, pl.ds(r0, nb), :]
            ni = ar * hi + ai * hr + h_scr[N_TILES + c, pl.ds(r0, nb), :]
            h_scr[c, pl.ds(r0, nb), :] = nr
            h_scr[N_TILES + c, pl.ds(r0, nb), :] = ni
            out.append((nr, ni))
        return tuple(o[0] for o in out) + tuple(o[1] for o in out)

    fin = tuple(hc_scr[:, LANES * c:LANES * (c + 1)] for c in range(2 * N_TILES))
    groups = len(fillers) + 1
    for g in range(groups):
        for t in range(TC * g // groups, TC * (g + 1) // groups):
            fin = step(t, fin)
        if g < len(fillers):
            fillers[g]()
    for c in range(2 * N_TILES):
        hc_scr[:, LANES * c:LANES * (c + 1)] = fin[c]
        hlast_ref[:, LANES * c:LANES * (c + 1)] = fin[c]

    ys = []

    def out_piece(i):
        def emit():
            h_re = jnp.concatenate([h_scr[4 * i + k] for k in range(4)], axis=1).astype(BF16)
            h_im = jnp.concatenate([h_scr[N_TILES + 4 * i + k] for k in range(4)], axis=1).astype(BF16)
            ys.append(_dot(h_re, cre_ref[i]) + _dot(h_im, cim_ref[i]))
        return emit

    def finish(fillers=()):
        fillers = list(fillers)
        y = jnp.concatenate(ys, axis=1) + d_ref[...] * u
        if fillers:
            fillers.pop(0)()
        z = _gelu_tanh(y)
        if fillers:
            fillers.pop(0)()
        gate = _sigmoid(_dot(z.astype(BF16), gw_ref[...]) + gb_ref[...])
        while fillers:
            fillers.pop(0)()
        z_ref[...] = pltpu.einshape("(tb)c->btc", z * gate, b=nb).astype(BF16)

    return [out_piece(i) for i in range(3)] + [finish]


def _s5_weight_specs(layer):
    ls = lambda *shape: _layer_spec(shape, layer)
    return [ls(3, LANES, 512), ls(3, LANES, 512), ls(3, 512, LANES), ls(3, 512, LANES),
            ls(1, SSM_LANES), ls(1, SSM_LANES), ls(1, SSM_WIDTH), ls(SSM_WIDTH, SSM_WIDTH), ls(1, SSM_WIDTH)]


def _s5_weights(p):
    return (p["bre"], p["bim"], p["cre"], p["cim"], p["ar"], p["ai"], p["ssm_d"], p["glu_w"], p["glu_b"])


def _s5_sample_kernel(u_ref, h0r_ref, h0i_ref, bre_ref, bim_ref, cre_ref, cim_ref, ar_ref, ai_ref, d_ref,
                      gw_ref, gb_ref, z_ref, hr_ref, hi_ref):
    u = u_ref[...]
    res, ims = _s5_bu(u.astype(BF16), bre_ref, bim_ref)
    h_re, h_im = [], []
    for i in range(3):
        sl = slice(512 * i, 512 * (i + 1))
        ar, ai = ar_ref[:, sl], ai_ref[:, sl]
        h0r, h0i = h0r_ref[sl, :].T, h0i_ref[sl, :].T
        nr = ar * h0r - ai * h0i + res[i]
        ni = ar * h0i + ai * h0r + ims[i]
        hr_ref[sl, :] = nr.T
        hi_ref[sl, :] = ni.T
        h_re.append(nr)
        h_im.append(ni)
    z_ref[...] = _s5_out(u, h_re, h_im, cre_ref, cim_ref, d_ref, gw_ref, gb_ref).astype(BF16)


def _s5_sample(u, h0r, h0i, p, layer):
    nb = u.shape[0]
    full = lambda c: pl.BlockSpec((nb, c), lambda i: (0, 0))
    state = pl.BlockSpec((None, SSM_LANES, nb), lambda i: (layer, 0, 0))
    state_out = pl.BlockSpec((SSM_LANES, nb), lambda i: (0, 0))
    return pl.pallas_call(
        _s5_sample_kernel,
        grid=(1,),
        in_specs=[full(SSM_WIDTH), state, state] + _s5_weight_specs(layer),
        out_specs=[full(SSM_WIDTH), state_out, state_out],
        out_shape=[jax.ShapeDtypeStruct((nb, SSM_WIDTH), BF16),
                   jax.ShapeDtypeStruct((SSM_LANES, nb), F32),
                   jax.ShapeDtypeStruct((SSM_LANES, nb), F32)],
        compiler_params=_cparams(1),
        name="s5_sample",
    )(u, h0r, h0i, *_s5_weights(p))


def _ret_gammas():
    return [1.0 - 2.0 ** (-5.0 - h) for h in range(RET_HEADS)]


def _ret_consts():
    log_g = np.log1p(-(2.0 ** (-5.0 - np.arange(RET_HEADS, dtype=np.float64))))
    idx = np.arange(TC)
    diff = idx[:, None] - idx[None, :]
    dmask = np.where(diff >= 0, np.exp(log_g[:, None, None] * np.maximum(diff, 0)), 0.0)
    inner = np.exp(log_g[:, None] * (idx + 1))
    kvd = np.exp(log_g[:, None] * (TC - 1 - idx))
    cd = np.exp(log_g * TC)
    indec = np.zeros((2, TC, LANES))
    kvdec = np.zeros((2, TC, LANES))
    cdec = np.zeros((2, LANES, LANES))
    blk = np.zeros((LANES, LANES))
    for j in range(2):
        for hh in range(2):
            sl = slice(HALF * hh, HALF * (hh + 1))
            indec[j, :, sl] = inner[2 * j + hh][:, None]
            kvdec[j, :, sl] = kvd[2 * j + hh][:, None]
            cdec[j, sl, sl] = cd[2 * j + hh]
            blk[sl, sl] = 1.0
    as32 = lambda a: jnp.asarray(a, F32)
    return as32(dmask), as32(kvdec), as32(indec), as32(cdec), as32(blk)


def _halfnorm(o, lo):
    zero = jnp.zeros_like(o)
    s_lo = jnp.sum(jnp.where(lo, o, zero), -1, keepdims=True)
    s_hi = jnp.sum(jnp.where(lo, zero, o), -1, keepdims=True)
    d = o - jnp.where(lo, s_lo, s_hi) * (1.0 / HALF)
    d2 = d * d
    v_lo = jnp.sum(jnp.where(lo, d2, zero), -1, keepdims=True)
    v_hi = jnp.sum(jnp.where(lo, zero, d2), -1, keepdims=True)
    var = jnp.where(lo, v_lo, v_hi) * (1.0 / HALF)
    return d * lax.rsqrt(var + LN_EPS)


def _ret_prompt_body(n, pr_ref, dmask_ref, kvdec_ref, indec_ref, cdec_ref, blk_ref, gng_ref, gnb_ref,
                     o_ref, slast_ref, s_scr):
    nb = pr_ref.shape[0]

    lo = lax.broadcasted_iota(jnp.int32, (TC, LANES), 1) < HALF
    zero = jnp.zeros((TC, LANES), F32)
    for b in range(nb):
        for j in range(2):
            q2 = pr_ref[b, :, LANES * j:LANES * (j + 1)]
            k2 = pr_ref[b, :, 256 + LANES * j:256 + LANES * (j + 1)]
            v2 = pr_ref[b, :, 512 + LANES * j:512 + LANES * (j + 1)]
            g2 = pr_ref[b, :, 768 + LANES * j:768 + LANES * (j + 1)]
            k2b = k2.astype(BF16)
            v2b = v2.astype(BF16)
            state = s_scr[b, j]
            qs = jnp.concatenate([jnp.where(lo, q2, zero), jnp.where(lo, zero, q2)], axis=0).astype(BF16)
            sc = _dot_nt(qs, k2b)
            qd = (q2 * indec_ref[j]).astype(BF16)
            lhs = jnp.concatenate(
                [jnp.concatenate([(sc[hh * TC:(hh + 1) * TC] * dmask_ref[2 * j + hh]).astype(BF16), qd], axis=1)
                 for hh in range(2)], axis=0)
            o2 = _dot(lhs, jnp.concatenate([v2b, state.astype(BF16)], axis=0))
            o = jnp.where(lo, o2[0:TC], o2[TC:2 * TC])
            kd = (k2 * kvdec_ref[j]).astype(BF16)
            s_new = cdec_ref[j] * state + blk_ref[...] * _dot_tn(kd, v2b)
            s_scr[b, j] = s_new
            slast_ref[b, 2 * j] = s_new[0:HALF, 0:HALF]
            slast_ref[b, 2 * j + 1] = pltpu.roll(s_new, HALF, 1)[HALF:LANES, 0:HALF]
            sl = slice(LANES * j, LANES * (j + 1))
            o = _halfnorm(o, lo) * gng_ref[:, sl] + gnb_ref[:, sl]
            o_ref[b, :, sl] = (g2 * _sigmoid(g2) * o).astype(BF16)


def _att_bias():
    slopes = _alibi_slopes(ATT_Q_HEADS)
    qpos = np.arange(TC)[:, None] + TC
    kpos = np.arange(2 * TC)[None, :]
    dist = qpos - kpos
    valid = (dist >= 0) & (dist <= WINDOW)
    prev_cur = np.stack([np.where(valid, -slopes[h] * dist, NEG_INF) for h in range(ATT_Q_HEADS)])
    cur_prev = np.concatenate([prev_cur[:, :, TC:], prev_cur[:, :, :TC]], axis=2)
    first = np.where(np.arange(2 * TC) >= TC, NEG_INF, 0.0)[None, :]
    return jnp.asarray(cur_prev, F32), jnp.asarray(prev_cur, F32), jnp.asarray(first, F32)


def _att_prompt_body(n, layer, sink_ref, pa_ref, bias_cp_ref, bias_pc_ref, first_ref, o_ref, kout_ref, vout_ref,
                     k_scr, v_scr, bias_scr):
    nb = pa_ref.shape[0]

    even = n % 2 == 0
    row0 = pl.multiple_of((n % 2) * TC, TC)
    first = jnp.where(n == 0, first_ref[...], jnp.zeros_like(first_ref))
    for head in range(ATT_Q_HEADS):
        bias_scr[head] = jnp.where(even, bias_cp_ref[head], bias_pc_ref[head]) + first

    lo = lax.broadcasted_iota(jnp.int32, (TC, LANES), 1) < HALF
    zero = jnp.zeros((TC, LANES), F32)
    swap = lambda a: pltpu.roll(a, HALF, 1)
    moved = [head % 2 != head // ATT_GROUP for head in range(ATT_Q_HEADS)]

    def scores(b):
        kcur = pa_ref[b, :, ATT_WIDTH:ATT_WIDTH + LANES]
        vcur = pa_ref[b, :, ATT_WIDTH + LANES:ATT_WIDTH + 2 * LANES]
        kout_ref[b] = kcur
        vout_ref[b] = vcur
        k_scr[b, pl.ds(row0, TC), :] = kcur.astype(BF16)
        v_scr[b, pl.ds(row0, TC), 0:LANES] = vcur.astype(BF16)
        rows = []
        for head in range(ATT_Q_HEADS):
            qcol = pa_ref[b, :, LANES * (head // 2):LANES * (head // 2 + 1)]
            if moved[head]:
                qcol = swap(qcol)
            rows.append(jnp.where(lo, qcol, zero) if head // ATT_GROUP == 0 else jnp.where(lo, zero, qcol))
        qs = jnp.concatenate(rows, axis=0).astype(BF16)
        return _dot_nt(qs, k_scr[b])

    def softmax(s_all):
        es, ms = [], []
        for head in range(ATT_Q_HEADS):
            s = s_all[head * TC:(head + 1) * TC] + bias_scr[head]
            m = jnp.maximum(jnp.max(s, -1, keepdims=True), sink_ref[layer, head])
            es.append(jnp.exp(s - m).astype(BF16))
            ms.append(m)
        return jnp.concatenate(es, axis=0), ms

    def values(b, e_all, ms):
        r_all = _dot(e_all, v_scr[b])
        outs = []
        for head in range(ATT_Q_HEADS):
            r = r_all[head * TC:(head + 1) * TC]
            den = r[:, LANES:2 * LANES] + jnp.exp(sink_ref[layer, head] - ms[head])
            out = r[:, 0:LANES] / den
            outs.append(swap(out) if moved[head] else out)
        for c in range(3):
            o_ref[b, :, LANES * c:LANES * (c + 1)] = jnp.where(lo, outs[2 * c], outs[2 * c + 1]).astype(BF16)

    s_all, soft = {}, {}
    for k in range(nb + 2):
        if k < nb:
            s_all[k] = scores(k)
        if 0 <= k - 1 < nb:
            soft[k - 1] = softmax(s_all.pop(k - 1))
        if 0 <= k - 2 < nb:
            values(k - 2, *soft.pop(k - 2))


def _mixer_kernel(sink_ref, x_ref, w_ref,
                  bre_ref, bim_ref, cre_ref, cim_ref, ar_ref, ai_ref, d_ref, gw_ref, gb_ref,
                  dmask_ref, kvdec_ref, indec_ref, cdec_ref, blk_ref, gng_ref, gnb_ref,
                  bias_cp_ref, bias_pc_ref, first_ref,
                  z_ref, or_ref, oa_ref, hlast_ref, slast_ref, kout_ref, vout_ref,
                  pr_scr, pa_scr, h_scr, hc_scr, s_scr, k_scr, v_scr, bias_scr, *, layer):
    n = pl.program_id(0)
    nb = x_ref.shape[0]

    @pl.when(n == 0)
    def _():
        hc_scr[...] = jnp.zeros_like(hc_scr)
        s_scr[...] = jnp.zeros_like(s_scr)
        k_scr[...] = jnp.zeros_like(k_scr)
        v_scr[:, :, 0:LANES] = jnp.zeros((nb, 2 * TC, LANES), BF16)
        v_scr[:, :, LANES:2 * LANES] = jnp.ones((nb, 2 * TC, LANES), BF16)

    xb = x_ref[...].reshape(nb * TC, D_MODEL).astype(BF16)
    u3 = _dot(xb, w_ref[:, 0:COL_RET]).reshape(nb, TC, SSM_WIDTH)

    def project(dst, col0, c0, c1):
        def emit():
            dst[:, :, c0:c1] = _dot(xb, w_ref[:, col0 + c0:col0 + c1]).reshape(nb, TC, c1 - c0)
        return emit

    pieces = [project(pr_scr, COL_RET, c, c + MXU_COLS) for c in range(0, RET_COLS, MXU_COLS)]
    pieces += [project(pa_scr, COL_ATT, c, min(c + MXU_COLS, ATT_COLS)) for c in range(0, ATT_COLS, MXU_COLS)]
    *s5_maps, s5_finish = _s5_prompt_body(n, u3, bre_ref, bim_ref, cre_ref, cim_ref, ar_ref, ai_ref, d_ref, gw_ref,
                                          gb_ref, z_ref, hlast_ref, h_scr, hc_scr, fillers=pieces[:SCAN_PIECES])
    for emit in s5_maps:
        emit()
    s5_finish(pieces[SCAN_PIECES:])
    _ret_prompt_body(n, pr_scr, dmask_ref, kvdec_ref, indec_ref, cdec_ref, blk_ref, gng_ref, gnb_ref,
                     or_ref, slast_ref, s_scr)
    _att_prompt_body(n, layer, sink_ref, pa_scr, bias_cp_ref, bias_pc_ref, first_ref, oa_ref, kout_ref, vout_ref,
                     k_scr, v_scr, bias_scr)


def _mixer_prompt(x3, p, layer):
    nb, length, _ = x3.shape
    chunk = lambda c: pl.BlockSpec((nb, TC, c), lambda n: (0, n, 0))
    fixed = lambda *shape: pl.BlockSpec(shape, lambda n: (0,) * len(shape))
    ret_consts = _ret_consts()
    att_consts = _att_bias()
    return pl.pallas_call(
        functools.partial(_mixer_kernel, layer=layer),
        grid=(length // TC,),
        in_specs=[pl.BlockSpec(memory_space=pltpu.SMEM), chunk(D_MODEL), _layer_spec((D_MODEL, IN_WIDTH), layer)]
                 + _s5_weight_specs(layer)
                 + [_const_spec(c.shape) for c in ret_consts]
                 + [_layer_spec((1, RET_WIDTH), layer), _layer_spec((1, RET_WIDTH), layer)]
                 + [_const_spec(c.shape) for c in att_consts],
        out_specs=[chunk(SSM_WIDTH), chunk(RET_WIDTH), chunk(ATT_WIDTH),
                   fixed(nb, 2 * SSM_LANES), fixed(nb, RET_HEADS, RET_D, RET_D), fixed(nb, TC, LANES),
                   fixed(nb, TC, LANES)],
        out_shape=[jax.ShapeDtypeStruct((nb, length, SSM_WIDTH), BF16),
                   jax.ShapeDtypeStruct((nb, length, RET_WIDTH), BF16),
                   jax.ShapeDtypeStruct((nb, length, ATT_WIDTH), BF16),
                   jax.ShapeDtypeStruct((nb, 2 * SSM_LANES), F32),
                   jax.ShapeDtypeStruct((nb, RET_HEADS, RET_D, RET_D), F32),
                   jax.ShapeDtypeStruct((nb, TC, LANES), F32),
                   jax.ShapeDtypeStruct((nb, TC, LANES), F32)],
        scratch_shapes=[pltpu.VMEM((nb, TC, RET_COLS), F32), pltpu.VMEM((nb, TC, ATT_COLS), F32),
                        pltpu.VMEM((2 * N_TILES, nb * TC, LANES), F32), pltpu.VMEM((nb, 2 * SSM_LANES), F32),
                        pltpu.VMEM((nb, 2, LANES, LANES), F32),
                        pltpu.VMEM((nb, 2 * TC, LANES), BF16), pltpu.VMEM((nb, 2 * TC, 2 * LANES), BF16),
                        pltpu.VMEM((ATT_Q_HEADS, TC, 2 * TC), F32)],
        compiler_params=_cparams(1),
        name="mixer",
    )(p["sinks"], x3, p["w_in"], *_s5_weights(p), *ret_consts, p["gn_g"], p["gn_b"], *att_consts)


def _mix_sample_kernel(qt_ref, kt_ref, v4_ref, g4_ref, s_ref, qb_ref, kn_ref, vn_ref, kc_ref, vc_ref,
                       gng_ref, gnb_ref, bias_ref, sink_ref, or_ref, oa_ref, sn_ref, kout_ref, vout_ref):
    gammas = _ret_gammas()
    sink = sink_ref[:, 0:1]
    bias = bias_ref[...]

    rows = range(SAMPLE_ROWS)
    qt, kt, v4, g4 = qt_ref[...], kt_ref[...], v4_ref[...], g4_ref[...]
    for h in range(RET_HEADS):
        sl = slice(RET_D * h, RET_D * (h + 1))
        sn = gammas[h] * s_ref[:, sl, :] + kt[:, :, h:h + 1] * v4[:, h:h + 1, :]
        sn_ref[:, sl, :] = sn
        o = jnp.sum(qt[:, :, h:h + 1] * sn, axis=1, keepdims=True)
        d = o - jnp.mean(o, -1, keepdims=True)
        var = jnp.mean(d * d, -1, keepdims=True)
        o = d * lax.rsqrt(var + LN_EPS) * gng_ref[h:h + 1, :] + gnb_ref[h:h + 1, :]
        gate = g4[:, h:h + 1, :]
        or_ref[:, h:h + 1, :] = gate * _sigmoid(gate) * o

    q_all = jnp.concatenate([qb_ref[i] for i in rows], axis=0)
    knew = jnp.concatenate([jnp.broadcast_to(kn_ref[i], (8, LANES)) for i in rows], axis=0)
    vnew = jnp.concatenate([jnp.broadcast_to(vn_ref[i], (8, LANES)) for i in rows], axis=0)
    sink_all = jnp.concatenate([sink] * SAMPLE_ROWS, axis=0)
    s = jnp.concatenate([_dot_nt(qb_ref[i].astype(BF16), kc_ref[i].astype(BF16)) + bias for i in rows], axis=0)
    s_self = jnp.sum(q_all * knew, -1, keepdims=True)
    m = jnp.maximum(jnp.maximum(jnp.max(s, -1, keepdims=True), s_self), sink_all)
    e = jnp.exp(s - m)
    e_self = jnp.exp(s_self - m)
    den = jnp.sum(e, -1, keepdims=True) + e_self + jnp.exp(sink_all - m)
    pv = jnp.concatenate([_dot(e[8 * i:8 * (i + 1)].astype(BF16), vc_ref[i].astype(BF16)) for i in rows], axis=0)
    o = (pv + e_self * vnew) / den
    for i in rows:
        oa_ref[i] = o[8 * i:8 * (i + 1)]

    kout_ref[:, 0:WINDOW - 1, :] = kc_ref[:, 1:WINDOW, :]
    kout_ref[:, WINDOW - 1:WINDOW, :] = kn_ref[...]
    vout_ref[:, 0:WINDOW - 1, :] = vc_ref[:, 1:WINDOW, :]
    vout_ref[:, WINDOW - 1:WINDOW, :] = vn_ref[...]


def _mix_sample(qt, kt, v4, g4, s_all, qb, kn, vn, kc_all, vc_all, p, bias8, layer):
    nb = qt.shape[0]
    bb = SAMPLE_ROWS
    blk = lambda *dims: pl.BlockSpec((bb,) + dims, lambda i: (i,) + (0,) * len(dims))
    lblk = lambda *dims: pl.BlockSpec((None, bb) + dims, lambda i: (layer, i) + (0,) * len(dims))
    in_specs = [blk(RET_D, RET_HEADS), blk(RET_D, RET_HEADS), blk(RET_HEADS, RET_D), blk(RET_HEADS, RET_D),
                lblk(RET_WIDTH, RET_D), blk(8, LANES), blk(1, LANES), blk(1, LANES),
                lblk(WINDOW, LANES), lblk(WINDOW, LANES),
                _layer_spec((RET_HEADS, RET_D), layer), _layer_spec((RET_HEADS, RET_D), layer),
                _const_spec((8, LANES)), _layer_spec((8, LANES), layer)]
    args = [qt, kt, v4, g4, s_all, qb, kn, vn, kc_all, vc_all, p["gn_g4"], p["gn_b4"], bias8, p["sink8"]]
    state_dims = ((RET_WIDTH, RET_D), (WINDOW, LANES), (WINDOW, LANES))
    return pl.pallas_call(
        _mix_sample_kernel,
        grid=(nb // bb,),
        in_specs=in_specs,
        out_specs=[blk(RET_HEADS, RET_D), blk(8, LANES)] + [blk(*dims) for dims in state_dims],
        out_shape=[jax.ShapeDtypeStruct((nb, RET_HEADS, RET_D), F32),
                   jax.ShapeDtypeStruct((nb, 8, LANES), F32)]
                  + [jax.ShapeDtypeStruct((nb,) + dims, F32) for dims in state_dims],
        compiler_params=_cparams(1),
        name="mix_sample",
    )(*args)


def _outffn_rows(z_ref, r_ref, a_ref, x_ref, wo_ref, g1_ref, b1_ref, w1_ref, w3_ref, w2_ref, g2_ref, b2_ref, o_ref):
    c1, c2 = SSM_WIDTH, SSM_WIDTH + RET_WIDTH
    tm = x_ref.shape[0]
    parts = FFN_PARTS if tm % (16 * FFN_PARTS) == 0 else 1
    rows = [slice(tm // parts * k, tm // parts * (k + 1)) for k in range(parts)]

    def mixed(sl):
        mix = _dot(z_ref[sl], wo_ref[0:c1]) + _dot(r_ref[sl], wo_ref[c1:c2]) + _dot(a_ref[sl], wo_ref[c2:D_MODEL])
        return _layernorm(ALPHA * x_ref[sl] + mix, g1_ref[...], b1_ref[...])

    def hidden(x):
        xb = x.astype(BF16)
        h1 = _dot(xb, w1_ref[...])
        h3 = _dot(xb, w3_ref[...])
        return (h1 * _sigmoid(h1) * h3).astype(BF16)

    def out(sl, x, hid):
        o_ref[sl] = _layernorm(ALPHA * x + _dot(hid, w2_ref[...]), g2_ref[...], b2_ref[...])

    xs, hs = {}, {}
    for k in range(parts + 2):
        if k < parts:
            xs[k] = mixed(rows[k])
        if 0 <= k - 1 < parts:
            hs[k - 1] = hidden(xs[k - 1])
        if 0 <= k - 2 < parts:
            out(rows[k - 2], xs.pop(k - 2), hs.pop(k - 2))


def _outffn_kernel(zp_ref, rp_ref, ap_ref, xp_ref, zs_ref, rs_ref, as_ref, xs_ref, *rest, prompt_steps):
    weights, (op_ref, os_ref) = rest[:-2], rest[-2:]
    i = pl.program_id(0)

    @pl.when(i < prompt_steps)
    def _():
        _outffn_rows(zp_ref, rp_ref, ap_ref, xp_ref, *weights, op_ref)

    @pl.when(i == prompt_steps)
    def _():
        _outffn_rows(zs_ref, rs_ref, as_ref, xs_ref, *weights, os_ref)


def _outffn(prompt, sample, p, layer):
    tp, ts = prompt[3].shape[0], sample[3].shape[0]
    tm = min(ROW_TILE, tp)
    steps = tp // tm
    tile = lambda c: pl.BlockSpec((tm, c), lambda i: (jnp.minimum(i, steps - 1), 0))
    whole = lambda c: pl.BlockSpec((ts, c), lambda i: (0, 0))
    widths = (SSM_WIDTH, RET_WIDTH, ATT_WIDTH, D_MODEL)
    vec = _layer_spec((1, D_MODEL), layer)
    return pl.pallas_call(
        functools.partial(_outffn_kernel, prompt_steps=steps),
        grid=(steps + 1,),
        in_specs=[tile(c) for c in widths] + [whole(c) for c in widths]
                 + [_layer_spec((D_MODEL, D_MODEL), layer), vec, vec,
                    _layer_spec((D_MODEL, FFN_HIDDEN), layer), _layer_spec((D_MODEL, FFN_HIDDEN), layer),
                    _layer_spec((FFN_HIDDEN, D_MODEL), layer), vec, vec],
        out_specs=[tile(D_MODEL), whole(D_MODEL)],
        out_shape=[jax.ShapeDtypeStruct((tp, D_MODEL), F32), jax.ShapeDtypeStruct((ts, D_MODEL), F32)],
        compiler_params=_cparams(1),
        name="outffn",
    )(*prompt, *sample, p["w_out"], p["ln1_g"], p["ln1_b"], p["w1"], p["w3"], p["w2"], p["ln2_g"], p["ln2_b"])


def _prep_params(w_in, lam_re, lam_im, log_step, b_re, b_im, c_re, c_im, ssm_d, glu_w, glu_b,
                 gn_g, gn_b, sinks, w_out, ln1_g, ln1_b, w1, w3, w2, ln2_g, ln2_b):
    col_scale = np.ones((IN_WIDTH,), np.float32)
    col_scale[COL_RET + RET_WIDTH:COL_RET + 2 * RET_WIDTH] = RET_D ** -0.5
    col_scale[COL_ATT:COL_ATT + ATT_WIDTH] = ATT_D ** -0.5

    step = jnp.exp(log_step)[..., None]
    mag = jnp.exp(lam_re * step)
    a_re, a_im = mag * jnp.cos(lam_im * step), mag * jnp.sin(lam_im * step)
    den = lam_re * lam_re + lam_im * lam_im
    k_re = ((a_re - 1.0) * lam_re + a_im * lam_im) / den
    k_im = (a_im * lam_re - (a_re - 1.0) * lam_im) / den
    bb_re = k_re[..., None] * b_re - k_im[..., None] * b_im
    bb_im = k_re[..., None] * b_im + k_im[..., None] * b_re
    eye8 = jnp.eye(8, dtype=F32)

    def b_blocks(m):
        m = m.reshape(DEPTH, 3, 8, SSM_STATE, SSM_CH)
        return jnp.einsum("ligph,gk->lighkp", m, eye8).reshape(DEPTH, 3, LANES, 512).astype(BF16)

    def c_blocks(m):
        m = m.reshape(DEPTH, 3, 8, SSM_CH, SSM_STATE)
        return jnp.einsum("lighp,gk->likpgh", m, eye8).reshape(DEPTH, 3, 512, LANES).astype(BF16)

    row = lambda a: a.reshape(DEPTH, 1, -1)
    sink8 = jnp.concatenate([sinks, jnp.zeros((DEPTH, 2), F32)], axis=1)
    return dict(
        w_in=(w_in * col_scale).astype(BF16),
        bre=b_blocks(bb_re), bim=b_blocks(bb_im), cre=c_blocks(c_re), cim=c_blocks(-c_im),
        ar=row(a_re), ai=row(a_im),
        ssm_d=row(ssm_d), glu_w=glu_w.astype(BF16), glu_b=row(glu_b),
        gn_g=row(gn_g), gn_b=row(gn_b),
        gn_g4=gn_g.reshape(DEPTH, RET_HEADS, RET_D), gn_b4=gn_b.reshape(DEPTH, RET_HEADS, RET_D),
        sinks=sinks, sink8=jnp.broadcast_to(sink8[:, :, None], (DEPTH, 8, LANES)),
        w_out=w_out.astype(BF16), ln1_g=row(ln1_g), ln1_b=row(ln1_b),
        w1=w1.astype(BF16), w3=w3.astype(BF16), w2=w2.astype(BF16), ln2_g=row(ln2_g), ln2_b=row(ln2_b),
    )


def _prompt_mix(x3, p, layer):
    nb, length, _ = x3.shape
    t = nb * length
    z, o_r, o_a, hlast, s4, klast, vlast = _mixer_prompt(x3, p, layer)
    rows = (z.reshape(t, SSM_WIDTH), o_r.reshape(t, RET_WIDTH), o_a.reshape(t, ATT_WIDTH), x3.reshape(t, D_MODEL))
    return rows, (hlast, s4, klast, vlast)


def _sample_bias8():
    slopes = _alibi_slopes(ATT_Q_HEADS)
    dist = WINDOW - np.arange(WINDOW)
    bias = np.zeros((8, WINDOW))
    for h in range(ATT_Q_HEADS):
        bias[h] = -slopes[h] * dist
    return jnp.asarray(bias, F32)


def _sample_mix(x2, p, layer, h0r, h0i, s_all, kc_all, vc_all):
    nb = x2.shape[0]
    pu, pr, pa = _inproj(x2, p, layer)
    z, h_re, h_im = _s5_sample(pu, h0r, h0i, p, layer)
    heads = lambda a: a.reshape(nb, RET_HEADS, RET_D)
    q4, k4, v4, g4 = (heads(pr[:, RET_WIDTH * i:RET_WIDTH * (i + 1)]) for i in range(4))
    q6 = pa[:, 0:ATT_WIDTH].reshape(nb, ATT_Q_HEADS, ATT_D)
    kv_of = np.arange(ATT_Q_HEADS) // ATT_GROUP
    qb = jnp.zeros((nb, 8, ATT_KV_HEADS, ATT_D), F32).at[:, np.arange(ATT_Q_HEADS), kv_of].set(q6)
    kn = pa[:, ATT_WIDTH:ATT_WIDTH + LANES].reshape(nb, 1, LANES)
    vn = pa[:, ATT_WIDTH + LANES:ATT_WIDTH + 2 * LANES].reshape(nb, 1, LANES)
    o_r, o8, s_new, k_out, v_out = _mix_sample(
        q4.transpose(0, 2, 1), k4.transpose(0, 2, 1), v4, g4, s_all, qb.reshape(nb, 8, LANES), kn, vn,
        kc_all, vc_all, p, _sample_bias8(), layer)
    o_r = o_r.reshape(nb, RET_WIDTH).astype(BF16)
    o_a = o8.reshape(nb, 8, ATT_KV_HEADS, ATT_D)[:, np.arange(ATT_Q_HEADS), kv_of].reshape(nb, ATT_WIDTH).astype(BF16)
    return (z, o_r, o_a, x2), (h_re, h_im, s_new, k_out, v_out)


def kernel(x_prompt, x_sample, state_ssm_re, state_ssm_im, state_ret, cache_win_k, cache_win_v, w_in, ssm_lambda_re, ssm_lambda_im, ssm_log_step, ssm_b_re, ssm_b_im, ssm_c_re, ssm_c_im, ssm_d, ssm_glu_w, ssm_glu_b, ret_gn_g, ret_gn_b, attn_sinks, w_out, ln1_g, ln1_b, ffn_w1, ffn_w3, ffn_w2, ln2_g, ln2_b):
    p = _prep_params(w_in, ssm_lambda_re, ssm_lambda_im, ssm_log_step, ssm_b_re, ssm_b_im, ssm_c_re, ssm_c_im,
                     ssm_d, ssm_glu_w, ssm_glu_b, ret_gn_g, ret_gn_b, attn_sinks, w_out, ln1_g, ln1_b,
                     ffn_w1, ffn_w3, ffn_w2, ln2_g, ln2_b)

    xp = x_prompt
    pb, length, _ = x_prompt.shape
    nb = x_sample.shape[0]
    xs = x_sample.reshape(nb, D_MODEL)
    h0r = state_ssm_re.transpose(0, 2, 3, 1).reshape(DEPTH, SSM_LANES, nb)
    h0i = state_ssm_im.transpose(0, 2, 3, 1).reshape(DEPTH, SSM_LANES, nb)
    s_all = state_ret.reshape(DEPTH, nb, RET_WIDTH, RET_D)
    kc_all = cache_win_k.reshape(DEPTH, nb, WINDOW, LANES)
    vc_all = cache_win_v.reshape(DEPTH, nb, WINDOW, LANES)
    p_out, s_out = [], []
    for layer in range(DEPTH):
        prompt_rows, states = _prompt_mix(xp, p, layer)
        p_out.append(states)
        sample_rows, states = _sample_mix(xs, p, layer, h0r, h0i, s_all, kc_all, vc_all)
        s_out.append(states)
        yp, xs = _outffn(prompt_rows, sample_rows, p, layer)
        xp = yp.reshape(pb, length, D_MODEL)
    stack = lambda outs, i: jnp.stack([o[i] for o in outs])
    p_h = stack(p_out, 0)
    p_kv_shape = (DEPTH, pb, TC, ATT_KV_HEADS, ATT_D)
    ssm_out = lambda i: stack(s_out, i).reshape(DEPTH, SSM_GROUPS, SSM_STATE, nb).transpose(0, 3, 1, 2)
    s_kv_shape = (DEPTH, nb, WINDOW, ATT_KV_HEADS, ATT_D)

    return (xp, xs.reshape(nb, 1, D_MODEL),
            p_h[:, :, :SSM_LANES].reshape(DEPTH, pb, SSM_GROUPS, SSM_STATE),
            p_h[:, :, SSM_LANES:].reshape(DEPTH, pb, SSM_GROUPS, SSM_STATE),
            stack(p_out, 1), stack(p_out, 2).reshape(p_kv_shape), stack(p_out, 3).reshape(p_kv_shape),
            ssm_out(0), ssm_out(1),
            stack(s_out, 2).reshape(DEPTH, nb, RET_HEADS, RET_D, RET_D),
            stack(s_out, 3).reshape(s_kv_shape), stack(s_out, 4).reshape(s_kv_shape))
```

```python
import functools
import math

import numpy as np
import jax
import jax.numpy as jnp
from jax import lax
from jax.experimental import pallas as pl
from jax.experimental.pallas import tpu as pltpu

F32 = jnp.float32
BF16 = jnp.bfloat16

D_MODEL = 1024
DEPTH = 2
SSM_GROUPS = 24
SSM_CH = 16
SSM_STATE = 64
SSM_WIDTH = SSM_GROUPS * SSM_CH
SSM_LANES = SSM_GROUPS * SSM_STATE
RET_HEADS = 4
RET_D = 64
RET_WIDTH = RET_HEADS * RET_D
ATT_Q_HEADS = 6
ATT_KV_HEADS = 2
ATT_GROUP = ATT_Q_HEADS // ATT_KV_HEADS
ATT_D = 64
ATT_WIDTH = ATT_Q_HEADS * ATT_D
WINDOW = 128
FFN_HIDDEN = 2816
IN_WIDTH = 2048
ALPHA = (2 * DEPTH) ** 0.25
LN_EPS = 1e-5
NEG_INF = -1e30

COL_RET = SSM_WIDTH
COL_ATT = SSM_WIDTH + 4 * RET_WIDTH
RET_COLS = 4 * RET_WIDTH
ATT_COLS = ATT_WIDTH + 2 * ATT_KV_HEADS * ATT_D

TC = 128
LANES = 128
N_TILES = SSM_LANES // LANES
HALF = 64
MXU_COLS = 256
SCAN_PIECES = 4
ROW_TILE = 512
FFN_PARTS = 2
SAMPLE_ROWS = 16
VMEM_LIMIT = 56 * 2 ** 20


def _alibi_slopes(n):
    def pow2(k):
        start = 2.0 ** (-8.0 / k)
        return [start ** (i + 1) for i in range(k)]
    if n & (n - 1) == 0:
        return pow2(n)
    c = 2 ** int(math.floor(math.log2(n)))
    return pow2(c) + pow2(2 * c)[0::2][: n - c]


def _cparams(n_axes):
    return pltpu.CompilerParams(dimension_semantics=("arbitrary",) * n_axes, vmem_limit_bytes=VMEM_LIMIT)


def _const_spec(shape):
    nd = len(shape)
    return pl.BlockSpec(shape, lambda *_: (0,) * nd, pipeline_mode=pl.Buffered(1))


def _layer_spec(shape, layer):
    nd = len(shape)
    return pl.BlockSpec((None,) + tuple(shape), lambda *_: (layer,) + (0,) * nd, pipeline_mode=pl.Buffered(1))


def _dot(a, b):
    return jnp.dot(a, b, preferred_element_type=F32)


def _dot_nt(a, b):
    return lax.dot_general(a, b, (((1,), (1,)), ((), ())), preferred_element_type=F32)


def _dot_tn(a, b):
    return lax.dot_general(a, b, (((0,), (0,)), ((), ())), preferred_element_type=F32)


def _layernorm(v, g, b):
    mu = jnp.mean(v, -1, keepdims=True)
    d = v - mu
    var = jnp.mean(d * d, -1, keepdims=True)
    return d * lax.rsqrt(var + LN_EPS) * g + b


def _gelu_tanh(x):
    return 0.5 * x * (1.0 + jnp.tanh(math.sqrt(2.0 / math.pi) * (x + 0.044715 * (x * x * x))))


def _sigmoid(x):
    return 1.0 / (1.0 + jnp.exp(-x))


def _inproj_kernel(x_ref, w_ref, pu_ref, pr_ref, pa_ref):
    xb = x_ref[...].astype(BF16)
    pu_ref[...] = _dot(xb, w_ref[:, 0:COL_RET])
    pr_ref[...] = _dot(xb, w_ref[:, COL_RET:COL_ATT])
    pa_ref[...] = _dot(xb, w_ref[:, COL_ATT:IN_WIDTH])


def _inproj(x2, p, layer):
    t = x2.shape[0]
    tm = min(ROW_TILE, t)
    row = lambda c: pl.BlockSpec((tm, c), lambda i: (i, 0))
    return pl.pallas_call(
        _inproj_kernel,
        grid=(t // tm,),
        in_specs=[row(D_MODEL), _layer_spec((D_MODEL, IN_WIDTH), layer)],
        out_specs=[row(SSM_WIDTH), row(RET_COLS), row(ATT_COLS)],
        out_shape=[jax.ShapeDtypeStruct((t, SSM_WIDTH), F32),
                   jax.ShapeDtypeStruct((t, RET_COLS), F32),
                   jax.ShapeDtypeStruct((t, ATT_COLS), F32)],
        compiler_params=_cparams(1),
        name="inproj",
    )(x2, p["w_in"])


def _s5_bu(ub, bre_ref, bim_ref):
    res, ims = [], []
    for i in range(3):
        ui = ub[:, LANES * i:LANES * (i + 1)]
        res.append(_dot(ui, bre_ref[i]))
        ims.append(_dot(ui, bim_ref[i]))
    return res, ims


def _s5_out(u, h_re, h_im, cre_ref, cim_ref, d_ref, gw_ref, gb_ref):
    ys = []
    for i in range(3):
        ys.append(_dot(h_re[i].astype(BF16), cre_ref[i]) + _dot(h_im[i].astype(BF16), cim_ref[i]))
    y = jnp.concatenate(ys, axis=1) + d_ref[...] * u
    z = _gelu_tanh(y)
    gate = _sigmoid(_dot(z.astype(BF16), gw_ref[...]) + gb_ref[...])
    return z * gate


def _s5_prompt_body(n, u3, bre_ref, bim_ref, cre_ref, cim_ref, ar_ref, ai_ref, d_ref, gw_ref, gb_ref,
                    z_ref, hlast_ref, h_scr, hc_scr, fillers=()):
    nb = u3.shape[0]

    u = pltpu.einshape("btc->(tb)c", u3)

    res, ims = _s5_bu(u.astype(BF16), bre_ref, bim_ref)
    for i in range(3):
        for k in range(4):
            h_scr[4 * i + k] = res[i][:, LANES * k:LANES * (k + 1)]
            h_scr[N_TILES + 4 * i + k] = ims[i][:, LANES * k:LANES * (k + 1)]

    def step(t, carry):
        r0 = t * nb
        out = []
        for c in range(N_TILES):
            sl = slice(LANES * c, LANES * (c + 1))
            ar, ai = ar_ref[:, sl], ai_ref[:, sl]
            hr, hi = carry[c], carry[N_TILES + c]
            nr = ar * hr - ai * hi + h_scr[c, pl.ds(r0, nb), :]
            ni = ar * hi + ai * hr + h_scr[N_TILES + c, pl.ds(r0, nb), :]
            h_scr[c, pl.ds(r0, nb), :] = nr
            h_scr[N_TILES + c, pl.ds(r0, nb), :] = ni
            out.append((nr, ni))
        return tuple(o[0] for o in out) + tuple(o[1] for o in out)

    fin = tuple(hc_scr[:, LANES * c:LANES * (c + 1)] for c in range(2 * N_TILES))
    groups = len(fillers) + 1
    for g in range(groups):
        for t in range(TC * g // groups, TC * (g + 1) // groups):
            fin = step(t, fin)
        if g < len(fillers):
            fillers[g]()
    for c in range(2 * N_TILES):
        hc_scr[:, LANES * c:LANES * (c + 1)] = fin[c]
        hlast_ref[:, LANES * c:LANES * (c + 1)] = fin[c]

    ys = []

    def out_piece(i):
        def emit():
            h_re = jnp.concatenate([h_scr[4 * i + k] for k in range(4)], axis=1).astype(BF16)
            h_im = jnp.concatenate([h_scr[N_TILES + 4 * i + k] for k in range(4)], axis=1).astype(BF16)
            ys.append(_dot(h_re, cre_ref[i]) + _dot(h_im, cim_ref[i]))
        return emit

    def finish(fillers=()):
        fillers = list(fillers)
        y = jnp.concatenate(ys, axis=1) + d_ref[...] * u
        if fillers:
            fillers.pop(0)()
        z = _gelu_tanh(y)
        if fillers:
            fillers.pop(0)()
        gate = _sigmoid(_dot(z.astype(BF16), gw_ref[...]) + gb_ref[...])
        while fillers:
            fillers.pop(0)()
        z_ref[...] = pltpu.einshape("(tb)c->btc", z * gate, b=nb).astype(BF16)

    return [out_piece(i) for i in range(3)] + [finish]


def _s5_weight_specs(layer):
    ls = lambda *shape: _layer_spec(shape, layer)
    return [ls(3, LANES, 512), ls(3, LANES, 512), ls(3, 512, LANES), ls(3, 512, LANES),
            ls(1, SSM_LANES), ls(1, SSM_LANES), ls(1, SSM_WIDTH), ls(SSM_WIDTH, SSM_WIDTH), ls(1, SSM_WIDTH)]


def _s5_weights(p):
    return (p["bre"], p["bim"], p["cre"], p["cim"], p["ar"], p["ai"], p["ssm_d"], p["glu_w"], p["glu_b"])


def _s5_sample_kernel(u_ref, h0r_ref, h0i_ref, bre_ref, bim_ref, cre_ref, cim_ref, ar_ref, ai_ref, d_ref,
                      gw_ref, gb_ref, z_ref, hr_ref, hi_ref):
    u = u_ref[...]
    res, ims = _s5_bu(u.astype(BF16), bre_ref, bim_ref)
    h_re, h_im = [], []
    for i in range(3):
        sl = slice(512 * i, 512 * (i + 1))
        ar, ai = ar_ref[:, sl], ai_ref[:, sl]
        h0r, h0i = h0r_ref[:, sl], h0i_ref[:, sl]
        nr = ar * h0r - ai * h0i + res[i]
        ni = ar * h0i + ai * h0r + ims[i]
        hr_ref[:, sl] = nr
        hi_ref[:, sl] = ni
        h_re.append(nr)
        h_im.append(ni)
    z_ref[...] = _s5_out(u, h_re, h_im, cre_ref, cim_ref, d_ref, gw_ref, gb_ref).astype(BF16)


def _s5_sample(u, h0r, h0i, p, layer):
    nb = u.shape[0]
    full = lambda c: pl.BlockSpec((nb, c), lambda i: (0, 0))
    state = pl.BlockSpec((None, nb, SSM_LANES), lambda i: (layer, 0, 0))
    return pl.pallas_call(
        _s5_sample_kernel,
        grid=(1,),
        in_specs=[full(SSM_WIDTH), state, state] + _s5_weight_specs(layer),
        out_specs=[full(SSM_WIDTH), full(SSM_LANES), full(SSM_LANES)],
        out_shape=[jax.ShapeDtypeStruct((nb, SSM_WIDTH), BF16),
                   jax.ShapeDtypeStruct((nb, SSM_LANES), F32),
                   jax.ShapeDtypeStruct((nb, SSM_LANES), F32)],
        compiler_params=_cparams(1),
        name="s5_sample",
    )(u, h0r, h0i, *_s5_weights(p))


def _ret_gammas():
    return [1.0 - 2.0 ** (-5.0 - h) for h in range(RET_HEADS)]


def _ret_consts():
    log_g = np.log1p(-(2.0 ** (-5.0 - np.arange(RET_HEADS, dtype=np.float64))))
    idx = np.arange(TC)
    diff = idx[:, None] - idx[None, :]
    dmask = np.where(diff >= 0, np.exp(log_g[:, None, None] * np.maximum(diff, 0)), 0.0)
    inner = np.exp(log_g[:, None] * (idx + 1))
    kvd = np.exp(log_g[:, None] * (TC - 1 - idx))
    cd = np.exp(log_g * TC)
    indec = np.zeros((2, TC, LANES))
    kvdec = np.zeros((2, TC, LANES))
    cdec = np.zeros((2, LANES, LANES))
    blk = np.zeros((LANES, LANES))
    for j in range(2):
        for hh in range(2):
            sl = slice(HALF * hh, HALF * (hh + 1))
            indec[j, :, sl] = inner[2 * j + hh][:, None]
            kvdec[j, :, sl] = kvd[2 * j + hh][:, None]
            cdec[j, sl, sl] = cd[2 * j + hh]
            blk[sl, sl] = 1.0
    as32 = lambda a: jnp.asarray(a, F32)
    return as32(dmask), as32(kvdec), as32(indec), as32(cdec), as32(blk)


def _halfnorm(o, lo):
    zero = jnp.zeros_like(o)
    s_lo = jnp.sum(jnp.where(lo, o, zero), -1, keepdims=True)
    s_hi = jnp.sum(jnp.where(lo, zero, o), -1, keepdims=True)
    d = o - jnp.where(lo, s_lo, s_hi) * (1.0 / HALF)
    d2 = d * d
    v_lo = jnp.sum(jnp.where(lo, d2, zero), -1, keepdims=True)
    v_hi = jnp.sum(jnp.where(lo, zero, d2), -1, keepdims=True)
    var = jnp.where(lo, v_lo, v_hi) * (1.0 / HALF)
    return d * lax.rsqrt(var + LN_EPS)


def _ret_prompt_body(n, pr_ref, dmask_ref, kvdec_ref, indec_ref, cdec_ref, blk_ref, gng_ref, gnb_ref,
                     o_ref, slast_ref, s_scr):
    nb = pr_ref.shape[0]

    lo = lax.broadcasted_iota(jnp.int32, (TC, LANES), 1) < HALF
    zero = jnp.zeros((TC, LANES), F32)

    def row(b):
        for j in range(2):
            q2 = pr_ref[b, :, LANES * j:LANES * (j + 1)]
            k2 = pr_ref[b, :, 256 + LANES * j:256 + LANES * (j + 1)]
            v2 = pr_ref[b, :, 512 + LANES * j:512 + LANES * (j + 1)]
            g2 = pr_ref[b, :, 768 + LANES * j:768 + LANES * (j + 1)]
            k2b = k2.astype(BF16)
            v2b = v2.astype(BF16)
            state = s_scr[b, j]
            qs = jnp.concatenate([jnp.where(lo, q2, zero), jnp.where(lo, zero, q2)], axis=0).astype(BF16)
            sc = _dot_nt(qs, k2b)
            qd = (q2 * indec_ref[j]).astype(BF16)
            lhs = jnp.concatenate(
                [jnp.concatenate([(sc[hh * TC:(hh + 1) * TC] * dmask_ref[2 * j + hh]).astype(BF16), qd], axis=1)
                 for hh in range(2)], axis=0)
            o2 = _dot(lhs, jnp.concatenate([v2b, state.astype(BF16)], axis=0))
            o = jnp.where(lo, o2[0:TC], o2[TC:2 * TC])
            kd = (k2 * kvdec_ref[j]).astype(BF16)
            s_new = cdec_ref[j] * state + blk_ref[...] * _dot_tn(kd, v2b)
            s_scr[b, j] = s_new
            slast_ref[b, j] = s_new
            sl = slice(LANES * j, LANES * (j + 1))
            o = _halfnorm(o, lo) * gng_ref[:, sl] + gnb_ref[:, sl]
            o_ref[b, :, sl] = (g2 * _sigmoid(g2) * o).astype(BF16)

    return [functools.partial(row, b) for b in range(nb)]


def _att_bias():
    slopes = _alibi_slopes(ATT_Q_HEADS)
    qpos = np.arange(TC)[:, None] + TC
    kpos = np.arange(2 * TC)[None, :]
    dist = qpos - kpos
    valid = (dist >= 0) & (dist <= WINDOW)
    prev_cur = np.stack([np.where(valid, -slopes[h] * dist, NEG_INF) for h in range(ATT_Q_HEADS)])
    cur_prev = np.concatenate([prev_cur[:, :, TC:], prev_cur[:, :, :TC]], axis=2)
    first = np.where(np.arange(2 * TC) >= TC, NEG_INF, 0.0)[None, :]
    return jnp.asarray(cur_prev, F32), jnp.asarray(prev_cur, F32), jnp.asarray(first, F32)


def _att_prompt_body(n, layer, sink_ref, pa_ref, bias_cp_ref, bias_pc_ref, first_ref, o_ref, kout_ref, vout_ref,
                     k_scr, v_scr, bias_scr, fillers=()):
    nb = pa_ref.shape[0]
    fillers = list(fillers)

    even = n % 2 == 0
    row0 = pl.multiple_of((n % 2) * TC, TC)
    first = jnp.where(n == 0, first_ref[...], jnp.zeros_like(first_ref))
    for head in range(ATT_Q_HEADS):
        bias_scr[head] = jnp.where(even, bias_cp_ref[head], bias_pc_ref[head]) + first

    lo = lax.broadcasted_iota(jnp.int32, (TC, LANES), 1) < HALF
    zero = jnp.zeros((TC, LANES), F32)
    swap = lambda a: pltpu.roll(a, HALF, 1)
    moved = [head % 2 != head // ATT_GROUP for head in range(ATT_Q_HEADS)]

    def scores(b):
        kcur = pa_ref[b, :, ATT_WIDTH:ATT_WIDTH + LANES]
        vcur = pa_ref[b, :, ATT_WIDTH + LANES:ATT_WIDTH + 2 * LANES]
        kout_ref[b] = kcur
        vout_ref[b] = vcur
        k_scr[b, pl.ds(row0, TC), :] = kcur.astype(BF16)
        v_scr[b, pl.ds(row0, TC), 0:LANES] = vcur.astype(BF16)
        rows = []
        for head in range(ATT_Q_HEADS):
            qcol = pa_ref[b, :, LANES * (head // 2):LANES * (head // 2 + 1)]
            if moved[head]:
                qcol = swap(qcol)
            rows.append(jnp.where(lo, qcol, zero) if head // ATT_GROUP == 0 else jnp.where(lo, zero, qcol))
        qs = jnp.concatenate(rows, axis=0).astype(BF16)
        return _dot_nt(qs, k_scr[b])

    def softmax(s_all):
        es, ms = [], []
        for head in range(ATT_Q_HEADS):
            s = s_all[head * TC:(head + 1) * TC] + bias_scr[head]
            m = jnp.maximum(jnp.max(s, -1, keepdims=True), sink_ref[layer, head])
            es.append(jnp.exp(s - m).astype(BF16))
            ms.append(m)
        return jnp.concatenate(es, axis=0), ms

    def values(b, e_all, ms):
        r_all = _dot(e_all, v_scr[b])
        outs = []
        for head in range(ATT_Q_HEADS):
            r = r_all[head * TC:(head + 1) * TC]
            den = r[:, LANES:2 * LANES] + jnp.exp(sink_ref[layer, head] - ms[head])
            out = r[:, 0:LANES] / den
            outs.append(swap(out) if moved[head] else out)
        for c in range(3):
            o_ref[b, :, LANES * c:LANES * (c + 1)] = jnp.where(lo, outs[2 * c], outs[2 * c + 1]).astype(BF16)

    s_all, soft = {}, {}
    for k in range(nb + 2):
        if fillers:
            fillers.pop(0)()
        if k < nb:
            s_all[k] = scores(k)
        if 0 <= k - 1 < nb:
            soft[k - 1] = softmax(s_all.pop(k - 1))
        if 0 <= k - 2 < nb:
            values(k - 2, *soft.pop(k - 2))
    while fillers:
        fillers.pop(0)()


def _mixer_kernel(sink_ref, x_ref, w_ref,
                  bre_ref, bim_ref, cre_ref, cim_ref, ar_ref, ai_ref, d_ref, gw_ref, gb_ref,
                  dmask_ref, kvdec_ref, indec_ref, cdec_ref, blk_ref, gng_ref, gnb_ref,
                  bias_cp_ref, bias_pc_ref, first_ref,
                  z_ref, or_ref, oa_ref, hlast_ref, slast_ref, kout_ref, vout_ref,
                  pr_scr, pa_scr, h_scr, hc_scr, s_scr, k_scr, v_scr, bias_scr, *, layer):
    n = pl.program_id(0)
    nb = x_ref.shape[0]

    @pl.when(n == 0)
    def _():
        hc_scr[...] = jnp.zeros_like(hc_scr)
        s_scr[...] = jnp.zeros_like(s_scr)
        k_scr[...] = jnp.zeros_like(k_scr)
        v_scr[:, :, 0:LANES] = jnp.zeros((nb, 2 * TC, LANES), BF16)
        v_scr[:, :, LANES:2 * LANES] = jnp.ones((nb, 2 * TC, LANES), BF16)

    xb = x_ref[...].reshape(nb * TC, D_MODEL).astype(BF16)
    u3 = _dot(xb, w_ref[:, 0:COL_RET]).reshape(nb, TC, SSM_WIDTH)

    def project(dst, col0, c0, c1):
        def emit():
            dst[:, :, c0:c1] = _dot(xb, w_ref[:, col0 + c0:col0 + c1]).reshape(nb, TC, c1 - c0)
        return emit

    pieces = [project(pr_scr, COL_RET, c, c + MXU_COLS) for c in range(0, RET_COLS, MXU_COLS)]
    pieces += [project(pa_scr, COL_ATT, c, min(c + MXU_COLS, ATT_COLS)) for c in range(0, ATT_COLS, MXU_COLS)]
    *s5_maps, s5_finish = _s5_prompt_body(n, u3, bre_ref, bim_ref, cre_ref, cim_ref, ar_ref, ai_ref, d_ref, gw_ref,
                                          gb_ref, z_ref, hlast_ref, h_scr, hc_scr, fillers=pieces[:SCAN_PIECES])
    for emit in s5_maps:
        emit()
    s5_finish(pieces[SCAN_PIECES:])
    ret_rows = _ret_prompt_body(n, pr_scr, dmask_ref, kvdec_ref, indec_ref, cdec_ref, blk_ref, gng_ref, gnb_ref,
                                or_ref, slast_ref, s_scr)
    _att_prompt_body(n, layer, sink_ref, pa_scr, bias_cp_ref, bias_pc_ref, first_ref, oa_ref, kout_ref, vout_ref,
                     k_scr, v_scr, bias_scr, fillers=ret_rows)


def _mixer_prompt(x3, p, layer):
    nb, length, _ = x3.shape
    chunk = lambda c: pl.BlockSpec((nb, TC, c), lambda n: (0, n, 0))
    fixed = lambda *shape: pl.BlockSpec(shape, lambda n: (0,) * len(shape))
    ret_consts = _ret_consts()
    att_consts = _att_bias()
    return pl.pallas_call(
        functools.partial(_mixer_kernel, layer=layer),
        grid=(length // TC,),
        in_specs=[pl.BlockSpec(memory_space=pltpu.SMEM), chunk(D_MODEL), _layer_spec((D_MODEL, IN_WIDTH), layer)]
                 + _s5_weight_specs(layer)
                 + [_const_spec(c.shape) for c in ret_consts]
                 + [_layer_spec((1, RET_WIDTH), layer), _layer_spec((1, RET_WIDTH), layer)]
                 + [_const_spec(c.shape) for c in att_consts],
        out_specs=[chunk(SSM_WIDTH), chunk(RET_WIDTH), chunk(ATT_WIDTH),
                   fixed(nb, 2 * SSM_LANES), fixed(nb, 2, LANES, LANES), fixed(nb, TC, LANES), fixed(nb, TC, LANES)],
        out_shape=[jax.ShapeDtypeStruct((nb, length, SSM_WIDTH), BF16),
                   jax.ShapeDtypeStruct((nb, length, RET_WIDTH), BF16),
                   jax.ShapeDtypeStruct((nb, length, ATT_WIDTH), BF16),
                   jax.ShapeDtypeStruct((nb, 2 * SSM_LANES), F32),
                   jax.ShapeDtypeStruct((nb, 2, LANES, LANES), F32),
                   jax.ShapeDtypeStruct((nb, TC, LANES), F32),
                   jax.ShapeDtypeStruct((nb, TC, LANES), F32)],
        scratch_shapes=[pltpu.VMEM((nb, TC, RET_COLS), F32), pltpu.VMEM((nb, TC, ATT_COLS), F32),
                        pltpu.VMEM((2 * N_TILES, nb * TC, LANES), F32), pltpu.VMEM((nb, 2 * SSM_LANES), F32),
                        pltpu.VMEM((nb, 2, LANES, LANES), F32),
                        pltpu.VMEM((nb, 2 * TC, LANES), BF16), pltpu.VMEM((nb, 2 * TC, 2 * LANES), BF16),
                        pltpu.VMEM((ATT_Q_HEADS, TC, 2 * TC), F32)],
        compiler_params=_cparams(1),
        name="mixer",
    )(p["sinks"], x3, p["w_in"], *_s5_weights(p), *ret_consts, p["gn_g"], p["gn_b"], *att_consts)


def _mix_sample_kernel(qt_ref, kt_ref, v4_ref, g4_ref, s_ref, qb_ref, kn_ref, vn_ref, kc_ref, vc_ref,
                       gng_ref, gnb_ref, bias_ref, sink_ref, or_ref, oa_ref, sn_ref, kout_ref, vout_ref):
    gammas = _ret_gammas()
    sink = sink_ref[:, 0:1]
    bias = bias_ref[...]

    rows = range(SAMPLE_ROWS)
    qt, kt, v4, g4 = qt_ref[...], kt_ref[...], v4_ref[...], g4_ref[...]
    for h in range(RET_HEADS):
        sl = slice(RET_D * h, RET_D * (h + 1))
        sn = gammas[h] * s_ref[:, sl, :] + kt[:, :, h:h + 1] * v4[:, h:h + 1, :]
        sn_ref[:, sl, :] = sn
        o = jnp.sum(qt[:, :, h:h + 1] * sn, axis=1, keepdims=True)
        d = o - jnp.mean(o, -1, keepdims=True)
        var = jnp.mean(d * d, -1, keepdims=True)
        o = d * lax.rsqrt(var + LN_EPS) * gng_ref[h:h + 1, :] + gnb_ref[h:h + 1, :]
        gate = g4[:, h:h + 1, :]
        or_ref[:, h:h + 1, :] = gate * _sigmoid(gate) * o

    q_all = jnp.concatenate([qb_ref[i] for i in rows], axis=0)
    knew = jnp.concatenate([jnp.broadcast_to(kn_ref[i], (8, LANES)) for i in rows], axis=0)
    vnew = jnp.concatenate([jnp.broadcast_to(vn_ref[i], (8, LANES)) for i in rows], axis=0)
    sink_all = jnp.concatenate([sink] * SAMPLE_ROWS, axis=0)
    s = jnp.concatenate([_dot_nt(qb_ref[i].astype(BF16), kc_ref[i].astype(BF16)) + bias for i in rows], axis=0)
    s_self = jnp.sum(q_all * knew, -1, keepdims=True)
    m = jnp.maximum(jnp.maximum(jnp.max(s, -1, keepdims=True), s_self), sink_all)
    e = jnp.exp(s - m)
    e_self = jnp.exp(s_self - m)
    den = jnp.sum(e, -1, keepdims=True) + e_self + jnp.exp(sink_all - m)
    pv = jnp.concatenate([_dot(e[8 * i:8 * (i + 1)].astype(BF16), vc_ref[i].astype(BF16)) for i in rows], axis=0)
    o = (pv + e_self * vnew) / den
    for i in rows:
        oa_ref[i] = o[8 * i:8 * (i + 1)]

    kout_ref[:, 0:WINDOW - 1, :] = kc_ref[:, 1:WINDOW, :]
    kout_ref[:, WINDOW - 1:WINDOW, :] = kn_ref[...]
    vout_ref[:, 0:WINDOW - 1, :] = vc_ref[:, 1:WINDOW, :]
    vout_ref[:, WINDOW - 1:WINDOW, :] = vn_ref[...]


def _mix_sample(qt, kt, v4, g4, s_all, qb, kn, vn, kc_all, vc_all, p, bias8, layer):
    nb = qt.shape[0]
    bb = SAMPLE_ROWS
    blk = lambda *dims: pl.BlockSpec((bb,) + dims, lambda i: (i,) + (0,) * len(dims))
    lblk = lambda *dims: pl.BlockSpec((None, bb) + dims, lambda i: (layer, i) + (0,) * len(dims))
    in_specs = [blk(RET_D, RET_HEADS), blk(RET_D, RET_HEADS), blk(RET_HEADS, RET_D), blk(RET_HEADS, RET_D),
                lblk(RET_WIDTH, RET_D), blk(8, LANES), blk(1, LANES), blk(1, LANES),
                lblk(WINDOW, LANES), lblk(WINDOW, LANES),
                _layer_spec((RET_HEADS, RET_D), layer), _layer_spec((RET_HEADS, RET_D), layer),
                _const_spec((8, LANES)), _layer_spec((8, LANES), layer)]
    args = [qt, kt, v4, g4, s_all, qb, kn, vn, kc_all, vc_all, p["gn_g4"], p["gn_b4"], bias8, p["sink8"]]
    state_dims = ((RET_WIDTH, RET_D), (WINDOW, LANES), (WINDOW, LANES))
    return pl.pallas_call(
        _mix_sample_kernel,
        grid=(nb // bb,),
        in_specs=in_specs,
        out_specs=[blk(RET_HEADS, RET_D), blk(8, LANES)] + [blk(*dims) for dims in state_dims],
        out_shape=[jax.ShapeDtypeStruct((nb, RET_HEADS, RET_D), F32),
                   jax.ShapeDtypeStruct((nb, 8, LANES), F32)]
                  + [jax.ShapeDtypeStruct((nb,) + dims, F32) for dims in state_dims],
        compiler_params=_cparams(1),
        name="mix_sample",
    )(*args)


def _outffn_rows(z_ref, r_ref, a_ref, x_ref, wo_ref, g1_ref, b1_ref, w1_ref, w3_ref, w2_ref, g2_ref, b2_ref, o_ref):
    c1, c2 = SSM_WIDTH, SSM_WIDTH + RET_WIDTH
    tm = x_ref.shape[0]
    parts = FFN_PARTS if tm % (16 * FFN_PARTS) == 0 else 1
    rows = [slice(tm // parts * k, tm // parts * (k + 1)) for k in range(parts)]

    def mixed(sl):
        mix = _dot(z_ref[sl], wo_ref[0:c1]) + _dot(r_ref[sl], wo_ref[c1:c2]) + _dot(a_ref[sl], wo_ref[c2:D_MODEL])
        return _layernorm(ALPHA * x_ref[sl] + mix, g1_ref[...], b1_ref[...])

    def hidden(x):
        xb = x.astype(BF16)
        h1 = _dot(xb, w1_ref[...])
        h3 = _dot(xb, w3_ref[...])
        return (h1 * _sigmoid(h1) * h3).astype(BF16)

    def out(sl, x, hid):
        o_ref[sl] = _layernorm(ALPHA * x + _dot(hid, w2_ref[...]), g2_ref[...], b2_ref[...])

    xs, hs = {}, {}
    for k in range(parts + 2):
        if k < parts:
            xs[k] = mixed(rows[k])
        if 0 <= k - 1 < parts:
            hs[k - 1] = hidden(xs[k - 1])
        if 0 <= k - 2 < parts:
            out(rows[k - 2], xs.pop(k - 2), hs.pop(k - 2))


def _outffn_kernel(zp_ref, rp_ref, ap_ref, xp_ref, zs_ref, rs_ref, as_ref, xs_ref, *rest, prompt_steps):
    weights, (op_ref, os_ref) = rest[:-2], rest[-2:]
    i = pl.program_id(0)

    @pl.when(i < prompt_steps)
    def _():
        _outffn_rows(zp_ref, rp_ref, ap_ref, xp_ref, *weights, op_ref)

    @pl.when(i == prompt_steps)
    def _():
        _outffn_rows(zs_ref, rs_ref, as_ref, xs_ref, *weights, os_ref)


def _outffn(prompt, sample, p, layer):
    tp, ts = prompt[3].shape[0], sample[3].shape[0]
    tm = min(ROW_TILE, tp)
    steps = tp // tm
    tile = lambda c: pl.BlockSpec((tm, c), lambda i: (jnp.minimum(i, steps - 1), 0))
    whole = lambda c: pl.BlockSpec((ts, c), lambda i: (0, 0))
    widths = (SSM_WIDTH, RET_WIDTH, ATT_WIDTH, D_MODEL)
    vec = _layer_spec((1, D_MODEL), layer)
    return pl.pallas_call(
        functools.partial(_outffn_kernel, prompt_steps=steps),
        grid=(steps + 1,),
        in_specs=[tile(c) for c in widths] + [whole(c) for c in widths]
                 + [_layer_spec((D_MODEL, D_MODEL), layer), vec, vec,
                    _layer_spec((D_MODEL, FFN_HIDDEN), layer), _layer_spec((D_MODEL, FFN_HIDDEN), layer),
                    _layer_spec((FFN_HIDDEN, D_MODEL), layer), vec, vec],
        out_specs=[tile(D_MODEL), whole(D_MODEL)],
        out_shape=[jax.ShapeDtypeStruct((tp, D_MODEL), F32), jax.ShapeDtypeStruct((ts, D_MODEL), F32)],
        compiler_params=_cparams(1),
        name="outffn",
    )(*prompt, *sample, p["w_out"], p["ln1_g"], p["ln1_b"], p["w1"], p["w3"], p["w2"], p["ln2_g"], p["ln2_b"])


def _prep_params(w_in, lam_re, lam_im, log_step, b_re, b_im, c_re, c_im, ssm_d, glu_w, glu_b,
                 gn_g, gn_b, sinks, w_out, ln1_g, ln1_b, w1, w3, w2, ln2_g, ln2_b):
    col_scale = np.ones((IN_WIDTH,), np.float32)
    col_scale[COL_RET + RET_WIDTH:COL_RET + 2 * RET_WIDTH] = RET_D ** -0.5
    col_scale[COL_ATT:COL_ATT + ATT_WIDTH] = ATT_D ** -0.5

    step = jnp.exp(log_step)[..., None]
    mag = jnp.exp(lam_re * step)
    a_re, a_im = mag * jnp.cos(lam_im * step), mag * jnp.sin(lam_im * step)
    den = lam_re * lam_re + lam_im * lam_im
    k_re = ((a_re - 1.0) * lam_re + a_im * lam_im) / den
    k_im = (a_im * lam_re - (a_re - 1.0) * lam_im) / den
    bb_re = k_re[..., None] * b_re - k_im[..., None] * b_im
    bb_im = k_re[..., None] * b_im + k_im[..., None] * b_re
    eye8 = jnp.eye(8, dtype=F32)

    def b_blocks(m):
        m = m.reshape(DEPTH, 3, 8, SSM_STATE, SSM_CH)
        return jnp.einsum("ligph,gk->lighkp", m, eye8).reshape(DEPTH, 3, LANES, 512).astype(BF16)

    def c_blocks(m):
        m = m.reshape(DEPTH, 3, 8, SSM_CH, SSM_STATE)
        return jnp.einsum("lighp,gk->likpgh", m, eye8).reshape(DEPTH, 3, 512, LANES).astype(BF16)

    row = lambda a: a.reshape(DEPTH, 1, -1)
    sink8 = jnp.concatenate([sinks, jnp.zeros((DEPTH, 2), F32)], axis=1)
    return dict(
        w_in=(w_in * col_scale).astype(BF16),
        bre=b_blocks(bb_re), bim=b_blocks(bb_im), cre=c_blocks(c_re), cim=c_blocks(-c_im),
        ar=row(a_re), ai=row(a_im),
        ssm_d=row(ssm_d), glu_w=glu_w.astype(BF16), glu_b=row(glu_b),
        gn_g=row(gn_g), gn_b=row(gn_b),
        gn_g4=gn_g.reshape(DEPTH, RET_HEADS, RET_D), gn_b4=gn_b.reshape(DEPTH, RET_HEADS, RET_D),
        sinks=sinks, sink8=jnp.broadcast_to(sink8[:, :, None], (DEPTH, 8, LANES)),
        w_out=w_out.astype(BF16), ln1_g=row(ln1_g), ln1_b=row(ln1_b),
        w1=w1.astype(BF16), w3=w3.astype(BF16), w2=w2.astype(BF16), ln2_g=row(ln2_g), ln2_b=row(ln2_b),
    )


def _prompt_mix(x3, p, layer):
    nb, length, _ = x3.shape
    t = nb * length
    z, o_r, o_a, hlast, sblk, klast, vlast = _mixer_prompt(x3, p, layer)
    rows = (z.reshape(t, SSM_WIDTH), o_r.reshape(t, RET_WIDTH), o_a.reshape(t, ATT_WIDTH), x3.reshape(t, D_MODEL))
    s4 = jnp.stack([sblk[:, h // 2, HALF * (h % 2):HALF * (h % 2 + 1), HALF * (h % 2):HALF * (h % 2 + 1)]
                    for h in range(RET_HEADS)], axis=1)
    return rows, (hlast, s4, klast, vlast)


def _sample_bias8():
    slopes = _alibi_slopes(ATT_Q_HEADS)
    dist = WINDOW - np.arange(WINDOW)
    bias = np.zeros((8, WINDOW))
    for h in range(ATT_Q_HEADS):
        bias[h] = -slopes[h] * dist
    return jnp.asarray(bias, F32)


def _sample_mix(x2, p, layer, h0r, h0i, s_all, kc_all, vc_all):
    nb = x2.shape[0]
    pu, pr, pa = _inproj(x2, p, layer)
    z, h_re, h_im = _s5_sample(pu, h0r, h0i, p, layer)
    heads = lambda a: a.reshape(nb, RET_HEADS, RET_D)
    q4, k4, v4, g4 = (heads(pr[:, RET_WIDTH * i:RET_WIDTH * (i + 1)]) for i in range(4))
    q6 = pa[:, 0:ATT_WIDTH].reshape(nb, ATT_Q_HEADS, ATT_D)
    kv_of = np.arange(ATT_Q_HEADS) // ATT_GROUP
    qb = jnp.zeros((nb, 8, ATT_KV_HEADS, ATT_D), F32).at[:, np.arange(ATT_Q_HEADS), kv_of].set(q6)
    kn = pa[:, ATT_WIDTH:ATT_WIDTH + LANES].reshape(nb, 1, LANES)
    vn = pa[:, ATT_WIDTH + LANES:ATT_WIDTH + 2 * LANES].reshape(nb, 1, LANES)
    o_r, o8, s_new, k_out, v_out = _mix_sample(
        q4.transpose(0, 2, 1), k4.transpose(0, 2, 1), v4, g4, s_all, qb.reshape(nb, 8, LANES), kn, vn,
        kc_all, vc_all, p, _sample_bias8(), layer)
    o_r = o_r.reshape(nb, RET_WIDTH).astype(BF16)
    o_a = o8.reshape(nb, 8, ATT_KV_HEADS, ATT_D)[:, np.arange(ATT_Q_HEADS), kv_of].reshape(nb, ATT_WIDTH).astype(BF16)
    return (z, o_r, o_a, x2), (h_re, h_im, s_new, k_out, v_out)


def kernel(x_prompt, x_sample, state_ssm_re, state_ssm_im, state_ret, cache_win_k, cache_win_v, w_in, ssm_lambda_re, ssm_lambda_im, ssm_log_step, ssm_b_re, ssm_b_im, ssm_c_re, ssm_c_im, ssm_d, ssm_glu_w, ssm_glu_b, ret_gn_g, ret_gn_b, attn_sinks, w_out, ln1_g, ln1_b, ffn_w1, ffn_w3, ffn_w2, ln2_g, ln2_b):
    p = _prep_params(w_in, ssm_lambda_re, ssm_lambda_im, ssm_log_step, ssm_b_re, ssm_b_im, ssm_c_re, ssm_c_im,
                     ssm_d, ssm_glu_w, ssm_glu_b, ret_gn_g, ret_gn_b, attn_sinks, w_out, ln1_g, ln1_b,
                     ffn_w1, ffn_w3, ffn_w2, ln2_g, ln2_b)

    xp = x_prompt
    pb, length, _ = x_prompt.shape
    nb = x_sample.shape[0]
    xs = x_sample.reshape(nb, D_MODEL)
    h0r = state_ssm_re.reshape(DEPTH, nb, SSM_LANES)
    h0i = state_ssm_im.reshape(DEPTH, nb, SSM_LANES)
    s_all = state_ret.reshape(DEPTH, nb, RET_WIDTH, RET_D)
    kc_all = cache_win_k.reshape(DEPTH, nb, WINDOW, LANES)
    vc_all = cache_win_v.reshape(DEPTH, nb, WINDOW, LANES)
    p_out, s_out = [], []
    for layer in range(DEPTH):
        prompt_rows, states = _prompt_mix(xp, p, layer)
        p_out.append(states)
        sample_rows, states = _sample_mix(xs, p, layer, h0r, h0i, s_all, kc_all, vc_all)
        s_out.append(states)
        yp, xs = _outffn(prompt_rows, sample_rows, p, layer)
        xp = yp.reshape(pb, length, D_MODEL)
    stack = lambda outs, i: jnp.stack([o[i] for o in outs])
    p_h = stack(p_out, 0)
    p_kv_shape = (DEPTH, pb, TC, ATT_KV_HEADS, ATT_D)
    ssm_shape = (DEPTH, nb, SSM_GROUPS, SSM_STATE)
    s_kv_shape = (DEPTH, nb, WINDOW, ATT_KV_HEADS, ATT_D)

    return (xp, xs.reshape(nb, 1, D_MODEL),
            p_h[:, :, :SSM_LANES].reshape(DEPTH, pb, SSM_GROUPS, SSM_STATE),
            p_h[:, :, SSM_LANES:].reshape(DEPTH, pb, SSM_GROUPS, SSM_STATE),
            stack(p_out, 1), stack(p_out, 2).reshape(p_kv_shape), stack(p_out, 3).reshape(p_kv_shape),
            stack(s_out, 0).reshape(ssm_shape), stack(s_out, 1).reshape(ssm_shape),
            stack(s_out, 2).reshape(DEPTH, nb, RET_HEADS, RET_D, RET_D),
            stack(s_out, 3).reshape(s_kv_shape), stack(s_out, 4).reshape(s_kv_shape))
```

```python
import functools
import math

import numpy as np
import jax
import jax.numpy as jnp
from jax import lax
from jax.experimental import pallas as pl
from jax.experimental.pallas import tpu as pltpu

F32 = jnp.float32
BF16 = jnp.bfloat16

D_MODEL = 1024
DEPTH = 2
SSM_GROUPS = 24
SSM_CH = 16
SSM_STATE = 64
SSM_WIDTH = SSM_GROUPS * SSM_CH
SSM_LANES = SSM_GROUPS * SSM_STATE
RET_HEADS = 4
RET_D = 64
RET_WIDTH = RET_HEADS * RET_D
ATT_Q_HEADS = 6
ATT_KV_HEADS = 2
ATT_GROUP = ATT_Q_HEADS // ATT_KV_HEADS
ATT_D = 64
ATT_WIDTH = ATT_Q_HEADS * ATT_D
WINDOW = 128
FFN_HIDDEN = 2816
IN_WIDTH = 2048
ALPHA = (2 * DEPTH) ** 0.25
LN_EPS = 1e-5
NEG_INF = -1e30

COL_RET = SSM_WIDTH
COL_ATT = SSM_WIDTH + 4 * RET_WIDTH
RET_COLS = 4 * RET_WIDTH
ATT_COLS = ATT_WIDTH + 2 * ATT_KV_HEADS * ATT_D

TC = 128
LANES = 128
N_TILES = SSM_LANES // LANES
HALF = LANES // 2
MXU_COLS = 256
SCAN_PIECES = 4
ROW_TILE = 512
FFN_PARTS = 2
SAMPLE_ROWS = 16
VMEM_LIMIT = 56 * 2 ** 20


def _alibi_slopes(n):
    def pow2(k):
        start = 2.0 ** (-8.0 / k)
        return [start ** (i + 1) for i in range(k)]
    if n & (n - 1) == 0:
        return pow2(n)
    c = 2 ** int(math.floor(math.log2(n)))
    return pow2(c) + pow2(2 * c)[0::2][: n - c]


def _cparams(n_axes):
    return pltpu.CompilerParams(dimension_semantics=("arbitrary",) * n_axes, vmem_limit_bytes=VMEM_LIMIT)


def _const_spec(shape):
    nd = len(shape)
    return pl.BlockSpec(shape, lambda *_: (0,) * nd, pipeline_mode=pl.Buffered(1))


def _layer_spec(shape, layer):
    nd = len(shape)
    return pl.BlockSpec((None,) + tuple(shape), lambda *_: (layer,) + (0,) * nd, pipeline_mode=pl.Buffered(1))


def _dot(a, b):
    return jnp.dot(a, b, preferred_element_type=F32)


def _dot_nt(a, b):
    return lax.dot_general(a, b, (((1,), (1,)), ((), ())), preferred_element_type=F32)


def _dot_tn(a, b):
    return lax.dot_general(a, b, (((0,), (0,)), ((), ())), preferred_element_type=F32)


def _layernorm(v, g, b):
    mu = jnp.mean(v, -1, keepdims=True)
    d = v - mu
    var = jnp.mean(d * d, -1, keepdims=True)
    return d * lax.rsqrt(var + LN_EPS) * g + b


def _gelu_tanh(x):
    return 0.5 * x * (1.0 + jnp.tanh(math.sqrt(2.0 / math.pi) * (x + 0.044715 * (x * x * x))))


def _sigmoid(x):
    return 1.0 / (1.0 + jnp.exp(-x))


def _inproj_kernel(x_ref, w_ref, pu_ref, pr_ref, pa_ref):
    xb = x_ref[...].astype(BF16)
    pu_ref[...] = _dot(xb, w_ref[:, 0:COL_RET])
    pr_ref[...] = _dot(xb, w_ref[:, COL_RET:COL_ATT])
    pa_ref[...] = _dot(xb, w_ref[:, COL_ATT:IN_WIDTH])


def _inproj(x2, p, layer):
    t = x2.shape[0]
    tm = min(ROW_TILE, t)
    row = lambda c: pl.BlockSpec((tm, c), lambda i: (i, 0))
    return pl.pallas_call(
        _inproj_kernel,
        grid=(t // tm,),
        in_specs=[row(D_MODEL), _layer_spec((D_MODEL, IN_WIDTH), layer)],
        out_specs=[row(SSM_WIDTH), row(RET_COLS), row(ATT_COLS)],
        out_shape=[jax.ShapeDtypeStruct((t, SSM_WIDTH), F32),
                   jax.ShapeDtypeStruct((t, RET_COLS), F32),
                   jax.ShapeDtypeStruct((t, ATT_COLS), F32)],
        compiler_params=_cparams(1),
        name="inproj",
    )(x2, p["w_in"])


def _s5_bu(ub, bre_ref, bim_ref):
    res, ims = [], []
    for i in range(3):
        ui = ub[:, LANES * i:LANES * (i + 1)]
        res.append(_dot(ui, bre_ref[i]))
        ims.append(_dot(ui, bim_ref[i]))
    return res, ims


def _s5_out(u, h_re, h_im, cre_ref, cim_ref, d_ref, gw_ref, gb_ref):
    ys = []
    for i in range(3):
        ys.append(_dot(h_re[i].astype(BF16), cre_ref[i]) + _dot(h_im[i].astype(BF16), cim_ref[i]))
    y = jnp.concatenate(ys, axis=1) + d_ref[...] * u
    z = _gelu_tanh(y)
    gate = _sigmoid(_dot(z.astype(BF16), gw_ref[...]) + gb_ref[...])
    return z * gate


def _s5_prompt_body(n, u3, bre_ref, bim_ref, cre_ref, cim_ref, ar_ref, ai_ref, d_ref, gw_ref, gb_ref,
                    z_ref, hlast_ref, h_scr, hc_scr, fillers=()):
    nb = u3.shape[0]

    u = pltpu.einshape("btc->(tb)c", u3)

    res, ims = _s5_bu(u.astype(BF16), bre_ref, bim_ref)
    for i in range(3):
        for k in range(4):
            h_scr[4 * i + k] = res[i][:, LANES * k:LANES * (k + 1)]
            h_scr[N_TILES + 4 * i + k] = ims[i][:, LANES * k:LANES * (k + 1)]

    def step(t, carry):
        r0 = t * nb
        out = []
        for c in range(N_TILES):
            sl = slice(LANES * c, LANES * (c + 1))
            ar, ai = ar_ref[:, sl], ai_ref[:, sl]
            hr, hi = carry[c], carry[N_TILES + c]
            nr = ar * hr - ai * hi + h_scr[c, pl.ds(r0, nb), :]
            ni = ar * hi + ai * hr + h_scr[N_TILES + c, pl.ds(r0, nb), :]
            h_scr[c, pl.ds(r0, nb), :] = nr
            h_scr[N_TILES + c, pl.ds(r0, nb), :] = ni
            out.append((nr, ni))
        return tuple(o[0] for o in out) + tuple(o[1] for o in out)

    fin = tuple(hc_scr[:, LANES * c:LANES * (c + 1)] for c in range(2 * N_TILES))
    groups = len(fillers) + 1
    for g in range(groups):
        for t in range(TC * g // groups, TC * (g + 1) // groups):
            fin = step(t, fin)
        if g < len(fillers):
            fillers[g]()
    for c in range(2 * N_TILES):
        hc_scr[:, LANES * c:LANES * (c + 1)] = fin[c]
        hlast_ref[:, LANES * c:LANES * (c + 1)] = fin[c]

    ys = []

    def out_piece(i):
        def emit():
            h_re = jnp.concatenate([h_scr[4 * i + k] for k in range(4)], axis=1).astype(BF16)
            h_im = jnp.concatenate([h_scr[N_TILES + 4 * i + k] for k in range(4)], axis=1).astype(BF16)
            ys.append(_dot(h_re, cre_ref[i]) + _dot(h_im, cim_ref[i]))
        return emit

    def finish(fillers=()):
        fillers = list(fillers)
        y = jnp.concatenate(ys, axis=1) + d_ref[...] * u
        if fillers:
            fillers.pop(0)()
        z = _gelu_tanh(y)
        if fillers:
            fillers.pop(0)()
        gate = _sigmoid(_dot(z.astype(BF16), gw_ref[...]) + gb_ref[...])
        while fillers:
            fillers.pop(0)()
        z_ref[...] = pltpu.einshape("(tb)c->btc", z * gate, b=nb).astype(BF16)

    return [out_piece(i) for i in range(3)] + [finish]


def _s5_weight_specs(layer):
    ls = lambda *shape: _layer_spec(shape, layer)
    return [ls(3, LANES, 512), ls(3, LANES, 512), ls(3, 512, LANES), ls(3, 512, LANES),
            ls(1, SSM_LANES), ls(1, SSM_LANES), ls(1, SSM_WIDTH), ls(SSM_WIDTH, SSM_WIDTH), ls(1, SSM_WIDTH)]


def _s5_weights(p):
    return (p["bre"], p["bim"], p["cre"], p["cim"], p["ar"], p["ai"], p["ssm_d"], p["glu_w"], p["glu_b"])


def _s5_sample_kernel(u_ref, h0r_ref, h0i_ref, bre_ref, bim_ref, cre_ref, cim_ref, ar_ref, ai_ref, d_ref,
                      gw_ref, gb_ref, z_ref, hr_ref, hi_ref):
    u = u_ref[...]
    res, ims = _s5_bu(u.astype(BF16), bre_ref, bim_ref)
    h_re, h_im = [], []
    for i in range(3):
        sl = slice(512 * i, 512 * (i + 1))
        ar, ai = ar_ref[:, sl], ai_ref[:, sl]
        h0r, h0i = h0r_ref[:, sl], h0i_ref[:, sl]
        nr = ar * h0r - ai * h0i + res[i]
        ni = ar * h0i + ai * h0r + ims[i]
        hr_ref[:, sl] = nr
        hi_ref[:, sl] = ni
        h_re.append(nr)
        h_im.append(ni)
    z_ref[...] = _s5_out(u, h_re, h_im, cre_ref, cim_ref, d_ref, gw_ref, gb_ref).astype(BF16)


def _s5_sample(u, h0r, h0i, p, layer):
    nb = u.shape[0]
    full = lambda c: pl.BlockSpec((nb, c), lambda i: (0, 0))
    state = pl.BlockSpec((None, nb, SSM_LANES), lambda i: (layer, 0, 0))
    return pl.pallas_call(
        _s5_sample_kernel,
        grid=(1,),
        in_specs=[full(SSM_WIDTH), state, state] + _s5_weight_specs(layer),
        out_specs=[full(SSM_WIDTH), full(SSM_LANES), full(SSM_LANES)],
        out_shape=[jax.ShapeDtypeStruct((nb, SSM_WIDTH), BF16),
                   jax.ShapeDtypeStruct((nb, SSM_LANES), F32),
                   jax.ShapeDtypeStruct((nb, SSM_LANES), F32)],
        compiler_params=_cparams(1),
        name="s5_sample",
    )(u, h0r, h0i, *_s5_weights(p))


def _ret_gammas():
    return [1.0 - 2.0 ** (-5.0 - h) for h in range(RET_HEADS)]


def _ret_consts():
    log_g = np.log1p(-(2.0 ** (-5.0 - np.arange(RET_HEADS, dtype=np.float64))))
    idx = np.arange(TC)
    diff = idx[:, None] - idx[None, :]
    dmask = np.where(diff >= 0, np.exp(log_g[:, None, None] * np.maximum(diff, 0)), 0.0)
    inner = np.exp(log_g[:, None] * (idx + 1))
    kvd = np.exp(log_g[:, None] * (TC - 1 - idx))
    cd = np.exp(log_g * TC)
    indec = np.zeros((2, TC, LANES))
    kvdec = np.zeros((2, TC, LANES))
    cdec = np.zeros((2, LANES, LANES))
    blk = np.zeros((LANES, LANES))
    for j in range(2):
        for hh in range(2):
            sl = slice(HALF * hh, HALF * (hh + 1))
            indec[j, :, sl] = inner[2 * j + hh][:, None]
            kvdec[j, :, sl] = kvd[2 * j + hh][:, None]
            cdec[j, sl, sl] = cd[2 * j + hh]
            blk[sl, sl] = 1.0
    as32 = lambda a: jnp.asarray(a, F32)
    return as32(dmask), as32(kvdec), as32(indec), as32(cdec), as32(blk)


def _halfnorm(o, lo):
    zero = jnp.zeros_like(o)
    s_lo = jnp.sum(jnp.where(lo, o, zero), -1, keepdims=True)
    s_hi = jnp.sum(jnp.where(lo, zero, o), -1, keepdims=True)
    d = o - jnp.where(lo, s_lo, s_hi) * (1.0 / HALF)
    d2 = d * d
    v_lo = jnp.sum(jnp.where(lo, d2, zero), -1, keepdims=True)
    v_hi = jnp.sum(jnp.where(lo, zero, d2), -1, keepdims=True)
    var = jnp.where(lo, v_lo, v_hi) * (1.0 / HALF)
    return d * lax.rsqrt(var + LN_EPS)


def _ret_prompt_body(n, pr_ref, dmask_ref, kvdec_ref, indec_ref, cdec_ref, blk_ref, gng_ref, gnb_ref,
                     o_ref, slast_ref, s_scr):
    nb = pr_ref.shape[0]

    lo = lax.broadcasted_iota(jnp.int32, (TC, LANES), 1) < HALF
    zero = jnp.zeros((TC, LANES), F32)

    def row(b):
        for j in range(2):
            q2 = pr_ref[b, :, LANES * j:LANES * (j + 1)]
            k2 = pr_ref[b, :, 256 + LANES * j:256 + LANES * (j + 1)]
            v2 = pr_ref[b, :, 512 + LANES * j:512 + LANES * (j + 1)]
            g2 = pr_ref[b, :, 768 + LANES * j:768 + LANES * (j + 1)]
            k2b = k2.astype(BF16)
            v2b = v2.astype(BF16)
            state = s_scr[b, j]
            qs = jnp.concatenate([jnp.where(lo, q2, zero), jnp.where(lo, zero, q2)], axis=0).astype(BF16)
            sc = _dot_nt(qs, k2b)
            qd = (q2 * indec_ref[j]).astype(BF16)
            lhs = jnp.concatenate(
                [jnp.concatenate([(sc[hh * TC:(hh + 1) * TC] * dmask_ref[2 * j + hh]).astype(BF16), qd], axis=1)
                 for hh in range(2)], axis=0)
            o2 = _dot(lhs, jnp.concatenate([v2b, state.astype(BF16)], axis=0))
            o = jnp.where(lo, o2[0:TC], o2[TC:2 * TC])
            kd = (k2 * kvdec_ref[j]).astype(BF16)
            s_new = cdec_ref[j] * state + blk_ref[...] * _dot_tn(kd, v2b)
            s_scr[b, j] = s_new
            slast_ref[b, j] = s_new
            sl = slice(LANES * j, LANES * (j + 1))
            o = _halfnorm(o, lo) * gng_ref[:, sl] + gnb_ref[:, sl]
            o_ref[b, :, sl] = (g2 * _sigmoid(g2) * o).astype(BF16)

    return [functools.partial(row, b) for b in range(nb)]


def _att_bias():
    slopes = _alibi_slopes(ATT_Q_HEADS)
    qpos = np.arange(TC)[:, None] + TC
    kpos = np.arange(2 * TC)[None, :]
    dist = qpos - kpos
    valid = (dist >= 0) & (dist <= WINDOW)
    prev_cur = np.stack([np.where(valid, -slopes[h] * dist, NEG_INF) for h in range(ATT_Q_HEADS)])
    cur_prev = np.concatenate([prev_cur[:, :, TC:], prev_cur[:, :, :TC]], axis=2)
    first = np.where(np.arange(2 * TC) >= TC, NEG_INF, 0.0)[None, :]
    return jnp.asarray(cur_prev, F32), jnp.asarray(prev_cur, F32), jnp.asarray(first, F32)


def _att_prompt_body(n, layer, sink_ref, pa_ref, bias_cp_ref, bias_pc_ref, first_ref, o_ref, kout_ref, vout_ref,
                     k_scr, v_scr, bias_scr, fillers=()):
    nb = pa_ref.shape[0]
    fillers = list(fillers)

    even = n % 2 == 0
    row0 = pl.multiple_of((n % 2) * TC, TC)
    first = jnp.where(n == 0, first_ref[...], jnp.zeros_like(first_ref))
    for head in range(ATT_Q_HEADS):
        bias_scr[head] = jnp.where(even, bias_cp_ref[head], bias_pc_ref[head]) + first

    lo = lax.broadcasted_iota(jnp.int32, (TC, LANES), 1) < HALF
    zero = jnp.zeros((TC, LANES), F32)
    swap = lambda a: pltpu.roll(a, HALF, 1)
    moved = [head % 2 != head // ATT_GROUP for head in range(ATT_Q_HEADS)]

    def scores(b):
        kcur = pa_ref[b, :, ATT_WIDTH:ATT_WIDTH + LANES]
        vcur = pa_ref[b, :, ATT_WIDTH + LANES:ATT_WIDTH + 2 * LANES]
        kout_ref[b] = kcur
        vout_ref[b] = vcur
        k_scr[b, pl.ds(row0, TC), :] = kcur.astype(BF16)
        v_scr[b, pl.ds(row0, TC), 0:LANES] = vcur.astype(BF16)
        rows = []
        for head in range(ATT_Q_HEADS):
            qcol = pa_ref[b, :, LANES * (head // 2):LANES * (head // 2 + 1)]
            if moved[head]:
                qcol = swap(qcol)
            rows.append(jnp.where(lo, qcol, zero) if head // ATT_GROUP == 0 else jnp.where(lo, zero, qcol))
        qs = jnp.concatenate(rows, axis=0).astype(BF16)
        return _dot_nt(qs, k_scr[b])

    def softmax(s_all):
        es, ms = [], []
        for head in range(ATT_Q_HEADS):
            s = s_all[head * TC:(head + 1) * TC] + bias_scr[head]
            m = jnp.maximum(jnp.max(s, -1, keepdims=True), sink_ref[layer, head])
            es.append(jnp.exp(s - m).astype(BF16))
            ms.append(m)
        return jnp.concatenate(es, axis=0), ms

    def values(b, e_all, ms):
        r_all = _dot(e_all, v_scr[b])
        outs = []
        for head in range(ATT_Q_HEADS):
            r = r_all[head * TC:(head + 1) * TC]
            den = r[:, LANES:2 * LANES] + jnp.exp(sink_ref[layer, head] - ms[head])
            out = r[:, 0:LANES] / den
            outs.append(swap(out) if moved[head] else out)
        for c in range(3):
            o_ref[b, :, LANES * c:LANES * (c + 1)] = jnp.where(lo, outs[2 * c], outs[2 * c + 1]).astype(BF16)

    s_all, soft = {}, {}
    for k in range(nb + 2):
        if fillers:
            fillers.pop(0)()
        if 0 <= k - 2 < nb:
            values(k - 2, *soft.pop(k - 2))
        if k < nb:
            s_all[k] = scores(k)
        if 0 <= k - 1 < nb:
            soft[k - 1] = softmax(s_all.pop(k - 1))
    while fillers:
        fillers.pop(0)()


def _mixer_kernel(sink_ref, x_ref, w_ref,
                  bre_ref, bim_ref, cre_ref, cim_ref, ar_ref, ai_ref, d_ref, gw_ref, gb_ref,
                  dmask_ref, kvdec_ref, indec_ref, cdec_ref, blk_ref, gng_ref, gnb_ref,
                  bias_cp_ref, bias_pc_ref, first_ref,
                  z_ref, or_ref, oa_ref, hlast_ref, slast_ref, kout_ref, vout_ref,
                  pr_scr, pa_scr, h_scr, hc_scr, s_scr, k_scr, v_scr, bias_scr, *, layer):
    n = pl.program_id(0)
    nb = x_ref.shape[0]

    @pl.when(n == 0)
    def _():
        hc_scr[...] = jnp.zeros_like(hc_scr)
        s_scr[...] = jnp.zeros_like(s_scr)
        k_scr[...] = jnp.zeros_like(k_scr)
        v_scr[:, :, 0:LANES] = jnp.zeros((nb, 2 * TC, LANES), BF16)
        v_scr[:, :, LANES:2 * LANES] = jnp.ones((nb, 2 * TC, LANES), BF16)

    xb = x_ref[...].reshape(nb * TC, D_MODEL).astype(BF16)
    u3 = _dot(xb, w_ref[:, 0:COL_RET]).reshape(nb, TC, SSM_WIDTH)

    def project(dst, col0, c0, c1):
        def emit():
            dst[:, :, c0:c1] = _dot(xb, w_ref[:, col0 + c0:col0 + c1]).reshape(nb, TC, c1 - c0)
        return emit

    pieces = [project(pr_scr, COL_RET, c, c + MXU_COLS) for c in range(0, RET_COLS, MXU_COLS)]
    pieces += [project(pa_scr, COL_ATT, c, min(c + MXU_COLS, ATT_COLS)) for c in range(0, ATT_COLS, MXU_COLS)]
    *s5_maps, s5_finish = _s5_prompt_body(n, u3, bre_ref, bim_ref, cre_ref, cim_ref, ar_ref, ai_ref, d_ref, gw_ref,
                                          gb_ref, z_ref, hlast_ref, h_scr, hc_scr, fillers=pieces[:SCAN_PIECES])
    for emit in s5_maps:
        emit()
    s5_finish(pieces[SCAN_PIECES:])
    ret_rows = _ret_prompt_body(n, pr_scr, dmask_ref, kvdec_ref, indec_ref, cdec_ref, blk_ref, gng_ref, gnb_ref,
                                or_ref, slast_ref, s_scr)
    _att_prompt_body(n, layer, sink_ref, pa_scr, bias_cp_ref, bias_pc_ref, first_ref, oa_ref, kout_ref, vout_ref,
                     k_scr, v_scr, bias_scr, fillers=ret_rows)


def _mixer_prompt(x3, p, layer):
    nb, length, _ = x3.shape
    chunk = lambda c: pl.BlockSpec((nb, TC, c), lambda n: (0, n, 0))
    fixed = lambda *shape: pl.BlockSpec(shape, lambda n: (0,) * len(shape))
    ret_consts = _ret_consts()
    att_consts = _att_bias()
    return pl.pallas_call(
        functools.partial(_mixer_kernel, layer=layer),
        grid=(length // TC,),
        in_specs=[pl.BlockSpec(memory_space=pltpu.SMEM), chunk(D_MODEL), _layer_spec((D_MODEL, IN_WIDTH), layer)]
                 + _s5_weight_specs(layer)
                 + [_const_spec(c.shape) for c in ret_consts]
                 + [_layer_spec((1, RET_WIDTH), layer), _layer_spec((1, RET_WIDTH), layer)]
                 + [_const_spec(c.shape) for c in att_consts],
        out_specs=[chunk(SSM_WIDTH), chunk(RET_WIDTH), chunk(ATT_WIDTH),
                   fixed(nb, 2 * SSM_LANES), fixed(nb, 2, LANES, LANES), fixed(nb, TC, LANES), fixed(nb, TC, LANES)],
        out_shape=[jax.ShapeDtypeStruct((nb, length, SSM_WIDTH), BF16),
                   jax.ShapeDtypeStruct((nb, length, RET_WIDTH), BF16),
                   jax.ShapeDtypeStruct((nb, length, ATT_WIDTH), BF16),
                   jax.ShapeDtypeStruct((nb, 2 * SSM_LANES), F32),
                   jax.ShapeDtypeStruct((nb, 2, LANES, LANES), F32),
                   jax.ShapeDtypeStruct((nb, TC, LANES), F32),
                   jax.ShapeDtypeStruct((nb, TC, LANES), F32)],
        scratch_shapes=[pltpu.VMEM((nb, TC, RET_COLS), F32), pltpu.VMEM((nb, TC, ATT_COLS), F32),
                        pltpu.VMEM((2 * N_TILES, nb * TC, LANES), F32), pltpu.VMEM((nb, 2 * SSM_LANES), F32),
                        pltpu.VMEM((nb, 2, LANES, LANES), F32),
                        pltpu.VMEM((nb, 2 * TC, LANES), BF16), pltpu.VMEM((nb, 2 * TC, 2 * LANES), BF16),
                        pltpu.VMEM((ATT_Q_HEADS, TC, 2 * TC), F32)],
        compiler_params=_cparams(1),
        name="mixer",
    )(p["sinks"], x3, p["w_in"], *_s5_weights(p), *ret_consts, p["gn_g"], p["gn_b"], *att_consts)


def _mix_sample_kernel(qt_ref, kt_ref, v4_ref, g4_ref, s_ref, qb_ref, kn_ref, vn_ref, kc_ref, vc_ref,
                       gng_ref, gnb_ref, bias_ref, sink_ref, or_ref, oa_ref, sn_ref, kout_ref, vout_ref):
    gammas = _ret_gammas()
    sink = sink_ref[:, 0:1]
    bias = bias_ref[...]

    rows = range(SAMPLE_ROWS)
    qt, kt, v4, g4 = qt_ref[...], kt_ref[...], v4_ref[...], g4_ref[...]
    for h in range(RET_HEADS):
        sl = slice(RET_D * h, RET_D * (h + 1))
        sn = gammas[h] * s_ref[:, sl, :] + kt[:, :, h:h + 1] * v4[:, h:h + 1, :]
        sn_ref[:, sl, :] = sn
        o = jnp.sum(qt[:, :, h:h + 1] * sn, axis=1, keepdims=True)
        d = o - jnp.mean(o, -1, keepdims=True)
        var = jnp.mean(d * d, -1, keepdims=True)
        o = d * lax.rsqrt(var + LN_EPS) * gng_ref[h:h + 1, :] + gnb_ref[h:h + 1, :]
        gate = g4[:, h:h + 1, :]
        or_ref[:, h:h + 1, :] = gate * _sigmoid(gate) * o

    q_all = jnp.concatenate([qb_ref[i] for i in rows], axis=0)
    knew = jnp.concatenate([jnp.broadcast_to(kn_ref[i], (8, LANES)) for i in rows], axis=0)
    vnew = jnp.concatenate([jnp.broadcast_to(vn_ref[i], (8, LANES)) for i in rows], axis=0)
    sink_all = jnp.concatenate([sink] * SAMPLE_ROWS, axis=0)
    s = jnp.concatenate([_dot_nt(qb_ref[i].astype(BF16), kc_ref[i].astype(BF16)) + bias for i in rows], axis=0)
    s_self = jnp.sum(q_all * knew, -1, keepdims=True)
    m = jnp.maximum(jnp.maximum(jnp.max(s, -1, keepdims=True), s_self), sink_all)
    e = jnp.exp(s - m)
    e_self = jnp.exp(s_self - m)
    den = jnp.sum(e, -1, keepdims=True) + e_self + jnp.exp(sink_all - m)
    pv = jnp.concatenate([_dot(e[8 * i:8 * (i + 1)].astype(BF16), vc_ref[i].astype(BF16)) for i in rows], axis=0)
    o = (pv + e_self * vnew) / den
    for i in rows:
        oa_ref[i] = o[8 * i:8 * (i + 1)]

    kout_ref[:, 0:WINDOW - 1, :] = kc_ref[:, 1:WINDOW, :]
    kout_ref[:, WINDOW - 1:WINDOW, :] = kn_ref[...]
    vout_ref[:, 0:WINDOW - 1, :] = vc_ref[:, 1:WINDOW, :]
    vout_ref[:, WINDOW - 1:WINDOW, :] = vn_ref[...]


def _mix_sample(qt, kt, v4, g4, s_all, qb, kn, vn, kc_all, vc_all, p, bias8, layer):
    nb = qt.shape[0]
    bb = SAMPLE_ROWS
    blk = lambda *dims: pl.BlockSpec((bb,) + dims, lambda i: (i,) + (0,) * len(dims))
    lblk = lambda *dims: pl.BlockSpec((None, bb) + dims, lambda i: (layer, i) + (0,) * len(dims))
    in_specs = [blk(RET_D, RET_HEADS), blk(RET_D, RET_HEADS), blk(RET_HEADS, RET_D), blk(RET_HEADS, RET_D),
                lblk(RET_WIDTH, RET_D), blk(8, LANES), blk(1, LANES), blk(1, LANES),
                lblk(WINDOW, LANES), lblk(WINDOW, LANES),
                _layer_spec((RET_HEADS, RET_D), layer), _layer_spec((RET_HEADS, RET_D), layer),
                _const_spec((8, LANES)), _layer_spec((8, LANES), layer)]
    args = [qt, kt, v4, g4, s_all, qb, kn, vn, kc_all, vc_all, p["gn_g4"], p["gn_b4"], bias8, p["sink8"]]
    state_dims = ((RET_WIDTH, RET_D), (WINDOW, LANES), (WINDOW, LANES))
    return pl.pallas_call(
        _mix_sample_kernel,
        grid=(nb // bb,),
        in_specs=in_specs,
        out_specs=[blk(RET_HEADS, RET_D), blk(8, LANES)] + [blk(*dims) for dims in state_dims],
        out_shape=[jax.ShapeDtypeStruct((nb, RET_HEADS, RET_D), F32),
                   jax.ShapeDtypeStruct((nb, 8, LANES), F32)]
                  + [jax.ShapeDtypeStruct((nb,) + dims, F32) for dims in state_dims],
        compiler_params=_cparams(1),
        name="mix_sample",
    )(*args)


def _outffn_rows(z_ref, r_ref, a_ref, x_ref, wo_ref, g1_ref, b1_ref, w1_ref, w3_ref, w2_ref, g2_ref, b2_ref, o_ref):
    c1, c2 = SSM_WIDTH, SSM_WIDTH + RET_WIDTH
    tm = x_ref.shape[0]
    parts = FFN_PARTS if tm % (16 * FFN_PARTS) == 0 else 1
    rows = [slice(tm // parts * k, tm // parts * (k + 1)) for k in range(parts)]

    def mixed(sl):
        mix = _dot(z_ref[sl], wo_ref[0:c1]) + _dot(r_ref[sl], wo_ref[c1:c2]) + _dot(a_ref[sl], wo_ref[c2:D_MODEL])
        return _layernorm(ALPHA * x_ref[sl] + mix, g1_ref[...], b1_ref[...])

    def hidden(x):
        xb = x.astype(BF16)
        h1 = _dot(xb, w1_ref[...])
        h3 = _dot(xb, w3_ref[...])
        return (h1 * _sigmoid(h1) * h3).astype(BF16)

    def out(sl, x, hid):
        o_ref[sl] = _layernorm(ALPHA * x + _dot(hid, w2_ref[...]), g2_ref[...], b2_ref[...])

    xs, hs = {}, {}
    for k in range(parts + 2):
        if k < parts:
            xs[k] = mixed(rows[k])
        if 0 <= k - 1 < parts:
            hs[k - 1] = hidden(xs[k - 1])
        if 0 <= k - 2 < parts:
            out(rows[k - 2], xs.pop(k - 2), hs.pop(k - 2))


def _outffn_kernel(zp_ref, rp_ref, ap_ref, xp_ref, zs_ref, rs_ref, as_ref, xs_ref, *rest, prompt_steps):
    weights, (op_ref, os_ref) = rest[:-2], rest[-2:]
    i = pl.program_id(0)

    @pl.when(i < prompt_steps)
    def _():
        _outffn_rows(zp_ref, rp_ref, ap_ref, xp_ref, *weights, op_ref)

    @pl.when(i == prompt_steps)
    def _():
        _outffn_rows(zs_ref, rs_ref, as_ref, xs_ref, *weights, os_ref)


def _outffn(prompt, sample, p, layer):
    tp, ts = prompt[3].shape[0], sample[3].shape[0]
    tm = min(ROW_TILE, tp)
    steps = tp // tm
    tile = lambda c: pl.BlockSpec((tm, c), lambda i: (jnp.minimum(i, steps - 1), 0))
    whole = lambda c: pl.BlockSpec((ts, c), lambda i: (0, 0))
    widths = (SSM_WIDTH, RET_WIDTH, ATT_WIDTH, D_MODEL)
    vec = _layer_spec((1, D_MODEL), layer)
    return pl.pallas_call(
        functools.partial(_outffn_kernel, prompt_steps=steps),
        grid=(steps + 1,),
        in_specs=[tile(c) for c in widths] + [whole(c) for c in widths]
                 + [_layer_spec((D_MODEL, D_MODEL), layer), vec, vec,
                    _layer_spec((D_MODEL, FFN_HIDDEN), layer), _layer_spec((D_MODEL, FFN_HIDDEN), layer),
                    _layer_spec((FFN_HIDDEN, D_MODEL), layer), vec, vec],
        out_specs=[tile(D_MODEL), whole(D_MODEL)],
        out_shape=[jax.ShapeDtypeStruct((tp, D_MODEL), F32), jax.ShapeDtypeStruct((ts, D_MODEL), F32)],
        compiler_params=_cparams(1),
        name="outffn",
    )(*prompt, *sample, p["w_out"], p["ln1_g"], p["ln1_b"], p["w1"], p["w3"], p["w2"], p["ln2_g"], p["ln2_b"])


def _prep_params(w_in, lam_re, lam_im, log_step, b_re, b_im, c_re, c_im, ssm_d, glu_w, glu_b,
                 gn_g, gn_b, sinks, w_out, ln1_g, ln1_b, w1, w3, w2, ln2_g, ln2_b):
    col_scale = np.ones((IN_WIDTH,), np.float32)
    col_scale[COL_RET + RET_WIDTH:COL_RET + 2 * RET_WIDTH] = RET_D ** -0.5
    col_scale[COL_ATT:COL_ATT + ATT_WIDTH] = ATT_D ** -0.5

    step = jnp.exp(log_step)[..., None]
    mag = jnp.exp(lam_re * step)
    a_re, a_im = mag * jnp.cos(lam_im * step), mag * jnp.sin(lam_im * step)
    den = lam_re * lam_re + lam_im * lam_im
    k_re = ((a_re - 1.0) * lam_re + a_im * lam_im) / den
    k_im = (a_im * lam_re - (a_re - 1.0) * lam_im) / den
    bb_re = k_re[..., None] * b_re - k_im[..., None] * b_im
    bb_im = k_re[..., None] * b_im + k_im[..., None] * b_re
    eye8 = jnp.eye(8, dtype=F32)

    def b_blocks(m):
        m = m.reshape(DEPTH, 3, 8, SSM_STATE, SSM_CH)
        return jnp.einsum("ligph,gk->lighkp", m, eye8).reshape(DEPTH, 3, LANES, 512).astype(BF16)

    def c_blocks(m):
        m = m.reshape(DEPTH, 3, 8, SSM_CH, SSM_STATE)
        return jnp.einsum("lighp,gk->likpgh", m, eye8).reshape(DEPTH, 3, 512, LANES).astype(BF16)

    row = lambda a: a.reshape(DEPTH, 1, -1)
    sink8 = jnp.concatenate([sinks, jnp.zeros((DEPTH, 2), F32)], axis=1)
    return dict(
        w_in=(w_in * col_scale).astype(BF16),
        bre=b_blocks(bb_re), bim=b_blocks(bb_im), cre=c_blocks(c_re), cim=c_blocks(-c_im),
        ar=row(a_re), ai=row(a_im),
        ssm_d=row(ssm_d), glu_w=glu_w.astype(BF16), glu_b=row(glu_b),
        gn_g=row(gn_g), gn_b=row(gn_b),
        gn_g4=gn_g.reshape(DEPTH, RET_HEADS, RET_D), gn_b4=gn_b.reshape(DEPTH, RET_HEADS, RET_D),
        sinks=sinks, sink8=jnp.broadcast_to(sink8[:, :, None], (DEPTH, 8, LANES)),
        w_out=w_out.astype(BF16), ln1_g=row(ln1_g), ln1_b=row(ln1_b),
        w1=w1.astype(BF16), w3=w3.astype(BF16), w2=w2.astype(BF16), ln2_g=row(ln2_g), ln2_b=row(ln2_b),
    )


def _prompt_mix(x3, p, layer):
    nb, length, _ = x3.shape
    t = nb * length
    z, o_r, o_a, hlast, sblk, klast, vlast = _mixer_prompt(x3, p, layer)
    rows = (z.reshape(t, SSM_WIDTH), o_r.reshape(t, RET_WIDTH), o_a.reshape(t, ATT_WIDTH), x3.reshape(t, D_MODEL))
    s4 = jnp.stack([sblk[:, h // 2, HALF * (h % 2):HALF * (h % 2 + 1), HALF * (h % 2):HALF * (h % 2 + 1)]
                    for h in range(RET_HEADS)], axis=1)
    return rows, (hlast, s4, klast, vlast)


def _sample_bias8():
    slopes = _alibi_slopes(ATT_Q_HEADS)
    dist = WINDOW - np.arange(WINDOW)
    bias = np.zeros((8, WINDOW))
    for h in range(ATT_Q_HEADS):
        bias[h] = -slopes[h] * dist
    return jnp.asarray(bias, F32)


def _sample_mix(x2, p, layer, h0r, h0i, s_all, kc_all, vc_all):
    nb = x2.shape[0]
    pu, pr, pa = _inproj(x2, p, layer)
    z, h_re, h_im = _s5_sample(pu, h0r, h0i, p, layer)
    heads = lambda a: a.reshape(nb, RET_HEADS, RET_D)
    q4, k4, v4, g4 = (heads(pr[:, RET_WIDTH * i:RET_WIDTH * (i + 1)]) for i in range(4))
    q6 = pa[:, 0:ATT_WIDTH].reshape(nb, ATT_Q_HEADS, ATT_D)
    kv_of = np.arange(ATT_Q_HEADS) // ATT_GROUP
    qb = jnp.zeros((nb, 8, ATT_KV_HEADS, ATT_D), F32).at[:, np.arange(ATT_Q_HEADS), kv_of].set(q6)
    kn = pa[:, ATT_WIDTH:ATT_WIDTH + LANES].reshape(nb, 1, LANES)
    vn = pa[:, ATT_WIDTH + LANES:ATT_WIDTH + 2 * LANES].reshape(nb, 1, LANES)
    o_r, o8, s_new, k_out, v_out = _mix_sample(
        q4.transpose(0, 2, 1), k4.transpose(0, 2, 1), v4, g4, s_all, qb.reshape(nb, 8, LANES), kn, vn,
        kc_all, vc_all, p, _sample_bias8(), layer)
    o_r = o_r.reshape(nb, RET_WIDTH).astype(BF16)
    o_a = o8.reshape(nb, 8, ATT_KV_HEADS, ATT_D)[:, np.arange(ATT_Q_HEADS), kv_of].reshape(nb, ATT_WIDTH).astype(BF16)
    return (z, o_r, o_a, x2), (h_re, h_im, s_new, k_out, v_out)


def kernel(x_prompt, x_sample, state_ssm_re, state_ssm_im, state_ret, cache_win_k, cache_win_v, w_in, ssm_lambda_re, ssm_lambda_im, ssm_log_step, ssm_b_re, ssm_b_im, ssm_c_re, ssm_c_im, ssm_d, ssm_glu_w, ssm_glu_b, ret_gn_g, ret_gn_b, attn_sinks, w_out, ln1_g, ln1_b, ffn_w1, ffn_w3, ffn_w2, ln2_g, ln2_b):
    p = _prep_params(w_in, ssm_lambda_re, ssm_lambda_im, ssm_log_step, ssm_b_re, ssm_b_im, ssm_c_re, ssm_c_im,
                     ssm_d, ssm_glu_w, ssm_glu_b, ret_gn_g, ret_gn_b, attn_sinks, w_out, ln1_g, ln1_b,
                     ffn_w1, ffn_w3, ffn_w2, ln2_g, ln2_b)

    xp = x_prompt
    pb, length, _ = x_prompt.shape
    nb = x_sample.shape[0]
    xs = x_sample.reshape(nb, D_MODEL)
    h0r = state_ssm_re.reshape(DEPTH, nb, SSM_LANES)
    h0i = state_ssm_im.reshape(DEPTH, nb, SSM_LANES)
    s_all = state_ret.reshape(DEPTH, nb, RET_WIDTH, RET_D)
    kc_all = cache_win_k.reshape(DEPTH, nb, WINDOW, LANES)
    vc_all = cache_win_v.reshape(DEPTH, nb, WINDOW, LANES)
    p_out, s_out = [], []
    for layer in range(DEPTH):
        prompt_rows, states = _prompt_mix(xp, p, layer)
        p_out.append(states)
        sample_rows, states = _sample_mix(xs, p, layer, h0r, h0i, s_all, kc_all, vc_all)
        s_out.append(states)
        yp, xs = _outffn(prompt_rows, sample_rows, p, layer)
        xp = yp.reshape(pb, length, D_MODEL)
    stack = lambda outs, i: jnp.stack([o[i] for o in outs])
    p_h = stack(p_out, 0)
    p_kv_shape = (DEPTH, pb, TC, ATT_KV_HEADS, ATT_D)
    ssm_shape = (DEPTH, nb, SSM_GROUPS, SSM_STATE)
    s_kv_shape = (DEPTH, nb, WINDOW, ATT_KV_HEADS, ATT_D)

    return (xp, xs.reshape(nb, 1, D_MODEL),
            p_h[:, :, :SSM_LANES].reshape(DEPTH, pb, SSM_GROUPS, SSM_STATE),
            p_h[:, :, SSM_LANES:].reshape(DEPTH, pb, SSM_GROUPS, SSM_STATE),
            stack(p_out, 1), stack(p_out, 2).reshape(p_kv_shape), stack(p_out, 3).reshape(p_kv_shape),
            stack(s_out, 0).reshape(ssm_shape), stack(s_out, 1).reshape(ssm_shape),
            stack(s_out, 2).reshape(DEPTH, nb, RET_HEADS, RET_D, RET_D),
            stack(s_out, 3).reshape(s_kv_shape), stack(s_out, 4).reshape(s_kv_shape))
```

```python
import functools
import math

import numpy as np
import jax
import jax.numpy as jnp
from jax import lax
from jax.experimental import pallas as pl
from jax.experimental.pallas import tpu as pltpu

F32 = jnp.float32
BF16 = jnp.bfloat16

D_MODEL = 1024
DEPTH = 2
SSM_GROUPS = 24
SSM_CH = 16
SSM_STATE = 64
SSM_WIDTH = SSM_GROUPS * SSM_CH
SSM_LANES = SSM_GROUPS * SSM_STATE
RET_HEADS = 4
RET_D = 64
RET_WIDTH = RET_HEADS * RET_D
ATT_Q_HEADS = 6
ATT_KV_HEADS = 2
ATT_GROUP = ATT_Q_HEADS // ATT_KV_HEADS
ATT_D = 64
ATT_WIDTH = ATT_Q_HEADS * ATT_D
WINDOW = 128
FFN_HIDDEN = 2816
IN_WIDTH = 2048
ALPHA = (2 * DEPTH) ** 0.25
LN_EPS = 1e-5
NEG_INF = -1e30

COL_RET = SSM_WIDTH
COL_ATT = SSM_WIDTH + 4 * RET_WIDTH
RET_COLS = 4 * RET_WIDTH
ATT_COLS = ATT_WIDTH + 2 * ATT_KV_HEADS * ATT_D

TC = 128
LANES = 128
N_TILES = SSM_LANES // LANES
HALF = LANES // 2
MXU_COLS = 256
SCAN_PIECES = 4
ROW_TILE = 512
FFN_PARTS = 2
SAMPLE_ROWS = 16
VMEM_LIMIT = 56 * 2 ** 20


def _alibi_slopes(n):
    def pow2(k):
        start = 2.0 ** (-8.0 / k)
        return [start ** (i + 1) for i in range(k)]
    if n & (n - 1) == 0:
        return pow2(n)
    c = 2 ** int(math.floor(math.log2(n)))
    return pow2(c) + pow2(2 * c)[0::2][: n - c]


def _cparams(n_axes):
    return pltpu.CompilerParams(dimension_semantics=("arbitrary",) * n_axes, vmem_limit_bytes=VMEM_LIMIT)


def _const_spec(shape):
    nd = len(shape)
    return pl.BlockSpec(shape, lambda *_: (0,) * nd, pipeline_mode=pl.Buffered(1))


def _layer_spec(shape, layer):
    nd = len(shape)
    return pl.BlockSpec((None,) + tuple(shape), lambda *_: (layer,) + (0,) * nd, pipeline_mode=pl.Buffered(1))


def _dot(a, b):
    return jnp.dot(a, b, preferred_element_type=F32)


def _dot_nt(a, b):
    return lax.dot_general(a, b, (((1,), (1,)), ((), ())), preferred_element_type=F32)


def _dot_tn(a, b):
    return lax.dot_general(a, b, (((0,), (0,)), ((), ())), preferred_element_type=F32)


def _layernorm(v, g, b):
    mu = jnp.mean(v, -1, keepdims=True)
    d = v - mu
    var = jnp.mean(d * d, -1, keepdims=True)
    return d * lax.rsqrt(var + LN_EPS) * g + b


def _gelu_tanh(x):
    return 0.5 * x * (1.0 + jnp.tanh(math.sqrt(2.0 / math.pi) * (x + 0.044715 * (x * x * x))))


def _sigmoid(x):
    return 1.0 / (1.0 + jnp.exp(-x))


def _inproj_kernel(x_ref, w_ref, pu_ref, pr_ref, pa_ref):
    xb = x_ref[...].astype(BF16)
    pu_ref[...] = _dot(xb, w_ref[:, 0:COL_RET])
    pr_ref[...] = _dot(xb, w_ref[:, COL_RET:COL_ATT])
    pa_ref[...] = _dot(xb, w_ref[:, COL_ATT:IN_WIDTH])


def _inproj(x2, p, layer):
    t = x2.shape[0]
    tm = min(ROW_TILE, t)
    row = lambda c: pl.BlockSpec((tm, c), lambda i: (i, 0))
    return pl.pallas_call(
        _inproj_kernel,
        grid=(t // tm,),
        in_specs=[row(D_MODEL), _layer_spec((D_MODEL, IN_WIDTH), layer)],
        out_specs=[row(SSM_WIDTH), row(RET_COLS), row(ATT_COLS)],
        out_shape=[jax.ShapeDtypeStruct((t, SSM_WIDTH), F32),
                   jax.ShapeDtypeStruct((t, RET_COLS), F32),
                   jax.ShapeDtypeStruct((t, ATT_COLS), F32)],
        compiler_params=_cparams(1),
        name="inproj",
    )(x2, p["w_in"])


def _s5_bu(ub, bre_ref, bim_ref):
    res, ims = [], []
    for i in range(3):
        ui = ub[:, LANES * i:LANES * (i + 1)]
        res.append(_dot(ui, bre_ref[i]))
        ims.append(_dot(ui, bim_ref[i]))
    return res, ims


def _s5_out(u, h_re, h_im, cre_ref, cim_ref, d_ref, gw_ref, gb_ref):
    ys = []
    for i in range(3):
        ys.append(_dot(h_re[i].astype(BF16), cre_ref[i]) + _dot(h_im[i].astype(BF16), cim_ref[i]))
    y = jnp.concatenate(ys, axis=1) + d_ref[...] * u
    z = _gelu_tanh(y)
    gate = _sigmoid(_dot(z.astype(BF16), gw_ref[...]) + gb_ref[...])
    return z * gate


def _s5_prompt_body(n, u3, bre_ref, bim_ref, cre_ref, cim_ref, ar_ref, ai_ref, d_ref, gw_ref, gb_ref,
                    z_ref, hlast_ref, h_scr, hc_scr, fillers=()):
    nb = u3.shape[0]

    u = pltpu.einshape("btc->(tb)c", u3)

    res, ims = _s5_bu(u.astype(BF16), bre_ref, bim_ref)
    for i in range(3):
        for k in range(4):
            h_scr[4 * i + k] = res[i][:, LANES * k:LANES * (k + 1)]
            h_scr[N_TILES + 4 * i + k] = ims[i][:, LANES * k:LANES * (k + 1)]

    def step(t, carry):
        r0 = t * nb
        out = []
        for c in range(N_TILES):
            sl = slice(LANES * c, LANES * (c + 1))
            ar, ai = ar_ref[:, sl], ai_ref[:, sl]
            hr, hi = carry[c], carry[N_TILES + c]
            nr = ar * hr - ai * hi + h_scr[c, pl.ds(r0, nb), :]
            ni = ar * hi + ai * hr + h_scr[N_TILES + c, pl.ds(r0, nb), :]
            h_scr[c, pl.ds(r0, nb), :] = nr
            h_scr[N_TILES + c, pl.ds(r0, nb), :] = ni
            out.append((nr, ni))
        return tuple(o[0] for o in out) + tuple(o[1] for o in out)

    fin = tuple(hc_scr[:, LANES * c:LANES * (c + 1)] for c in range(2 * N_TILES))
    groups = len(fillers) + 1
    for g in range(groups):
        for t in range(TC * g // groups, TC * (g + 1) // groups):
            fin = step(t, fin)
        if g < len(fillers):
            fillers[g]()
    for c in range(2 * N_TILES):
        hc_scr[:, LANES * c:LANES * (c + 1)] = fin[c]
        hlast_ref[:, LANES * c:LANES * (c + 1)] = fin[c]

    ys = []

    def out_piece(i):
        def emit():
            h_re = jnp.concatenate([h_scr[4 * i + k] for k in range(4)], axis=1).astype(BF16)
            h_im = jnp.concatenate([h_scr[N_TILES + 4 * i + k] for k in range(4)], axis=1).astype(BF16)
            ys.append(_dot(h_re, cre_ref[i]) + _dot(h_im, cim_ref[i]))
        return emit

    def finish(fillers=()):
        fillers = list(fillers)
        y = jnp.concatenate(ys, axis=1) + d_ref[...] * u
        if fillers:
            fillers.pop(0)()
        z = _gelu_tanh(y)
        if fillers:
            fillers.pop(0)()
        gate = _sigmoid(_dot(z.astype(BF16), gw_ref[...]) + gb_ref[...])
        while fillers:
            fillers.pop(0)()
        z_ref[...] = pltpu.einshape("(tb)c->btc", z * gate, b=nb).astype(BF16)

    return [out_piece(i) for i in range(3)] + [finish]


def _s5_weight_specs(layer):
    ls = lambda *shape: _layer_spec(shape, layer)
    return [ls(3, LANES, 512), ls(3, LANES, 512), ls(3, 512, LANES), ls(3, 512, LANES),
            ls(1, SSM_LANES), ls(1, SSM_LANES), ls(1, SSM_WIDTH), ls(SSM_WIDTH, SSM_WIDTH), ls(1, SSM_WIDTH)]


def _s5_weights(p):
    return (p["bre"], p["bim"], p["cre"], p["cim"], p["ar"], p["ai"], p["ssm_d"], p["glu_w"], p["glu_b"])


def _s5_sample_kernel(u_ref, h0r_ref, h0i_ref, bre_ref, bim_ref, cre_ref, cim_ref, ar_ref, ai_ref, d_ref,
                      gw_ref, gb_ref, z_ref, hr_ref, hi_ref):
    u = u_ref[...]
    res, ims = _s5_bu(u.astype(BF16), bre_ref, bim_ref)
    h_re, h_im = [], []
    for i in range(3):
        sl = slice(512 * i, 512 * (i + 1))
        ar, ai = ar_ref[:, sl], ai_ref[:, sl]
        h0r, h0i = h0r_ref[:, sl], h0i_ref[:, sl]
        nr = ar * h0r - ai * h0i + res[i]
        ni = ar * h0i + ai * h0r + ims[i]
        hr_ref[:, sl] = nr
        hi_ref[:, sl] = ni
        h_re.append(nr)
        h_im.append(ni)
    z_ref[...] = _s5_out(u, h_re, h_im, cre_ref, cim_ref, d_ref, gw_ref, gb_ref).astype(BF16)


def _s5_sample(u, h0r, h0i, p, layer):
    nb = u.shape[0]
    full = lambda c: pl.BlockSpec((nb, c), lambda i: (0, 0))
    state = pl.BlockSpec((None, nb, SSM_LANES), lambda i: (layer, 0, 0))
    return pl.pallas_call(
        _s5_sample_kernel,
        grid=(1,),
        in_specs=[full(SSM_WIDTH), state, state] + _s5_weight_specs(layer),
        out_specs=[full(SSM_WIDTH), full(SSM_LANES), full(SSM_LANES)],
        out_shape=[jax.ShapeDtypeStruct((nb, SSM_WIDTH), BF16),
                   jax.ShapeDtypeStruct((nb, SSM_LANES), F32),
                   jax.ShapeDtypeStruct((nb, SSM_LANES), F32)],
        compiler_params=_cparams(1),
        name="s5_sample",
    )(u, h0r, h0i, *_s5_weights(p))


def _ret_gammas():
    return [1.0 - 2.0 ** (-5.0 - h) for h in range(RET_HEADS)]


def _ret_consts():
    log_g = np.log1p(-(2.0 ** (-5.0 - np.arange(RET_HEADS, dtype=np.float64))))
    idx = np.arange(TC)
    diff = idx[:, None] - idx[None, :]
    dmask = np.where(diff >= 0, np.exp(log_g[:, None, None] * np.maximum(diff, 0)), 0.0)
    inner = np.exp(log_g[:, None] * (idx + 1))
    kvd = np.exp(log_g[:, None] * (TC - 1 - idx))
    cd = np.exp(log_g * TC)
    indec = np.zeros((2, TC, LANES))
    kvdec = np.zeros((2, TC, LANES))
    cdec = np.zeros((2, LANES, LANES))
    blk = np.zeros((LANES, LANES))
    for j in range(2):
        for hh in range(2):
            sl = slice(HALF * hh, HALF * (hh + 1))
            indec[j, :, sl] = inner[2 * j + hh][:, None]
            kvdec[j, :, sl] = kvd[2 * j + hh][:, None]
            cdec[j, sl, sl] = cd[2 * j + hh]
            blk[sl, sl] = 1.0
    as32 = lambda a: jnp.asarray(a, F32)
    return as32(dmask), as32(kvdec), as32(indec), as32(cdec), as32(blk)


def _halfnorm(o, lo):
    zero = jnp.zeros_like(o)
    s_lo = jnp.sum(jnp.where(lo, o, zero), -1, keepdims=True)
    s_hi = jnp.sum(jnp.where(lo, zero, o), -1, keepdims=True)
    d = o - jnp.where(lo, s_lo, s_hi) * (1.0 / HALF)
    d2 = d * d
    v_lo = jnp.sum(jnp.where(lo, d2, zero), -1, keepdims=True)
    v_hi = jnp.sum(jnp.where(lo, zero, d2), -1, keepdims=True)
    var = jnp.where(lo, v_lo, v_hi) * (1.0 / HALF)
    return d * lax.rsqrt(var + LN_EPS)


def _ret_prompt_body(n, pr_ref, dmask_ref, kvdec_ref, indec_ref, cdec_ref, blk_ref, gng_ref, gnb_ref,
                     o_ref, slast_ref, s_scr):
    nb = pr_ref.shape[0]

    lo = lax.broadcasted_iota(jnp.int32, (TC, LANES), 1) < HALF
    zero = jnp.zeros((TC, LANES), F32)

    def row(b):
        for j in range(2):
            q2 = pr_ref[b, :, LANES * j:LANES * (j + 1)]
            k2 = pr_ref[b, :, 256 + LANES * j:256 + LANES * (j + 1)]
            v2 = pr_ref[b, :, 512 + LANES * j:512 + LANES * (j + 1)]
            g2 = pr_ref[b, :, 768 + LANES * j:768 + LANES * (j + 1)]
            k2b = k2.astype(BF16)
            v2b = v2.astype(BF16)
            state = s_scr[b, j]
            qs = jnp.concatenate([jnp.where(lo, q2, zero), jnp.where(lo, zero, q2)], axis=0).astype(BF16)
            sc = _dot_nt(qs, k2b)
            qd = (q2 * indec_ref[j]).astype(BF16)
            lhs = jnp.concatenate(
                [jnp.concatenate([(sc[hh * TC:(hh + 1) * TC] * dmask_ref[2 * j + hh]).astype(BF16), qd], axis=1)
                 for hh in range(2)], axis=0)
            o2 = _dot(lhs, jnp.concatenate([v2b, state.astype(BF16)], axis=0))
            o = jnp.where(lo, o2[0:TC], o2[TC:2 * TC])
            kd = (k2 * kvdec_ref[j]).astype(BF16)
            s_new = cdec_ref[j] * state + blk_ref[...] * _dot_tn(kd, v2b)
            s_scr[b, j] = s_new
            slast_ref[b, j] = s_new
            sl = slice(LANES * j, LANES * (j + 1))
            o = _halfnorm(o, lo) * gng_ref[:, sl] + gnb_ref[:, sl]
            o_ref[b, :, sl] = (g2 * _sigmoid(g2) * o).astype(BF16)

    return [functools.partial(row, b) for b in range(nb)]


def _att_bias():
    slopes = _alibi_slopes(ATT_Q_HEADS)
    qpos = np.arange(TC)[:, None] + TC
    kpos = np.arange(2 * TC)[None, :]
    dist = qpos - kpos
    valid = (dist >= 0) & (dist <= WINDOW)
    prev_cur = np.stack([np.where(valid, -slopes[h] * dist, NEG_INF) for h in range(ATT_Q_HEADS)])
    cur_prev = np.concatenate([prev_cur[:, :, TC:], prev_cur[:, :, :TC]], axis=2)
    first = np.where(np.arange(2 * TC) >= TC, NEG_INF, 0.0)[None, :]
    return jnp.asarray(cur_prev, F32), jnp.asarray(prev_cur, F32), jnp.asarray(first, F32)


def _att_prompt_body(n, layer, sink_ref, pa_ref, bias_cp_ref, bias_pc_ref, first_ref, o_ref, kout_ref, vout_ref,
                     k_scr, v_scr, bias_scr, fillers=()):
    nb = pa_ref.shape[0]
    fillers = list(fillers)

    even = n % 2 == 0
    row0 = pl.multiple_of((n % 2) * TC, TC)
    first = jnp.where(n == 0, first_ref[...], jnp.zeros_like(first_ref))
    for head in range(ATT_Q_HEADS):
        bias_scr[head] = jnp.where(even, bias_cp_ref[head], bias_pc_ref[head]) + first

    lo = lax.broadcasted_iota(jnp.int32, (TC, LANES), 1) < HALF
    zero = jnp.zeros((TC, LANES), F32)
    swap = lambda a: pltpu.roll(a, HALF, 1)
    moved = [head % 2 != head // ATT_GROUP for head in range(ATT_Q_HEADS)]

    def scores(b):
        kcur = pa_ref[b, :, ATT_WIDTH:ATT_WIDTH + LANES]
        vcur = pa_ref[b, :, ATT_WIDTH + LANES:ATT_WIDTH + 2 * LANES]
        kout_ref[b] = kcur
        vout_ref[b] = vcur
        k_scr[b, pl.ds(row0, TC), :] = kcur.astype(BF16)
        v_scr[b, pl.ds(row0, TC), 0:LANES] = vcur.astype(BF16)
        rows = []
        for head in range(ATT_Q_HEADS):
            qcol = pa_ref[b, :, LANES * (head // 2):LANES * (head // 2 + 1)]
            if moved[head]:
                qcol = swap(qcol)
            rows.append(jnp.where(lo, qcol, zero) if head // ATT_GROUP == 0 else jnp.where(lo, zero, qcol))
        qs = jnp.concatenate(rows, axis=0).astype(BF16)
        return _dot_nt(qs, k_scr[b])

    def softmax(s_all):
        es, ms = [], []
        for head in range(ATT_Q_HEADS):
            s = s_all[head * TC:(head + 1) * TC] + bias_scr[head]
            m = jnp.maximum(jnp.max(s, -1, keepdims=True), sink_ref[layer, head])
            es.append(jnp.exp(s - m).astype(BF16))
            ms.append(m)
        return jnp.concatenate(es, axis=0), ms

    def values(b, e_all, ms):
        r_all = _dot(e_all, v_scr[b])
        outs = []
        for head in range(ATT_Q_HEADS):
            r = r_all[head * TC:(head + 1) * TC]
            den = r[:, LANES:2 * LANES] + jnp.exp(sink_ref[layer, head] - ms[head])
            out = r[:, 0:LANES] / den
            outs.append(swap(out) if moved[head] else out)
        for c in range(3):
            o_ref[b, :, LANES * c:LANES * (c + 1)] = jnp.where(lo, outs[2 * c], outs[2 * c + 1]).astype(BF16)

    s_all, soft = {}, {}
    for k in range(nb + 2):
        if fillers:
            fillers.pop(0)()
        if 0 <= k - 2 < nb:
            values(k - 2, *soft.pop(k - 2))
        if k < nb:
            s_all[k] = scores(k)
        if 0 <= k - 1 < nb:
            soft[k - 1] = softmax(s_all.pop(k - 1))
    while fillers:
        fillers.pop(0)()


def _mixer_kernel(sink_ref, x_ref, w_ref,
                  bre_ref, bim_ref, cre_ref, cim_ref, ar_ref, ai_ref, d_ref, gw_ref, gb_ref,
                  dmask_ref, kvdec_ref, indec_ref, cdec_ref, blk_ref, gng_ref, gnb_ref,
                  bias_cp_ref, bias_pc_ref, first_ref,
                  mix_ref, hlast_ref, slast_ref, kout_ref, vout_ref,
                  pr_scr, pa_scr, h_scr, hc_scr, s_scr, k_scr, v_scr, bias_scr, *, layer):
    n = pl.program_id(0)
    nb = x_ref.shape[0]
    z_ref = mix_ref.at[:, :, 0:SSM_WIDTH]
    or_ref = mix_ref.at[:, :, SSM_WIDTH:SSM_WIDTH + RET_WIDTH]
    oa_ref = mix_ref.at[:, :, SSM_WIDTH + RET_WIDTH:D_MODEL]

    @pl.when(n == 0)
    def _():
        hc_scr[...] = jnp.zeros_like(hc_scr)
        s_scr[...] = jnp.zeros_like(s_scr)
        k_scr[...] = jnp.zeros_like(k_scr)
        v_scr[:, :, 0:LANES] = jnp.zeros((nb, 2 * TC, LANES), BF16)
        v_scr[:, :, LANES:2 * LANES] = jnp.ones((nb, 2 * TC, LANES), BF16)

    xb = x_ref[...].reshape(nb * TC, D_MODEL).astype(BF16)
    u3 = _dot(xb, w_ref[:, 0:COL_RET]).reshape(nb, TC, SSM_WIDTH)

    def project(dst, col0, c0, c1):
        def emit():
            dst[:, :, c0:c1] = _dot(xb, w_ref[:, col0 + c0:col0 + c1]).reshape(nb, TC, c1 - c0)
        return emit

    pieces = [project(pr_scr, COL_RET, c, c + MXU_COLS) for c in range(0, RET_COLS, MXU_COLS)]
    pieces += [project(pa_scr, COL_ATT, c, min(c + MXU_COLS, ATT_COLS)) for c in range(0, ATT_COLS, MXU_COLS)]
    *s5_maps, s5_finish = _s5_prompt_body(n, u3, bre_ref, bim_ref, cre_ref, cim_ref, ar_ref, ai_ref, d_ref, gw_ref,
                                          gb_ref, z_ref, hlast_ref, h_scr, hc_scr, fillers=pieces[:SCAN_PIECES])
    for emit in s5_maps:
        emit()
    s5_finish(pieces[SCAN_PIECES:])
    ret_rows = _ret_prompt_body(n, pr_scr, dmask_ref, kvdec_ref, indec_ref, cdec_ref, blk_ref, gng_ref, gnb_ref,
                                or_ref, slast_ref, s_scr)
    _att_prompt_body(n, layer, sink_ref, pa_scr, bias_cp_ref, bias_pc_ref, first_ref, oa_ref, kout_ref, vout_ref,
                     k_scr, v_scr, bias_scr, fillers=ret_rows)


def _mixer_prompt(x3, p, layer):
    nb, length, _ = x3.shape
    chunk = lambda c: pl.BlockSpec((nb, TC, c), lambda n: (0, n, 0))
    fixed = lambda *shape: pl.BlockSpec(shape, lambda n: (0,) * len(shape))
    ret_consts = _ret_consts()
    att_consts = _att_bias()
    return pl.pallas_call(
        functools.partial(_mixer_kernel, layer=layer),
        grid=(length // TC,),
        in_specs=[pl.BlockSpec(memory_space=pltpu.SMEM), chunk(D_MODEL), _layer_spec((D_MODEL, IN_WIDTH), layer)]
                 + _s5_weight_specs(layer)
                 + [_const_spec(c.shape) for c in ret_consts]
                 + [_layer_spec((1, RET_WIDTH), layer), _layer_spec((1, RET_WIDTH), layer)]
                 + [_const_spec(c.shape) for c in att_consts],
        out_specs=[chunk(D_MODEL),
                   fixed(nb, 2 * SSM_LANES), fixed(nb, 2, LANES, LANES), fixed(nb, TC, LANES), fixed(nb, TC, LANES)],
        out_shape=[jax.ShapeDtypeStruct((nb, length, D_MODEL), BF16),
                   jax.ShapeDtypeStruct((nb, 2 * SSM_LANES), F32),
                   jax.ShapeDtypeStruct((nb, 2, LANES, LANES), F32),
                   jax.ShapeDtypeStruct((nb, TC, LANES), F32),
                   jax.ShapeDtypeStruct((nb, TC, LANES), F32)],
        scratch_shapes=[pltpu.VMEM((nb, TC, RET_COLS), F32), pltpu.VMEM((nb, TC, ATT_COLS), F32),
                        pltpu.VMEM((2 * N_TILES, nb * TC, LANES), F32), pltpu.VMEM((nb, 2 * SSM_LANES), F32),
                        pltpu.VMEM((nb, 2, LANES, LANES), F32),
                        pltpu.VMEM((nb, 2 * TC, LANES), BF16), pltpu.VMEM((nb, 2 * TC, 2 * LANES), BF16),
                        pltpu.VMEM((ATT_Q_HEADS, TC, 2 * TC), F32)],
        compiler_params=_cparams(1),
        name="mixer",
    )(p["sinks"], x3, p["w_in"], *_s5_weights(p), *ret_consts, p["gn_g"], p["gn_b"], *att_consts)


def _mix_sample_kernel(qt_ref, kt_ref, v4_ref, g4_ref, s_ref, qb_ref, kn_ref, vn_ref, kc_ref, vc_ref,
                       gng_ref, gnb_ref, bias_ref, sink_ref, or_ref, oa_ref, sn_ref, kout_ref, vout_ref):
    gammas = _ret_gammas()
    sink = sink_ref[:, 0:1]
    bias = bias_ref[...]

    rows = range(SAMPLE_ROWS)
    qt, kt, v4, g4 = qt_ref[...], kt_ref[...], v4_ref[...], g4_ref[...]
    for h in range(RET_HEADS):
        sl = slice(RET_D * h, RET_D * (h + 1))
        sn = gammas[h] * s_ref[:, sl, :] + kt[:, :, h:h + 1] * v4[:, h:h + 1, :]
        sn_ref[:, sl, :] = sn
        o = jnp.sum(qt[:, :, h:h + 1] * sn, axis=1, keepdims=True)
        d = o - jnp.mean(o, -1, keepdims=True)
        var = jnp.mean(d * d, -1, keepdims=True)
        o = d * lax.rsqrt(var + LN_EPS) * gng_ref[h:h + 1, :] + gnb_ref[h:h + 1, :]
        gate = g4[:, h:h + 1, :]
        or_ref[:, h:h + 1, :] = gate * _sigmoid(gate) * o

    q_all = jnp.concatenate([qb_ref[i] for i in rows], axis=0)
    knew = jnp.concatenate([jnp.broadcast_to(kn_ref[i], (8, LANES)) for i in rows], axis=0)
    vnew = jnp.concatenate([jnp.broadcast_to(vn_ref[i], (8, LANES)) for i in rows], axis=0)
    sink_all = jnp.concatenate([sink] * SAMPLE_ROWS, axis=0)
    s = jnp.concatenate([_dot_nt(qb_ref[i].astype(BF16), kc_ref[i].astype(BF16)) + bias for i in rows], axis=0)
    s_self = jnp.sum(q_all * knew, -1, keepdims=True)
    m = jnp.maximum(jnp.maximum(jnp.max(s, -1, keepdims=True), s_self), sink_all)
    e = jnp.exp(s - m)
    e_self = jnp.exp(s_self - m)
    den = jnp.sum(e, -1, keepdims=True) + e_self + jnp.exp(sink_all - m)
    pv = jnp.concatenate([_dot(e[8 * i:8 * (i + 1)].astype(BF16), vc_ref[i].astype(BF16)) for i in rows], axis=0)
    o = (pv + e_self * vnew) / den
    for i in rows:
        oa_ref[i] = o[8 * i:8 * (i + 1)]

    kout_ref[:, 0:WINDOW - 1, :] = kc_ref[:, 1:WINDOW, :]
    kout_ref[:, WINDOW - 1:WINDOW, :] = kn_ref[...]
    vout_ref[:, 0:WINDOW - 1, :] = vc_ref[:, 1:WINDOW, :]
    vout_ref[:, WINDOW - 1:WINDOW, :] = vn_ref[...]


def _mix_sample(qt, kt, v4, g4, s_all, qb, kn, vn, kc_all, vc_all, p, bias8, layer):
    nb = qt.shape[0]
    bb = SAMPLE_ROWS
    blk = lambda *dims: pl.BlockSpec((bb,) + dims, lambda i: (i,) + (0,) * len(dims))
    lblk = lambda *dims: pl.BlockSpec((None, bb) + dims, lambda i: (layer, i) + (0,) * len(dims))
    in_specs = [blk(RET_D, RET_HEADS), blk(RET_D, RET_HEADS), blk(RET_HEADS, RET_D), blk(RET_HEADS, RET_D),
                lblk(RET_WIDTH, RET_D), blk(8, LANES), blk(1, LANES), blk(1, LANES),
                lblk(WINDOW, LANES), lblk(WINDOW, LANES),
                _layer_spec((RET_HEADS, RET_D), layer), _layer_spec((RET_HEADS, RET_D), layer),
                _const_spec((8, LANES)), _layer_spec((8, LANES), layer)]
    args = [qt, kt, v4, g4, s_all, qb, kn, vn, kc_all, vc_all, p["gn_g4"], p["gn_b4"], bias8, p["sink8"]]
    state_dims = ((RET_WIDTH, RET_D), (WINDOW, LANES), (WINDOW, LANES))
    return pl.pallas_call(
        _mix_sample_kernel,
        grid=(nb // bb,),
        in_specs=in_specs,
        out_specs=[blk(RET_HEADS, RET_D), blk(8, LANES)] + [blk(*dims) for dims in state_dims],
        out_shape=[jax.ShapeDtypeStruct((nb, RET_HEADS, RET_D), F32),
                   jax.ShapeDtypeStruct((nb, 8, LANES), F32)]
                  + [jax.ShapeDtypeStruct((nb,) + dims, F32) for dims in state_dims],
        compiler_params=_cparams(1),
        name="mix_sample",
    )(*args)


def _outffn_rows(m_ref, x_ref, wo_ref, g1_ref, b1_ref, w1_ref, w3_ref, w2_ref, g2_ref, b2_ref, o_ref):
    tm = x_ref.shape[0]
    parts = FFN_PARTS if tm % (16 * FFN_PARTS) == 0 else 1
    rows = [slice(tm // parts * k, tm // parts * (k + 1)) for k in range(parts)]

    def mixed(sl):
        return _layernorm(ALPHA * x_ref[sl] + _dot(m_ref[sl], wo_ref[...]), g1_ref[...], b1_ref[...])

    def hidden(x):
        xb = x.astype(BF16)
        h1 = _dot(xb, w1_ref[...])
        h3 = _dot(xb, w3_ref[...])
        return (h1 * _sigmoid(h1) * h3).astype(BF16)

    def out(sl, x, hid):
        o_ref[sl] = _layernorm(ALPHA * x + _dot(hid, w2_ref[...]), g2_ref[...], b2_ref[...])

    xs, hs = {}, {}
    for k in range(parts + 2):
        if k < parts:
            xs[k] = mixed(rows[k])
        if 0 <= k - 1 < parts:
            hs[k - 1] = hidden(xs[k - 1])
        if 0 <= k - 2 < parts:
            out(rows[k - 2], xs.pop(k - 2), hs.pop(k - 2))


def _outffn_kernel(mp_ref, xp_ref, ms_ref, xs_ref, *rest, prompt_steps):
    weights, (op_ref, os_ref) = rest[:-2], rest[-2:]
    i = pl.program_id(0)

    @pl.when(i < prompt_steps)
    def _():
        _outffn_rows(mp_ref, xp_ref, *weights, op_ref)

    @pl.when(i == prompt_steps)
    def _():
        _outffn_rows(ms_ref, xs_ref, *weights, os_ref)


def _outffn(prompt, sample, p, layer):
    tp, ts = prompt[1].shape[0], sample[1].shape[0]
    tm = min(ROW_TILE, tp)
    steps = tp // tm
    tile = lambda c: pl.BlockSpec((tm, c), lambda i: (jnp.minimum(i, steps - 1), 0))
    whole = lambda c: pl.BlockSpec((ts, c), lambda i: (0, 0))
    widths = (D_MODEL, D_MODEL)
    vec = _layer_spec((1, D_MODEL), layer)
    return pl.pallas_call(
        functools.partial(_outffn_kernel, prompt_steps=steps),
        grid=(steps + 1,),
        in_specs=[tile(c) for c in widths] + [whole(c) for c in widths]
                 + [_layer_spec((D_MODEL, D_MODEL), layer), vec, vec,
                    _layer_spec((D_MODEL, FFN_HIDDEN), layer), _layer_spec((D_MODEL, FFN_HIDDEN), layer),
                    _layer_spec((FFN_HIDDEN, D_MODEL), layer), vec, vec],
        out_specs=[tile(D_MODEL), whole(D_MODEL)],
        out_shape=[jax.ShapeDtypeStruct((tp, D_MODEL), F32), jax.ShapeDtypeStruct((ts, D_MODEL), F32)],
        compiler_params=_cparams(1),
        name="outffn",
    )(*prompt, *sample, p["w_out"], p["ln1_g"], p["ln1_b"], p["w1"], p["w3"], p["w2"], p["ln2_g"], p["ln2_b"])


def _prep_params(w_in, lam_re, lam_im, log_step, b_re, b_im, c_re, c_im, ssm_d, glu_w, glu_b,
                 gn_g, gn_b, sinks, w_out, ln1_g, ln1_b, w1, w3, w2, ln2_g, ln2_b):
    col_scale = np.ones((IN_WIDTH,), np.float32)
    col_scale[COL_RET + RET_WIDTH:COL_RET + 2 * RET_WIDTH] = RET_D ** -0.5
    col_scale[COL_ATT:COL_ATT + ATT_WIDTH] = ATT_D ** -0.5

    step = jnp.exp(log_step)[..., None]
    mag = jnp.exp(lam_re * step)
    a_re, a_im = mag * jnp.cos(lam_im * step), mag * jnp.sin(lam_im * step)
    den = lam_re * lam_re + lam_im * lam_im
    k_re = ((a_re - 1.0) * lam_re + a_im * lam_im) / den
    k_im = (a_im * lam_re - (a_re - 1.0) * lam_im) / den
    bb_re = k_re[..., None] * b_re - k_im[..., None] * b_im
    bb_im = k_re[..., None] * b_im + k_im[..., None] * b_re
    eye8 = jnp.eye(8, dtype=F32)

    def b_blocks(m):
        m = m.reshape(DEPTH, 3, 8, SSM_STATE, SSM_CH)
        return jnp.einsum("ligph,gk->lighkp", m, eye8).reshape(DEPTH, 3, LANES, 512).astype(BF16)

    def c_blocks(m):
        m = m.reshape(DEPTH, 3, 8, SSM_CH, SSM_STATE)
        return jnp.einsum("lighp,gk->likpgh", m, eye8).reshape(DEPTH, 3, 512, LANES).astype(BF16)

    row = lambda a: a.reshape(DEPTH, 1, -1)
    sink8 = jnp.concatenate([sinks, jnp.zeros((DEPTH, 2), F32)], axis=1)
    return dict(
        w_in=(w_in * col_scale).astype(BF16),
        bre=b_blocks(bb_re), bim=b_blocks(bb_im), cre=c_blocks(c_re), cim=c_blocks(-c_im),
        ar=row(a_re), ai=row(a_im),
        ssm_d=row(ssm_d), glu_w=glu_w.astype(BF16), glu_b=row(glu_b),
        gn_g=row(gn_g), gn_b=row(gn_b),
        gn_g4=gn_g.reshape(DEPTH, RET_HEADS, RET_D), gn_b4=gn_b.reshape(DEPTH, RET_HEADS, RET_D),
        sinks=sinks, sink8=jnp.broadcast_to(sink8[:, :, None], (DEPTH, 8, LANES)),
        w_out=w_out.astype(BF16), ln1_g=row(ln1_g), ln1_b=row(ln1_b),
        w1=w1.astype(BF16), w3=w3.astype(BF16), w2=w2.astype(BF16), ln2_g=row(ln2_g), ln2_b=row(ln2_b),
    )


def _prompt_mix(x3, p, layer):
    nb, length, _ = x3.shape
    t = nb * length
    mix, hlast, sblk, klast, vlast = _mixer_prompt(x3, p, layer)
    rows = (mix.reshape(t, D_MODEL), x3.reshape(t, D_MODEL))
    s4 = jnp.stack([sblk[:, h // 2, HALF * (h % 2):HALF * (h % 2 + 1), HALF * (h % 2):HALF * (h % 2 + 1)]
                    for h in range(RET_HEADS)], axis=1)
    return rows, (hlast, s4, klast, vlast)


def _sample_bias8():
    slopes = _alibi_slopes(ATT_Q_HEADS)
    dist = WINDOW - np.arange(WINDOW)
    bias = np.zeros((8, WINDOW))
    for h in range(ATT_Q_HEADS):
        bias[h] = -slopes[h] * dist
    return jnp.asarray(bias, F32)


def _sample_mix(x2, p, layer, h0r, h0i, s_all, kc_all, vc_all):
    nb = x2.shape[0]
    pu, pr, pa = _inproj(x2, p, layer)
    z, h_re, h_im = _s5_sample(pu, h0r, h0i, p, layer)
    heads = lambda a: a.reshape(nb, RET_HEADS, RET_D)
    q4, k4, v4, g4 = (heads(pr[:, RET_WIDTH * i:RET_WIDTH * (i + 1)]) for i in range(4))
    q6 = pa[:, 0:ATT_WIDTH].reshape(nb, ATT_Q_HEADS, ATT_D)
    kv_of = np.arange(ATT_Q_HEADS) // ATT_GROUP
    qb = jnp.zeros((nb, 8, ATT_KV_HEADS, ATT_D), F32).at[:, np.arange(ATT_Q_HEADS), kv_of].set(q6)
    kn = pa[:, ATT_WIDTH:ATT_WIDTH + LANES].reshape(nb, 1, LANES)
    vn = pa[:, ATT_WIDTH + LANES:ATT_WIDTH + 2 * LANES].reshape(nb, 1, LANES)
    o_r, o8, s_new, k_out, v_out = _mix_sample(
        q4.transpose(0, 2, 1), k4.transpose(0, 2, 1), v4, g4, s_all, qb.reshape(nb, 8, LANES), kn, vn,
        kc_all, vc_all, p, _sample_bias8(), layer)
    o_r = o_r.reshape(nb, RET_WIDTH).astype(BF16)
    o_a = o8.reshape(nb, 8, ATT_KV_HEADS, ATT_D)[:, np.arange(ATT_Q_HEADS), kv_of].reshape(nb, ATT_WIDTH).astype(BF16)
    return (jnp.concatenate([z, o_r, o_a], axis=1), x2), (h_re, h_im, s_new, k_out, v_out)


def kernel(x_prompt, x_sample, state_ssm_re, state_ssm_im, state_ret, cache_win_k, cache_win_v, w_in, ssm_lambda_re, ssm_lambda_im, ssm_log_step, ssm_b_re, ssm_b_im, ssm_c_re, ssm_c_im, ssm_d, ssm_glu_w, ssm_glu_b, ret_gn_g, ret_gn_b, attn_sinks, w_out, ln1_g, ln1_b, ffn_w1, ffn_w3, ffn_w2, ln2_g, ln2_b):
    p = _prep_params(w_in, ssm_lambda_re, ssm_lambda_im, ssm_log_step, ssm_b_re, ssm_b_im, ssm_c_re, ssm_c_im,
                     ssm_d, ssm_glu_w, ssm_glu_b, ret_gn_g, ret_gn_b, attn_sinks, w_out, ln1_g, ln1_b,
                     ffn_w1, ffn_w3, ffn_w2, ln2_g, ln2_b)

    xp = x_prompt
    pb, length, _ = x_prompt.shape
    nb = x_sample.shape[0]
    xs = x_sample.reshape(nb, D_MODEL)
    h0r = state_ssm_re.reshape(DEPTH, nb, SSM_LANES)
    h0i = state_ssm_im.reshape(DEPTH, nb, SSM_LANES)
    s_all = state_ret.reshape(DEPTH, nb, RET_WIDTH, RET_D)
    kc_all = cache_win_k.reshape(DEPTH, nb, WINDOW, LANES)
    vc_all = cache_win_v.reshape(DEPTH, nb, WINDOW, LANES)
    p_out, s_out = [], []
    for layer in range(DEPTH):
        prompt_rows, states = _prompt_mix(xp, p, layer)
        p_out.append(states)
        sample_rows, states = _sample_mix(xs, p, layer, h0r, h0i, s_all, kc_all, vc_all)
        s_out.append(states)
        yp, xs = _outffn(prompt_rows, sample_rows, p, layer)
        xp = yp.reshape(pb, length, D_MODEL)
    stack = lambda outs, i: jnp.stack([o[i] for o in outs])
    p_h = stack(p_out, 0)
    p_kv_shape = (DEPTH, pb, TC, ATT_KV_HEADS, ATT_D)
    ssm_shape = (DEPTH, nb, SSM_GROUPS, SSM_STATE)
    s_kv_shape = (DEPTH, nb, WINDOW, ATT_KV_HEADS, ATT_D)

    return (xp, xs.reshape(nb, 1, D_MODEL),
            p_h[:, :, :SSM_LANES].reshape(DEPTH, pb, SSM_GROUPS, SSM_STATE),
            p_h[:, :, SSM_LANES:].reshape(DEPTH, pb, SSM_GROUPS, SSM_STATE),
            stack(p_out, 1), stack(p_out, 2).reshape(p_kv_shape), stack(p_out, 3).reshape(p_kv_shape),
            stack(s_out, 0).reshape(ssm_shape), stack(s_out, 1).reshape(ssm_shape),
            stack(s_out, 2).reshape(DEPTH, nb, RET_HEADS, RET_D, RET_D),
            stack(s_out, 3).reshape(s_kv_shape), stack(s_out, 4).reshape(s_kv_shape))
```

```python
import functools
import math

import numpy as np
import jax
import jax.numpy as jnp
from jax import lax
from jax.experimental import pallas as pl
from jax.experimental.pallas import tpu as pltpu

F32 = jnp.float32
BF16 = jnp.bfloat16

D_MODEL = 1024
DEPTH = 2
SSM_GROUPS = 24
SSM_CH = 16
SSM_STATE = 64
SSM_WIDTH = SSM_GROUPS * SSM_CH
SSM_LANES = SSM_GROUPS * SSM_STATE
RET_HEADS = 4
RET_D = 64
RET_WIDTH = RET_HEADS * RET_D
ATT_Q_HEADS = 6
ATT_KV_HEADS = 2
ATT_GROUP = ATT_Q_HEADS // ATT_KV_HEADS
ATT_D = 64
ATT_WIDTH = ATT_Q_HEADS * ATT_D
WINDOW = 128
FFN_HIDDEN = 2816
IN_WIDTH = 2048
ALPHA = (2 * DEPTH) ** 0.25
LN_EPS = 1e-5
NEG_INF = -1e30

COL_RET = SSM_WIDTH
COL_ATT = SSM_WIDTH + 4 * RET_WIDTH
RET_COLS = 4 * RET_WIDTH
ATT_COLS = ATT_WIDTH + 2 * ATT_KV_HEADS * ATT_D

TC = 128
LANES = 128
N_TILES = SSM_LANES // LANES
HALF = LANES // 2
MXU_COLS = 256
SCAN_PIECES = 4
ROW_TILE = 512
FFN_PARTS = 2
SAMPLE_ROWS = 16
VMEM_LIMIT = 56 * 2 ** 20


def _alibi_slopes(n):
    def pow2(k):
        start = 2.0 ** (-8.0 / k)
        return [start ** (i + 1) for i in range(k)]
    if n & (n - 1) == 0:
        return pow2(n)
    c = 2 ** int(math.floor(math.log2(n)))
    return pow2(c) + pow2(2 * c)[0::2][: n - c]


def _cparams(n_axes):
    return pltpu.CompilerParams(dimension_semantics=("arbitrary",) * n_axes, vmem_limit_bytes=VMEM_LIMIT)


def _const_spec(shape):
    nd = len(shape)
    return pl.BlockSpec(shape, lambda *_: (0,) * nd, pipeline_mode=pl.Buffered(1))


def _layer_spec(shape, layer):
    nd = len(shape)
    return pl.BlockSpec((None,) + tuple(shape), lambda *_: (layer,) + (0,) * nd, pipeline_mode=pl.Buffered(1))


def _dot(a, b):
    return jnp.dot(a, b, preferred_element_type=F32)


def _dot_nt(a, b):
    return lax.dot_general(a, b, (((1,), (1,)), ((), ())), preferred_element_type=F32)


def _dot_tn(a, b):
    return lax.dot_general(a, b, (((0,), (0,)), ((), ())), preferred_element_type=F32)


def _layernorm(v, g, b):
    mu = jnp.mean(v, -1, keepdims=True)
    d = v - mu
    var = jnp.mean(d * d, -1, keepdims=True)
    return d * lax.rsqrt(var + LN_EPS) * g + b


def _gelu_tanh(x):
    return 0.5 * x * (1.0 + jnp.tanh(math.sqrt(2.0 / math.pi) * (x + 0.044715 * (x * x * x))))


def _sigmoid(x):
    return 1.0 / (1.0 + jnp.exp(-x))


def _inproj_kernel(x_ref, w_ref, pu_ref, pr_ref, pa_ref):
    xb = x_ref[...].astype(BF16)
    pu_ref[...] = _dot(xb, w_ref[:, 0:COL_RET])
    pr_ref[...] = _dot(xb, w_ref[:, COL_RET:COL_ATT])
    pa_ref[...] = _dot(xb, w_ref[:, COL_ATT:IN_WIDTH])


def _inproj(x2, p, layer):
    t = x2.shape[0]
    tm = min(ROW_TILE, t)
    row = lambda c: pl.BlockSpec((tm, c), lambda i: (i, 0))
    return pl.pallas_call(
        _inproj_kernel,
        grid=(t // tm,),
        in_specs=[row(D_MODEL), _layer_spec((D_MODEL, IN_WIDTH), layer)],
        out_specs=[row(SSM_WIDTH), row(RET_COLS), row(ATT_COLS)],
        out_shape=[jax.ShapeDtypeStruct((t, SSM_WIDTH), F32),
                   jax.ShapeDtypeStruct((t, RET_COLS), F32),
                   jax.ShapeDtypeStruct((t, ATT_COLS), F32)],
        compiler_params=_cparams(1),
        name="inproj",
    )(x2, p["w_in"])


def _s5_bu(ub, bre_ref, bim_ref):
    res, ims = [], []
    for i in range(3):
        ui = ub[:, LANES * i:LANES * (i + 1)]
        res.append(_dot(ui, bre_ref[i]))
        ims.append(_dot(ui, bim_ref[i]))
    return res, ims


def _s5_out(u, h_re, h_im, cre_ref, cim_ref, d_ref, gw_ref, gb_ref):
    ys = []
    for i in range(3):
        ys.append(_dot(h_re[i].astype(BF16), cre_ref[i]) + _dot(h_im[i].astype(BF16), cim_ref[i]))
    y = jnp.concatenate(ys, axis=1) + d_ref[...] * u
    z = _gelu_tanh(y)
    gate = _sigmoid(_dot(z.astype(BF16), gw_ref[...]) + gb_ref[...])
    return z * gate


def _s5_prompt_body(n, u3, bre_ref, bim_ref, cre_ref, cim_ref, ar_ref, ai_ref, d_ref, gw_ref, gb_ref,
                    z_ref, hlast_ref, h_scr, hc_scr, fillers=()):
    nb = u3.shape[0]

    u = pltpu.einshape("btc->(tb)c", u3)

    res, ims = _s5_bu(u.astype(BF16), bre_ref, bim_ref)
    for i in range(3):
        for k in range(4):
            h_scr[4 * i + k] = res[i][:, LANES * k:LANES * (k + 1)]
            h_scr[N_TILES + 4 * i + k] = ims[i][:, LANES * k:LANES * (k + 1)]

    def step(t, carry):
        r0 = t * nb
        out = []
        for c in range(N_TILES):
            sl = slice(LANES * c, LANES * (c + 1))
            ar, ai = ar_ref[:, sl], ai_ref[:, sl]
            hr, hi = carry[c], carry[N_TILES + c]
            nr = ar * hr - ai * hi + h_scr[c, pl.ds(r0, nb), :]
            ni = ar * hi + ai * hr + h_scr[N_TILES + c, pl.ds(r0, nb), :]
            h_scr[c, pl.ds(r0, nb), :] = nr
            h_scr[N_TILES + c, pl.ds(r0, nb), :] = ni
            out.append((nr, ni))
        return tuple(o[0] for o in out) + tuple(o[1] for o in out)

    fin = tuple(hc_scr[:, LANES * c:LANES * (c + 1)] for c in range(2 * N_TILES))
    groups = len(fillers) + 1
    for g in range(groups):
        for t in range(TC * g // groups, TC * (g + 1) // groups):
            fin = step(t, fin)
        if g < len(fillers):
            fillers[g]()
    for c in range(2 * N_TILES):
        hc_scr[:, LANES * c:LANES * (c + 1)] = fin[c]
        hlast_ref[:, LANES * c:LANES * (c + 1)] = fin[c]

    ys = []

    def out_piece(i):
        def emit():
            h_re = jnp.concatenate([h_scr[4 * i + k] for k in range(4)], axis=1).astype(BF16)
            h_im = jnp.concatenate([h_scr[N_TILES + 4 * i + k] for k in range(4)], axis=1).astype(BF16)
            ys.append(_dot(h_re, cre_ref[i]) + _dot(h_im, cim_ref[i]))
        return emit

    def finish(fillers=()):
        fillers = list(fillers)
        y = jnp.concatenate(ys, axis=1) + d_ref[...] * u
        if fillers:
            fillers.pop(0)()
        z = _gelu_tanh(y)
        if fillers:
            fillers.pop(0)()
        gate = _sigmoid(_dot(z.astype(BF16), gw_ref[...]) + gb_ref[...])
        while fillers:
            fillers.pop(0)()
        z_ref[...] = pltpu.einshape("(tb)c->btc", z * gate, b=nb).astype(BF16)

    return [out_piece(i) for i in range(3)] + [finish]


def _s5_weight_specs(layer):
    ls = lambda *shape: _layer_spec(shape, layer)
    return [ls(3, LANES, 512), ls(3, LANES, 512), ls(3, 512, LANES), ls(3, 512, LANES),
            ls(1, SSM_LANES), ls(1, SSM_LANES), ls(1, SSM_WIDTH), ls(SSM_WIDTH, SSM_WIDTH), ls(1, SSM_WIDTH)]


def _s5_weights(p):
    return (p["bre"], p["bim"], p["cre"], p["cim"], p["ar"], p["ai"], p["ssm_d"], p["glu_w"], p["glu_b"])


def _s5_sample_kernel(u_ref, h0r_ref, h0i_ref, bre_ref, bim_ref, cre_ref, cim_ref, ar_ref, ai_ref, d_ref,
                      gw_ref, gb_ref, z_ref, hr_ref, hi_ref):
    u = u_ref[...]
    res, ims = _s5_bu(u.astype(BF16), bre_ref, bim_ref)
    h_re, h_im = [], []
    for i in range(3):
        sl = slice(512 * i, 512 * (i + 1))
        ar, ai = ar_ref[:, sl], ai_ref[:, sl]
        h0r, h0i = h0r_ref[:, sl], h0i_ref[:, sl]
        nr = ar * h0r - ai * h0i + res[i]
        ni = ar * h0i + ai * h0r + ims[i]
        hr_ref[:, sl] = nr
        hi_ref[:, sl] = ni
        h_re.append(nr)
        h_im.append(ni)
    z_ref[...] = _s5_out(u, h_re, h_im, cre_ref, cim_ref, d_ref, gw_ref, gb_ref).astype(BF16)


def _s5_sample(u, h0r, h0i, p, layer):
    nb = u.shape[0]
    full = lambda c: pl.BlockSpec((nb, c), lambda i: (0, 0))
    state = pl.BlockSpec((None, nb, SSM_LANES), lambda i: (layer, 0, 0))
    return pl.pallas_call(
        _s5_sample_kernel,
        grid=(1,),
        in_specs=[full(SSM_WIDTH), state, state] + _s5_weight_specs(layer),
        out_specs=[full(SSM_WIDTH), full(SSM_LANES), full(SSM_LANES)],
        out_shape=[jax.ShapeDtypeStruct((nb, SSM_WIDTH), BF16),
                   jax.ShapeDtypeStruct((nb, SSM_LANES), F32),
                   jax.ShapeDtypeStruct((nb, SSM_LANES), F32)],
        compiler_params=_cparams(1),
        name="s5_sample",
    )(u, h0r, h0i, *_s5_weights(p))


def _ret_gammas():
    return [1.0 - 2.0 ** (-5.0 - h) for h in range(RET_HEADS)]


def _ret_consts():
    log_g = np.log1p(-(2.0 ** (-5.0 - np.arange(RET_HEADS, dtype=np.float64))))
    idx = np.arange(TC)
    diff = idx[:, None] - idx[None, :]
    dmask = np.where(diff >= 0, np.exp(log_g[:, None, None] * np.maximum(diff, 0)), 0.0)
    inner = np.exp(log_g[:, None] * (idx + 1))
    kvd = np.exp(log_g[:, None] * (TC - 1 - idx))
    cd = np.exp(log_g * TC)
    indec = np.zeros((2, TC, LANES))
    kvdec = np.zeros((2, TC, LANES))
    cdec = np.zeros((2, LANES, LANES))
    blk = np.zeros((LANES, LANES))
    for j in range(2):
        for hh in range(2):
            sl = slice(HALF * hh, HALF * (hh + 1))
            indec[j, :, sl] = inner[2 * j + hh][:, None]
            kvdec[j, :, sl] = kvd[2 * j + hh][:, None]
            cdec[j, sl, sl] = cd[2 * j + hh]
            blk[sl, sl] = 1.0
    as32 = lambda a: jnp.asarray(a, F32)
    return as32(dmask), as32(kvdec), as32(indec), as32(cdec), as32(blk)


def _halfnorm(o, lo):
    zero = jnp.zeros_like(o)
    s_lo = jnp.sum(jnp.where(lo, o, zero), -1, keepdims=True)
    s_hi = jnp.sum(jnp.where(lo, zero, o), -1, keepdims=True)
    d = o - jnp.where(lo, s_lo, s_hi) * (1.0 / HALF)
    d2 = d * d
    v_lo = jnp.sum(jnp.where(lo, d2, zero), -1, keepdims=True)
    v_hi = jnp.sum(jnp.where(lo, zero, d2), -1, keepdims=True)
    var = jnp.where(lo, v_lo, v_hi) * (1.0 / HALF)
    return d * lax.rsqrt(var + LN_EPS)


def _ret_prompt_body(n, pr_ref, dmask_ref, kvdec_ref, indec_ref, cdec_ref, blk_ref, gng_ref, gnb_ref,
                     o_ref, slast_ref, s_scr):
    nb = pr_ref.shape[0]

    lo = lax.broadcasted_iota(jnp.int32, (TC, LANES), 1) < HALF
    zero = jnp.zeros((TC, LANES), F32)

    def row(b):
        for j in range(2):
            q2 = pr_ref[b, :, LANES * j:LANES * (j + 1)]
            k2 = pr_ref[b, :, 256 + LANES * j:256 + LANES * (j + 1)]
            v2 = pr_ref[b, :, 512 + LANES * j:512 + LANES * (j + 1)]
            g2 = pr_ref[b, :, 768 + LANES * j:768 + LANES * (j + 1)]
            k2b = k2.astype(BF16)
            v2b = v2.astype(BF16)
            state = s_scr[b, j]
            qs = jnp.concatenate([jnp.where(lo, q2, zero), jnp.where(lo, zero, q2)], axis=0).astype(BF16)
            sc = _dot_nt(qs, k2b)
            qd = (q2 * indec_ref[j]).astype(BF16)
            lhs = jnp.concatenate(
                [jnp.concatenate([(sc[hh * TC:(hh + 1) * TC] * dmask_ref[2 * j + hh]).astype(BF16), qd], axis=1)
                 for hh in range(2)], axis=0)
            o2 = _dot(lhs, jnp.concatenate([v2b, state.astype(BF16)], axis=0))
            o = jnp.where(lo, o2[0:TC], o2[TC:2 * TC])
            kd = (k2 * kvdec_ref[j]).astype(BF16)
            s_new = cdec_ref[j] * state + blk_ref[...] * _dot_tn(kd, v2b)
            s_scr[b, j] = s_new
            slast_ref[b, j] = s_new
            sl = slice(LANES * j, LANES * (j + 1))
            o = _halfnorm(o, lo) * gng_ref[:, sl] + gnb_ref[:, sl]
            o_ref[b, :, sl] = (g2 * _sigmoid(g2) * o).astype(BF16)

    return [functools.partial(row, b) for b in range(nb)]


def _att_bias():
    slopes = _alibi_slopes(ATT_Q_HEADS)
    qpos = np.arange(TC)[:, None] + TC
    kpos = np.arange(2 * TC)[None, :]
    dist = qpos - kpos
    valid = (dist >= 0) & (dist <= WINDOW)
    prev_cur = np.stack([np.where(valid, -slopes[h] * dist, NEG_INF) for h in range(ATT_Q_HEADS)])
    cur_prev = np.concatenate([prev_cur[:, :, TC:], prev_cur[:, :, :TC]], axis=2)
    first = np.where(np.arange(2 * TC) >= TC, NEG_INF, 0.0)[None, :]
    return jnp.asarray(cur_prev, F32), jnp.asarray(prev_cur, F32), jnp.asarray(first, F32)


def _att_prompt_body(n, layer, sink_ref, pa_ref, bias_cp_ref, bias_pc_ref, first_ref, o_ref, kout_ref, vout_ref,
                     k_scr, v_scr, bias_scr, fillers=()):
    nb = pa_ref.shape[0]
    fillers = list(fillers)

    even = n % 2 == 0
    row0 = pl.multiple_of((n % 2) * TC, TC)
    first = jnp.where(n == 0, first_ref[...], jnp.zeros_like(first_ref))
    for head in range(ATT_Q_HEADS):
        bias_scr[head] = jnp.where(even, bias_cp_ref[head], bias_pc_ref[head]) + first

    lo = lax.broadcasted_iota(jnp.int32, (TC, LANES), 1) < HALF
    zero = jnp.zeros((TC, LANES), F32)
    swap = lambda a: pltpu.roll(a, HALF, 1)
    moved = [head % 2 != head // ATT_GROUP for head in range(ATT_Q_HEADS)]

    def scores(b):
        kcur = pa_ref[b, :, ATT_WIDTH:ATT_WIDTH + LANES]
        vcur = pa_ref[b, :, ATT_WIDTH + LANES:ATT_WIDTH + 2 * LANES]
        kout_ref[b] = kcur
        vout_ref[b] = vcur
        k_scr[b, pl.ds(row0, TC), :] = kcur.astype(BF16)
        v_scr[b, pl.ds(row0, TC), 0:LANES] = vcur.astype(BF16)
        rows = []
        for head in range(ATT_Q_HEADS):
            qcol = pa_ref[b, :, LANES * (head // 2):LANES * (head // 2 + 1)]
            if moved[head]:
                qcol = swap(qcol)
            rows.append(jnp.where(lo, qcol, zero) if head // ATT_GROUP == 0 else jnp.where(lo, zero, qcol))
        qs = jnp.concatenate(rows, axis=0).astype(BF16)
        return _dot_nt(qs, k_scr[b])

    def softmax(s_all):
        es, ms = [], []
        for head in range(ATT_Q_HEADS):
            s = s_all[head * TC:(head + 1) * TC] + bias_scr[head]
            m = jnp.maximum(jnp.max(s, -1, keepdims=True), sink_ref[layer, head])
            es.append(jnp.exp(s - m).astype(BF16))
            ms.append(m)
        return jnp.concatenate(es, axis=0), ms

    def values(b, e_all, ms):
        r_all = _dot(e_all, v_scr[b])
        outs = []
        for head in range(ATT_Q_HEADS):
            r = r_all[head * TC:(head + 1) * TC]
            den = r[:, LANES:2 * LANES] + jnp.exp(sink_ref[layer, head] - ms[head])
            out = r[:, 0:LANES] / den
            outs.append(swap(out) if moved[head] else out)
        for c in range(3):
            o_ref[b, :, LANES * c:LANES * (c + 1)] = jnp.where(lo, outs[2 * c], outs[2 * c + 1]).astype(BF16)

    s_all, soft = {}, {}
    for k in range(nb + 2):
        if fillers:
            fillers.pop(0)()
        if 0 <= k - 2 < nb:
            values(k - 2, *soft.pop(k - 2))
        if k < nb:
            s_all[k] = scores(k)
        if 0 <= k - 1 < nb:
            soft[k - 1] = softmax(s_all.pop(k - 1))
    while fillers:
        fillers.pop(0)()


def _mixer_kernel(sink_ref, x_ref, w_ref,
                  bre_ref, bim_ref, cre_ref, cim_ref, ar_ref, ai_ref, d_ref, gw_ref, gb_ref,
                  dmask_ref, kvdec_ref, indec_ref, cdec_ref, blk_ref, gng_ref, gnb_ref,
                  bias_cp_ref, bias_pc_ref, first_ref,
                  mix_ref, hlast_ref, slast_ref, kout_ref, vout_ref,
                  pr_scr, pa_scr, h_scr, hc_scr, s_scr, k_scr, v_scr, bias_scr, *, layer):
    n = pl.program_id(0)
    nb = x_ref.shape[0]
    z_ref = mix_ref.at[:, :, 0:SSM_WIDTH]
    or_ref = mix_ref.at[:, :, SSM_WIDTH:SSM_WIDTH + RET_WIDTH]
    oa_ref = mix_ref.at[:, :, SSM_WIDTH + RET_WIDTH:D_MODEL]

    @pl.when(n == 0)
    def _():
        hc_scr[...] = jnp.zeros_like(hc_scr)
        s_scr[...] = jnp.zeros_like(s_scr)
        k_scr[...] = jnp.zeros_like(k_scr)
        v_scr[:, :, 0:LANES] = jnp.zeros((nb, 2 * TC, LANES), BF16)
        v_scr[:, :, LANES:2 * LANES] = jnp.ones((nb, 2 * TC, LANES), BF16)

    xb = x_ref[...].reshape(nb * TC, D_MODEL).astype(BF16)
    u3 = _dot(xb, w_ref[:, 0:COL_RET]).reshape(nb, TC, SSM_WIDTH)

    def project(dst, col0, c0, c1):
        def emit():
            dst[:, :, c0:c1] = _dot(xb, w_ref[:, col0 + c0:col0 + c1]).reshape(nb, TC, c1 - c0)
        return emit

    pieces = [project(pr_scr, COL_RET, c, c + MXU_COLS) for c in range(0, RET_COLS, MXU_COLS)]
    pieces += [project(pa_scr, COL_ATT, c, min(c + MXU_COLS, ATT_COLS)) for c in range(0, ATT_COLS, MXU_COLS)]
    *s5_maps, s5_finish = _s5_prompt_body(n, u3, bre_ref, bim_ref, cre_ref, cim_ref, ar_ref, ai_ref, d_ref, gw_ref,
                                          gb_ref, z_ref, hlast_ref, h_scr, hc_scr, fillers=pieces[:SCAN_PIECES])
    for emit in s5_maps:
        emit()
    s5_finish(pieces[SCAN_PIECES:])
    ret_rows = _ret_prompt_body(n, pr_scr, dmask_ref, kvdec_ref, indec_ref, cdec_ref, blk_ref, gng_ref, gnb_ref,
                                or_ref, slast_ref, s_scr)
    _att_prompt_body(n, layer, sink_ref, pa_scr, bias_cp_ref, bias_pc_ref, first_ref, oa_ref, kout_ref, vout_ref,
                     k_scr, v_scr, bias_scr, fillers=ret_rows)


def _mixer_prompt(x3, p, layer):
    nb, length, _ = x3.shape
    chunk = lambda c: pl.BlockSpec((nb, TC, c), lambda n: (0, n, 0))
    fixed = lambda *shape: pl.BlockSpec(shape, lambda n: (0,) * len(shape))
    ret_consts = _ret_consts()
    att_consts = _att_bias()
    return pl.pallas_call(
        functools.partial(_mixer_kernel, layer=layer),
        grid=(length // TC,),
        in_specs=[pl.BlockSpec(memory_space=pltpu.SMEM), chunk(D_MODEL), _layer_spec((D_MODEL, IN_WIDTH), layer)]
                 + _s5_weight_specs(layer)
                 + [_const_spec(c.shape) for c in ret_consts]
                 + [_layer_spec((1, RET_WIDTH), layer), _layer_spec((1, RET_WIDTH), layer)]
                 + [_const_spec(c.shape) for c in att_consts],
        out_specs=[chunk(D_MODEL),
                   fixed(nb, 2 * SSM_LANES), fixed(nb, 2, LANES, LANES), fixed(nb, TC, LANES), fixed(nb, TC, LANES)],
        out_shape=[jax.ShapeDtypeStruct((nb, length, D_MODEL), BF16),
                   jax.ShapeDtypeStruct((nb, 2 * SSM_LANES), F32),
                   jax.ShapeDtypeStruct((nb, 2, LANES, LANES), F32),
                   jax.ShapeDtypeStruct((nb, TC, LANES), F32),
                   jax.ShapeDtypeStruct((nb, TC, LANES), F32)],
        scratch_shapes=[pltpu.VMEM((nb, TC, RET_COLS), F32), pltpu.VMEM((nb, TC, ATT_COLS), F32),
                        pltpu.VMEM((2 * N_TILES, nb * TC, LANES), F32), pltpu.VMEM((nb, 2 * SSM_LANES), F32),
                        pltpu.VMEM((nb, 2, LANES, LANES), F32),
                        pltpu.VMEM((nb, 2 * TC, LANES), BF16), pltpu.VMEM((nb, 2 * TC, 2 * LANES), BF16),
                        pltpu.VMEM((ATT_Q_HEADS, TC, 2 * TC), F32)],
        compiler_params=_cparams(1),
        name="mixer",
    )(p["sinks"], x3, p["w_in"], *_s5_weights(p), *ret_consts, p["gn_g"], p["gn_b"], *att_consts)


def _mix_sample_kernel(qt_ref, kt_ref, v4_ref, g4_ref, s_ref, qb_ref, kn_ref, vn_ref, kc_ref, vc_ref,
                       gng_ref, gnb_ref, bias_ref, sink_ref, or_ref, oa_ref, sn_ref, kout_ref, vout_ref):
    gammas = _ret_gammas()
    sink = sink_ref[:, 0:1]
    bias = bias_ref[...]

    rows = range(SAMPLE_ROWS)
    qt, kt, v4, g4 = qt_ref[...], kt_ref[...], v4_ref[...], g4_ref[...]
    for h in range(RET_HEADS):
        sl = slice(RET_D * h, RET_D * (h + 1))
        sn = gammas[h] * s_ref[:, sl, :] + kt[:, :, h:h + 1] * v4[:, h:h + 1, :]
        sn_ref[:, sl, :] = sn
        o = jnp.sum(qt[:, :, h:h + 1] * sn, axis=1, keepdims=True)
        d = o - jnp.mean(o, -1, keepdims=True)
        var = jnp.mean(d * d, -1, keepdims=True)
        o = d * lax.rsqrt(var + LN_EPS) * gng_ref[h:h + 1, :] + gnb_ref[h:h + 1, :]
        gate = g4[:, h:h + 1, :]
        or_ref[:, h:h + 1, :] = gate * _sigmoid(gate) * o

    q_all = jnp.concatenate([qb_ref[i] for i in rows], axis=0)
    knew = jnp.concatenate([jnp.broadcast_to(kn_ref[i], (8, LANES)) for i in rows], axis=0)
    vnew = jnp.concatenate([jnp.broadcast_to(vn_ref[i], (8, LANES)) for i in rows], axis=0)
    sink_all = jnp.concatenate([sink] * SAMPLE_ROWS, axis=0)
    s = jnp.concatenate([_dot_nt(qb_ref[i].astype(BF16), kc_ref[i].astype(BF16)) + bias for i in rows], axis=0)
    s_self = jnp.sum(q_all * knew, -1, keepdims=True)
    m = jnp.maximum(jnp.maximum(jnp.max(s, -1, keepdims=True), s_self), sink_all)
    e = jnp.exp(s - m)
    e_self = jnp.exp(s_self - m)
    den = jnp.sum(e, -1, keepdims=True) + e_self + jnp.exp(sink_all - m)
    pv = jnp.concatenate([_dot(e[8 * i:8 * (i + 1)].astype(BF16), vc_ref[i].astype(BF16)) for i in rows], axis=0)
    o = (pv + e_self * vnew) / den
    for i in rows:
        oa_ref[i] = o[8 * i:8 * (i + 1)]

    kout_ref[:, 0:WINDOW - 1, :] = kc_ref[:, 1:WINDOW, :]
    kout_ref[:, WINDOW - 1:WINDOW, :] = kn_ref[...]
    vout_ref[:, 0:WINDOW - 1, :] = vc_ref[:, 1:WINDOW, :]
    vout_ref[:, WINDOW - 1:WINDOW, :] = vn_ref[...]


def _mix_sample(qt, kt, v4, g4, s_all, qb, kn, vn, kc_all, vc_all, p, bias8, layer):
    nb = qt.shape[0]
    bb = SAMPLE_ROWS
    blk = lambda *dims: pl.BlockSpec((bb,) + dims, lambda i: (i,) + (0,) * len(dims))
    lblk = lambda *dims: pl.BlockSpec((None, bb) + dims, lambda i: (layer, i) + (0,) * len(dims))
    in_specs = [blk(RET_D, RET_HEADS), blk(RET_D, RET_HEADS), blk(RET_HEADS, RET_D), blk(RET_HEADS, RET_D),
                lblk(RET_WIDTH, RET_D), blk(8, LANES), blk(1, LANES), blk(1, LANES),
                lblk(WINDOW, LANES), lblk(WINDOW, LANES),
                _layer_spec((RET_HEADS, RET_D), layer), _layer_spec((RET_HEADS, RET_D), layer),
                _const_spec((8, LANES)), _layer_spec((8, LANES), layer)]
    args = [qt, kt, v4, g4, s_all, qb, kn, vn, kc_all, vc_all, p["gn_g4"], p["gn_b4"], bias8, p["sink8"]]
    state_dims = ((RET_WIDTH, RET_D), (WINDOW, LANES), (WINDOW, LANES))
    return pl.pallas_call(
        _mix_sample_kernel,
        grid=(nb // bb,),
        in_specs=in_specs,
        out_specs=[blk(RET_HEADS, RET_D), blk(8, LANES)] + [blk(*dims) for dims in state_dims],
        out_shape=[jax.ShapeDtypeStruct((nb, RET_HEADS, RET_D), F32),
                   jax.ShapeDtypeStruct((nb, 8, LANES), F32)]
                  + [jax.ShapeDtypeStruct((nb,) + dims, F32) for dims in state_dims],
        compiler_params=_cparams(1),
        name="mix_sample",
    )(*args)


def _outffn_rows(m_ref, x_ref, wo_ref, g1_ref, b1_ref, w1_ref, w3_ref, w2_ref, g2_ref, b2_ref, o_ref):
    tm = x_ref.shape[0]
    parts = FFN_PARTS if tm % (16 * FFN_PARTS) == 0 else 1
    rows = [slice(tm // parts * k, tm // parts * (k + 1)) for k in range(parts)]

    def mixed(sl):
        return _layernorm(ALPHA * x_ref[sl] + _dot(m_ref[sl], wo_ref[...]), g1_ref[...], b1_ref[...])

    def hidden(x):
        xb = x.astype(BF16)
        h1 = _dot(xb, w1_ref[...])
        h3 = _dot(xb, w3_ref[...])
        return (h1 * _sigmoid(h1) * h3).astype(BF16)

    def out(sl, x, hid):
        o_ref[sl] = _layernorm(ALPHA * x + _dot(hid, w2_ref[...]), g2_ref[...], b2_ref[...])

    xs, hs = {}, {}
    for k in range(parts + 2):
        if 0 <= k - 2 < parts:
            out(rows[k - 2], xs.pop(k - 2), hs.pop(k - 2))
        if k < parts:
            xs[k] = mixed(rows[k])
        if 0 <= k - 1 < parts:
            hs[k - 1] = hidden(xs[k - 1])


def _outffn_kernel(mp_ref, xp_ref, ms_ref, xs_ref, *rest, prompt_steps):
    weights, (op_ref, os_ref) = rest[:-2], rest[-2:]
    i = pl.program_id(0)

    @pl.when(i < prompt_steps)
    def _():
        _outffn_rows(mp_ref, xp_ref, *weights, op_ref)

    @pl.when(i == prompt_steps)
    def _():
        _outffn_rows(ms_ref, xs_ref, *weights, os_ref)


def _outffn(prompt, sample, p, layer):
    tp, ts = prompt[1].shape[0], sample[1].shape[0]
    tm = min(ROW_TILE, tp)
    steps = tp // tm
    tile = lambda c: pl.BlockSpec((tm, c), lambda i: (jnp.minimum(i, steps - 1), 0))
    whole = lambda c: pl.BlockSpec((ts, c), lambda i: (0, 0))
    widths = (D_MODEL, D_MODEL)
    vec = _layer_spec((1, D_MODEL), layer)
    return pl.pallas_call(
        functools.partial(_outffn_kernel, prompt_steps=steps),
        grid=(steps + 1,),
        in_specs=[tile(c) for c in widths] + [whole(c) for c in widths]
                 + [_layer_spec((D_MODEL, D_MODEL), layer), vec, vec,
                    _layer_spec((D_MODEL, FFN_HIDDEN), layer), _layer_spec((D_MODEL, FFN_HIDDEN), layer),
                    _layer_spec((FFN_HIDDEN, D_MODEL), layer), vec, vec],
        out_specs=[tile(D_MODEL), whole(D_MODEL)],
        out_shape=[jax.ShapeDtypeStruct((tp, D_MODEL), F32), jax.ShapeDtypeStruct((ts, D_MODEL), F32)],
        compiler_params=_cparams(1),
        name="outffn",
    )(*prompt, *sample, p["w_out"], p["ln1_g"], p["ln1_b"], p["w1"], p["w3"], p["w2"], p["ln2_g"], p["ln2_b"])


def _prep_params(w_in, lam_re, lam_im, log_step, b_re, b_im, c_re, c_im, ssm_d, glu_w, glu_b,
                 gn_g, gn_b, sinks, w_out, ln1_g, ln1_b, w1, w3, w2, ln2_g, ln2_b):
    col_scale = np.ones((IN_WIDTH,), np.float32)
    col_scale[COL_RET + RET_WIDTH:COL_RET + 2 * RET_WIDTH] = RET_D ** -0.5
    col_scale[COL_ATT:COL_ATT + ATT_WIDTH] = ATT_D ** -0.5

    step = jnp.exp(log_step)[..., None]
    mag = jnp.exp(lam_re * step)
    a_re, a_im = mag * jnp.cos(lam_im * step), mag * jnp.sin(lam_im * step)
    den = lam_re * lam_re + lam_im * lam_im
    k_re = ((a_re - 1.0) * lam_re + a_im * lam_im) / den
    k_im = (a_im * lam_re - (a_re - 1.0) * lam_im) / den
    bb_re = k_re[..., None] * b_re - k_im[..., None] * b_im
    bb_im = k_re[..., None] * b_im + k_im[..., None] * b_re
    eye8 = jnp.eye(8, dtype=F32)

    def b_blocks(m):
        m = m.reshape(DEPTH, 3, 8, SSM_STATE, SSM_CH)
        return jnp.einsum("ligph,gk->lighkp", m, eye8).reshape(DEPTH, 3, LANES, 512).astype(BF16)

    def c_blocks(m):
        m = m.reshape(DEPTH, 3, 8, SSM_CH, SSM_STATE)
        return jnp.einsum("lighp,gk->likpgh", m, eye8).reshape(DEPTH, 3, 512, LANES).astype(BF16)

    row = lambda a: a.reshape(DEPTH, 1, -1)
    sink8 = jnp.concatenate([sinks, jnp.zeros((DEPTH, 2), F32)], axis=1)
    return dict(
        w_in=(w_in * col_scale).astype(BF16),
        bre=b_blocks(bb_re), bim=b_blocks(bb_im), cre=c_blocks(c_re), cim=c_blocks(-c_im),
        ar=row(a_re), ai=row(a_im),
        ssm_d=row(ssm_d), glu_w=glu_w.astype(BF16), glu_b=row(glu_b),
        gn_g=row(gn_g), gn_b=row(gn_b),
        gn_g4=gn_g.reshape(DEPTH, RET_HEADS, RET_D), gn_b4=gn_b.reshape(DEPTH, RET_HEADS, RET_D),
        sinks=sinks, sink8=jnp.broadcast_to(sink8[:, :, None], (DEPTH, 8, LANES)),
        w_out=w_out.astype(BF16), ln1_g=row(ln1_g), ln1_b=row(ln1_b),
        w1=w1.astype(BF16), w3=w3.astype(BF16), w2=w2.astype(BF16), ln2_g=row(ln2_g), ln2_b=row(ln2_b),
    )


def _prompt_mix(x3, p, layer):
    nb, length, _ = x3.shape
    t = nb * length
    mix, hlast, sblk, klast, vlast = _mixer_prompt(x3, p, layer)
    rows = (mix.reshape(t, D_MODEL), x3.reshape(t, D_MODEL))
    s4 = jnp.stack([sblk[:, h // 2, HALF * (h % 2):HALF * (h % 2 + 1), HALF * (h % 2):HALF * (h % 2 + 1)]
                    for h in range(RET_HEADS)], axis=1)
    return rows, (hlast, s4, klast, vlast)


def _sample_bias8():
    slopes = _alibi_slopes(ATT_Q_HEADS)
    dist = WINDOW - np.arange(WINDOW)
    bias = np.zeros((8, WINDOW))
    for h in range(ATT_Q_HEADS):
        bias[h] = -slopes[h] * dist
    return jnp.asarray(bias, F32)


def _sample_mix(x2, p, layer, h0r, h0i, s_all, kc_all, vc_all):
    nb = x2.shape[0]
    pu, pr, pa = _inproj(x2, p, layer)
    z, h_re, h_im = _s5_sample(pu, h0r, h0i, p, layer)
    heads = lambda a: a.reshape(nb, RET_HEADS, RET_D)
    q4, k4, v4, g4 = (heads(pr[:, RET_WIDTH * i:RET_WIDTH * (i + 1)]) for i in range(4))
    q6 = pa[:, 0:ATT_WIDTH].reshape(nb, ATT_Q_HEADS, ATT_D)
    kv_of = np.arange(ATT_Q_HEADS) // ATT_GROUP
    qb = jnp.zeros((nb, 8, ATT_KV_HEADS, ATT_D), F32).at[:, np.arange(ATT_Q_HEADS), kv_of].set(q6)
    kn = pa[:, ATT_WIDTH:ATT_WIDTH + LANES].reshape(nb, 1, LANES)
    vn = pa[:, ATT_WIDTH + LANES:ATT_WIDTH + 2 * LANES].reshape(nb, 1, LANES)
    o_r, o8, s_new, k_out, v_out = _mix_sample(
        q4.transpose(0, 2, 1), k4.transpose(0, 2, 1), v4, g4, s_all, qb.reshape(nb, 8, LANES), kn, vn,
        kc_all, vc_all, p, _sample_bias8(), layer)
    o_r = o_r.reshape(nb, RET_WIDTH).astype(BF16)
    o_a = o8.reshape(nb, 8, ATT_KV_HEADS, ATT_D)[:, np.arange(ATT_Q_HEADS), kv_of].reshape(nb, ATT_WIDTH).astype(BF16)
    return (jnp.concatenate([z, o_r, o_a], axis=1), x2), (h_re, h_im, s_new, k_out, v_out)


def kernel(x_prompt, x_sample, state_ssm_re, state_ssm_im, state_ret, cache_win_k, cache_win_v, w_in, ssm_lambda_re, ssm_lambda_im, ssm_log_step, ssm_b_re, ssm_b_im, ssm_c_re, ssm_c_im, ssm_d, ssm_glu_w, ssm_glu_b, ret_gn_g, ret_gn_b, attn_sinks, w_out, ln1_g, ln1_b, ffn_w1, ffn_w3, ffn_w2, ln2_g, ln2_b):
    p = _prep_params(w_in, ssm_lambda_re, ssm_lambda_im, ssm_log_step, ssm_b_re, ssm_b_im, ssm_c_re, ssm_c_im,
                     ssm_d, ssm_glu_w, ssm_glu_b, ret_gn_g, ret_gn_b, attn_sinks, w_out, ln1_g, ln1_b,
                     ffn_w1, ffn_w3, ffn_w2, ln2_g, ln2_b)

    xp = x_prompt
    pb, length, _ = x_prompt.shape
    nb = x_sample.shape[0]
    xs = x_sample.reshape(nb, D_MODEL)
    h0r = state_ssm_re.reshape(DEPTH, nb, SSM_LANES)
    h0i = state_ssm_im.reshape(DEPTH, nb, SSM_LANES)
    s_all = state_ret.reshape(DEPTH, nb, RET_WIDTH, RET_D)
    kc_all = cache_win_k.reshape(DEPTH, nb, WINDOW, LANES)
    vc_all = cache_win_v.reshape(DEPTH, nb, WINDOW, LANES)
    p_out, s_out = [], []
    for layer in range(DEPTH):
        prompt_rows, states = _prompt_mix(xp, p, layer)
        p_out.append(states)
        sample_rows, states = _sample_mix(xs, p, layer, h0r, h0i, s_all, kc_all, vc_all)
        s_out.append(states)
        yp, xs = _outffn(prompt_rows, sample_rows, p, layer)
        xp = yp.reshape(pb, length, D_MODEL)
    stack = lambda outs, i: jnp.stack([o[i] for o in outs])
    p_h = stack(p_out, 0)
    p_kv_shape = (DEPTH, pb, TC, ATT_KV_HEADS, ATT_D)
    ssm_shape = (DEPTH, nb, SSM_GROUPS, SSM_STATE)
    s_kv_shape = (DEPTH, nb, WINDOW, ATT_KV_HEADS, ATT_D)

    return (xp, xs.reshape(nb, 1, D_MODEL),
            p_h[:, :, :SSM_LANES].reshape(DEPTH, pb, SSM_GROUPS, SSM_STATE),
            p_h[:, :, SSM_LANES:].reshape(DEPTH, pb, SSM_GROUPS, SSM_STATE),
            stack(p_out, 1), stack(p_out, 2).reshape(p_kv_shape), stack(p_out, 3).reshape(p_kv_shape),
            stack(s_out, 0).reshape(ssm_shape), stack(s_out, 1).reshape(ssm_shape),
            stack(s_out, 2).reshape(DEPTH, nb, RET_HEADS, RET_D, RET_D),
            stack(s_out, 3).reshape(s_kv_shape), stack(s_out, 4).reshape(s_kv_shape))
```
